```python
import jax, jax.numpy as jnp
from jax import lax
import numpy as np

D_MODEL = 1024
BATCH = 1
SEQ = 16384
DEPTH = 4

GRID_W = 64
CTX_LEN = 256

MLA_HEADS = 8
QK_NOPE = 64
QK_ROPE = 32
V_HEAD = 64
Q_LORA = 384
KV_LORA = 256
ROPE_BASE = 10000.0
Q_BLOCK = 128
MLA_SCALE = (QK_NOPE + QK_ROPE) ** -0.5

MLSTM_HEADS = 4
MQK = 64
MV = 128
CHUNK = 64
FGATE_BIAS_LO = 3.0
FGATE_BIAS_HI = 6.0

N_GROUPS = 4
EXPERTS_PER_GROUP = 8
N_EXPERTS = N_GROUPS * EXPERTS_PER_GROUP
TOP_K_IN_GROUP = 2
D_EXPERT = 256

DEEPNORM_ALPHA = (2 * DEPTH) ** 0.25
DEEPNORM_BETA = (8 * DEPTH) ** -0.25
LN_EPS = 1e-6

IN_SIZES = (Q_LORA, KV_LORA + QK_ROPE, MLSTM_HEADS * MQK, MLSTM_HEADS * MQK,
            MLSTM_HEADS * MV, MLSTM_HEADS * MV, 4 * MLSTM_HEADS, 2 * D_MODEL)
D_IN = sum(IN_SIZES)

F32 = jnp.float32

kernel_name = 'hybrid_mla_mlstm_hmoe_dit'


def layer_norm(x, g=None, b=None):
    xf = x.astype(F32)
    mu = xf.mean(-1, keepdims=True)
    var = jnp.square(xf - mu).mean(-1, keepdims=True)
    y = (xf - mu) * lax.rsqrt(var + LN_EPS)
    if g is not None:
        y = y * g + b
    return y.astype(x.dtype)


def rms_norm(x, g):
    xf = x.astype(F32)
    y = xf * lax.rsqrt(jnp.mean(xf * xf, -1, keepdims=True) + LN_EPS) * g
    return y.astype(x.dtype)


def modulate(x, shift, scale):
    return layer_norm(x) * (1 + scale) + shift


def rope_tables(rows):
    row, col = jnp.meshgrid(jnp.arange(rows, dtype=F32), jnp.arange(GRID_W, dtype=F32), indexing='ij')
    row, col = row.reshape(-1), col.reshape(-1)
    half = QK_ROPE // 2
    inv = ROPE_BASE ** (-jnp.arange(0, half, 2, dtype=F32) / half)
    ar, ac = row[:, None] * inv, col[:, None] * inv
    ang = jnp.concatenate([ar, ar, ac, ac], axis=-1)
    ang = jnp.concatenate([jnp.zeros((CTX_LEN, QK_ROPE), F32), ang], axis=0)
    return jnp.cos(ang), jnp.sin(ang)


def apply_rope(x, cos, sin):
    a1, a2, b1, b2 = jnp.split(x, 4, axis=-1)
    rot = jnp.concatenate([-a2, a1, -b2, b1], axis=-1)
    return (x * cos + rot * sin).astype(x.dtype)


def attend(qn, qr, kn, kr, v):
    s = jnp.einsum('bqhd,bkhd->bhqk', qn, kn) + jnp.einsum('bqhr,bkr->bhqk', qr, kr)
    p = jax.nn.softmax(s.astype(F32) * MLA_SCALE, axis=-1)
    return jnp.einsum('bhqk,bkhd->bqhd', p.astype(v.dtype), v)


def mla_branch(pqd, pkv, cos, sin, w_uq, w_uk, w_uv, g_qn, g_kvn):
    B, T, _ = pqd.shape
    S = T - CTX_LEN
    q = (rms_norm(pqd, g_qn) @ w_uq).reshape(B, T, MLA_HEADS, QK_NOPE + QK_ROPE)
    qn = q[..., :QK_NOPE]
    qr = apply_rope(q[..., QK_NOPE:], cos[:, None], sin[:, None])
    ckv = rms_norm(pkv[..., :KV_LORA], g_kvn)
    kr = apply_rope(pkv[..., KV_LORA:], cos, sin)
    kn = (ckv @ w_uk).reshape(B, T, MLA_HEADS, QK_NOPE)
    v = (ckv @ w_uv).reshape(B, T, MLA_HEADS, V_HEAD)
    oc = attend(qn[:, :CTX_LEN], qr[:, :CTX_LEN], kn[:, :CTX_LEN], kr[:, :CTX_LEN], v[:, :CTX_LEN])
    nb = S // Q_BLOCK

    def blocks(a):
        return jnp.moveaxis(a[:, CTX_LEN:].reshape(B, nb, Q_BLOCK, *a.shape[2:]), 1, 0)

    ol = lax.map(lambda qs: attend(qs[0], qs[1], kn, kr, v), (blocks(qn), blocks(qr)))
    ol = jnp.moveaxis(ol, 0, 1).reshape(B, S, MLA_HEADS * V_HEAD)
    return jnp.concatenate([oc.reshape(B, CTX_LEN, MLA_HEADS * V_HEAD), ol], axis=1)


def mlstm_chunked(q, k, v, ig, lf, state):
    B, H, T, _ = q.shape
    nc = T // CHUNK

    def chunks(a):
        return jnp.moveaxis(a.reshape(B, H, nc, CHUNK, *a.shape[3:]), 2, 0)

    tril = jnp.tril(jnp.ones((CHUNK, CHUNK), bool))

    def step(carry, xs):
        C, n, m = carry
        qc, kc, vc, ic, fc = xs
        b = jnp.cumsum(fc, axis=-1)
        dlog = jnp.where(tril, b[..., :, None] - b[..., None, :] + ic[..., None, :], -jnp.inf)
        inter = b + m[..., None]
        mj = jnp.maximum(inter, dlog.max(-1))
        w = jnp.exp(dlog - mj[..., None])
        sc = jnp.exp(inter - mj)
        qk = jnp.einsum('bhjd,bhsd->bhjs', qc, kc) * w
        num = jnp.einsum('bhjs,bhsv->bhjv', qk, vc) + sc[..., None] * jnp.einsum('bhjd,bhdv->bhjv', qc, C)
        nq = qk.sum(-1) + sc * jnp.einsum('bhjd,bhd->bhj', qc, n)
        h = num / jnp.maximum(jnp.abs(nq), jnp.exp(-mj))[..., None]
        bl = b[..., -1]
        wlog = bl[..., None] - b + ic
        m_new = jnp.maximum(bl + m, wlog.max(-1))
        ws = jnp.exp(wlog - m_new[..., None])
        sd = jnp.exp(bl + m - m_new)
        C_new = sd[..., None, None] * C + jnp.einsum('bhs,bhsd,bhsv->bhdv', ws, kc, vc)
        n_new = sd[..., None] * n + jnp.einsum('bhs,bhsd->bhd', ws, kc)
        return (C_new, n_new, m_new), h

    state, h = lax.scan(step, state, tuple(chunks(a) for a in (q, k, v, ig, lf)))
    return jnp.moveaxis(h, 0, 2).reshape(B, H, T, MV), state


def mlstm_branch(pq, pk, pv, po, pg, b_gates, g_mh):
    B, T, _ = pq.shape

    def heads(a):
        return jnp.moveaxis(a.reshape(B, T, MLSTM_HEADS, -1), 1, 2).astype(F32)

    q = heads(pq) * MQK ** -0.5
    k = heads(pk)
    v = heads(pv)
    gates = jnp.moveaxis((pg.astype(F32) + b_gates.astype(F32)).reshape(B, T, 4, MLSTM_HEADS), 1, 3)
    zero = (jnp.zeros((B, MLSTM_HEADS, MQK, MV), F32),
            jnp.zeros((B, MLSTM_HEADS, MQK), F32),
            jnp.zeros((B, MLSTM_HEADS), F32))

    def direction(ig, f_pre, reverse):
        lf = jax.nn.log_sigmoid(f_pre)

        def seg(a, lo, hi):
            a = a[:, :, lo:hi]
            return jnp.flip(a, axis=2) if reverse else a

        args = (q, k, v, ig, lf)
        hc, st = mlstm_chunked(*(seg(a, 0, CTX_LEN) for a in args), zero)
        hl, _ = mlstm_chunked(*(seg(a, CTX_LEN, T) for a in args), st)
        if reverse:
            hc, hl = jnp.flip(hc, axis=2), jnp.flip(hl, axis=2)
        return jnp.concatenate([hc, hl], axis=2)

    h = direction(gates[:, 0], gates[:, 1], False) + direction(gates[:, 2], gates[:, 3], True)
    h = layer_norm(h) * g_mh.reshape(MLSTM_HEADS, 1, MV)
    h = jnp.moveaxis(h, 1, 2).reshape(B, T, MLSTM_HEADS * MV)
    return (h * jax.nn.sigmoid(po.astype(F32))).astype(pq.dtype)


def token_mixer(hc, hl, cos, sin, w_in, b_gates, w_uq, w_uk, w_uv, g_qn, g_kvn, g_mh,
                w_bo_mla, w_bo_mlstm, w_out):
    h = jnp.concatenate([hc, hl], axis=1)
    p = h @ w_in
    pqd, pkv, pq, pk, pv, po, pg, pm = jnp.split(p, np.cumsum(IN_SIZES)[:-1].tolist(), axis=-1)
    y_mla = mla_branch(pqd, pkv, cos, sin, w_uq, w_uk, w_uv, g_qn, g_kvn) @ w_bo_mla
    y_mlstm = mlstm_branch(pq, pk, pv, po, pg, b_gates, g_mh) @ w_bo_mlstm
    g_a, g_b = jnp.split(jax.nn.sigmoid(pm), 2, axis=-1)
    y = (g_a * y_mla + g_b * y_mlstm) @ w_out
    return y[:, :CTX_LEN], y[:, CTX_LEN:]


def hier_moe(h, w_rg, b_rg, w_re, b_re, w_e_gate, w_e_up, w_e_down):
    B, N, _ = h.shape
    lg = (h @ w_rg + b_rg).astype(F32)
    grp = jnp.argmax(lg, axis=-1)
    p_grp = jnp.take_along_axis(jax.nn.softmax(lg, axis=-1), grp[..., None], axis=-1)
    le = (h @ w_re + b_re).astype(F32).reshape(B, N, N_GROUPS, EXPERTS_PER_GROUP)
    le = jnp.take_along_axis(le, grp[..., None, None], axis=2)[..., 0, :]
    w_top, i_top = lax.top_k(jax.nn.softmax(le, axis=-1), TOP_K_IN_GROUP)
    w_top = w_top / w_top.sum(-1, keepdims=True) * p_grp
    eid = grp[..., None] * EXPERTS_PER_GROUP + i_top
    comb = jnp.einsum('bnk,bnke->bne', w_top, jax.nn.one_hot(eid, N_EXPERTS, dtype=F32))
    y = jnp.zeros_like(h)
    for g in range(N_GROUPS):
        sl = slice(g * EXPERTS_PER_GROUP, (g + 1) * EXPERTS_PER_GROUP)
        a = jnp.einsum('bnd,edf->bnef', h, w_e_gate[sl])
        u = jnp.einsum('bnd,edf->bnef', h, w_e_up[sl])
        hid = jax.nn.silu(a) * u * comb[..., sl, None].astype(h.dtype)
        y = y + jnp.einsum('bnef,efd->bnd', hid, w_e_down[sl])
    return y


def setup_inputs(seed: int = 0) -> dict:
    key = jax.random.key(seed)
    ks = iter(jax.random.split(key, 40))
    L, D = DEPTH, D_MODEL

    def nrm(shape, scale):
        return jax.random.normal(next(ks), shape, F32) * scale

    def gain(shape):
        return 1.0 + nrm(shape, 0.05)

    fg = jnp.linspace(FGATE_BIAS_LO, FGATE_BIAS_HI, MLSTM_HEADS, dtype=F32)
    b_gates = jnp.concatenate([nrm((L, MLSTM_HEADS), 0.1), fg + nrm((L, MLSTM_HEADS), 0.1),
                               nrm((L, MLSTM_HEADS), 0.1), fg + nrm((L, MLSTM_HEADS), 0.1)], axis=-1)
    return {
        'x': nrm((BATCH, SEQ, D), 1.0),
        'c': nrm((BATCH, D), 1.0),
        'ctx': nrm((BATCH, CTX_LEN, D), 1.0),
        'c_ctx': nrm((D,), 1.0),
        'w_ada': nrm((L, D, 6 * D), D ** -0.5),
        'b_ada': nrm((L, 6 * D), 0.1),
        'w_in': nrm((L, D, D_IN), D ** -0.5),
        'b_gates': b_gates,
        'w_uq': nrm((L, Q_LORA, MLA_HEADS * (QK_NOPE + QK_ROPE)), Q_LORA ** -0.5),
        'w_uk': nrm((L, KV_LORA, MLA_HEADS * QK_NOPE), KV_LORA ** -0.5),
        'w_uv': nrm((L, KV_LORA, MLA_HEADS * V_HEAD), KV_LORA ** -0.5),
        'g_qn': gain((L, Q_LORA)),
        'g_kvn': gain((L, KV_LORA)),
        'g_mh': gain((L, MLSTM_HEADS * MV)),
        'w_bo_mla': nrm((L, MLA_HEADS * V_HEAD, D), (MLA_HEADS * V_HEAD) ** -0.5 * DEEPNORM_BETA),
        'w_bo_mlstm': nrm((L, MLSTM_HEADS * MV, D), (MLSTM_HEADS * MV) ** -0.5 * DEEPNORM_BETA),
        'w_out': nrm((L, D, D), D ** -0.5 * DEEPNORM_BETA),
        'ln1_g': gain((L, D)),
        'ln1_b': nrm((L, D), 0.02),
        'w_rg': nrm((L, D, N_GROUPS), D ** -0.5),
        'b_rg': nrm((L, N_GROUPS), 0.01),
        'w_re': nrm((L, D, N_EXPERTS), D ** -0.5),
        'b_re': nrm((L, N_EXPERTS), 0.01),
        'w_e_gate': nrm((L, N_EXPERTS, D, D_EXPERT), D ** -0.5),
        'w_e_up': nrm((L, N_EXPERTS, D, D_EXPERT), D ** -0.5),
        'w_e_down': nrm((L, N_EXPERTS, D_EXPERT, D), D_EXPERT ** -0.5 * DEEPNORM_BETA),
        'ln2_g': gain((L, D)),
        'ln2_b': nrm((L, D), 0.02),
    }


def reference(x, c, ctx, c_ctx, w_ada, b_ada, w_in, b_gates, w_uq, w_uk, w_uv, g_qn, g_kvn, g_mh,
              w_bo_mla, w_bo_mlstm, w_out, ln1_g, ln1_b, w_rg, b_rg, w_re, b_re,
              w_e_gate, w_e_up, w_e_down, ln2_g, ln2_b):
    ROWS = x.shape[1] // GRID_W
    cos, sin = rope_tables(ROWS)
    xl, xc = x, ctx
    s_lat, s_ctx = jax.nn.silu(c), jax.nn.silu(c_ctx)
    for l in range(DEPTH):
        last = l == DEPTH - 1
        sh1l, sc1l, g1l, sh2l, sc2l, g2l = [m[:, None, :] for m in jnp.split(s_lat @ w_ada[l] + b_ada[l], 6, axis=-1)]
        sh1c, sc1c, g1c, sh2c, sc2c, g2c = jnp.split(s_ctx @ w_ada[l] + b_ada[l], 6, axis=-1)
        yc, yl = token_mixer(modulate(xc, sh1c, sc1c), modulate(xl, sh1l, sc1l), cos, sin,
                             w_in[l], b_gates[l], w_uq[l], w_uk[l], w_uv[l], g_qn[l], g_kvn[l], g_mh[l],
                             w_bo_mla[l], w_bo_mlstm[l], w_out[l])
        xl = layer_norm(DEEPNORM_ALPHA * xl + g1l * yl, ln1_g[l], ln1_b[l])
        moe_args = (w_rg[l], b_rg[l], w_re[l], b_re[l], w_e_gate[l], w_e_up[l], w_e_down[l])
        if not last:
            xc = layer_norm(DEEPNORM_ALPHA * xc + g1c * yc, ln1_g[l], ln1_b[l])
            xc = layer_norm(DEEPNORM_ALPHA * xc + g2c * hier_moe(modulate(xc, sh2c, sc2c), *moe_args),
                            ln2_g[l], ln2_b[l])
        xl = layer_norm(DEEPNORM_ALPHA * xl + g2l * hier_moe(modulate(xl, sh2l, sc2l), *moe_args),
                        ln2_g[l], ln2_b[l])
    return xl
```

```python
import functools
import math

import jax
import jax.numpy as jnp
from jax import lax
from jax.experimental import pallas as pl
from jax.experimental.pallas import tpu as pltpu

F32 = jnp.float32
BF16 = jnp.bfloat16

D = 1024
DEPTH = 4
GRID_W = 64
CTX = 256
H_MLA = 8
QK_NOPE = 64
QK_ROPE = 32
V_HEAD = 64
Q_LORA = 384
KV_LORA = 256
ROPE_BASE = 10000.0
MLA_SCALE = (QK_NOPE + QK_ROPE) ** -0.5
H_M = 4
MQK = 64
MV = 128
N_GROUPS = 4
EPG = 8
N_EXP = N_GROUPS * EPG
D_EXP = 256
ALPHA = (2 * DEPTH) ** 0.25
LN_EPS = 1e-6

LANES = 128
VMEM_LIMIT = 56 * 1024 * 1024

HP = LANES
ONES_ROW = V_HEAD
QSCALE = MLA_SCALE * math.log2(math.e)

SEG_PQD = Q_LORA
SEG_CKV = KV_LORA
SEG_KR = 2 * HP
SEG_Q = H_M * MQK
SEG_K = H_M * MQK
SEG_V = H_M * MV
SEG_PO = H_M * MV
SEG_PG = LANES
SEG_PM = 2 * D
IN_SEGS = (SEG_PQD, SEG_CKV, SEG_KR, SEG_Q, SEG_K, SEG_V, SEG_PO, SEG_PG, SEG_PM)
IN_DTYPES = (F32, F32, F32, BF16, BF16, BF16, BF16, F32, BF16)
NP_IN = sum(IN_SEGS)

ROW_TILE = 640
ATT_TQ = 256
ATT_TK = 256
M_CHUNK = 256

_NT = (((1,), (1,)), ((), ()))
_TN = (((0,), (0,)), ((), ()))


def _cparams(n_grid):
    return pltpu.CompilerParams(dimension_semantics=("arbitrary",) * n_grid, vmem_limit_bytes=VMEM_LIMIT)


def _ln(x):
    mu = jnp.mean(x, axis=-1, keepdims=True)
    xc = x - mu
    var = jnp.mean(xc * xc, axis=-1, keepdims=True)
    return xc * lax.rsqrt(var + LN_EPS)


def _rms(x, g):
    return x * lax.rsqrt(jnp.mean(x * x, axis=-1, keepdims=True) + LN_EPS) * g


def _sigmoid(x):
    return 1.0 / (1.0 + jnp.exp(-x))


def _is_ctx_rows(tm):
    rows = pl.program_id(0) * tm + lax.broadcasted_iota(jnp.int32, (tm, 1), 0)
    return rows < CTX


def _mod(mod_ref, k, is_ctx):
    lat = mod_ref[0:1, k * D:(k + 1) * D]
    ctx = mod_ref[1:2, k * D:(k + 1) * D]
    return jnp.where(is_ctx, ctx, lat)


def _ada_kernel(c_ref, w_ref, b_ref, o_ref):
    c = c_ref[...]
    s = c * _sigmoid(c)
    o_ref[...] = jnp.dot(s, w_ref[...], preferred_element_type=F32,
                         precision=lax.Precision.HIGHEST) + b_ref[...]


def _ada(cc, w_ada, b_ada):
    tn = 1536
    n = 6 * D
    return pl.pallas_call(
        _ada_kernel,
        grid=(DEPTH, n // tn),
        in_specs=[pl.BlockSpec((8, D), lambda l, j: (0, 0)),
                  pl.BlockSpec((None, D, tn), lambda l, j: (l, 0, j)),
                  pl.BlockSpec((None, 1, tn), lambda l, j: (l, 0, j))],
        out_specs=pl.BlockSpec((None, 8, tn), lambda l, j: (l, 0, j)),
        out_shape=jax.ShapeDtypeStruct((DEPTH, 8, n), F32),
        compiler_params=_cparams(2),
        name="ada",
    )(cc, w_ada, b_ada.reshape(DEPTH, 1, n))


def _inproj_kernel(x_ref, mod_ref, w_ref, *o_refs, tm):
    is_ctx = _is_ctx_rows(tm)
    h = _ln(x_ref[...]) * (1.0 + _mod(mod_ref, 1, is_ctx)) + _mod(mod_ref, 0, is_ctx)
    hb = h.astype(BF16)
    off = 0
    for o in o_refs:
        n = o.shape[-1]
        o[...] = jnp.dot(hb, w_ref[:, off:off + n], preferred_element_type=F32).astype(o.dtype)
        off += n


def _inproj(x_all, mod, w_in_p, tm):
    t = x_all.shape[0]
    return pl.pallas_call(
        functools.partial(_inproj_kernel, tm=tm),
        grid=(t // tm,),
        in_specs=[pl.BlockSpec((tm, D), lambda i: (i, 0)),
                  pl.BlockSpec((8, 6 * D), lambda i: (0, 0)),
                  pl.BlockSpec((D, NP_IN), lambda i: (0, 0))],
        out_specs=[pl.BlockSpec((tm, n), lambda i: (i, 0)) for n in IN_SEGS],
        out_shape=[jax.ShapeDtypeStruct((t, n), dt) for n, dt in zip(IN_SEGS, IN_DTYPES)],
        compiler_params=_cparams(1),
        name="inproj",
    )(x_all, mod, w_in_p)


def _mla_prep_kernel(pqd_ref, ckv_ref, kr_ref, cos_ref, sin_ref, cost_ref, sint_ref, gq_ref, gk_ref,
                     wqt_ref, wqrt_ref, wk_ref, wvt_ref, qt_ref, k_ref, vt_ref, *, tm):
    qn = _rms(pqd_ref[...], gq_ref[...]).astype(BF16)
    qa = lax.dot_general(wqt_ref[...], qn, _NT, preferred_element_type=F32)
    qb = lax.dot_general(wqrt_ref[...], qn, _NT, preferred_element_type=F32)
    cost = cost_ref[...]
    sint = sint_ref[...]
    ckv = _rms(ckv_ref[...], gk_ref[...]).astype(BF16)
    kn = jnp.dot(ckv, wk_ref[...], preferred_element_type=F32)
    kr = kr_ref[:, :HP] * cos_ref[...] + kr_ref[:, HP:] * sin_ref[...]
    vt = lax.dot_general(wvt_ref[...], ckv, _NT, preferred_element_type=F32)
    is_ones_row = lax.broadcasted_iota(jnp.int32, (HP, tm), 0) == ONES_ROW
    for h in range(H_MLA):
        sl = slice(h * HP, (h + 1) * HP)
        qt_ref[h] = ((qa[sl] * cost + qb[sl] * sint) * QSCALE).astype(BF16)
        k_ref[h] = (kn[:, sl] + kr).astype(BF16)
        vt_ref[h] = jnp.where(is_ones_row, 1.0, vt[sl]).astype(BF16)


def _mla_prep(pqd, ckv, kr, tabs, g_qn, g_kvn, wqt, wqrt, wk, wvt, tm):
    t = pqd.shape[0]
    cos, sin, cost, sint = tabs
    row = lambda n: pl.BlockSpec((tm, n), lambda i: (i, 0))
    col = lambda n: pl.BlockSpec((n, tm), lambda i: (0, i))
    full = lambda a: pl.BlockSpec(a.shape, lambda i: (0,) * a.ndim)
    hd = H_MLA * HP
    return pl.pallas_call(
        functools.partial(_mla_prep_kernel, tm=tm),
        grid=(t // tm,),
        in_specs=[row(Q_LORA), row(KV_LORA), row(SEG_KR), row(HP), row(HP), col(HP), col(HP),
                  full(g_qn), full(g_kvn), full(wqt), full(wqrt), full(wk), full(wvt)],
        out_specs=[pl.BlockSpec((H_MLA, HP, tm), lambda i: (0, 0, i)),
                   pl.BlockSpec((H_MLA, tm, HP), lambda i: (0, i, 0)),
                   pl.BlockSpec((H_MLA, HP, tm), lambda i: (0, 0, i))],
        out_shape=[jax.ShapeDtypeStruct((H_MLA, HP, t), BF16),
                   jax.ShapeDtypeStruct((H_MLA, t, HP), BF16),
                   jax.ShapeDtypeStruct((H_MLA, HP, t), BF16)],
        compiler_params=_cparams(1),
        name="mla_prep",
    )(pqd, ckv, kr, cos, sin, cost, sint, g_qn, g_kvn, wqt, wqrt, wk, wvt)


def _attn_kernel(qt_ref, k_ref, vt_ref, o_ref, *, tq, tk, t):
    i = pl.program_id(1)
    qt = qt_ref[...]
    n_kv = jnp.where(i * tq < CTX, CTX // tk, t // tk)

    def step(j, carry):
        m, acc = carry
        off = pl.multiple_of(j * tk, tk)
        s = jnp.dot(k_ref[pl.ds(off, tk), :], qt, preferred_element_type=F32)
        m_new = jnp.maximum(m, jnp.max(s, axis=0, keepdims=True))
        alpha = jnp.exp2(m - m_new)
        p = jnp.exp2(s - m_new).astype(BF16)
        acc = alpha * acc + jnp.dot(vt_ref[:, pl.ds(off, tk)], p, preferred_element_type=F32)
        return m_new, acc

    m0 = jnp.full((1, tq), -1e30, F32)
    acc0 = jnp.zeros((HP, tq), F32)
    _, acc = lax.fori_loop(0, n_kv, step, (m0, acc0))
    out = acc / acc[ONES_ROW:ONES_ROW + 1, :]
    o_ref[...] = out.T.astype(o_ref.dtype)


def _attention(qt, k, vt, tq, tk):
    t = k.shape[1]
    return pl.pallas_call(
        functools.partial(_attn_kernel, tq=tq, tk=tk, t=t),
        grid=(H_MLA, t // tq),
        in_specs=[pl.BlockSpec((None, HP, tq), lambda h, i: (h, 0, i)),
                  pl.BlockSpec((None, t, HP), lambda h, i: (h, 0, 0)),
                  pl.BlockSpec((None, HP, t), lambda h, i: (h, 0, 0))],
        out_specs=pl.BlockSpec((tq, HP), lambda h, i: (i, h)),
        out_shape=jax.ShapeDtypeStruct((t, H_MLA * HP), BF16),
        compiler_params=_cparams(2),
        name="attention",
    )(qt, k, vt)


def _split3(x):
    hi = x.astype(BF16)
    r = x - hi.astype(F32)
    mid = r.astype(BF16)
    lo = (r - mid.astype(F32)).astype(BF16)
    return hi, mid, lo


def _log_sigmoid(x):
    return jnp.minimum(x, 0.0) - jnp.log(1.0 + jnp.exp(-jnp.abs(x)))


def _mlstm_kernel(q_ref, k_ref, v_ref, gc_ref, gr_ref, bc_ref, br_ref, o_ref, c_scr, m_scr, *, L, rev):
    @pl.when(pl.program_id(0) == 0)
    def _():
        c_scr[...] = jnp.zeros_like(c_scr)
        m_scr[...] = jnp.zeros_like(m_scr)

    row = lax.broadcasted_iota(jnp.int32, (L, L), 0)
    col = lax.broadcasted_iota(jnp.int32, (L, L), 1)
    mask = (row <= col) if rev else (row >= col)
    maskt = (row >= col) if rev else (row <= col)
    mask_b = mask.astype(BF16)
    maskt_b = maskt.astype(BF16)

    gc = gc_ref[...] + bc_ref[...]
    gr = gr_ref[...] + br_ref[...]
    lfc = _log_sigmoid(gc)
    lfr = _log_sigmoid(gr)
    b_col = sum(jnp.dot(mask_b, p, preferred_element_type=F32) for p in _split3(lfc))
    b_row = sum(jnp.dot(p, maskt_b, preferred_element_type=F32) for p in _split3(lfr))
    ones_blk = (lax.broadcasted_iota(jnp.int32, (L, LANES), 1) == 0).astype(BF16)

    for h in range(H_M):
        ci = 8 * rev + h
        cf = ci + 4
        b_c = b_col[:, cf:cf + 1]
        i_c = gc[:, ci:ci + 1]
        b_r = b_row[cf:cf + 1, :]
        i_r = gr[ci:ci + 1, :]
        bl = jnp.sum(lfc[:, cf:cf + 1], axis=0, keepdims=True)
        m = m_scr[h][0:1, 0:1]
        c_aug = c_scr[h]

        dlog = jnp.where(mask, b_c - (b_r - i_r), -jnp.inf)
        inter = b_c + m
        mj = jnp.maximum(inter, jnp.max(dlog, axis=-1, keepdims=True))
        w = jnp.exp(dlog - mj)
        sc = jnp.exp(inter - mj)

        q = q_ref[:, h * MQK:(h + 1) * MQK] * (MQK ** -0.5)
        k = k_ref[:, h * MQK:(h + 1) * MQK]
        v_aug = jnp.concatenate([v_ref[:, h * MV:(h + 1) * MV], ones_blk], axis=1)
        qk = lax.dot_general(q, k, _NT, preferred_element_type=F32) * w
        num = (jnp.dot(qk.astype(BF16), v_aug, preferred_element_type=F32)
               + sc * jnp.dot(q, c_aug.astype(BF16), preferred_element_type=F32))
        den = jnp.maximum(jnp.abs(num[:, MV:MV + 1]), jnp.exp(-mj))
        o_ref[:, h * MV:(h + 1) * MV] = num[:, :MV] / den

        wlog = bl - b_c + i_c
        m_new = jnp.maximum(bl + m, jnp.max(wlog, axis=0, keepdims=True))
        ws = jnp.exp(wlog - m_new)
        sd = jnp.exp(bl + m - m_new)
        wv = (ws * v_aug.astype(F32)).astype(BF16)
        c_scr[h] = sd * c_aug + lax.dot_general(k, wv, _TN, preferred_element_type=F32)
        m_scr[h] = jnp.broadcast_to(m_new, m_scr.shape[1:])


def _mlstm(mq, mk, mv, pg, pg_t, bg_c, bg_r, rev, L):
    t = mq.shape[0]
    nc = t // L
    if rev:
        idx = lambda c: jnp.where(c == 0, 0, nc - c)
    else:
        idx = lambda c: c
    return pl.pallas_call(
        functools.partial(_mlstm_kernel, L=L, rev=int(rev)),
        grid=(nc,),
        in_specs=[pl.BlockSpec((L, SEG_Q), lambda c: (idx(c), 0)),
                  pl.BlockSpec((L, SEG_K), lambda c: (idx(c), 0)),
                  pl.BlockSpec((L, SEG_V), lambda c: (idx(c), 0)),
                  pl.BlockSpec((L, LANES), lambda c: (idx(c), 0)),
                  pl.BlockSpec((16, L), lambda c: (0, idx(c))),
                  pl.BlockSpec((1, LANES), lambda c: (0, 0)),
                  pl.BlockSpec((16, 1), lambda c: (0, 0))],
        out_specs=pl.BlockSpec((L, H_M * MV), lambda c: (idx(c), 0)),
        out_shape=jax.ShapeDtypeStruct((t, H_M * MV), F32),
        scratch_shapes=[pltpu.VMEM((H_M, MQK, 2 * MV), F32), pltpu.VMEM((H_M, 8, LANES), F32)],
        compiler_params=_cparams(1),
        name="mlstm_rev" if rev else "mlstm_fwd",
    )(mq, mk, mv, pg, pg_t, bg_c, bg_r)


def _router(logits):
    tm = logits.shape[0]
    lane = lax.broadcasted_iota(jnp.int32, (tm, LANES), 1)
    big = jnp.int32(LANES)
    neg = -jnp.inf
    is_g = lane < N_GROUPS
    lg = jnp.where(is_g, logits, neg)
    gmax = jnp.max(lg, axis=-1, keepdims=True)
    grp = jnp.min(jnp.where(lg == gmax, lane, big), axis=-1, keepdims=True)
    p_grp = 1.0 / jnp.sum(jnp.where(is_g, jnp.exp(lg - gmax), 0.0), axis=-1, keepdims=True)
    e_lo = N_GROUPS + grp * EPG
    in_grp = (lane >= e_lo) & (lane < e_lo + EPG)
    le = jnp.where(in_grp, logits, neg)
    l1 = jnp.max(le, axis=-1, keepdims=True)
    i1 = jnp.min(jnp.where(le == l1, lane, big), axis=-1, keepdims=True)
    le2 = jnp.where(lane == i1, neg, le)
    l2 = jnp.max(le2, axis=-1, keepdims=True)
    i2 = jnp.min(jnp.where(le2 == l2, lane, big), axis=-1, keepdims=True)
    r = jnp.exp(l2 - l1)
    w1 = p_grp / (1.0 + r)
    w2 = w1 * r
    return jnp.where(lane == i1, w1, 0.0) + jnp.where(lane == i2, w2, 0.0)


def _merge_kernel(att_ref, hf_ref, hb_ref, po_ref, pm_ref, x_ref, mod_ref, gmh_ref, l1g_ref, l1b_ref,
                  wa_ref, wm_ref, wo_ref, wr_ref, br_ref, x1_ref, hm_ref, comb_ref, *, tm):
    is_ctx = _is_ctx_rows(tm)
    y_mla = jnp.dot(att_ref[...], wa_ref[...], preferred_element_type=F32)
    hs = hf_ref[...] + hb_ref[...]
    parts = []
    for h in range(H_M):
        sl = slice(h * MV, (h + 1) * MV)
        parts.append(_ln(hs[:, sl]))
    hn = jnp.concatenate(parts, axis=1) * gmh_ref[...] * _sigmoid(po_ref[...].astype(F32))
    y_ml = jnp.dot(hn.astype(BF16), wm_ref[...], preferred_element_type=F32)
    g_a = _sigmoid(pm_ref[:, :D].astype(F32))
    g_b = _sigmoid(pm_ref[:, D:].astype(F32))
    y = jnp.dot((g_a * y_mla + g_b * y_ml).astype(BF16), wo_ref[...], preferred_element_type=F32)
    x1 = _ln(ALPHA * x_ref[...] + _mod(mod_ref, 2, is_ctx) * y) * l1g_ref[...] + l1b_ref[...]
    x1_ref[...] = x1
    hm = _ln(x1) * (1.0 + _mod(mod_ref, 4, is_ctx)) + _mod(mod_ref, 3, is_ctx)
    hm_ref[...] = hm.astype(BF16)
    h_hi = hm.astype(BF16)
    h_lo = (hm - h_hi.astype(F32)).astype(BF16)
    logits = (jnp.dot(h_hi, wr_ref[0], preferred_element_type=F32)
              + jnp.dot(h_hi, wr_ref[1], preferred_element_type=F32)
              + jnp.dot(h_lo, wr_ref[0], preferred_element_type=F32)) + br_ref[...]
    comb_ref[...] = _router(logits)


def _merge(att, hf, hb, po, pm, x_all, mod, g_mh, l1g, l1b, wa, wm, wo, wr, br, tm):
    t = x_all.shape[0]
    row = lambda n: pl.BlockSpec((tm, n), lambda i: (i, 0))
    full = lambda a: pl.BlockSpec(a.shape, lambda i: (0,) * a.ndim)
    return pl.pallas_call(
        functools.partial(_merge_kernel, tm=tm),
        grid=(t // tm,),
        in_specs=[row(H_MLA * HP), row(H_M * MV), row(H_M * MV), row(SEG_PO), row(SEG_PM), row(D),
                  full(mod), full(g_mh), full(l1g), full(l1b), full(wa), full(wm), full(wo), full(wr), full(br)],
        out_specs=[row(D), row(D), row(LANES)],
        out_shape=[jax.ShapeDtypeStruct((t, D), F32), jax.ShapeDtypeStruct((t, D), BF16),
                   jax.ShapeDtypeStruct((t, LANES), F32)],
        compiler_params=_cparams(1),
        name="merge",
    )(att, hf, hb, po, pm, x_all, mod, g_mh, l1g, l1b, wa, wm, wo, wr, br)


def _moe_kernel(hm_ref, comb_ref, x1_ref, mod_ref, l2g_ref, l2b_ref, wgu_ref, wd_ref, o_ref, acc_ref, *, tm):
    e = pl.program_id(1)

    @pl.when(e == 0)
    def _():
        acc_ref[...] = jnp.zeros_like(acc_ref)

    lane = lax.broadcasted_iota(jnp.int32, (tm, LANES), 1)
    c_e = jnp.sum(jnp.where(lane == e + N_GROUPS, comb_ref[...], 0.0), axis=-1, keepdims=True)
    au = jnp.dot(hm_ref[...], wgu_ref[...], preferred_element_type=F32)
    a = au[:, :D_EXP]
    hid = a * _sigmoid(a) * au[:, D_EXP:] * c_e
    acc_ref[...] += jnp.dot(hid.astype(BF16), wd_ref[...], preferred_element_type=F32)

    @pl.when(e == N_EXP - 1)
    def _():
        is_ctx = _is_ctx_rows(tm)
        z = ALPHA * x1_ref[...] + _mod(mod_ref, 5, is_ctx) * acc_ref[...]
        o_ref[...] = _ln(z) * l2g_ref[...] + l2b_ref[...]


def _moe(hm, comb, x1, mod, l2g, l2b, wgu, wd, tm):
    t = hm.shape[0]
    row = lambda n: pl.BlockSpec((tm, n), lambda i, e: (i, 0))
    full = lambda a: pl.BlockSpec(a.shape, lambda i, e: (0,) * a.ndim)
    return pl.pallas_call(
        functools.partial(_moe_kernel, tm=tm),
        grid=(t // tm, N_EXP),
        in_specs=[row(D), row(LANES), row(D), full(mod), full(l2g), full(l2b),
                  pl.BlockSpec((None, D, 2 * D_EXP), lambda i, e: (e, 0, 0)),
                  pl.BlockSpec((None, D_EXP, D), lambda i, e: (e, 0, 0))],
        out_specs=row(D),
        out_shape=jax.ShapeDtypeStruct((t, D), F32),
        scratch_shapes=[pltpu.VMEM((tm, D), F32)],
        compiler_params=_cparams(2),
        name="moe",
    )(hm, comb, x1, mod, l2g, l2b, wgu, wd)


def _rot_cols(w):
    a1, a2, b1, b2 = jnp.split(w, 4, axis=-1)
    return jnp.concatenate([-a2, a1, -b2, b1], axis=-1)


def _place_rope(w):
    return jnp.pad(w, ((0, 0), (QK_NOPE, HP - QK_NOPE - QK_ROPE)))


def _rope_tables(t):
    rows = (t - CTX) // GRID_W
    row, col = jnp.meshgrid(jnp.arange(rows, dtype=F32), jnp.arange(GRID_W, dtype=F32), indexing="ij")
    row, col = row.reshape(-1), col.reshape(-1)
    half = QK_ROPE // 2
    inv = ROPE_BASE ** (-jnp.arange(0, half, 2, dtype=F32) / half)
    ar, ac = row[:, None] * inv, col[:, None] * inv
    ang = jnp.concatenate([ar, ar, ac, ac], axis=-1)
    ang = jnp.concatenate([jnp.zeros((CTX, QK_ROPE), F32), ang], axis=0)
    pad = ((0, 0), (QK_NOPE, HP - QK_NOPE - QK_ROPE))
    cos = jnp.pad(jnp.cos(ang), pad) + jnp.pad(jnp.ones((t, QK_NOPE), F32), ((0, 0), (0, HP - QK_NOPE)))
    sin = jnp.pad(jnp.sin(ang), pad)
    return cos, sin, cos.T, sin.T


def _layer_weights(l, w_in, w_uq, w_uk, w_uv, w_bo_mla, w_bo_mlstm, w_out, w_rg, b_rg, w_re, b_re,
                   w_e_gate, w_e_up, w_e_down):
    wi = w_in[l]
    o = 0
    segs = []
    for n in (Q_LORA, KV_LORA, QK_ROPE, SEG_Q, SEG_K, SEG_V, SEG_PO, 4 * H_M, SEG_PM):
        segs.append(wi[:, o:o + n])
        o += n
    s_pqd, s_ckv, s_kr, s_q, s_k, s_v, s_po, s_pg, s_pm = segs
    w_in_p = jnp.concatenate(
        [s_pqd, s_ckv, _place_rope(s_kr), _place_rope(_rot_cols(s_kr)), s_q, s_k, s_v, s_po,
         jnp.pad(s_pg, ((0, 0), (0, SEG_PG - 4 * H_M))), s_pm], axis=1).astype(BF16)

    uq = w_uq[l].reshape(Q_LORA, H_MLA, QK_NOPE + QK_ROPE)
    padh = ((0, 0), (0, 0), (0, HP - QK_NOPE - QK_ROPE))
    wq = jnp.pad(uq, padh).reshape(Q_LORA, H_MLA * HP)
    wqr = jnp.pad(jnp.concatenate([jnp.zeros_like(uq[..., :QK_NOPE]), _rot_cols(uq[..., QK_NOPE:])], axis=-1),
                  padh).reshape(Q_LORA, H_MLA * HP)
    padv = ((0, 0), (0, 0), (0, HP - V_HEAD))
    wk = jnp.pad(w_uk[l].reshape(KV_LORA, H_MLA, QK_NOPE), padv).reshape(KV_LORA, H_MLA * HP)
    wv = jnp.pad(w_uv[l].reshape(KV_LORA, H_MLA, V_HEAD), padv).reshape(KV_LORA, H_MLA * HP)
    wa = jnp.pad(w_bo_mla[l].reshape(H_MLA, V_HEAD, D), ((0, 0), (0, HP - V_HEAD), (0, 0))).reshape(H_MLA * HP, D)

    wr = jnp.pad(jnp.concatenate([w_rg[l], w_re[l]], axis=1), ((0, 0), (0, LANES - N_GROUPS - N_EXP)))
    wr_hi = wr.astype(BF16)
    wr_lo = (wr - wr_hi.astype(F32)).astype(BF16)
    br = jnp.pad(jnp.concatenate([b_rg[l], b_re[l]]), (0, LANES - N_GROUPS - N_EXP)).reshape(1, LANES)
    return dict(
        w_in=w_in_p, wqt=wq.T.astype(BF16), wqrt=wqr.T.astype(BF16), wk=wk.astype(BF16), wvt=wv.T.astype(BF16),
        wa=wa.astype(BF16), wm=w_bo_mlstm[l].astype(BF16), wo=w_out[l].astype(BF16),
        wr=jnp.stack([wr_hi, wr_lo]), br=br,
        wgu=jnp.concatenate([w_e_gate[l], w_e_up[l]], axis=-1).astype(BF16), wd=w_e_down[l].astype(BF16))


def kernel(x, c, ctx, c_ctx, w_ada, b_ada, w_in, b_gates, w_uq, w_uk, w_uv, g_qn, g_kvn, g_mh, w_bo_mla,
           w_bo_mlstm, w_out, ln1_g, ln1_b, w_rg, b_rg, w_re, b_re, w_e_gate, w_e_up, w_e_down, ln2_g, ln2_b):
    assert x.shape[0] == 1 and c.shape[0] == 1 and ctx.shape[1] == CTX
    x_all = jnp.concatenate([ctx[0], x[0]], axis=0)
    t = x_all.shape[0]
    tm = ROW_TILE if t % ROW_TILE == 0 else M_CHUNK
    assert t % tm == 0 and t % ATT_TQ == 0 and t % ATT_TK == 0 and t % M_CHUNK == 0

    cc = jnp.pad(jnp.concatenate([c, c_ctx[None]], axis=0), ((0, 6), (0, 0)))
    mods = _ada(cc, w_ada, b_ada)
    tabs = _rope_tables(t)
    row2 = lambda a: a.reshape(1, -1)

    for l in range(DEPTH):
        w = _layer_weights(l, w_in, w_uq, w_uk, w_uv, w_bo_mla, w_bo_mlstm, w_out, w_rg, b_rg, w_re, b_re,
                           w_e_gate, w_e_up, w_e_down)
        mod = mods[l]
        pqd, ckv, kr, mq, mk, mv, po, pg, pm = _inproj(x_all, mod, w["w_in"], tm)
        qt, kk, vt = _mla_prep(pqd, ckv, kr, tabs, row2(g_qn[l]), row2(g_kvn[l]),
                               w["wqt"], w["wqrt"], w["wk"], w["wvt"], tm)
        att = _attention(qt, kk, vt, ATT_TQ, ATT_TK)
        pg_t = pg[:, :16].T
        bg_c = jnp.pad(b_gates[l], (0, LANES - 16)).reshape(1, LANES)
        bg_r = b_gates[l].reshape(16, 1)
        hf = _mlstm(mq, mk, mv, pg, pg_t, bg_c, bg_r, False, M_CHUNK)
        hb = _mlstm(mq, mk, mv, pg, pg_t, bg_c, bg_r, True, M_CHUNK)
        x1, hm, comb = _merge(att, hf, hb, po, pm, x_all, mod, row2(g_mh[l]), row2(ln1_g[l]), row2(ln1_b[l]),
                              w["wa"], w["wm"], w["wo"], w["wr"], w["br"], tm)
        x_all = _moe(hm, comb, x1, mod, row2(ln2_g[l]), row2(ln2_b[l]), w["wgu"], w["wd"], tm)
    return x_all[CTX:][None]
```

```python
import functools
import math

import jax
import jax.numpy as jnp
from jax import lax
from jax.experimental import pallas as pl
from jax.experimental.pallas import tpu as pltpu

F32 = jnp.float32
BF16 = jnp.bfloat16

D = 1024
DEPTH = 4
GRID_W = 64
CTX = 256
H_MLA = 8
QK_NOPE = 64
QK_ROPE = 32
V_HEAD = 64
Q_LORA = 384
KV_LORA = 256
ROPE_BASE = 10000.0
MLA_SCALE = (QK_NOPE + QK_ROPE) ** -0.5
H_M = 4
MQK = 64
MV = 128
N_GROUPS = 4
EPG = 8
N_EXP = N_GROUPS * EPG
D_EXP = 256
ALPHA = (2 * DEPTH) ** 0.25
LN_EPS = 1e-6

LANES = 128
VMEM_LIMIT = 56 * 1024 * 1024

HP = LANES
ONES_ROW = V_HEAD
QSCALE = MLA_SCALE * math.log2(math.e)

SEG_PQD = Q_LORA
SEG_CKV = KV_LORA
SEG_KR = 2 * HP
SEG_Q = H_M * MQK
SEG_K = H_M * MQK
SEG_V = H_M * MV
SEG_PO = H_M * MV
SEG_PG = LANES
SEG_PM = 2 * D
IN_SEGS = (SEG_PQD, SEG_CKV, SEG_KR, SEG_Q, SEG_K, SEG_V, SEG_PO, SEG_PG, SEG_PM)
IN_DTYPES = (F32, F32, F32, BF16, BF16, BF16, BF16, F32, BF16)
NP_IN = sum(IN_SEGS)

ROW_TILE = 640
ATT_TQ = 1280
ATT_CB = 256
ATT_TK = 1280
M_CHUNK = 256

_NT = (((1,), (1,)), ((), ()))
_TN = (((0,), (0,)), ((), ()))


def _cparams(n_grid):
    return pltpu.CompilerParams(dimension_semantics=("arbitrary",) * n_grid, vmem_limit_bytes=VMEM_LIMIT)


def _ln(x):
    mu = jnp.mean(x, axis=-1, keepdims=True)
    xc = x - mu
    var = jnp.mean(xc * xc, axis=-1, keepdims=True)
    return xc * lax.rsqrt(var + LN_EPS)


def _rms(x, g):
    return x * lax.rsqrt(jnp.mean(x * x, axis=-1, keepdims=True) + LN_EPS) * g


def _sigmoid(x):
    return 1.0 / (1.0 + jnp.exp(-x))


def _is_ctx_rows(tm):
    rows = pl.program_id(0) * tm + lax.broadcasted_iota(jnp.int32, (tm, 1), 0)
    return rows < CTX


def _mod(mod_ref, k, is_ctx):
    lat = mod_ref[0:1, k * D:(k + 1) * D]
    ctx = mod_ref[1:2, k * D:(k + 1) * D]
    return jnp.where(is_ctx, ctx, lat)


def _ada_kernel(c_ref, w_ref, b_ref, o_ref):
    c = c_ref[...]
    s = c * _sigmoid(c)
    o_ref[...] = jnp.dot(s, w_ref[...], preferred_element_type=F32,
                         precision=lax.Precision.HIGHEST) + b_ref[...]


def _ada(cc, w_ada, b_ada):
    tn = 1536
    n = 6 * D
    return pl.pallas_call(
        _ada_kernel,
        grid=(DEPTH, n // tn),
        in_specs=[pl.BlockSpec((8, D), lambda l, j: (0, 0)),
                  pl.BlockSpec((None, D, tn), lambda l, j: (l, 0, j)),
                  pl.BlockSpec((None, 1, tn), lambda l, j: (l, 0, j))],
        out_specs=pl.BlockSpec((None, 8, tn), lambda l, j: (l, 0, j)),
        out_shape=jax.ShapeDtypeStruct((DEPTH, 8, n), F32),
        compiler_params=_cparams(2),
        name="ada",
    )(cc, w_ada, b_ada.reshape(DEPTH, 1, n))


def _inproj_kernel(x_ref, mod_ref, w_ref, *o_refs, tm):
    is_ctx = _is_ctx_rows(tm)
    h = _ln(x_ref[...]) * (1.0 + _mod(mod_ref, 1, is_ctx)) + _mod(mod_ref, 0, is_ctx)
    hb = h.astype(BF16)
    off = 0
    for o in o_refs:
        n = o.shape[-1]
        o[...] = jnp.dot(hb, w_ref[:, off:off + n], preferred_element_type=F32).astype(o.dtype)
        off += n


def _inproj(x_all, mod, w_in_p, tm):
    t = x_all.shape[0]
    return pl.pallas_call(
        functools.partial(_inproj_kernel, tm=tm),
        grid=(t // tm,),
        in_specs=[pl.BlockSpec((tm, D), lambda i: (i, 0)),
                  pl.BlockSpec((8, 6 * D), lambda i: (0, 0)),
                  pl.BlockSpec((D, NP_IN), lambda i: (0, 0))],
        out_specs=[pl.BlockSpec((tm, n), lambda i: (i, 0)) for n in IN_SEGS],
        out_shape=[jax.ShapeDtypeStruct((t, n), dt) for n, dt in zip(IN_SEGS, IN_DTYPES)],
        compiler_params=_cparams(1),
        name="inproj",
    )(x_all, mod, w_in_p)


def _mla_prep_kernel(pqd_ref, ckv_ref, kr_ref, cos_ref, sin_ref, cost_ref, sint_ref, gq_ref, gk_ref,
                     wqt_ref, wqrt_ref, wk_ref, wvt_ref, qt_ref, k_ref, vt_ref, *, tm):
    qn = _rms(pqd_ref[...], gq_ref[...]).astype(BF16)
    qa = lax.dot_general(wqt_ref[...], qn, _NT, preferred_element_type=F32)
    qb = lax.dot_general(wqrt_ref[...], qn, _NT, preferred_element_type=F32)
    cost = cost_ref[...]
    sint = sint_ref[...]
    ckv = _rms(ckv_ref[...], gk_ref[...]).astype(BF16)
    kn = jnp.dot(ckv, wk_ref[...], preferred_element_type=F32)
    kr = kr_ref[:, :HP] * cos_ref[...] + kr_ref[:, HP:] * sin_ref[...]
    vt = lax.dot_general(wvt_ref[...], ckv, _NT, preferred_element_type=F32)
    is_ones_row = lax.broadcasted_iota(jnp.int32, (HP, tm), 0) == ONES_ROW
    for h in range(H_MLA):
        sl = slice(h * HP, (h + 1) * HP)
        qt_ref[h] = ((qa[sl] * cost + qb[sl] * sint) * QSCALE).astype(BF16)
        k_ref[h] = (kn[:, sl] + kr).astype(BF16)
        vt_ref[h] = jnp.where(is_ones_row, 1.0, vt[sl]).astype(BF16)


def _mla_prep(pqd, ckv, kr, tabs, g_qn, g_kvn, wqt, wqrt, wk, wvt, tm):
    t = pqd.shape[0]
    cos, sin, cost, sint = tabs
    row = lambda n: pl.BlockSpec((tm, n), lambda i: (i, 0))
    col = lambda n: pl.BlockSpec((n, tm), lambda i: (0, i))
    full = lambda a: pl.BlockSpec(a.shape, lambda i: (0,) * a.ndim)
    hd = H_MLA * HP
    return pl.pallas_call(
        functools.partial(_mla_prep_kernel, tm=tm),
        grid=(t // tm,),
        in_specs=[row(Q_LORA), row(KV_LORA), row(SEG_KR), row(HP), row(HP), col(HP), col(HP),
                  full(g_qn), full(g_kvn), full(wqt), full(wqrt), full(wk), full(wvt)],
        out_specs=[pl.BlockSpec((H_MLA, HP, tm), lambda i: (0, 0, i)),
                   pl.BlockSpec((H_MLA, tm, HP), lambda i: (0, i, 0)),
                   pl.BlockSpec((H_MLA, HP, tm), lambda i: (0, 0, i))],
        out_shape=[jax.ShapeDtypeStruct((H_MLA, HP, t), BF16),
                   jax.ShapeDtypeStruct((H_MLA, t, HP), BF16),
                   jax.ShapeDtypeStruct((H_MLA, HP, t), BF16)],
        compiler_params=_cparams(1),
        name="mla_prep",
    )(pqd, ckv, kr, cos, sin, cost, sint, g_qn, g_kvn, wqt, wqrt, wk, wvt)


def _scores(k, qt, m):
    s = jnp.dot(k, qt, preferred_element_type=F32)
    return s, jnp.maximum(m, jnp.max(s, axis=0, keepdims=True))


def _accumulate(s, vt, m, m_new, acc):
    p = jnp.exp2(s - m_new).astype(BF16)
    return jnp.exp2(m - m_new) * acc + jnp.dot(vt, p, preferred_element_type=F32)


def _attn_kernel(qt_ref, k_ref, vt_ref, o_ref, m_ref, acc_ref, s_ref, mp_ref, *, tq, tk, t, cb):
    ncb = tq // cb
    n_kv = t // tk
    m_ref[...] = jnp.full(m_ref.shape, -1e30, F32)
    acc_ref[...] = jnp.zeros(acc_ref.shape, F32)
    s_ref[...] = jnp.full(s_ref.shape, -jnp.inf, F32)
    mp_ref[...] = jnp.full(mp_ref.shape, -1e30, F32)

    def step(j, carry):
        off = pl.multiple_of(j * tk, tk)
        off_prev = pl.multiple_of(jnp.maximum(j - 1, 0) * tk, tk)
        k = k_ref[pl.ds(off, tk), :]
        vt = vt_ref[:, pl.ds(off, tk)]
        pending = (ncb - 1, s_ref[...], mp_ref[0], mp_ref[1], vt_ref[:, pl.ds(off_prev, tk)])
        for c in range(ncb):
            m_old = m_ref[c]
            s, m_new = _scores(k, qt_ref[:, c * cb:(c + 1) * cb], m_old)
            m_ref[c] = m_new
            pc, ps, pm_old, pm_new, pvt = pending
            acc_ref[pc] = _accumulate(ps, pvt, pm_old, pm_new, acc_ref[pc])
            pending = (c, s, m_old, m_new, vt)
        s_ref[...] = pending[1]
        mp_ref[0] = pending[2]
        mp_ref[1] = pending[3]
        return carry

    lax.fori_loop(0, n_kv, step, 0)
    acc_ref[ncb - 1] = _accumulate(s_ref[...], vt_ref[:, (n_kv - 1) * tk:], mp_ref[0], mp_ref[1], acc_ref[ncb - 1])
    for c in range(ncb):
        acc = acc_ref[c]
        o_ref[c * cb:(c + 1) * cb, :] = (acc / acc[ONES_ROW:ONES_ROW + 1, :]).T.astype(o_ref.dtype)

    @pl.when(pl.program_id(1) == 0)
    def _():
        m0 = jnp.full((1, CTX), -1e30, F32)
        s, m_new = _scores(k_ref[0:CTX, :], qt_ref[:, 0:CTX], m0)
        acc_c = _accumulate(s, vt_ref[:, 0:CTX], m0, m_new, jnp.zeros((HP, CTX), F32))
        o_ref[0:CTX, :] = (acc_c / acc_c[ONES_ROW:ONES_ROW + 1, :]).T.astype(o_ref.dtype)


def _attention(qt, k, vt, tq, tk, cb):
    t = k.shape[1]
    return pl.pallas_call(
        functools.partial(_attn_kernel, tq=tq, tk=tk, t=t, cb=cb),
        grid=(H_MLA, t // tq),
        in_specs=[pl.BlockSpec((None, HP, tq), lambda h, i: (h, 0, i)),
                  pl.BlockSpec((None, t, HP), lambda h, i: (h, 0, 0)),
                  pl.BlockSpec((None, HP, t), lambda h, i: (h, 0, 0))],
        out_specs=pl.BlockSpec((tq, HP), lambda h, i: (i, h)),
        out_shape=jax.ShapeDtypeStruct((t, H_MLA * HP), BF16),
        scratch_shapes=[pltpu.VMEM((tq // cb, 1, cb), F32), pltpu.VMEM((tq // cb, HP, cb), F32),
                        pltpu.VMEM((tk, cb), F32), pltpu.VMEM((2, 1, cb), F32)],
        compiler_params=_cparams(2),
        name="attention",
    )(qt, k, vt)


def _split3(x):
    hi = x.astype(BF16)
    r = x - hi.astype(F32)
    mid = r.astype(BF16)
    lo = (r - mid.astype(F32)).astype(BF16)
    return hi, mid, lo


def _log_sigmoid(x):
    return jnp.minimum(x, 0.0) - jnp.log(1.0 + jnp.exp(-jnp.abs(x)))


def _mlstm_kernel(q_ref, k_ref, v_ref, gc_ref, gr_ref, bc_ref, br_ref, o_ref, c_scr, m_scr, *, L, rev):
    @pl.when(pl.program_id(0) == 0)
    def _():
        c_scr[...] = jnp.zeros_like(c_scr)
        m_scr[...] = jnp.zeros_like(m_scr)

    row = lax.broadcasted_iota(jnp.int32, (L, L), 0)
    col = lax.broadcasted_iota(jnp.int32, (L, L), 1)
    mask = (row <= col) if rev else (row >= col)
    maskt = (row >= col) if rev else (row <= col)
    mask_b = mask.astype(BF16)
    maskt_b = maskt.astype(BF16)

    gc = gc_ref[...] + bc_ref[...]
    gr = gr_ref[...] + br_ref[...]
    lfc = _log_sigmoid(gc)
    lfr = _log_sigmoid(gr)
    b_col = sum(jnp.dot(mask_b, p, preferred_element_type=F32) for p in _split3(lfc))
    b_row = sum(jnp.dot(p, maskt_b, preferred_element_type=F32) for p in _split3(lfr))
    ones_blk = (lax.broadcasted_iota(jnp.int32, (L, LANES), 1) == 0).astype(BF16)

    for h in range(H_M):
        ci = 8 * rev + h
        cf = ci + 4
        b_c = b_col[:, cf:cf + 1]
        i_c = gc[:, ci:ci + 1]
        b_r = b_row[cf:cf + 1, :]
        i_r = gr[ci:ci + 1, :]
        bl = jnp.sum(lfc[:, cf:cf + 1], axis=0, keepdims=True)
        m = m_scr[h][0:1, 0:1]
        c_aug = c_scr[h]

        dlog = jnp.where(mask, b_c - (b_r - i_r), -jnp.inf)
        inter = b_c + m
        mj = jnp.maximum(inter, jnp.max(dlog, axis=-1, keepdims=True))
        w = jnp.exp(dlog - mj)
        sc = jnp.exp(inter - mj)

        q = q_ref[:, h * MQK:(h + 1) * MQK] * (MQK ** -0.5)
        k = k_ref[:, h * MQK:(h + 1) * MQK]
        v_aug = jnp.concatenate([v_ref[:, h * MV:(h + 1) * MV], ones_blk], axis=1)
        qk = lax.dot_general(q, k, _NT, preferred_element_type=F32) * w
        num = (jnp.dot(qk.astype(BF16), v_aug, preferred_element_type=F32)
               + sc * jnp.dot(q, c_aug.astype(BF16), preferred_element_type=F32))
        den = jnp.maximum(jnp.abs(num[:, MV:MV + 1]), jnp.exp(-mj))
        o_ref[:, h * MV:(h + 1) * MV] = num[:, :MV] / den

        wlog = bl - b_c + i_c
        m_new = jnp.maximum(bl + m, jnp.max(wlog, axis=0, keepdims=True))
        ws = jnp.exp(wlog - m_new)
        sd = jnp.exp(bl + m - m_new)
        wv = (ws * v_aug.astype(F32)).astype(BF16)
        c_scr[h] = sd * c_aug + lax.dot_general(k, wv, _TN, preferred_element_type=F32)
        m_scr[h] = jnp.broadcast_to(m_new, m_scr.shape[1:])


def _mlstm(mq, mk, mv, pg, pg_t, bg_c, bg_r, rev, L):
    t = mq.shape[0]
    nc = t // L
    if rev:
        idx = lambda c: jnp.where(c == 0, 0, nc - c)
    else:
        idx = lambda c: c
    return pl.pallas_call(
        functools.partial(_mlstm_kernel, L=L, rev=int(rev)),
        grid=(nc,),
        in_specs=[pl.BlockSpec((L, SEG_Q), lambda c: (idx(c), 0)),
                  pl.BlockSpec((L, SEG_K), lambda c: (idx(c), 0)),
                  pl.BlockSpec((L, SEG_V), lambda c: (idx(c), 0)),
                  pl.BlockSpec((L, LANES), lambda c: (idx(c), 0)),
                  pl.BlockSpec((16, L), lambda c: (0, idx(c))),
                  pl.BlockSpec((1, LANES), lambda c: (0, 0)),
                  pl.BlockSpec((16, 1), lambda c: (0, 0))],
        out_specs=pl.BlockSpec((L, H_M * MV), lambda c: (idx(c), 0)),
        out_shape=jax.ShapeDtypeStruct((t, H_M * MV), F32),
        scratch_shapes=[pltpu.VMEM((H_M, MQK, 2 * MV), F32), pltpu.VMEM((H_M, 8, LANES), F32)],
        compiler_params=_cparams(1),
        name="mlstm_rev" if rev else "mlstm_fwd",
    )(mq, mk, mv, pg, pg_t, bg_c, bg_r)


def _router(logits):
    tm = logits.shape[0]
    lane = lax.broadcasted_iota(jnp.int32, (tm, LANES), 1)
    big = jnp.int32(LANES)
    neg = -jnp.inf
    is_g = lane < N_GROUPS
    lg = jnp.where(is_g, logits, neg)
    gmax = jnp.max(lg, axis=-1, keepdims=True)
    grp = jnp.min(jnp.where(lg == gmax, lane, big), axis=-1, keepdims=True)
    p_grp = 1.0 / jnp.sum(jnp.where(is_g, jnp.exp(lg - gmax), 0.0), axis=-1, keepdims=True)
    e_lo = N_GROUPS + grp * EPG
    in_grp = (lane >= e_lo) & (lane < e_lo + EPG)
    le = jnp.where(in_grp, logits, neg)
    l1 = jnp.max(le, axis=-1, keepdims=True)
    i1 = jnp.min(jnp.where(le == l1, lane, big), axis=-1, keepdims=True)
    le2 = jnp.where(lane == i1, neg, le)
    l2 = jnp.max(le2, axis=-1, keepdims=True)
    i2 = jnp.min(jnp.where(le2 == l2, lane, big), axis=-1, keepdims=True)
    r = jnp.exp(l2 - l1)
    w1 = p_grp / (1.0 + r)
    w2 = w1 * r
    return jnp.where(lane == i1, w1, 0.0) + jnp.where(lane == i2, w2, 0.0)


def _merge_kernel(att_ref, hf_ref, hb_ref, po_ref, pm_ref, x_ref, mod_ref, gmh_ref, l1g_ref, l1b_ref,
                  wa_ref, wm_ref, wo_ref, wr_ref, br_ref, x1_ref, hm_ref, comb_ref, *, tm):
    is_ctx = _is_ctx_rows(tm)
    y_mla = jnp.dot(att_ref[...], wa_ref[...], preferred_element_type=F32)
    hs = hf_ref[...] + hb_ref[...]
    parts = []
    for h in range(H_M):
        sl = slice(h * MV, (h + 1) * MV)
        parts.append(_ln(hs[:, sl]))
    hn = jnp.concatenate(parts, axis=1) * gmh_ref[...] * _sigmoid(po_ref[...].astype(F32))
    y_ml = jnp.dot(hn.astype(BF16), wm_ref[...], preferred_element_type=F32)
    g_a = _sigmoid(pm_ref[:, :D].astype(F32))
    g_b = _sigmoid(pm_ref[:, D:].astype(F32))
    y = jnp.dot((g_a * y_mla + g_b * y_ml).astype(BF16), wo_ref[...], preferred_element_type=F32)
    x1 = _ln(ALPHA * x_ref[...] + _mod(mod_ref, 2, is_ctx) * y) * l1g_ref[...] + l1b_ref[...]
    x1_ref[...] = x1
    hm = _ln(x1) * (1.0 + _mod(mod_ref, 4, is_ctx)) + _mod(mod_ref, 3, is_ctx)
    hm_ref[...] = hm.astype(BF16)
    h_hi = hm.astype(BF16)
    h_lo = (hm - h_hi.astype(F32)).astype(BF16)
    logits = (jnp.dot(h_hi, wr_ref[0], preferred_element_type=F32)
              + jnp.dot(h_hi, wr_ref[1], preferred_element_type=F32)
              + jnp.dot(h_lo, wr_ref[0], preferred_element_type=F32)) + br_ref[...]
    comb_ref[...] = _router(logits)


def _merge(att, hf, hb, po, pm, x_all, mod, g_mh, l1g, l1b, wa, wm, wo, wr, br, tm):
    t = x_all.shape[0]
    row = lambda n: pl.BlockSpec((tm, n), lambda i: (i, 0))
    full = lambda a: pl.BlockSpec(a.shape, lambda i: (0,) * a.ndim)
    return pl.pallas_call(
        functools.partial(_merge_kernel, tm=tm),
        grid=(t // tm,),
        in_specs=[row(H_MLA * HP), row(H_M * MV), row(H_M * MV), row(SEG_PO), row(SEG_PM), row(D),
                  full(mod), full(g_mh), full(l1g), full(l1b), full(wa), full(wm), full(wo), full(wr), full(br)],
        out_specs=[row(D), row(D), row(LANES)],
        out_shape=[jax.ShapeDtypeStruct((t, D), F32), jax.ShapeDtypeStruct((t, D), BF16),
                   jax.ShapeDtypeStruct((t, LANES), F32)],
        compiler_params=_cparams(1),
        name="merge",
    )(att, hf, hb, po, pm, x_all, mod, g_mh, l1g, l1b, wa, wm, wo, wr, br)


def _moe_kernel(hm_ref, comb_ref, x1_ref, mod_ref, l2g_ref, l2b_ref, wgu_ref, wd_ref, o_ref, acc_ref, *, tm):
    e = pl.program_id(1)

    @pl.when(e == 0)
    def _():
        acc_ref[...] = jnp.zeros_like(acc_ref)

    lane = lax.broadcasted_iota(jnp.int32, (tm, LANES), 1)
    c_e = jnp.sum(jnp.where(lane == e + N_GROUPS, comb_ref[...], 0.0), axis=-1, keepdims=True)
    au = jnp.dot(hm_ref[...], wgu_ref[...], preferred_element_type=F32)
    a = au[:, :D_EXP]
    hid = a * _sigmoid(a) * au[:, D_EXP:] * c_e
    acc_ref[...] += jnp.dot(hid.astype(BF16), wd_ref[...], preferred_element_type=F32)

    @pl.when(e == N_EXP - 1)
    def _():
        is_ctx = _is_ctx_rows(tm)
        z = ALPHA * x1_ref[...] + _mod(mod_ref, 5, is_ctx) * acc_ref[...]
        o_ref[...] = _ln(z) * l2g_ref[...] + l2b_ref[...]


def _moe(hm, comb, x1, mod, l2g, l2b, wgu, wd, tm):
    t = hm.shape[0]
    row = lambda n: pl.BlockSpec((tm, n), lambda i, e: (i, 0))
    full = lambda a: pl.BlockSpec(a.shape, lambda i, e: (0,) * a.ndim)
    return pl.pallas_call(
        functools.partial(_moe_kernel, tm=tm),
        grid=(t // tm, N_EXP),
        in_specs=[row(D), row(LANES), row(D), full(mod), full(l2g), full(l2b),
                  pl.BlockSpec((None, D, 2 * D_EXP), lambda i, e: (e, 0, 0)),
                  pl.BlockSpec((None, D_EXP, D), lambda i, e: (e, 0, 0))],
        out_specs=row(D),
        out_shape=jax.ShapeDtypeStruct((t, D), F32),
        scratch_shapes=[pltpu.VMEM((tm, D), F32)],
        compiler_params=_cparams(2),
        name="moe",
    )(hm, comb, x1, mod, l2g, l2b, wgu, wd)


def _rot_cols(w):
    a1, a2, b1, b2 = jnp.split(w, 4, axis=-1)
    return jnp.concatenate([-a2, a1, -b2, b1], axis=-1)


def _place_rope(w):
    return jnp.pad(w, ((0, 0), (QK_NOPE, HP - QK_NOPE - QK_ROPE)))


def _rope_tables(t):
    rows = (t - CTX) // GRID_W
    row, col = jnp.meshgrid(jnp.arange(rows, dtype=F32), jnp.arange(GRID_W, dtype=F32), indexing="ij")
    row, col = row.reshape(-1), col.reshape(-1)
    half = QK_ROPE // 2
    inv = ROPE_BASE ** (-jnp.arange(0, half, 2, dtype=F32) / half)
    ar, ac = row[:, None] * inv, col[:, None] * inv
    ang = jnp.concatenate([ar, ar, ac, ac], axis=-1)
    ang = jnp.concatenate([jnp.zeros((CTX, QK_ROPE), F32), ang], axis=0)
    pad = ((0, 0), (QK_NOPE, HP - QK_NOPE - QK_ROPE))
    cos = jnp.pad(jnp.cos(ang), pad) + jnp.pad(jnp.ones((t, QK_NOPE), F32), ((0, 0), (0, HP - QK_NOPE)))
    sin = jnp.pad(jnp.sin(ang), pad)
    return cos, sin, cos.T, sin.T


def _layer_weights(l, w_in, w_uq, w_uk, w_uv, w_bo_mla, w_bo_mlstm, w_out, w_rg, b_rg, w_re, b_re,
                   w_e_gate, w_e_up, w_e_down):
    wi = w_in[l]
    o = 0
    segs = []
    for n in (Q_LORA, KV_LORA, QK_ROPE, SEG_Q, SEG_K, SEG_V, SEG_PO, 4 * H_M, SEG_PM):
        segs.append(wi[:, o:o + n])
        o += n
    s_pqd, s_ckv, s_kr, s_q, s_k, s_v, s_po, s_pg, s_pm = segs
    w_in_p = jnp.concatenate(
        [s_pqd, s_ckv, _place_rope(s_kr), _place_rope(_rot_cols(s_kr)), s_q, s_k, s_v, s_po,
         jnp.pad(s_pg, ((0, 0), (0, SEG_PG - 4 * H_M))), s_pm], axis=1).astype(BF16)

    uq = w_uq[l].reshape(Q_LORA, H_MLA, QK_NOPE + QK_ROPE)
    padh = ((0, 0), (0, 0), (0, HP - QK_NOPE - QK_ROPE))
    wq = jnp.pad(uq, padh).reshape(Q_LORA, H_MLA * HP)
    wqr = jnp.pad(jnp.concatenate([jnp.zeros_like(uq[..., :QK_NOPE]), _rot_cols(uq[..., QK_NOPE:])], axis=-1),
                  padh).reshape(Q_LORA, H_MLA * HP)
    padv = ((0, 0), (0, 0), (0, HP - V_HEAD))
    wk = jnp.pad(w_uk[l].reshape(KV_LORA, H_MLA, QK_NOPE), padv).reshape(KV_LORA, H_MLA * HP)
    wv = jnp.pad(w_uv[l].reshape(KV_LORA, H_MLA, V_HEAD), padv).reshape(KV_LORA, H_MLA * HP)
    wa = jnp.pad(w_bo_mla[l].reshape(H_MLA, V_HEAD, D), ((0, 0), (0, HP - V_HEAD), (0, 0))).reshape(H_MLA * HP, D)

    wr = jnp.pad(jnp.concatenate([w_rg[l], w_re[l]], axis=1), ((0, 0), (0, LANES - N_GROUPS - N_EXP)))
    wr_hi = wr.astype(BF16)
    wr_lo = (wr - wr_hi.astype(F32)).astype(BF16)
    br = jnp.pad(jnp.concatenate([b_rg[l], b_re[l]]), (0, LANES - N_GROUPS - N_EXP)).reshape(1, LANES)
    return dict(
        w_in=w_in_p, wqt=wq.T.astype(BF16), wqrt=wqr.T.astype(BF16), wk=wk.astype(BF16), wvt=wv.T.astype(BF16),
        wa=wa.astype(BF16), wm=w_bo_mlstm[l].astype(BF16), wo=w_out[l].astype(BF16),
        wr=jnp.stack([wr_hi, wr_lo]), br=br,
        wgu=jnp.concatenate([w_e_gate[l], w_e_up[l]], axis=-1).astype(BF16), wd=w_e_down[l].astype(BF16))


def kernel(x, c, ctx, c_ctx, w_ada, b_ada, w_in, b_gates, w_uq, w_uk, w_uv, g_qn, g_kvn, g_mh, w_bo_mla,
           w_bo_mlstm, w_out, ln1_g, ln1_b, w_rg, b_rg, w_re, b_re, w_e_gate, w_e_up, w_e_down, ln2_g, ln2_b):
    assert x.shape[0] == 1 and c.shape[0] == 1 and ctx.shape[1] == CTX
    x_all = jnp.concatenate([ctx[0], x[0]], axis=0)
    t = x_all.shape[0]
    tm = ROW_TILE if t % ROW_TILE == 0 else M_CHUNK
    assert t % tm == 0 and t % ATT_TQ == 0 and t % ATT_TK == 0 and t % M_CHUNK == 0

    cc = jnp.pad(jnp.concatenate([c, c_ctx[None]], axis=0), ((0, 6), (0, 0)))
    mods = _ada(cc, w_ada, b_ada)
    tabs = _rope_tables(t)
    row2 = lambda a: a.reshape(1, -1)

    for l in range(DEPTH):
        w = _layer_weights(l, w_in, w_uq, w_uk, w_uv, w_bo_mla, w_bo_mlstm, w_out, w_rg, b_rg, w_re, b_re,
                           w_e_gate, w_e_up, w_e_down)
        mod = mods[l]
        pqd, ckv, kr, mq, mk, mv, po, pg, pm = _inproj(x_all, mod, w["w_in"], tm)
        qt, kk, vt = _mla_prep(pqd, ckv, kr, tabs, row2(g_qn[l]), row2(g_kvn[l]),
                               w["wqt"], w["wqrt"], w["wk"], w["wvt"], tm)
        att = _attention(qt, kk, vt, ATT_TQ, ATT_TK, ATT_CB)
        pg_t = pg[:, :16].T
        bg_c = jnp.pad(b_gates[l], (0, LANES - 16)).reshape(1, LANES)
        bg_r = b_gates[l].reshape(16, 1)
        hf = _mlstm(mq, mk, mv, pg, pg_t, bg_c, bg_r, False, M_CHUNK)
        hb = _mlstm(mq, mk, mv, pg, pg_t, bg_c, bg_r, True, M_CHUNK)
        x1, hm, comb = _merge(att, hf, hb, po, pm, x_all, mod, row2(g_mh[l]), row2(ln1_g[l]), row2(ln1_b[l]),
                              w["wa"], w["wm"], w["wo"], w["wr"], w["br"], tm)
        x_all = _moe(hm, comb, x1, mod, row2(ln2_g[l]), row2(ln2_b[l]), w["wgu"], w["wd"], tm)
    return x_all[CTX:][None]
```

```python
import functools
import math

import jax
import jax.numpy as jnp
from jax import lax
from jax.experimental import pallas as pl
from jax.experimental.pallas import tpu as pltpu

F32 = jnp.float32
BF16 = jnp.bfloat16

D = 1024
DEPTH = 4
GRID_W = 64
CTX = 256
H_MLA = 8
QK_NOPE = 64
QK_ROPE = 32
V_HEAD = 64
Q_LORA = 384
KV_LORA = 256
ROPE_BASE = 10000.0
MLA_SCALE = (QK_NOPE + QK_ROPE) ** -0.5
H_M = 4
MQK = 64
MV = 128
N_GROUPS = 4
EPG = 8
N_EXP = N_GROUPS * EPG
D_EXP = 256
ALPHA = (2 * DEPTH) ** 0.25
LN_EPS = 1e-6

LANES = 128
VMEM_LIMIT = 56 * 1024 * 1024

HP = LANES
ONES_ROW = V_HEAD
BF16_ROWS = 16
VP = (V_HEAD + 1 + BF16_ROWS - 1) // BF16_ROWS * BF16_ROWS
QSCALE = MLA_SCALE * math.log2(math.e)

SEG_PQD = Q_LORA
SEG_CKV = KV_LORA
SEG_KR = 2 * HP
SEG_Q = H_M * MQK
SEG_K = H_M * MQK
SEG_V = H_M * MV
SEG_PO = H_M * MV
SEG_PG = LANES
SEG_PM = 2 * D
IN_SEGS = (SEG_PQD, SEG_CKV, SEG_KR, SEG_Q, SEG_K, SEG_V, SEG_PO, SEG_PG, SEG_PM)
IN_DTYPES = (F32, F32, F32, BF16, BF16, BF16, BF16, F32, BF16)
NP_IN = sum(IN_SEGS)

ROW_TILE = 640
ATT_TQ = 3328
ATT_CB = 256
ATT_TK = 1280
ATT_DEPTH = 2
M_CHUNK = 256

_NT = (((1,), (1,)), ((), ()))
_TN = (((0,), (0,)), ((), ()))


def _cparams(n_grid):
    return pltpu.CompilerParams(dimension_semantics=("arbitrary",) * n_grid, vmem_limit_bytes=VMEM_LIMIT)


def _ln(x):
    mu = jnp.mean(x, axis=-1, keepdims=True)
    xc = x - mu
    var = jnp.mean(xc * xc, axis=-1, keepdims=True)
    return xc * lax.rsqrt(var + LN_EPS)


def _rms(x, g):
    return x * lax.rsqrt(jnp.mean(x * x, axis=-1, keepdims=True) + LN_EPS) * g


def _sigmoid(x):
    return 1.0 / (1.0 + jnp.exp(-x))


def _is_ctx_rows(tm):
    rows = pl.program_id(0) * tm + lax.broadcasted_iota(jnp.int32, (tm, 1), 0)
    return rows < CTX


def _mod(mod_ref, k, is_ctx):
    lat = mod_ref[0:1, k * D:(k + 1) * D]
    ctx = mod_ref[1:2, k * D:(k + 1) * D]
    return jnp.where(is_ctx, ctx, lat)


def _ada_kernel(c_ref, w_ref, b_ref, o_ref):
    c = c_ref[...]
    s = c * _sigmoid(c)
    o_ref[...] = jnp.dot(s, w_ref[...], preferred_element_type=F32,
                         precision=lax.Precision.HIGHEST) + b_ref[...]


def _ada(cc, w_ada, b_ada):
    tn = 1536
    n = 6 * D
    return pl.pallas_call(
        _ada_kernel,
        grid=(DEPTH, n // tn),
        in_specs=[pl.BlockSpec((8, D), lambda l, j: (0, 0)),
                  pl.BlockSpec((None, D, tn), lambda l, j: (l, 0, j)),
                  pl.BlockSpec((None, 1, tn), lambda l, j: (l, 0, j))],
        out_specs=pl.BlockSpec((None, 8, tn), lambda l, j: (l, 0, j)),
        out_shape=jax.ShapeDtypeStruct((DEPTH, 8, n), F32),
        compiler_params=_cparams(2),
        name="ada",
    )(cc, w_ada, b_ada.reshape(DEPTH, 1, n))


def _inproj_kernel(x_ref, mod_ref, w_ref, *o_refs, tm):
    is_ctx = _is_ctx_rows(tm)
    h = _ln(x_ref[...]) * (1.0 + _mod(mod_ref, 1, is_ctx)) + _mod(mod_ref, 0, is_ctx)
    hb = h.astype(BF16)
    off = 0
    for o in o_refs:
        n = o.shape[-1]
        o[...] = jnp.dot(hb, w_ref[:, off:off + n], preferred_element_type=F32).astype(o.dtype)
        off += n


def _inproj(x_all, mod, w_in_p, tm):
    t = x_all.shape[0]
    return pl.pallas_call(
        functools.partial(_inproj_kernel, tm=tm),
        grid=(t // tm,),
        in_specs=[pl.BlockSpec((tm, D), lambda i: (i, 0)),
                  pl.BlockSpec((8, 6 * D), lambda i: (0, 0)),
                  pl.BlockSpec((D, NP_IN), lambda i: (0, 0))],
        out_specs=[pl.BlockSpec((tm, n), lambda i: (i, 0)) for n in IN_SEGS],
        out_shape=[jax.ShapeDtypeStruct((t, n), dt) for n, dt in zip(IN_SEGS, IN_DTYPES)],
        compiler_params=_cparams(1),
        name="inproj",
    )(x_all, mod, w_in_p)


def _mla_prep_kernel(pqd_ref, ckv_ref, kr_ref, cos_ref, sin_ref, cost_ref, sint_ref, gq_ref, gk_ref,
                     wqt_ref, wqrt_ref, wk_ref, wvt_ref, qt_ref, k_ref, vt_ref, *, tm):
    qn = _rms(pqd_ref[...], gq_ref[...]).astype(BF16)
    qa = lax.dot_general(wqt_ref[...], qn, _NT, preferred_element_type=F32)
    qb = lax.dot_general(wqrt_ref[...], qn, _NT, preferred_element_type=F32)
    cost = cost_ref[...]
    sint = sint_ref[...]
    ckv = _rms(ckv_ref[...], gk_ref[...]).astype(BF16)
    kn = jnp.dot(ckv, wk_ref[...], preferred_element_type=F32)
    kr = kr_ref[:, :HP] * cos_ref[...] + kr_ref[:, HP:] * sin_ref[...]
    vt = lax.dot_general(wvt_ref[...], ckv, _NT, preferred_element_type=F32)
    is_ones_row = lax.broadcasted_iota(jnp.int32, (VP, tm), 0) == ONES_ROW
    for h in range(H_MLA):
        sl = slice(h * HP, (h + 1) * HP)
        qt_ref[h] = ((qa[sl] * cost + qb[sl] * sint) * QSCALE).astype(BF16)
        k_ref[h] = (kn[:, sl] + kr).astype(BF16)
        vt_ref[h] = jnp.where(is_ones_row, 1.0, vt[h * VP:(h + 1) * VP]).astype(BF16)


def _mla_prep(pqd, ckv, kr, tabs, g_qn, g_kvn, wqt, wqrt, wk, wvt, tm):
    t = pqd.shape[0]
    cos, sin, cost, sint = tabs
    row = lambda n: pl.BlockSpec((tm, n), lambda i: (i, 0))
    col = lambda n: pl.BlockSpec((n, tm), lambda i: (0, i))
    full = lambda a: pl.BlockSpec(a.shape, lambda i: (0,) * a.ndim)
    hd = H_MLA * HP
    return pl.pallas_call(
        functools.partial(_mla_prep_kernel, tm=tm),
        grid=(t // tm,),
        in_specs=[row(Q_LORA), row(KV_LORA), row(SEG_KR), row(HP), row(HP), col(HP), col(HP),
                  full(g_qn), full(g_kvn), full(wqt), full(wqrt), full(wk), full(wvt)],
        out_specs=[pl.BlockSpec((H_MLA, HP, tm), lambda i: (0, 0, i)),
                   pl.BlockSpec((H_MLA, tm, HP), lambda i: (0, i, 0)),
                   pl.BlockSpec((H_MLA, VP, tm), lambda i: (0, 0, i))],
        out_shape=[jax.ShapeDtypeStruct((H_MLA, HP, t), BF16),
                   jax.ShapeDtypeStruct((H_MLA, t, HP), BF16),
                   jax.ShapeDtypeStruct((H_MLA, VP, t), BF16)],
        compiler_params=_cparams(1),
        name="mla_prep",
    )(pqd, ckv, kr, cos, sin, cost, sint, g_qn, g_kvn, wqt, wqrt, wk, wvt)


def _scores(k, qt, m):
    s = jnp.dot(k, qt, preferred_element_type=F32)
    return s, jnp.maximum(m, jnp.max(s, axis=0, keepdims=True))


def _accumulate(s, vt, m, m_new, acc):
    p = jnp.exp2(s - m_new).astype(BF16)
    return jnp.exp2(m - m_new) * acc + jnp.dot(vt, p, preferred_element_type=F32)


def _normalized_rows(acc):
    o = acc / acc[ONES_ROW:ONES_ROW + 1, :]
    return jnp.concatenate([o, jnp.zeros((HP - VP, o.shape[1]), F32)], axis=0).T


def _attn_kernel(qt_ref, k_ref, vt_ref, o_ref, m_ref, acc_ref, s_ref, mp_ref, *, tq, tk, t, cb, depth):
    ncb = tq // cb
    n_kv = t // tk
    m_ref[...] = jnp.full(m_ref.shape, -1e30, F32)
    acc_ref[...] = jnp.zeros(acc_ref.shape, F32)
    s_ref[...] = jnp.full(s_ref.shape, -jnp.inf, F32)
    mp_ref[...] = jnp.full(mp_ref.shape, -1e30, F32)

    def carried(vt_prev):
        return [(ncb - depth + d, s_ref[d], mp_ref[d, 0], mp_ref[d, 1], vt_prev) for d in range(depth)]

    def drain(entry):
        pc, ps, pm_old, pm_new, pvt = entry
        acc_ref[pc] = _accumulate(ps, pvt, pm_old, pm_new, acc_ref[pc])

    def step(j, carry):
        off = pl.multiple_of(j * tk, tk)
        off_prev = pl.multiple_of(jnp.maximum(j - 1, 0) * tk, tk)
        k = k_ref[pl.ds(off, tk), :]
        vt = vt_ref[:, pl.ds(off, tk)]
        pending = carried(vt_ref[:, pl.ds(off_prev, tk)])
        for c in range(ncb):
            m_old = m_ref[c]
            s, m_new = _scores(k, qt_ref[:, c * cb:(c + 1) * cb], m_old)
            m_ref[c] = m_new
            drain(pending.pop(0))
            pending.append((c, s, m_old, m_new, vt))
        for d, (_, ps, pm_old, pm_new, _) in enumerate(pending):
            s_ref[d] = ps
            mp_ref[d, 0] = pm_old
            mp_ref[d, 1] = pm_new
        return carry

    lax.fori_loop(0, n_kv, step, 0)
    for entry in carried(vt_ref[:, (n_kv - 1) * tk:]):
        drain(entry)
    for c in range(ncb):
        o_ref[c * cb:(c + 1) * cb, :] = _normalized_rows(acc_ref[c]).astype(o_ref.dtype)

    @pl.when(pl.program_id(1) == 0)
    def _():
        m0 = jnp.full((1, CTX), -1e30, F32)
        s, m_new = _scores(k_ref[0:CTX, :], qt_ref[:, 0:CTX], m0)
        acc_c = _accumulate(s, vt_ref[:, 0:CTX], m0, m_new, jnp.zeros((VP, CTX), F32))
        o_ref[0:CTX, :] = _normalized_rows(acc_c).astype(o_ref.dtype)


def _attention(qt, k, vt, tq, tk, cb, depth):
    t = k.shape[1]
    assert depth <= tq // cb
    return pl.pallas_call(
        functools.partial(_attn_kernel, tq=tq, tk=tk, t=t, cb=cb, depth=depth),
        grid=(H_MLA, t // tq),
        in_specs=[pl.BlockSpec((None, HP, tq), lambda h, i: (h, 0, i)),
                  pl.BlockSpec((None, t, HP), lambda h, i: (h, 0, 0)),
                  pl.BlockSpec((None, VP, t), lambda h, i: (h, 0, 0))],
        out_specs=pl.BlockSpec((tq, HP), lambda h, i: (i, h)),
        out_shape=jax.ShapeDtypeStruct((t, H_MLA * HP), BF16),
        scratch_shapes=[pltpu.VMEM((tq // cb, 1, cb), F32), pltpu.VMEM((tq // cb, VP, cb), F32),
                        pltpu.VMEM((depth, tk, cb), F32), pltpu.VMEM((depth, 2, 1, cb), F32)],
        compiler_params=_cparams(2),
        name="attention",
    )(qt, k, vt)


def _split3(x):
    hi = x.astype(BF16)
    r = x - hi.astype(F32)
    mid = r.astype(BF16)
    lo = (r - mid.astype(F32)).astype(BF16)
    return hi, mid, lo


def _log_sigmoid(x):
    return jnp.minimum(x, 0.0) - jnp.log(1.0 + jnp.exp(-jnp.abs(x)))


def _mlstm_kernel(q_ref, k_ref, v_ref, gc_ref, gr_ref, bc_ref, br_ref, o_ref, c_scr, m_scr, *, L, rev):
    @pl.when(pl.program_id(0) == 0)
    def _():
        c_scr[...] = jnp.zeros_like(c_scr)
        m_scr[...] = jnp.zeros_like(m_scr)

    row = lax.broadcasted_iota(jnp.int32, (L, L), 0)
    col = lax.broadcasted_iota(jnp.int32, (L, L), 1)
    mask = (row <= col) if rev else (row >= col)
    maskt = (row >= col) if rev else (row <= col)
    mask_b = mask.astype(BF16)
    maskt_b = maskt.astype(BF16)

    gc = gc_ref[...] + bc_ref[...]
    gr = gr_ref[...] + br_ref[...]
    lfc = _log_sigmoid(gc)
    lfr = _log_sigmoid(gr)
    b_col = sum(jnp.dot(mask_b, p, preferred_element_type=F32) for p in _split3(lfc))
    b_row = sum(jnp.dot(p, maskt_b, preferred_element_type=F32) for p in _split3(lfr))
    ones_blk = (lax.broadcasted_iota(jnp.int32, (L, LANES), 1) == 0).astype(BF16)

    for h in range(H_M):
        ci = 8 * rev + h
        cf = ci + 4
        b_c = b_col[:, cf:cf + 1]
        i_c = gc[:, ci:ci + 1]
        b_r = b_row[cf:cf + 1, :]
        i_r = gr[ci:ci + 1, :]
        bl = jnp.sum(lfc[:, cf:cf + 1], axis=0, keepdims=True)
        m = m_scr[h][0:1, 0:1]
        c_aug = c_scr[h]

        dlog = jnp.where(mask, b_c - (b_r - i_r), -jnp.inf)
        inter = b_c + m
        mj = jnp.maximum(inter, jnp.max(dlog, axis=-1, keepdims=True))
        w = jnp.exp(dlog - mj)
        sc = jnp.exp(inter - mj)

        q = q_ref[:, h * MQK:(h + 1) * MQK] * (MQK ** -0.5)
        k = k_ref[:, h * MQK:(h + 1) * MQK]
        v_aug = jnp.concatenate([v_ref[:, h * MV:(h + 1) * MV], ones_blk], axis=1)
        qk = lax.dot_general(q, k, _NT, preferred_element_type=F32) * w
        num = (jnp.dot(qk.astype(BF16), v_aug, preferred_element_type=F32)
               + sc * jnp.dot(q, c_aug.astype(BF16), preferred_element_type=F32))
        den = jnp.maximum(jnp.abs(num[:, MV:MV + 1]), jnp.exp(-mj))
        o_ref[:, h * MV:(h + 1) * MV] = num[:, :MV] / den

        wlog = bl - b_c + i_c
        m_new = jnp.maximum(bl + m, jnp.max(wlog, axis=0, keepdims=True))
        ws = jnp.exp(wlog - m_new)
        sd = jnp.exp(bl + m - m_new)
        wv = (ws * v_aug.astype(F32)).astype(BF16)
        c_scr[h] = sd * c_aug + lax.dot_general(k, wv, _TN, preferred_element_type=F32)
        m_scr[h] = jnp.broadcast_to(m_new, m_scr.shape[1:])


def _mlstm(mq, mk, mv, pg, pg_t, bg_c, bg_r, rev, L):
    t = mq.shape[0]
    nc = t // L
    if rev:
        idx = lambda c: jnp.where(c == 0, 0, nc - c)
    else:
        idx = lambda c: c
    return pl.pallas_call(
        functools.partial(_mlstm_kernel, L=L, rev=int(rev)),
        grid=(nc,),
        in_specs=[pl.BlockSpec((L, SEG_Q), lambda c: (idx(c), 0)),
                  pl.BlockSpec((L, SEG_K), lambda c: (idx(c), 0)),
                  pl.BlockSpec((L, SEG_V), lambda c: (idx(c), 0)),
                  pl.BlockSpec((L, LANES), lambda c: (idx(c), 0)),
                  pl.BlockSpec((16, L), lambda c: (0, idx(c))),
                  pl.BlockSpec((1, LANES), lambda c: (0, 0)),
                  pl.BlockSpec((16, 1), lambda c: (0, 0))],
        out_specs=pl.BlockSpec((L, H_M * MV), lambda c: (idx(c), 0)),
        out_shape=jax.ShapeDtypeStruct((t, H_M * MV), F32),
        scratch_shapes=[pltpu.VMEM((H_M, MQK, 2 * MV), F32), pltpu.VMEM((H_M, 8, LANES), F32)],
        compiler_params=_cparams(1),
        name="mlstm_rev" if rev else "mlstm_fwd",
    )(mq, mk, mv, pg, pg_t, bg_c, bg_r)


def _router(logits):
    tm = logits.shape[0]
    lane = lax.broadcasted_iota(jnp.int32, (tm, LANES), 1)
    big = jnp.int32(LANES)
    neg = -jnp.inf
    is_g = lane < N_GROUPS
    lg = jnp.where(is_g, logits, neg)
    gmax = jnp.max(lg, axis=-1, keepdims=True)
    grp = jnp.min(jnp.where(lg == gmax, lane, big), axis=-1, keepdims=True)
    p_grp = 1.0 / jnp.sum(jnp.where(is_g, jnp.exp(lg - gmax), 0.0), axis=-1, keepdims=True)
    e_lo = N_GROUPS + grp * EPG
    in_grp = (lane >= e_lo) & (lane < e_lo + EPG)
    le = jnp.where(in_grp, logits, neg)
    l1 = jnp.max(le, axis=-1, keepdims=True)
    i1 = jnp.min(jnp.where(le == l1, lane, big), axis=-1, keepdims=True)
    le2 = jnp.where(lane == i1, neg, le)
    l2 = jnp.max(le2, axis=-1, keepdims=True)
    i2 = jnp.min(jnp.where(le2 == l2, lane, big), axis=-1, keepdims=True)
    r = jnp.exp(l2 - l1)
    w1 = p_grp / (1.0 + r)
    w2 = w1 * r
    return jnp.where(lane == i1, w1, 0.0) + jnp.where(lane == i2, w2, 0.0)


def _merge_kernel(att_ref, hf_ref, hb_ref, po_ref, pm_ref, x_ref, mod_ref, gmh_ref, l1g_ref, l1b_ref,
                  wa_ref, wm_ref, wo_ref, wr_ref, br_ref, x1_ref, hm_ref, comb_ref, *, tm):
    is_ctx = _is_ctx_rows(tm)
    y_mla = jnp.dot(att_ref[...], wa_ref[...], preferred_element_type=F32)
    hs = hf_ref[...] + hb_ref[...]
    parts = []
    for h in range(H_M):
        sl = slice(h * MV, (h + 1) * MV)
        parts.append(_ln(hs[:, sl]))
    hn = jnp.concatenate(parts, axis=1) * gmh_ref[...] * _sigmoid(po_ref[...].astype(F32))
    y_ml = jnp.dot(hn.astype(BF16), wm_ref[...], preferred_element_type=F32)
    g_a = _sigmoid(pm_ref[:, :D].astype(F32))
    g_b = _sigmoid(pm_ref[:, D:].astype(F32))
    y = jnp.dot((g_a * y_mla + g_b * y_ml).astype(BF16), wo_ref[...], preferred_element_type=F32)
    x1 = _ln(ALPHA * x_ref[...] + _mod(mod_ref, 2, is_ctx) * y) * l1g_ref[...] + l1b_ref[...]
    x1_ref[...] = x1
    hm = _ln(x1) * (1.0 + _mod(mod_ref, 4, is_ctx)) + _mod(mod_ref, 3, is_ctx)
    hm_ref[...] = hm.astype(BF16)
    h_hi = hm.astype(BF16)
    h_lo = (hm - h_hi.astype(F32)).astype(BF16)
    logits = (jnp.dot(h_hi, wr_ref[0], preferred_element_type=F32)
              + jnp.dot(h_hi, wr_ref[1], preferred_element_type=F32)
              + jnp.dot(h_lo, wr_ref[0], preferred_element_type=F32)) + br_ref[...]
    comb_ref[...] = _router(logits)


def _merge(att, hf, hb, po, pm, x_all, mod, g_mh, l1g, l1b, wa, wm, wo, wr, br, tm):
    t = x_all.shape[0]
    row = lambda n: pl.BlockSpec((tm, n), lambda i: (i, 0))
    full = lambda a: pl.BlockSpec(a.shape, lambda i: (0,) * a.ndim)
    return pl.pallas_call(
        functools.partial(_merge_kernel, tm=tm),
        grid=(t // tm,),
        in_specs=[row(H_MLA * HP), row(H_M * MV), row(H_M * MV), row(SEG_PO), row(SEG_PM), row(D),
                  full(mod), full(g_mh), full(l1g), full(l1b), full(wa), full(wm), full(wo), full(wr), full(br)],
        out_specs=[row(D), row(D), row(LANES)],
        out_shape=[jax.ShapeDtypeStruct((t, D), F32), jax.ShapeDtypeStruct((t, D), BF16),
                   jax.ShapeDtypeStruct((t, LANES), F32)],
        compiler_params=_cparams(1),
        name="merge",
    )(att, hf, hb, po, pm, x_all, mod, g_mh, l1g, l1b, wa, wm, wo, wr, br)


def _moe_kernel(hm_ref, comb_ref, x1_ref, mod_ref, l2g_ref, l2b_ref, wgu_ref, wd_ref, o_ref, acc_ref, *, tm):
    e = pl.program_id(1)

    @pl.when(e == 0)
    def _():
        acc_ref[...] = jnp.zeros_like(acc_ref)

    lane = lax.broadcasted_iota(jnp.int32, (tm, LANES), 1)
    c_e = jnp.sum(jnp.where(lane == e + N_GROUPS, comb_ref[...], 0.0), axis=-1, keepdims=True)
    au = jnp.dot(hm_ref[...], wgu_ref[...], preferred_element_type=F32)
    a = au[:, :D_EXP]
    hid = a * _sigmoid(a) * au[:, D_EXP:] * c_e
    acc_ref[...] += jnp.dot(hid.astype(BF16), wd_ref[...], preferred_element_type=F32)

    @pl.when(e == N_EXP - 1)
    def _():
        is_ctx = _is_ctx_rows(tm)
        z = ALPHA * x1_ref[...] + _mod(mod_ref, 5, is_ctx) * acc_ref[...]
        o_ref[...] = _ln(z) * l2g_ref[...] + l2b_ref[...]


def _moe(hm, comb, x1, mod, l2g, l2b, wgu, wd, tm):
    t = hm.shape[0]
    row = lambda n: pl.BlockSpec((tm, n), lambda i, e: (i, 0))
    full = lambda a: pl.BlockSpec(a.shape, lambda i, e: (0,) * a.ndim)
    return pl.pallas_call(
        functools.partial(_moe_kernel, tm=tm),
        grid=(t // tm, N_EXP),
        in_specs=[row(D), row(LANES), row(D), full(mod), full(l2g), full(l2b),
                  pl.BlockSpec((None, D, 2 * D_EXP), lambda i, e: (e, 0, 0)),
                  pl.BlockSpec((None, D_EXP, D), lambda i, e: (e, 0, 0))],
        out_specs=row(D),
        out_shape=jax.ShapeDtypeStruct((t, D), F32),
        scratch_shapes=[pltpu.VMEM((tm, D), F32)],
        compiler_params=_cparams(2),
        name="moe",
    )(hm, comb, x1, mod, l2g, l2b, wgu, wd)


def _rot_cols(w):
    a1, a2, b1, b2 = jnp.split(w, 4, axis=-1)
    return jnp.concatenate([-a2, a1, -b2, b1], axis=-1)


def _place_rope(w):
    return jnp.pad(w, ((0, 0), (QK_NOPE, HP - QK_NOPE - QK_ROPE)))


def _rope_tables(t):
    rows = (t - CTX) // GRID_W
    row, col = jnp.meshgrid(jnp.arange(rows, dtype=F32), jnp.arange(GRID_W, dtype=F32), indexing="ij")
    row, col = row.reshape(-1), col.reshape(-1)
    half = QK_ROPE // 2
    inv = ROPE_BASE ** (-jnp.arange(0, half, 2, dtype=F32) / half)
    ar, ac = row[:, None] * inv, col[:, None] * inv
    ang = jnp.concatenate([ar, ar, ac, ac], axis=-1)
    ang = jnp.concatenate([jnp.zeros((CTX, QK_ROPE), F32), ang], axis=0)
    pad = ((0, 0), (QK_NOPE, HP - QK_NOPE - QK_ROPE))
    cos = jnp.pad(jnp.cos(ang), pad) + jnp.pad(jnp.ones((t, QK_NOPE), F32), ((0, 0), (0, HP - QK_NOPE)))
    sin = jnp.pad(jnp.sin(ang), pad)
    return cos, sin, cos.T, sin.T


def _layer_weights(l, w_in, w_uq, w_uk, w_uv, w_bo_mla, w_bo_mlstm, w_out, w_rg, b_rg, w_re, b_re,
                   w_e_gate, w_e_up, w_e_down):
    wi = w_in[l]
    o = 0
    segs = []
    for n in (Q_LORA, KV_LORA, QK_ROPE, SEG_Q, SEG_K, SEG_V, SEG_PO, 4 * H_M, SEG_PM):
        segs.append(wi[:, o:o + n])
        o += n
    s_pqd, s_ckv, s_kr, s_q, s_k, s_v, s_po, s_pg, s_pm = segs
    w_in_p = jnp.concatenate(
        [s_pqd, s_ckv, _place_rope(s_kr), _place_rope(_rot_cols(s_kr)), s_q, s_k, s_v, s_po,
         jnp.pad(s_pg, ((0, 0), (0, SEG_PG - 4 * H_M))), s_pm], axis=1).astype(BF16)

    uq = w_uq[l].reshape(Q_LORA, H_MLA, QK_NOPE + QK_ROPE)
    padh = ((0, 0), (0, 0), (0, HP - QK_NOPE - QK_ROPE))
    wq = jnp.pad(uq, padh).reshape(Q_LORA, H_MLA * HP)
    wqr = jnp.pad(jnp.concatenate([jnp.zeros_like(uq[..., :QK_NOPE]), _rot_cols(uq[..., QK_NOPE:])], axis=-1),
                  padh).reshape(Q_LORA, H_MLA * HP)
    padv = ((0, 0), (0, 0), (0, HP - V_HEAD))
    wk = jnp.pad(w_uk[l].reshape(KV_LORA, H_MLA, QK_NOPE), padv).reshape(KV_LORA, H_MLA * HP)
    wv = jnp.pad(w_uv[l].reshape(KV_LORA, H_MLA, V_HEAD), ((0, 0), (0, 0), (0, VP - V_HEAD))).reshape(KV_LORA, H_MLA * VP)
    wa = jnp.pad(w_bo_mla[l].reshape(H_MLA, V_HEAD, D), ((0, 0), (0, HP - V_HEAD), (0, 0))).reshape(H_MLA * HP, D)

    wr = jnp.pad(jnp.concatenate([w_rg[l], w_re[l]], axis=1), ((0, 0), (0, LANES - N_GROUPS - N_EXP)))
    wr_hi = wr.astype(BF16)
    wr_lo = (wr - wr_hi.astype(F32)).astype(BF16)
    br = jnp.pad(jnp.concatenate([b_rg[l], b_re[l]]), (0, LANES - N_GROUPS - N_EXP)).reshape(1, LANES)
    return dict(
        w_in=w_in_p, wqt=wq.T.astype(BF16), wqrt=wqr.T.astype(BF16), wk=wk.astype(BF16), wvt=wv.T.astype(BF16),
        wa=wa.astype(BF16), wm=w_bo_mlstm[l].astype(BF16), wo=w_out[l].astype(BF16),
        wr=jnp.stack([wr_hi, wr_lo]), br=br,
        wgu=jnp.concatenate([w_e_gate[l], w_e_up[l]], axis=-1).astype(BF16), wd=w_e_down[l].astype(BF16))


def kernel(x, c, ctx, c_ctx, w_ada, b_ada, w_in, b_gates, w_uq, w_uk, w_uv, g_qn, g_kvn, g_mh, w_bo_mla,
           w_bo_mlstm, w_out, ln1_g, ln1_b, w_rg, b_rg, w_re, b_re, w_e_gate, w_e_up, w_e_down, ln2_g, ln2_b):
    assert x.shape[0] == 1 and c.shape[0] == 1 and ctx.shape[1] == CTX
    x_all = jnp.concatenate([ctx[0], x[0]], axis=0)
    t = x_all.shape[0]
    tm = ROW_TILE if t % ROW_TILE == 0 else M_CHUNK
    tk = ATT_TK if t % ATT_TK == 0 else M_CHUNK
    assert t % tm == 0 and t % ATT_TQ == 0 and t % tk == 0 and t % M_CHUNK == 0

    cc = jnp.pad(jnp.concatenate([c, c_ctx[None]], axis=0), ((0, 6), (0, 0)))
    mods = _ada(cc, w_ada, b_ada)
    tabs = _rope_tables(t)
    row2 = lambda a: a.reshape(1, -1)

    for l in range(DEPTH):
        w = _layer_weights(l, w_in, w_uq, w_uk, w_uv, w_bo_mla, w_bo_mlstm, w_out, w_rg, b_rg, w_re, b_re,
                           w_e_gate, w_e_up, w_e_down)
        mod = mods[l]
        pqd, ckv, kr, mq, mk, mv, po, pg, pm = _inproj(x_all, mod, w["w_in"], tm)
        qt, kk, vt = _mla_prep(pqd, ckv, kr, tabs, row2(g_qn[l]), row2(g_kvn[l]),
                               w["wqt"], w["wqrt"], w["wk"], w["wvt"], tm)
        att = _attention(qt, kk, vt, ATT_TQ, tk, ATT_CB, ATT_DEPTH)
        pg_t = pg[:, :16].T
        bg_c = jnp.pad(b_gates[l], (0, LANES - 16)).reshape(1, LANES)
        bg_r = b_gates[l].reshape(16, 1)
        hf = _mlstm(mq, mk, mv, pg, pg_t, bg_c, bg_r, False, M_CHUNK)
        hb = _mlstm(mq, mk, mv, pg, pg_t, bg_c, bg_r, True, M_CHUNK)
        x1, hm, comb = _merge(att, hf, hb, po, pm, x_all, mod, row2(g_mh[l]), row2(ln1_g[l]), row2(ln1_b[l]),
                              w["wa"], w["wm"], w["wo"], w["wr"], w["br"], tm)
        x_all = _moe(hm, comb, x1, mod, row2(ln2_g[l]), row2(ln2_b[l]), w["wgu"], w["wd"], tm)
    return x_all[CTX:][None]
```

```python
import functools
import math

import jax
import jax.numpy as jnp
from jax import lax
from jax.experimental import pallas as pl
from jax.experimental.pallas import tpu as pltpu

F32 = jnp.float32
BF16 = jnp.bfloat16

D = 1024
DEPTH = 4
GRID_W = 64
CTX = 256
H_MLA = 8
QK_NOPE = 64
QK_ROPE = 32
V_HEAD = 64
Q_LORA = 384
KV_LORA = 256
ROPE_BASE = 10000.0
MLA_SCALE = (QK_NOPE + QK_ROPE) ** -0.5
H_M = 4
MQK = 64
MV = 128
N_GROUPS = 4
EPG = 8
N_EXP = N_GROUPS * EPG
D_EXP = 256
ALPHA = (2 * DEPTH) ** 0.25
LN_EPS = 1e-6

LANES = 128
VMEM_LIMIT = 56 * 1024 * 1024

HP = LANES
ONES_ROW = V_HEAD
BF16_ROWS = 16
VP = (V_HEAD + 1 + BF16_ROWS - 1) // BF16_ROWS * BF16_ROWS
QSCALE = MLA_SCALE * math.log2(math.e)

SEG_PQD = Q_LORA
SEG_CKV = KV_LORA
SEG_KR = 2 * HP
SEG_Q = H_M * MQK
SEG_K = H_M * MQK
SEG_V = H_M * MV
SEG_PO = H_M * MV
SEG_PG = LANES
SEG_PM = 2 * D
IN_SEGS = (SEG_PQD, SEG_CKV, SEG_KR, SEG_Q, SEG_K, SEG_V, SEG_PO, SEG_PG, SEG_PM)
IN_DTYPES = (F32, F32, F32, BF16, BF16, BF16, BF16, F32, BF16)
NP_IN = sum(IN_SEGS)

ROW_TILE = 640
ATT_TQ = 3328
ATT_CB = 256
ATT_TK = 1280
MAX_LAG_JUMP = 64.0
ATT_DEPTH = 2
M_CHUNK = 256

_NT = (((1,), (1,)), ((), ()))
_TN = (((0,), (0,)), ((), ()))


def _cparams(n_grid):
    return pltpu.CompilerParams(dimension_semantics=("arbitrary",) * n_grid, vmem_limit_bytes=VMEM_LIMIT)


def _ln(x):
    mu = jnp.mean(x, axis=-1, keepdims=True)
    xc = x - mu
    var = jnp.mean(xc * xc, axis=-1, keepdims=True)
    return xc * lax.rsqrt(var + LN_EPS)


def _rms(x, g):
    return x * lax.rsqrt(jnp.mean(x * x, axis=-1, keepdims=True) + LN_EPS) * g


def _sigmoid(x):
    return 1.0 / (1.0 + jnp.exp(-x))


def _is_ctx_rows(tm):
    rows = pl.program_id(0) * tm + lax.broadcasted_iota(jnp.int32, (tm, 1), 0)
    return rows < CTX


def _mod(mod_ref, k, is_ctx):
    lat = mod_ref[0:1, k * D:(k + 1) * D]
    ctx = mod_ref[1:2, k * D:(k + 1) * D]
    return jnp.where(is_ctx, ctx, lat)


def _ada_kernel(c_ref, w_ref, b_ref, o_ref):
    c = c_ref[...]
    s = c * _sigmoid(c)
    o_ref[...] = jnp.dot(s, w_ref[...], preferred_element_type=F32,
                         precision=lax.Precision.HIGHEST) + b_ref[...]


def _ada(cc, w_ada, b_ada):
    tn = 1536
    n = 6 * D
    return pl.pallas_call(
        _ada_kernel,
        grid=(DEPTH, n // tn),
        in_specs=[pl.BlockSpec((8, D), lambda l, j: (0, 0)),
                  pl.BlockSpec((None, D, tn), lambda l, j: (l, 0, j)),
                  pl.BlockSpec((None, 1, tn), lambda l, j: (l, 0, j))],
        out_specs=pl.BlockSpec((None, 8, tn), lambda l, j: (l, 0, j)),
        out_shape=jax.ShapeDtypeStruct((DEPTH, 8, n), F32),
        compiler_params=_cparams(2),
        name="ada",
    )(cc, w_ada, b_ada.reshape(DEPTH, 1, n))


def _inproj_kernel(x_ref, mod_ref, w_ref, *o_refs, tm):
    is_ctx = _is_ctx_rows(tm)
    h = _ln(x_ref[...]) * (1.0 + _mod(mod_ref, 1, is_ctx)) + _mod(mod_ref, 0, is_ctx)
    hb = h.astype(BF16)
    off = 0
    for o in o_refs:
        n = o.shape[-1]
        o[...] = jnp.dot(hb, w_ref[:, off:off + n], preferred_element_type=F32).astype(o.dtype)
        off += n


def _inproj(x_all, mod, w_in_p, tm):
    t = x_all.shape[0]
    return pl.pallas_call(
        functools.partial(_inproj_kernel, tm=tm),
        grid=(t // tm,),
        in_specs=[pl.BlockSpec((tm, D), lambda i: (i, 0)),
                  pl.BlockSpec((8, 6 * D), lambda i: (0, 0)),
                  pl.BlockSpec((D, NP_IN), lambda i: (0, 0))],
        out_specs=[pl.BlockSpec((tm, n), lambda i: (i, 0)) for n in IN_SEGS],
        out_shape=[jax.ShapeDtypeStruct((t, n), dt) for n, dt in zip(IN_SEGS, IN_DTYPES)],
        compiler_params=_cparams(1),
        name="inproj",
    )(x_all, mod, w_in_p)


def _mla_prep_kernel(pqd_ref, ckv_ref, kr_ref, cos_ref, sin_ref, cost_ref, sint_ref, gq_ref, gk_ref,
                     wqt_ref, wqrt_ref, wk_ref, wvt_ref, qt_ref, k_ref, vt_ref, *, tm):
    qn = _rms(pqd_ref[...], gq_ref[...]).astype(BF16)
    qa = lax.dot_general(wqt_ref[...], qn, _NT, preferred_element_type=F32)
    qb = lax.dot_general(wqrt_ref[...], qn, _NT, preferred_element_type=F32)
    cost = cost_ref[...]
    sint = sint_ref[...]
    ckv = _rms(ckv_ref[...], gk_ref[...]).astype(BF16)
    kn = jnp.dot(ckv, wk_ref[...], preferred_element_type=F32)
    kr = kr_ref[:, :HP] * cos_ref[...] + kr_ref[:, HP:] * sin_ref[...]
    vt = lax.dot_general(wvt_ref[...], ckv, _NT, preferred_element_type=F32)
    is_ones_row = lax.broadcasted_iota(jnp.int32, (VP, tm), 0) == ONES_ROW
    for h in range(H_MLA):
        sl = slice(h * HP, (h + 1) * HP)
        qt_ref[h] = ((qa[sl] * cost + qb[sl] * sint) * QSCALE).astype(BF16)
        k_ref[h] = (kn[:, sl] + kr).astype(BF16)
        vt_ref[h] = jnp.where(is_ones_row, 1.0, vt[h * VP:(h + 1) * VP]).astype(BF16)


def _mla_prep(pqd, ckv, kr, tabs, g_qn, g_kvn, wqt, wqrt, wk, wvt, tm):
    t = pqd.shape[0]
    cos, sin, cost, sint = tabs
    row = lambda n: pl.BlockSpec((tm, n), lambda i: (i, 0))
    col = lambda n: pl.BlockSpec((n, tm), lambda i: (0, i))
    full = lambda a: pl.BlockSpec(a.shape, lambda i: (0,) * a.ndim)
    hd = H_MLA * HP
    return pl.pallas_call(
        functools.partial(_mla_prep_kernel, tm=tm),
        grid=(t // tm,),
        in_specs=[row(Q_LORA), row(KV_LORA), row(SEG_KR), row(HP), row(HP), col(HP), col(HP),
                  full(g_qn), full(g_kvn), full(wqt), full(wqrt), full(wk), full(wvt)],
        out_specs=[pl.BlockSpec((H_MLA, HP, tm), lambda i: (0, 0, i)),
                   pl.BlockSpec((H_MLA, tm, HP), lambda i: (0, i, 0)),
                   pl.BlockSpec((H_MLA, VP, tm), lambda i: (0, 0, i))],
        out_shape=[jax.ShapeDtypeStruct((H_MLA, HP, t), BF16),
                   jax.ShapeDtypeStruct((H_MLA, t, HP), BF16),
                   jax.ShapeDtypeStruct((H_MLA, VP, t), BF16)],
        compiler_params=_cparams(1),
        name="mla_prep",
    )(pqd, ckv, kr, cos, sin, cost, sint, g_qn, g_kvn, wqt, wqrt, wk, wvt)


def _scores(k, qt, m):
    s = jnp.dot(k, qt, preferred_element_type=F32)
    return s, jnp.maximum(m, jnp.max(s, axis=0, keepdims=True))


def _accumulate(s, vt, m, m_new, acc):
    p = jnp.exp2(s - m_new).astype(BF16)
    return jnp.exp2(m - m_new) * acc + jnp.dot(vt, p, preferred_element_type=F32)


def _normalized_rows(acc):
    o = acc / acc[ONES_ROW:ONES_ROW + 1, :]
    return jnp.concatenate([o, jnp.zeros((HP - VP, o.shape[1]), F32)], axis=0).T


def _lagged_pass(qt_ref, k_ref, vt_ref, m_ref, acc_ref, p_ref, sc_ref, *, ncb, n_kv, tk, cb, depth):
    k0 = k_ref[0:BF16_ROWS, :]
    for c in range(ncb):
        s0 = jnp.dot(k0, qt_ref[:, c * cb:(c + 1) * cb], preferred_element_type=F32)
        m_ref[c] = jnp.max(s0, axis=0, keepdims=True)
    acc_ref[...] = jnp.zeros(acc_ref.shape, F32)
    p_ref[...] = jnp.zeros(p_ref.shape, BF16)
    sc_ref[...] = jnp.ones(sc_ref.shape, F32)

    def carried(vt_prev):
        return [(ncb - depth + d, p_ref[d], sc_ref[d], vt_prev) for d in range(depth)]

    def drain(entry):
        pc, pp, psc, pvt = entry
        acc_ref[pc] = (acc_ref[pc] + jnp.dot(pvt, pp, preferred_element_type=F32)) * psc

    def step(j, jump):
        off = pl.multiple_of(j * tk, tk)
        off_prev = pl.multiple_of(jnp.maximum(j - 1, 0) * tk, tk)
        k = k_ref[pl.ds(off, tk), :]
        vt = vt_ref[:, pl.ds(off, tk)]
        pending = carried(vt_ref[:, pl.ds(off_prev, tk)])
        for c in range(ncb):
            m = m_ref[c]
            s = jnp.dot(k, qt_ref[:, c * cb:(c + 1) * cb], preferred_element_type=F32)
            p = jnp.exp2(s - m).astype(BF16)
            cm = jnp.max(s, axis=0, keepdims=True)
            m_new = jnp.maximum(m, cm)
            jump = jnp.maximum(jump, cm - m)
            m_ref[c] = m_new
            drain(pending.pop(0))
            pending.append((c, p, jnp.exp2(m - m_new), vt))
        for d, (_, pp, psc, _) in enumerate(pending):
            p_ref[d] = pp
            sc_ref[d] = psc
        return jump

    jump = lax.fori_loop(0, n_kv, step, jnp.zeros((1, cb), F32))
    for entry in carried(vt_ref[:, (n_kv - 1) * tk:]):
        drain(entry)
    return jump


def _exact_pass(qt_ref, k_ref, vt_ref, m_ref, acc_ref, s_ref, mp_ref, *, ncb, n_kv, tk, cb, depth):
    m_ref[...] = jnp.full(m_ref.shape, -1e30, F32)
    acc_ref[...] = jnp.zeros(acc_ref.shape, F32)
    s_ref[...] = jnp.full(s_ref.shape, -jnp.inf, F32)
    mp_ref[...] = jnp.full(mp_ref.shape, -1e30, F32)

    def carried(vt_prev):
        return [(ncb - depth + d, s_ref[d], mp_ref[d, 0], mp_ref[d, 1], vt_prev) for d in range(depth)]

    def drain(entry):
        pc, ps, pm_old, pm_new, pvt = entry
        acc_ref[pc] = _accumulate(ps, pvt, pm_old, pm_new, acc_ref[pc])

    def step(j, carry):
        off = pl.multiple_of(j * tk, tk)
        off_prev = pl.multiple_of(jnp.maximum(j - 1, 0) * tk, tk)
        k = k_ref[pl.ds(off, tk), :]
        vt = vt_ref[:, pl.ds(off, tk)]
        pending = carried(vt_ref[:, pl.ds(off_prev, tk)])
        for c in range(ncb):
            m_old = m_ref[c]
            s, m_new = _scores(k, qt_ref[:, c * cb:(c + 1) * cb], m_old)
            m_ref[c] = m_new
            drain(pending.pop(0))
            pending.append((c, s, m_old, m_new, vt))
        for d, (_, ps, pm_old, pm_new, _) in enumerate(pending):
            s_ref[d] = ps
            mp_ref[d, 0] = pm_old
            mp_ref[d, 1] = pm_new
        return carry

    lax.fori_loop(0, n_kv, step, 0)
    for entry in carried(vt_ref[:, (n_kv - 1) * tk:]):
        drain(entry)


def _attn_kernel(qt_ref, k_ref, vt_ref, o_ref, m_ref, acc_ref, s_ref, mp_ref, p_ref, sc_ref, *,
                 tq, tk, t, cb, depth):
    geom = dict(ncb=tq // cb, n_kv=t // tk, tk=tk, cb=cb, depth=depth)

    def write_out():
        for c in range(tq // cb):
            o_ref[c * cb:(c + 1) * cb, :] = _normalized_rows(acc_ref[c]).astype(o_ref.dtype)

    jump = _lagged_pass(qt_ref, k_ref, vt_ref, m_ref, acc_ref, p_ref, sc_ref, **geom)
    write_out()

    @pl.when(jnp.max(jump) > MAX_LAG_JUMP)
    def _():
        _exact_pass(qt_ref, k_ref, vt_ref, m_ref, acc_ref, s_ref, mp_ref, **geom)
        write_out()

    @pl.when(pl.program_id(1) == 0)
    def _():
        m0 = jnp.full((1, CTX), -1e30, F32)
        s, m_new = _scores(k_ref[0:CTX, :], qt_ref[:, 0:CTX], m0)
        acc_c = _accumulate(s, vt_ref[:, 0:CTX], m0, m_new, jnp.zeros((VP, CTX), F32))
        o_ref[0:CTX, :] = _normalized_rows(acc_c).astype(o_ref.dtype)


def _attention(qt, k, vt, tq, tk, cb, depth):
    t = k.shape[1]
    assert depth <= tq // cb
    return pl.pallas_call(
        functools.partial(_attn_kernel, tq=tq, tk=tk, t=t, cb=cb, depth=depth),
        grid=(H_MLA, t // tq),
        in_specs=[pl.BlockSpec((None, HP, tq), lambda h, i: (h, 0, i)),
                  pl.BlockSpec((None, t, HP), lambda h, i: (h, 0, 0)),
                  pl.BlockSpec((None, VP, t), lambda h, i: (h, 0, 0))],
        out_specs=pl.BlockSpec((tq, HP), lambda h, i: (i, h)),
        out_shape=jax.ShapeDtypeStruct((t, H_MLA * HP), BF16),
        scratch_shapes=[pltpu.VMEM((tq // cb, 1, cb), F32), pltpu.VMEM((tq // cb, VP, cb), F32),
                        pltpu.VMEM((depth, tk, cb), F32), pltpu.VMEM((depth, 2, 1, cb), F32),
                        pltpu.VMEM((depth, tk, cb), BF16), pltpu.VMEM((depth, 1, cb), F32)],
        compiler_params=_cparams(2),
        name="attention",
    )(qt, k, vt)


def _split3(x):
    hi = x.astype(BF16)
    r = x - hi.astype(F32)
    mid = r.astype(BF16)
    lo = (r - mid.astype(F32)).astype(BF16)
    return hi, mid, lo


def _log_sigmoid(x):
    return jnp.minimum(x, 0.0) - jnp.log(1.0 + jnp.exp(-jnp.abs(x)))


def _mlstm_kernel(q_ref, k_ref, v_ref, gc_ref, gr_ref, bc_ref, br_ref, o_ref, c_scr, m_scr, *, L, rev):
    @pl.when(pl.program_id(0) == 0)
    def _():
        c_scr[...] = jnp.zeros_like(c_scr)
        m_scr[...] = jnp.zeros_like(m_scr)

    row = lax.broadcasted_iota(jnp.int32, (L, L), 0)
    col = lax.broadcasted_iota(jnp.int32, (L, L), 1)
    mask = (row <= col) if rev else (row >= col)
    maskt = (row >= col) if rev else (row <= col)
    mask_b = mask.astype(BF16)
    maskt_b = maskt.astype(BF16)

    gc = gc_ref[...] + bc_ref[...]
    gr = gr_ref[...] + br_ref[...]
    lfc = _log_sigmoid(gc)
    lfr = _log_sigmoid(gr)
    b_col = sum(jnp.dot(mask_b, p, preferred_element_type=F32) for p in _split3(lfc))
    b_row = sum(jnp.dot(p, maskt_b, preferred_element_type=F32) for p in _split3(lfr))
    ones_blk = (lax.broadcasted_iota(jnp.int32, (L, LANES), 1) == 0).astype(BF16)

    for h in range(H_M):
        ci = 8 * rev + h
        cf = ci + 4
        b_c = b_col[:, cf:cf + 1]
        i_c = gc[:, ci:ci + 1]
        b_r = b_row[cf:cf + 1, :]
        i_r = gr[ci:ci + 1, :]
        bl = jnp.sum(lfc[:, cf:cf + 1], axis=0, keepdims=True)
        m = m_scr[h][0:1, 0:1]
        c_aug = c_scr[h]

        dlog = jnp.where(mask, b_c - (b_r - i_r), -jnp.inf)
        inter = b_c + m
        mj = jnp.maximum(inter, jnp.max(dlog, axis=-1, keepdims=True))
        w = jnp.exp(dlog - mj)
        sc = jnp.exp(inter - mj)

        q = q_ref[:, h * MQK:(h + 1) * MQK] * (MQK ** -0.5)
        k = k_ref[:, h * MQK:(h + 1) * MQK]
        v_aug = jnp.concatenate([v_ref[:, h * MV:(h + 1) * MV], ones_blk], axis=1)
        qk = lax.dot_general(q, k, _NT, preferred_element_type=F32) * w
        num = (jnp.dot(qk.astype(BF16), v_aug, preferred_element_type=F32)
               + sc * jnp.dot(q, c_aug.astype(BF16), preferred_element_type=F32))
        den = jnp.maximum(jnp.abs(num[:, MV:MV + 1]), jnp.exp(-mj))
        o_ref[:, h * MV:(h + 1) * MV] = num[:, :MV] / den

        wlog = bl - b_c + i_c
        m_new = jnp.maximum(bl + m, jnp.max(wlog, axis=0, keepdims=True))
        ws = jnp.exp(wlog - m_new)
        sd = jnp.exp(bl + m - m_new)
        wv = (ws * v_aug.astype(F32)).astype(BF16)
        c_scr[h] = sd * c_aug + lax.dot_general(k, wv, _TN, preferred_element_type=F32)
        m_scr[h] = jnp.broadcast_to(m_new, m_scr.shape[1:])


def _mlstm(mq, mk, mv, pg, pg_t, bg_c, bg_r, rev, L):
    t = mq.shape[0]
    nc = t // L
    if rev:
        idx = lambda c: jnp.where(c == 0, 0, nc - c)
    else:
        idx = lambda c: c
    return pl.pallas_call(
        functools.partial(_mlstm_kernel, L=L, rev=int(rev)),
        grid=(nc,),
        in_specs=[pl.BlockSpec((L, SEG_Q), lambda c: (idx(c), 0)),
                  pl.BlockSpec((L, SEG_K), lambda c: (idx(c), 0)),
                  pl.BlockSpec((L, SEG_V), lambda c: (idx(c), 0)),
                  pl.BlockSpec((L, LANES), lambda c: (idx(c), 0)),
                  pl.BlockSpec((16, L), lambda c: (0, idx(c))),
                  pl.BlockSpec((1, LANES), lambda c: (0, 0)),
                  pl.BlockSpec((16, 1), lambda c: (0, 0))],
        out_specs=pl.BlockSpec((L, H_M * MV), lambda c: (idx(c), 0)),
        out_shape=jax.ShapeDtypeStruct((t, H_M * MV), F32),
        scratch_shapes=[pltpu.VMEM((H_M, MQK, 2 * MV), F32), pltpu.VMEM((H_M, 8, LANES), F32)],
        compiler_params=_cparams(1),
        name="mlstm_rev" if rev else "mlstm_fwd",
    )(mq, mk, mv, pg, pg_t, bg_c, bg_r)


def _router(logits):
    tm = logits.shape[0]
    lane = lax.broadcasted_iota(jnp.int32, (tm, LANES), 1)
    big = jnp.int32(LANES)
    neg = -jnp.inf
    is_g = lane < N_GROUPS
    lg = jnp.where(is_g, logits, neg)
    gmax = jnp.max(lg, axis=-1, keepdims=True)
    grp = jnp.min(jnp.where(lg == gmax, lane, big), axis=-1, keepdims=True)
    p_grp = 1.0 / jnp.sum(jnp.where(is_g, jnp.exp(lg - gmax), 0.0), axis=-1, keepdims=True)
    e_lo = N_GROUPS + grp * EPG
    in_grp = (lane >= e_lo) & (lane < e_lo + EPG)
    le = jnp.where(in_grp, logits, neg)
    l1 = jnp.max(le, axis=-1, keepdims=True)
    i1 = jnp.min(jnp.where(le == l1, lane, big), axis=-1, keepdims=True)
    le2 = jnp.where(lane == i1, neg, le)
    l2 = jnp.max(le2, axis=-1, keepdims=True)
    i2 = jnp.min(jnp.where(le2 == l2, lane, big), axis=-1, keepdims=True)
    r = jnp.exp(l2 - l1)
    w1 = p_grp / (1.0 + r)
    w2 = w1 * r
    return jnp.where(lane == i1, w1, 0.0) + jnp.where(lane == i2, w2, 0.0)


def _merge_kernel(att_ref, hf_ref, hb_ref, po_ref, pm_ref, x_ref, mod_ref, gmh_ref, l1g_ref, l1b_ref,
                  wa_ref, wm_ref, wo_ref, wr_ref, br_ref, x1_ref, hm_ref, comb_ref, *, tm):
    is_ctx = _is_ctx_rows(tm)
    y_mla = jnp.dot(att_ref[...], wa_ref[...], preferred_element_type=F32)
    hs = hf_ref[...] + hb_ref[...]
    parts = []
    for h in range(H_M):
        sl = slice(h * MV, (h + 1) * MV)
        parts.append(_ln(hs[:, sl]))
    hn = jnp.concatenate(parts, axis=1) * gmh_ref[...] * _sigmoid(po_ref[...].astype(F32))
    y_ml = jnp.dot(hn.astype(BF16), wm_ref[...], preferred_element_type=F32)
    g_a = _sigmoid(pm_ref[:, :D].astype(F32))
    g_b = _sigmoid(pm_ref[:, D:].astype(F32))
    y = jnp.dot((g_a * y_mla + g_b * y_ml).astype(BF16), wo_ref[...], preferred_element_type=F32)
    x1 = _ln(ALPHA * x_ref[...] + _mod(mod_ref, 2, is_ctx) * y) * l1g_ref[...] + l1b_ref[...]
    x1_ref[...] = x1
    hm = _ln(x1) * (1.0 + _mod(mod_ref, 4, is_ctx)) + _mod(mod_ref, 3, is_ctx)
    hm_ref[...] = hm.astype(BF16)
    h_hi = hm.astype(BF16)
    h_lo = (hm - h_hi.astype(F32)).astype(BF16)
    logits = (jnp.dot(h_hi, wr_ref[0], preferred_element_type=F32)
              + jnp.dot(h_hi, wr_ref[1], preferred_element_type=F32)
              + jnp.dot(h_lo, wr_ref[0], preferred_element_type=F32)) + br_ref[...]
    comb_ref[...] = _router(logits)


def _merge(att, hf, hb, po, pm, x_all, mod, g_mh, l1g, l1b, wa, wm, wo, wr, br, tm):
    t = x_all.shape[0]
    row = lambda n: pl.BlockSpec((tm, n), lambda i: (i, 0))
    full = lambda a: pl.BlockSpec(a.shape, lambda i: (0,) * a.ndim)
    return pl.pallas_call(
        functools.partial(_merge_kernel, tm=tm),
        grid=(t // tm,),
        in_specs=[row(H_MLA * HP), row(H_M * MV), row(H_M * MV), row(SEG_PO), row(SEG_PM), row(D),
                  full(mod), full(g_mh), full(l1g), full(l1b), full(wa), full(wm), full(wo), full(wr), full(br)],
        out_specs=[row(D), row(D), row(LANES)],
        out_shape=[jax.ShapeDtypeStruct((t, D), F32), jax.ShapeDtypeStruct((t, D), BF16),
                   jax.ShapeDtypeStruct((t, LANES), F32)],
        compiler_params=_cparams(1),
        name="merge",
    )(att, hf, hb, po, pm, x_all, mod, g_mh, l1g, l1b, wa, wm, wo, wr, br)


def _moe_kernel(hm_ref, comb_ref, x1_ref, mod_ref, l2g_ref, l2b_ref, wg_ref, wu_ref, wd_ref, o_ref, acc_ref, *, tm):
    grp = pl.program_id(1)

    @pl.when(grp == 0)
    def _():
        acc_ref[...] = jnp.zeros_like(acc_ref)

    hm = hm_ref[...]
    comb = comb_ref[...]
    lane = lax.broadcasted_iota(jnp.int32, (tm, LANES), 1)
    hid = []
    for g in range(EPG):
        c_e = jnp.sum(jnp.where(lane == N_GROUPS + grp * EPG + g, comb, 0.0), axis=-1, keepdims=True)
        a = jnp.dot(hm, wg_ref[g], preferred_element_type=F32)
        u = jnp.dot(hm, wu_ref[g], preferred_element_type=F32)
        hid.append((a * _sigmoid(a) * u * c_e).astype(BF16))
    acc_ref[...] += jnp.dot(jnp.concatenate(hid, axis=1), wd_ref[...].reshape(EPG * D_EXP, D),
                            preferred_element_type=F32)

    @pl.when(grp == N_GROUPS - 1)
    def _():
        is_ctx = _is_ctx_rows(tm)
        z = ALPHA * x1_ref[...] + _mod(mod_ref, 5, is_ctx) * acc_ref[...]
        o_ref[...] = _ln(z) * l2g_ref[...] + l2b_ref[...]


def _moe(hm, comb, x1, mod, l2g, l2b, wg, wu, wd, tm):
    t = hm.shape[0]
    row = lambda n: pl.BlockSpec((tm, n), lambda i, e: (i, 0))
    full = lambda a: pl.BlockSpec(a.shape, lambda i, e: (0,) * a.ndim)
    return pl.pallas_call(
        functools.partial(_moe_kernel, tm=tm),
        grid=(t // tm, N_GROUPS),
        in_specs=[row(D), row(LANES), row(D), full(mod), full(l2g), full(l2b),
                  pl.BlockSpec((EPG, D, D_EXP), lambda i, e: (e, 0, 0)),
                  pl.BlockSpec((EPG, D, D_EXP), lambda i, e: (e, 0, 0)),
                  pl.BlockSpec((EPG, D_EXP, D), lambda i, e: (e, 0, 0))],
        out_specs=row(D),
        out_shape=jax.ShapeDtypeStruct((t, D), F32),
        scratch_shapes=[pltpu.VMEM((tm, D), F32)],
        compiler_params=_cparams(2),
        name="moe",
    )(hm, comb, x1, mod, l2g, l2b, wg, wu, wd)


def _rot_cols(w):
    a1, a2, b1, b2 = jnp.split(w, 4, axis=-1)
    return jnp.concatenate([-a2, a1, -b2, b1], axis=-1)


def _place_rope(w):
    return jnp.pad(w, ((0, 0), (QK_NOPE, HP - QK_NOPE - QK_ROPE)))


def _rope_tables(t):
    rows = (t - CTX) // GRID_W
    row, col = jnp.meshgrid(jnp.arange(rows, dtype=F32), jnp.arange(GRID_W, dtype=F32), indexing="ij")
    row, col = row.reshape(-1), col.reshape(-1)
    half = QK_ROPE // 2
    inv = ROPE_BASE ** (-jnp.arange(0, half, 2, dtype=F32) / half)
    ar, ac = row[:, None] * inv, col[:, None] * inv
    ang = jnp.concatenate([ar, ar, ac, ac], axis=-1)
    ang = jnp.concatenate([jnp.zeros((CTX, QK_ROPE), F32), ang], axis=0)
    pad = ((0, 0), (QK_NOPE, HP - QK_NOPE - QK_ROPE))
    cos = jnp.pad(jnp.cos(ang), pad) + jnp.pad(jnp.ones((t, QK_NOPE), F32), ((0, 0), (0, HP - QK_NOPE)))
    sin = jnp.pad(jnp.sin(ang), pad)
    return cos, sin, cos.T, sin.T


def _layer_weights(l, w_in, w_uq, w_uk, w_uv, w_bo_mla, w_bo_mlstm, w_out, w_rg, b_rg, w_re, b_re,
                   w_e_gate, w_e_up, w_e_down):
    wi = w_in[l]
    o = 0
    segs = []
    for n in (Q_LORA, KV_LORA, QK_ROPE, SEG_Q, SEG_K, SEG_V, SEG_PO, 4 * H_M, SEG_PM):
        segs.append(wi[:, o:o + n])
        o += n
    s_pqd, s_ckv, s_kr, s_q, s_k, s_v, s_po, s_pg, s_pm = segs
    w_in_p = jnp.concatenate(
        [s_pqd, s_ckv, _place_rope(s_kr), _place_rope(_rot_cols(s_kr)), s_q, s_k, s_v, s_po,
         jnp.pad(s_pg, ((0, 0), (0, SEG_PG - 4 * H_M))), s_pm], axis=1).astype(BF16)

    uq = w_uq[l].reshape(Q_LORA, H_MLA, QK_NOPE + QK_ROPE)
    padh = ((0, 0), (0, 0), (0, HP - QK_NOPE - QK_ROPE))
    wq = jnp.pad(uq, padh).reshape(Q_LORA, H_MLA * HP)
    wqr = jnp.pad(jnp.concatenate([jnp.zeros_like(uq[..., :QK_NOPE]), _rot_cols(uq[..., QK_NOPE:])], axis=-1),
                  padh).reshape(Q_LORA, H_MLA * HP)
    padv = ((0, 0), (0, 0), (0, HP - V_HEAD))
    wk = jnp.pad(w_uk[l].reshape(KV_LORA, H_MLA, QK_NOPE), padv).reshape(KV_LORA, H_MLA * HP)
    wv = jnp.pad(w_uv[l].reshape(KV_LORA, H_MLA, V_HEAD), ((0, 0), (0, 0), (0, VP - V_HEAD))).reshape(KV_LORA, H_MLA * VP)
    wa = jnp.pad(w_bo_mla[l].reshape(H_MLA, V_HEAD, D), ((0, 0), (0, HP - V_HEAD), (0, 0))).reshape(H_MLA * HP, D)

    wr = jnp.pad(jnp.concatenate([w_rg[l], w_re[l]], axis=1), ((0, 0), (0, LANES - N_GROUPS - N_EXP)))
    wr_hi = wr.astype(BF16)
    wr_lo = (wr - wr_hi.astype(F32)).astype(BF16)
    br = jnp.pad(jnp.concatenate([b_rg[l], b_re[l]]), (0, LANES - N_GROUPS - N_EXP)).reshape(1, LANES)
    return dict(
        w_in=w_in_p, wqt=wq.T.astype(BF16), wqrt=wqr.T.astype(BF16), wk=wk.astype(BF16), wvt=wv.T.astype(BF16),
        wa=wa.astype(BF16), wm=w_bo_mlstm[l].astype(BF16), wo=w_out[l].astype(BF16),
        wr=jnp.stack([wr_hi, wr_lo]), br=br,
        wg=w_e_gate[l].astype(BF16), wu=w_e_up[l].astype(BF16), wd=w_e_down[l].astype(BF16))


def kernel(x, c, ctx, c_ctx, w_ada, b_ada, w_in, b_gates, w_uq, w_uk, w_uv, g_qn, g_kvn, g_mh, w_bo_mla,
           w_bo_mlstm, w_out, ln1_g, ln1_b, w_rg, b_rg, w_re, b_re, w_e_gate, w_e_up, w_e_down, ln2_g, ln2_b):
    assert x.shape[0] == 1 and c.shape[0] == 1 and ctx.shape[1] == CTX
    x_all = jnp.concatenate([ctx[0], x[0]], axis=0)
    t = x_all.shape[0]
    tm = ROW_TILE if t % ROW_TILE == 0 else M_CHUNK
    tk = ATT_TK if t % ATT_TK == 0 else M_CHUNK
    assert t % tm == 0 and t % ATT_TQ == 0 and t % tk == 0 and t % M_CHUNK == 0

    cc = jnp.pad(jnp.concatenate([c, c_ctx[None]], axis=0), ((0, 6), (0, 0)))
    mods = _ada(cc, w_ada, b_ada)
    tabs = _rope_tables(t)
    row2 = lambda a: a.reshape(1, -1)

    for l in range(DEPTH):
        w = _layer_weights(l, w_in, w_uq, w_uk, w_uv, w_bo_mla, w_bo_mlstm, w_out, w_rg, b_rg, w_re, b_re,
                           w_e_gate, w_e_up, w_e_down)
        mod = mods[l]
        pqd, ckv, kr, mq, mk, mv, po, pg, pm = _inproj(x_all, mod, w["w_in"], tm)
        qt, kk, vt = _mla_prep(pqd, ckv, kr, tabs, row2(g_qn[l]), row2(g_kvn[l]),
                               w["wqt"], w["wqrt"], w["wk"], w["wvt"], tm)
        att = _attention(qt, kk, vt, ATT_TQ, tk, ATT_CB, ATT_DEPTH)
        pg_t = pg[:, :16].T
        bg_c = jnp.pad(b_gates[l], (0, LANES - 16)).reshape(1, LANES)
        bg_r = b_gates[l].reshape(16, 1)
        hf = _mlstm(mq, mk, mv, pg, pg_t, bg_c, bg_r, False, M_CHUNK)
        hb = _mlstm(mq, mk, mv, pg, pg_t, bg_c, bg_r, True, M_CHUNK)
        x1, hm, comb = _merge(att, hf, hb, po, pm, x_all, mod, row2(g_mh[l]), row2(ln1_g[l]), row2(ln1_b[l]),
                              w["wa"], w["wm"], w["wo"], w["wr"], w["br"], tm)
        x_all = _moe(hm, comb, x1, mod, row2(ln2_g[l]), row2(ln2_b[l]), w["wg"], w["wu"], w["wd"], tm)
    return x_all[CTX:][None]
```

```python
import functools
import math

import jax
import jax.numpy as jnp
from jax import lax
from jax.experimental import pallas as pl
from jax.experimental.pallas import tpu as pltpu

F32 = jnp.float32
BF16 = jnp.bfloat16

D = 1024
DEPTH = 4
GRID_W = 64
CTX = 256
H_MLA = 8
QK_NOPE = 64
QK_ROPE = 32
V_HEAD = 64
Q_LORA = 384
KV_LORA = 256
ROPE_BASE = 10000.0
MLA_SCALE = (QK_NOPE + QK_ROPE) ** -0.5
H_M = 4
MQK = 64
MV = 128
N_GROUPS = 4
EPG = 8
N_EXP = N_GROUPS * EPG
D_EXP = 256
ALPHA = (2 * DEPTH) ** 0.25
LN_EPS = 1e-6

LANES = 128
VMEM_LIMIT = 56 * 1024 * 1024

HP = LANES
ONES_ROW = V_HEAD
BF16_ROWS = 16
VP = (V_HEAD + 1 + BF16_ROWS - 1) // BF16_ROWS * BF16_ROWS
QSCALE = MLA_SCALE * math.log2(math.e)

SEG_PQD = Q_LORA
SEG_CKV = KV_LORA
SEG_KR = 2 * HP
SEG_Q = H_M * MQK
SEG_K = H_M * MQK
SEG_V = H_M * MV
SEG_PO = H_M * MV
SEG_PG = LANES
SEG_PM = 2 * D
IN_SEGS = (SEG_PQD, SEG_CKV, SEG_KR, SEG_Q, SEG_K, SEG_V, SEG_PO, SEG_PG, SEG_PM)
IN_DTYPES = (F32, F32, F32, BF16, BF16, BF16, BF16, F32, BF16)
NP_IN = sum(IN_SEGS)

ROW_TILE = 640
ATT_TQ = 3328
ATT_CB = 256
ATT_TK = 1280
MAX_LAG_JUMP = 64.0
ATT_DEPTH = 2
M_CHUNK = 256
PLAN_GROUP = 0
PLAN_RANK = 1


def _moe_slots(tm):
    p = 1.0 / N_GROUPS
    want = tm * p + 4.0 * math.sqrt(tm * p * (1.0 - p))
    return min(tm, int(math.ceil(want / BF16_ROWS)) * BF16_ROWS)

_NT = (((1,), (1,)), ((), ()))
_TN = (((0,), (0,)), ((), ()))


def _cparams(n_grid):
    return pltpu.CompilerParams(dimension_semantics=("arbitrary",) * n_grid, vmem_limit_bytes=VMEM_LIMIT)


def _ln(x):
    mu = jnp.mean(x, axis=-1, keepdims=True)
    xc = x - mu
    var = jnp.mean(xc * xc, axis=-1, keepdims=True)
    return xc * lax.rsqrt(var + LN_EPS)


def _rms(x, g):
    return x * lax.rsqrt(jnp.mean(x * x, axis=-1, keepdims=True) + LN_EPS) * g


def _sigmoid(x):
    return 1.0 / (1.0 + jnp.exp(-x))


def _is_ctx_rows(tm):
    rows = pl.program_id(0) * tm + lax.broadcasted_iota(jnp.int32, (tm, 1), 0)
    return rows < CTX


def _mod(mod_ref, k, is_ctx):
    lat = mod_ref[0:1, k * D:(k + 1) * D]
    ctx = mod_ref[1:2, k * D:(k + 1) * D]
    return jnp.where(is_ctx, ctx, lat)


def _ada_kernel(c_ref, w_ref, b_ref, o_ref):
    c = c_ref[...]
    s = c * _sigmoid(c)
    o_ref[...] = jnp.dot(s, w_ref[...], preferred_element_type=F32,
                         precision=lax.Precision.HIGHEST) + b_ref[...]


def _ada(cc, w_ada, b_ada):
    tn = 1536
    n = 6 * D
    return pl.pallas_call(
        _ada_kernel,
        grid=(DEPTH, n // tn),
        in_specs=[pl.BlockSpec((8, D), lambda l, j: (0, 0)),
                  pl.BlockSpec((None, D, tn), lambda l, j: (l, 0, j)),
                  pl.BlockSpec((None, 1, tn), lambda l, j: (l, 0, j))],
        out_specs=pl.BlockSpec((None, 8, tn), lambda l, j: (l, 0, j)),
        out_shape=jax.ShapeDtypeStruct((DEPTH, 8, n), F32),
        compiler_params=_cparams(2),
        name="ada",
    )(cc, w_ada, b_ada.reshape(DEPTH, 1, n))


def _inproj_kernel(x_ref, mod_ref, w_ref, *o_refs, tm):
    is_ctx = _is_ctx_rows(tm)
    h = _ln(x_ref[...]) * (1.0 + _mod(mod_ref, 1, is_ctx)) + _mod(mod_ref, 0, is_ctx)
    hb = h.astype(BF16)
    off = 0
    for o in o_refs:
        n = o.shape[-1]
        o[...] = jnp.dot(hb, w_ref[:, off:off + n], preferred_element_type=F32).astype(o.dtype)
        off += n


def _inproj(x_all, mod, w_in_p, tm):
    t = x_all.shape[0]
    return pl.pallas_call(
        functools.partial(_inproj_kernel, tm=tm),
        grid=(t // tm,),
        in_specs=[pl.BlockSpec((tm, D), lambda i: (i, 0)),
                  pl.BlockSpec((8, 6 * D), lambda i: (0, 0)),
                  pl.BlockSpec((D, NP_IN), lambda i: (0, 0))],
        out_specs=[pl.BlockSpec((tm, n), lambda i: (i, 0)) for n in IN_SEGS],
        out_shape=[jax.ShapeDtypeStruct((t, n), dt) for n, dt in zip(IN_SEGS, IN_DTYPES)],
        compiler_params=_cparams(1),
        name="inproj",
    )(x_all, mod, w_in_p)


def _mla_prep_kernel(pqd_ref, ckv_ref, kr_ref, cos_ref, sin_ref, cost_ref, sint_ref, gq_ref, gk_ref,
                     wqt_ref, wqrt_ref, wk_ref, wvt_ref, qt_ref, k_ref, vt_ref, *, tm):
    qn = _rms(pqd_ref[...], gq_ref[...]).astype(BF16)
    qa = lax.dot_general(wqt_ref[...], qn, _NT, preferred_element_type=F32)
    qb = lax.dot_general(wqrt_ref[...], qn, _NT, preferred_element_type=F32)
    cost = cost_ref[...]
    sint = sint_ref[...]
    ckv = _rms(ckv_ref[...], gk_ref[...]).astype(BF16)
    kn = jnp.dot(ckv, wk_ref[...], preferred_element_type=F32)
    kr = kr_ref[:, :HP] * cos_ref[...] + kr_ref[:, HP:] * sin_ref[...]
    vt = lax.dot_general(wvt_ref[...], ckv, _NT, preferred_element_type=F32)
    is_ones_row = lax.broadcasted_iota(jnp.int32, (VP, tm), 0) == ONES_ROW
    for h in range(H_MLA):
        sl = slice(h * HP, (h + 1) * HP)
        qt_ref[h] = ((qa[sl] * cost + qb[sl] * sint) * QSCALE).astype(BF16)
        k_ref[h] = (kn[:, sl] + kr).astype(BF16)
        vt_ref[h] = jnp.where(is_ones_row, 1.0, vt[h * VP:(h + 1) * VP]).astype(BF16)


def _mla_prep(pqd, ckv, kr, tabs, g_qn, g_kvn, wqt, wqrt, wk, wvt, tm):
    t = pqd.shape[0]
    cos, sin, cost, sint = tabs
    row = lambda n: pl.BlockSpec((tm, n), lambda i: (i, 0))
    col = lambda n: pl.BlockSpec((n, tm), lambda i: (0, i))
    full = lambda a: pl.BlockSpec(a.shape, lambda i: (0,) * a.ndim)
    hd = H_MLA * HP
    return pl.pallas_call(
        functools.partial(_mla_prep_kernel, tm=tm),
        grid=(t // tm,),
        in_specs=[row(Q_LORA), row(KV_LORA), row(SEG_KR), row(HP), row(HP), col(HP), col(HP),
                  full(g_qn), full(g_kvn), full(wqt), full(wqrt), full(wk), full(wvt)],
        out_specs=[pl.BlockSpec((H_MLA, HP, tm), lambda i: (0, 0, i)),
                   pl.BlockSpec((H_MLA, tm, HP), lambda i: (0, i, 0)),
                   pl.BlockSpec((H_MLA, VP, tm), lambda i: (0, 0, i))],
        out_shape=[jax.ShapeDtypeStruct((H_MLA, HP, t), BF16),
                   jax.ShapeDtypeStruct((H_MLA, t, HP), BF16),
                   jax.ShapeDtypeStruct((H_MLA, VP, t), BF16)],
        compiler_params=_cparams(1),
        name="mla_prep",
    )(pqd, ckv, kr, cos, sin, cost, sint, g_qn, g_kvn, wqt, wqrt, wk, wvt)


def _scores(k, qt, m):
    s = jnp.dot(k, qt, preferred_element_type=F32)
    return s, jnp.maximum(m, jnp.max(s, axis=0, keepdims=True))


def _accumulate(s, vt, m, m_new, acc):
    p = jnp.exp2(s - m_new).astype(BF16)
    return jnp.exp2(m - m_new) * acc + jnp.dot(vt, p, preferred_element_type=F32)


def _normalized_rows(acc):
    o = acc / acc[ONES_ROW:ONES_ROW + 1, :]
    return jnp.concatenate([o, jnp.zeros((HP - VP, o.shape[1]), F32)], axis=0).T


def _lagged_pass(qt_ref, k_ref, vt_ref, m_ref, acc_ref, p_ref, sc_ref, *, ncb, n_kv, tk, cb, depth):
    k0 = k_ref[0:BF16_ROWS, :]
    for c in range(ncb):
        s0 = jnp.dot(k0, qt_ref[:, c * cb:(c + 1) * cb], preferred_element_type=F32)
        m_ref[c] = jnp.max(s0, axis=0, keepdims=True)
    acc_ref[...] = jnp.zeros(acc_ref.shape, F32)
    p_ref[...] = jnp.zeros(p_ref.shape, BF16)
    sc_ref[...] = jnp.ones(sc_ref.shape, F32)

    def carried(vt_prev):
        return [(ncb - depth + d, p_ref[d], sc_ref[d], vt_prev) for d in range(depth)]

    def drain(entry):
        pc, pp, psc, pvt = entry
        acc_ref[pc] = (acc_ref[pc] + jnp.dot(pvt, pp, preferred_element_type=F32)) * psc

    def step(j, jump):
        off = pl.multiple_of(j * tk, tk)
        off_prev = pl.multiple_of(jnp.maximum(j - 1, 0) * tk, tk)
        k = k_ref[pl.ds(off, tk), :]
        vt = vt_ref[:, pl.ds(off, tk)]
        pending = carried(vt_ref[:, pl.ds(off_prev, tk)])
        for c in range(ncb):
            m = m_ref[c]
            s = jnp.dot(k, qt_ref[:, c * cb:(c + 1) * cb], preferred_element_type=F32)
            p = jnp.exp2(s - m).astype(BF16)
            cm = jnp.max(s, axis=0, keepdims=True)
            m_new = jnp.maximum(m, cm)
            jump = jnp.maximum(jump, cm - m)
            m_ref[c] = m_new
            drain(pending.pop(0))
            pending.append((c, p, jnp.exp2(m - m_new), vt))
        for d, (_, pp, psc, _) in enumerate(pending):
            p_ref[d] = pp
            sc_ref[d] = psc
        return jump

    jump = lax.fori_loop(0, n_kv, step, jnp.zeros((1, cb), F32))
    for entry in carried(vt_ref[:, (n_kv - 1) * tk:]):
        drain(entry)
    return jump


def _exact_pass(qt_ref, k_ref, vt_ref, m_ref, acc_ref, s_ref, mp_ref, *, ncb, n_kv, tk, cb, depth):
    m_ref[...] = jnp.full(m_ref.shape, -1e30, F32)
    acc_ref[...] = jnp.zeros(acc_ref.shape, F32)
    s_ref[...] = jnp.full(s_ref.shape, -jnp.inf, F32)
    mp_ref[...] = jnp.full(mp_ref.shape, -1e30, F32)

    def carried(vt_prev):
        return [(ncb - depth + d, s_ref[d], mp_ref[d, 0], mp_ref[d, 1], vt_prev) for d in range(depth)]

    def drain(entry):
        pc, ps, pm_old, pm_new, pvt = entry
        acc_ref[pc] = _accumulate(ps, pvt, pm_old, pm_new, acc_ref[pc])

    def step(j, carry):
        off = pl.multiple_of(j * tk, tk)
        off_prev = pl.multiple_of(jnp.maximum(j - 1, 0) * tk, tk)
        k = k_ref[pl.ds(off, tk), :]
        vt = vt_ref[:, pl.ds(off, tk)]
        pending = carried(vt_ref[:, pl.ds(off_prev, tk)])
        for c in range(ncb):
            m_old = m_ref[c]
            s, m_new = _scores(k, qt_ref[:, c * cb:(c + 1) * cb], m_old)
            m_ref[c] = m_new
            drain(pending.pop(0))
            pending.append((c, s, m_old, m_new, vt))
        for d, (_, ps, pm_old, pm_new, _) in enumerate(pending):
            s_ref[d] = ps
            mp_ref[d, 0] = pm_old
            mp_ref[d, 1] = pm_new
        return carry

    lax.fori_loop(0, n_kv, step, 0)
    for entry in carried(vt_ref[:, (n_kv - 1) * tk:]):
        drain(entry)


def _attn_kernel(qt_ref, k_ref, vt_ref, o_ref, m_ref, acc_ref, s_ref, mp_ref, p_ref, sc_ref, *,
                 tq, tk, t, cb, depth):
    geom = dict(ncb=tq // cb, n_kv=t // tk, tk=tk, cb=cb, depth=depth)

    def write_out():
        for c in range(tq // cb):
            o_ref[c * cb:(c + 1) * cb, :] = _normalized_rows(acc_ref[c]).astype(o_ref.dtype)

    jump = _lagged_pass(qt_ref, k_ref, vt_ref, m_ref, acc_ref, p_ref, sc_ref, **geom)
    write_out()

    @pl.when(jnp.max(jump) > MAX_LAG_JUMP)
    def _():
        _exact_pass(qt_ref, k_ref, vt_ref, m_ref, acc_ref, s_ref, mp_ref, **geom)
        write_out()

    @pl.when(pl.program_id(1) == 0)
    def _():
        m0 = jnp.full((1, CTX), -1e30, F32)
        s, m_new = _scores(k_ref[0:CTX, :], qt_ref[:, 0:CTX], m0)
        acc_c = _accumulate(s, vt_ref[:, 0:CTX], m0, m_new, jnp.zeros((VP, CTX), F32))
        o_ref[0:CTX, :] = _normalized_rows(acc_c).astype(o_ref.dtype)


def _attention(qt, k, vt, tq, tk, cb, depth):
    t = k.shape[1]
    assert depth <= tq // cb
    return pl.pallas_call(
        functools.partial(_attn_kernel, tq=tq, tk=tk, t=t, cb=cb, depth=depth),
        grid=(H_MLA, t // tq),
        in_specs=[pl.BlockSpec((None, HP, tq), lambda h, i: (h, 0, i)),
                  pl.BlockSpec((None, t, HP), lambda h, i: (h, 0, 0)),
                  pl.BlockSpec((None, VP, t), lambda h, i: (h, 0, 0))],
        out_specs=pl.BlockSpec((tq, HP), lambda h, i: (i, h)),
        out_shape=jax.ShapeDtypeStruct((t, H_MLA * HP), BF16),
        scratch_shapes=[pltpu.VMEM((tq // cb, 1, cb), F32), pltpu.VMEM((tq // cb, VP, cb), F32),
                        pltpu.VMEM((depth, tk, cb), F32), pltpu.VMEM((depth, 2, 1, cb), F32),
                        pltpu.VMEM((depth, tk, cb), BF16), pltpu.VMEM((depth, 1, cb), F32)],
        compiler_params=_cparams(2),
        name="attention",
    )(qt, k, vt)


def _split3(x):
    hi = x.astype(BF16)
    r = x - hi.astype(F32)
    mid = r.astype(BF16)
    lo = (r - mid.astype(F32)).astype(BF16)
    return hi, mid, lo


def _log_sigmoid(x):
    return jnp.minimum(x, 0.0) - jnp.log(1.0 + jnp.exp(-jnp.abs(x)))


def _mlstm_kernel(q_ref, k_ref, v_ref, gc_ref, gr_ref, bc_ref, br_ref, o_ref, c_scr, m_scr, *, L, rev):
    @pl.when(pl.program_id(0) == 0)
    def _():
        c_scr[...] = jnp.zeros_like(c_scr)
        m_scr[...] = jnp.zeros_like(m_scr)

    row = lax.broadcasted_iota(jnp.int32, (L, L), 0)
    col = lax.broadcasted_iota(jnp.int32, (L, L), 1)
    mask = (row <= col) if rev else (row >= col)
    maskt = (row >= col) if rev else (row <= col)
    mask_b = mask.astype(BF16)
    maskt_b = maskt.astype(BF16)

    gc = gc_ref[...] + bc_ref[...]
    gr = gr_ref[...] + br_ref[...]
    lfc = _log_sigmoid(gc)
    lfr = _log_sigmoid(gr)
    b_col = sum(jnp.dot(mask_b, p, preferred_element_type=F32) for p in _split3(lfc))
    b_row = sum(jnp.dot(p, maskt_b, preferred_element_type=F32) for p in _split3(lfr))
    ones_blk = (lax.broadcasted_iota(jnp.int32, (L, LANES), 1) == 0).astype(BF16)

    for h in range(H_M):
        ci = 8 * rev + h
        cf = ci + 4
        b_c = b_col[:, cf:cf + 1]
        i_c = gc[:, ci:ci + 1]
        b_r = b_row[cf:cf + 1, :]
        i_r = gr[ci:ci + 1, :]
        bl = jnp.sum(lfc[:, cf:cf + 1], axis=0, keepdims=True)
        m = m_scr[h][0:1, 0:1]
        c_aug = c_scr[h]

        dlog = jnp.where(mask, b_c - (b_r - i_r), -jnp.inf)
        inter = b_c + m
        mj = jnp.maximum(inter, jnp.max(dlog, axis=-1, keepdims=True))
        w = jnp.exp(dlog - mj)
        sc = jnp.exp(inter - mj)

        q = q_ref[:, h * MQK:(h + 1) * MQK] * (MQK ** -0.5)
        k = k_ref[:, h * MQK:(h + 1) * MQK]
        v_aug = jnp.concatenate([v_ref[:, h * MV:(h + 1) * MV], ones_blk], axis=1)
        qk = lax.dot_general(q, k, _NT, preferred_element_type=F32) * w
        num = (jnp.dot(qk.astype(BF16), v_aug, preferred_element_type=F32)
               + sc * jnp.dot(q, c_aug.astype(BF16), preferred_element_type=F32))
        den = jnp.maximum(jnp.abs(num[:, MV:MV + 1]), jnp.exp(-mj))
        o_ref[:, h * MV:(h + 1) * MV] = num[:, :MV] / den

        wlog = bl - b_c + i_c
        m_new = jnp.maximum(bl + m, jnp.max(wlog, axis=0, keepdims=True))
        ws = jnp.exp(wlog - m_new)
        sd = jnp.exp(bl + m - m_new)
        wv = (ws * v_aug.astype(F32)).astype(BF16)
        c_scr[h] = sd * c_aug + lax.dot_general(k, wv, _TN, preferred_element_type=F32)
        m_scr[h] = jnp.broadcast_to(m_new, m_scr.shape[1:])


def _mlstm(mq, mk, mv, pg, pg_t, bg_c, bg_r, rev, L):
    t = mq.shape[0]
    nc = t // L
    if rev:
        idx = lambda c: jnp.where(c == 0, 0, nc - c)
    else:
        idx = lambda c: c
    return pl.pallas_call(
        functools.partial(_mlstm_kernel, L=L, rev=int(rev)),
        grid=(nc,),
        in_specs=[pl.BlockSpec((L, SEG_Q), lambda c: (idx(c), 0)),
                  pl.BlockSpec((L, SEG_K), lambda c: (idx(c), 0)),
                  pl.BlockSpec((L, SEG_V), lambda c: (idx(c), 0)),
                  pl.BlockSpec((L, LANES), lambda c: (idx(c), 0)),
                  pl.BlockSpec((16, L), lambda c: (0, idx(c))),
                  pl.BlockSpec((1, LANES), lambda c: (0, 0)),
                  pl.BlockSpec((16, 1), lambda c: (0, 0))],
        out_specs=pl.BlockSpec((L, H_M * MV), lambda c: (idx(c), 0)),
        out_shape=jax.ShapeDtypeStruct((t, H_M * MV), F32),
        scratch_shapes=[pltpu.VMEM((H_M, MQK, 2 * MV), F32), pltpu.VMEM((H_M, 8, LANES), F32)],
        compiler_params=_cparams(1),
        name="mlstm_rev" if rev else "mlstm_fwd",
    )(mq, mk, mv, pg, pg_t, bg_c, bg_r)


def _router(logits):
    tm = logits.shape[0]
    lane = lax.broadcasted_iota(jnp.int32, (tm, LANES), 1)
    big = jnp.int32(LANES)
    neg = -jnp.inf
    is_g = lane < N_GROUPS
    lg = jnp.where(is_g, logits, neg)
    gmax = jnp.max(lg, axis=-1, keepdims=True)
    grp = jnp.min(jnp.where(lg == gmax, lane, big), axis=-1, keepdims=True)
    p_grp = 1.0 / jnp.sum(jnp.where(is_g, jnp.exp(lg - gmax), 0.0), axis=-1, keepdims=True)
    e_lo = N_GROUPS + grp * EPG
    in_grp = (lane >= e_lo) & (lane < e_lo + EPG)
    le = jnp.where(in_grp, logits, neg)
    l1 = jnp.max(le, axis=-1, keepdims=True)
    i1 = jnp.min(jnp.where(le == l1, lane, big), axis=-1, keepdims=True)
    le2 = jnp.where(lane == i1, neg, le)
    l2 = jnp.max(le2, axis=-1, keepdims=True)
    i2 = jnp.min(jnp.where(le2 == l2, lane, big), axis=-1, keepdims=True)
    r = jnp.exp(l2 - l1)
    w1 = p_grp / (1.0 + r)
    w2 = w1 * r
    return jnp.where(lane == i1, w1, 0.0) + jnp.where(lane == i2, w2, 0.0), grp


def _merge_kernel(att_ref, hf_ref, hb_ref, po_ref, pm_ref, x_ref, mod_ref, gmh_ref, l1g_ref, l1b_ref,
                  wa_ref, wm_ref, wo_ref, wr_ref, br_ref, x1_ref, hm_ref, plan_ref, cnt_ref, *, tm):
    is_ctx = _is_ctx_rows(tm)
    y_mla = jnp.dot(att_ref[...], wa_ref[...], preferred_element_type=F32)
    hs = hf_ref[...] + hb_ref[...]
    parts = []
    for h in range(H_M):
        sl = slice(h * MV, (h + 1) * MV)
        parts.append(_ln(hs[:, sl]))
    hn = jnp.concatenate(parts, axis=1) * gmh_ref[...] * _sigmoid(po_ref[...].astype(F32))
    y_ml = jnp.dot(hn.astype(BF16), wm_ref[...], preferred_element_type=F32)
    g_a = _sigmoid(pm_ref[:, :D].astype(F32))
    g_b = _sigmoid(pm_ref[:, D:].astype(F32))
    y = jnp.dot((g_a * y_mla + g_b * y_ml).astype(BF16), wo_ref[...], preferred_element_type=F32)
    x1 = _ln(ALPHA * x_ref[...] + _mod(mod_ref, 2, is_ctx) * y) * l1g_ref[...] + l1b_ref[...]
    x1_ref[...] = x1
    hm = _ln(x1) * (1.0 + _mod(mod_ref, 4, is_ctx)) + _mod(mod_ref, 3, is_ctx)
    hm_ref[...] = hm.astype(BF16)
    h_hi = hm.astype(BF16)
    h_lo = (hm - h_hi.astype(F32)).astype(BF16)
    logits = (jnp.dot(h_hi, wr_ref[0], preferred_element_type=F32)
              + jnp.dot(h_hi, wr_ref[1], preferred_element_type=F32)
              + jnp.dot(h_lo, wr_ref[0], preferred_element_type=F32)) + br_ref[...]
    comb, grp = _router(logits)
    lane = lax.broadcasted_iota(jnp.int32, (tm, LANES), 1)
    in_group = lane == grp
    earlier = (lax.broadcasted_iota(jnp.int32, (tm, tm), 0)
               > lax.broadcasted_iota(jnp.int32, (tm, tm), 1)).astype(BF16)
    n_before = jnp.dot(earlier, in_group.astype(BF16), preferred_element_type=F32)
    rank = jnp.sum(jnp.where(in_group, n_before, 0.0), axis=-1, keepdims=True)
    plan_ref[...] = (comb + jnp.where(lane == PLAN_GROUP, grp.astype(F32), 0.0)
                     + jnp.where(lane == PLAN_RANK, rank, 0.0))
    cnt_ref[...] = jnp.broadcast_to(jnp.sum(in_group.astype(F32), axis=0, keepdims=True), cnt_ref.shape)


def _merge(att, hf, hb, po, pm, x_all, mod, g_mh, l1g, l1b, wa, wm, wo, wr, br, tm):
    t = x_all.shape[0]
    row = lambda n: pl.BlockSpec((tm, n), lambda i: (i, 0))
    full = lambda a: pl.BlockSpec(a.shape, lambda i: (0,) * a.ndim)
    return pl.pallas_call(
        functools.partial(_merge_kernel, tm=tm),
        grid=(t // tm,),
        in_specs=[row(H_MLA * HP), row(H_M * MV), row(H_M * MV), row(SEG_PO), row(SEG_PM), row(D),
                  full(mod), full(g_mh), full(l1g), full(l1b), full(wa), full(wm), full(wo), full(wr), full(br)],
        out_specs=[row(D), row(D), row(LANES), pl.BlockSpec((None, 8, LANES), lambda i: (i, 0, 0))],
        out_shape=[jax.ShapeDtypeStruct((t, D), F32), jax.ShapeDtypeStruct((t, D), BF16),
                   jax.ShapeDtypeStruct((t, LANES), F32), jax.ShapeDtypeStruct((t // tm, 8, LANES), F32)],
        compiler_params=_cparams(1),
        name="merge",
    )(att, hf, hb, po, pm, x_all, mod, g_mh, l1g, l1b, wa, wm, wo, wr, br)


def _moe_kernel(cnt_ref, hm_ref, plan_ref, plant_ref, x1_ref, mod_ref, l2g_ref, l2b_ref, wg_ref, wu_ref, wd_ref,
                o_ref, acc_ref, *, tm, slots):
    i = pl.program_id(0)
    grp = pl.program_id(1)

    @pl.when(grp == 0)
    def _():
        acc_ref[...] = jnp.zeros_like(acc_ref)

    n_rounds = (cnt_ref[i, grp] + (slots - 1)) // slots
    hm = hm_ref[...]
    plan = plan_ref[...]
    plan_parts = _split3(plan)
    grp_f = grp.astype(F32)
    in_grp_col = plan[:, PLAN_GROUP:PLAN_GROUP + 1] == grp_f
    rank_col = plan[:, PLAN_RANK:PLAN_RANK + 1]
    in_grp_row = plant_ref[PLAN_GROUP:PLAN_GROUP + 1, :] == grp_f
    rank_row = plant_ref[PLAN_RANK:PLAN_RANK + 1, :]
    slot_rows = lax.broadcasted_iota(jnp.int32, (slots, tm), 0).astype(F32)
    slot_cols = lax.broadcasted_iota(jnp.int32, (tm, slots), 1).astype(F32)
    lane = lax.broadcasted_iota(jnp.int32, (slots, LANES), 1)

    def one_round(r, carry):
        base = (r * slots).astype(F32)
        gather = ((slot_rows == rank_row - base) & in_grp_row).astype(BF16)
        xs = jnp.dot(gather, hm, preferred_element_type=F32).astype(BF16)
        cs = sum(jnp.dot(gather, p, preferred_element_type=F32) for p in plan_parts)
        hid = []
        for g in range(EPG):
            c_e = jnp.sum(jnp.where(lane == N_GROUPS + grp * EPG + g, cs, 0.0), axis=-1, keepdims=True)
            a = jnp.dot(xs, wg_ref[g], preferred_element_type=F32)
            u = jnp.dot(xs, wu_ref[g], preferred_element_type=F32)
            hid.append((a * _sigmoid(a) * u * c_e).astype(BF16))
        ys = jnp.dot(jnp.concatenate(hid, axis=1), wd_ref[...].reshape(EPG * D_EXP, D),
                     preferred_element_type=F32)
        ys_hi = ys.astype(BF16)
        ys_lo = (ys - ys_hi.astype(F32)).astype(BF16)
        scatter = ((slot_cols == rank_col - base) & in_grp_col).astype(BF16)
        acc_ref[...] += (jnp.dot(scatter, ys_hi, preferred_element_type=F32)
                         + jnp.dot(scatter, ys_lo, preferred_element_type=F32))
        return carry

    lax.fori_loop(0, n_rounds, one_round, 0)

    @pl.when(grp == N_GROUPS - 1)
    def _():
        is_ctx = _is_ctx_rows(tm)
        z = ALPHA * x1_ref[...] + _mod(mod_ref, 5, is_ctx) * acc_ref[...]
        o_ref[...] = _ln(z) * l2g_ref[...] + l2b_ref[...]


def _moe(counts, hm, plan, plan_t, x1, mod, l2g, l2b, wg, wu, wd, layer, tm, slots):
    t = hm.shape[0]
    row = lambda n: pl.BlockSpec((tm, n), lambda i, e, cnt: (i, 0))
    full = lambda a: pl.BlockSpec(a.shape, lambda i, e, cnt: (0,) * a.ndim)
    group_of = lambda a: pl.BlockSpec((None, EPG) + a.shape[2:], lambda i, e, cnt: (layer, e, 0, 0))
    return pl.pallas_call(
        functools.partial(_moe_kernel, tm=tm, slots=slots),
        grid_spec=pltpu.PrefetchScalarGridSpec(
            num_scalar_prefetch=1,
            grid=(t // tm, N_GROUPS),
            in_specs=[row(D), row(LANES), pl.BlockSpec((2, tm), lambda i, e, cnt: (0, i)), row(D),
                      full(mod), full(l2g), full(l2b), group_of(wg), group_of(wu), group_of(wd)],
            out_specs=row(D),
            scratch_shapes=[pltpu.VMEM((tm, D), F32)]),
        out_shape=jax.ShapeDtypeStruct((t, D), F32),
        compiler_params=_cparams(2),
        name="moe",
    )(counts, hm, plan, plan_t, x1, mod, l2g, l2b, wg, wu, wd)


def _rot_cols(w):
    a1, a2, b1, b2 = jnp.split(w, 4, axis=-1)
    return jnp.concatenate([-a2, a1, -b2, b1], axis=-1)


def _place_rope(w):
    return jnp.pad(w, ((0, 0), (QK_NOPE, HP - QK_NOPE - QK_ROPE)))


def _rope_tables(t):
    rows = (t - CTX) // GRID_W
    row, col = jnp.meshgrid(jnp.arange(rows, dtype=F32), jnp.arange(GRID_W, dtype=F32), indexing="ij")
    row, col = row.reshape(-1), col.reshape(-1)
    half = QK_ROPE // 2
    inv = ROPE_BASE ** (-jnp.arange(0, half, 2, dtype=F32) / half)
    ar, ac = row[:, None] * inv, col[:, None] * inv
    ang = jnp.concatenate([ar, ar, ac, ac], axis=-1)
    ang = jnp.concatenate([jnp.zeros((CTX, QK_ROPE), F32), ang], axis=0)
    pad = ((0, 0), (QK_NOPE, HP - QK_NOPE - QK_ROPE))
    cos = jnp.pad(jnp.cos(ang), pad) + jnp.pad(jnp.ones((t, QK_NOPE), F32), ((0, 0), (0, HP - QK_NOPE)))
    sin = jnp.pad(jnp.sin(ang), pad)
    return cos, sin, cos.T, sin.T


def _layer_weights(l, w_in, w_uq, w_uk, w_uv, w_bo_mla, w_bo_mlstm, w_out, w_rg, b_rg, w_re, b_re):
    wi = w_in[l]
    o = 0
    segs = []
    for n in (Q_LORA, KV_LORA, QK_ROPE, SEG_Q, SEG_K, SEG_V, SEG_PO, 4 * H_M, SEG_PM):
        segs.append(wi[:, o:o + n])
        o += n
    s_pqd, s_ckv, s_kr, s_q, s_k, s_v, s_po, s_pg, s_pm = segs
    w_in_p = jnp.concatenate(
        [s_pqd, s_ckv, _place_rope(s_kr), _place_rope(_rot_cols(s_kr)), s_q, s_k, s_v, s_po,
         jnp.pad(s_pg, ((0, 0), (0, SEG_PG - 4 * H_M))), s_pm], axis=1).astype(BF16)

    uq = w_uq[l].reshape(Q_LORA, H_MLA, QK_NOPE + QK_ROPE)
    padh = ((0, 0), (0, 0), (0, HP - QK_NOPE - QK_ROPE))
    wq = jnp.pad(uq, padh).reshape(Q_LORA, H_MLA * HP)
    wqr = jnp.pad(jnp.concatenate([jnp.zeros_like(uq[..., :QK_NOPE]), _rot_cols(uq[..., QK_NOPE:])], axis=-1),
                  padh).reshape(Q_LORA, H_MLA * HP)
    padv = ((0, 0), (0, 0), (0, HP - V_HEAD))
    wk = jnp.pad(w_uk[l].reshape(KV_LORA, H_MLA, QK_NOPE), padv).reshape(KV_LORA, H_MLA * HP)
    wv = jnp.pad(w_uv[l].reshape(KV_LORA, H_MLA, V_HEAD), ((0, 0), (0, 0), (0, VP - V_HEAD))).reshape(KV_LORA, H_MLA * VP)
    wa = jnp.pad(w_bo_mla[l].reshape(H_MLA, V_HEAD, D), ((0, 0), (0, HP - V_HEAD), (0, 0))).reshape(H_MLA * HP, D)

    wr = jnp.pad(jnp.concatenate([w_rg[l], w_re[l]], axis=1), ((0, 0), (0, LANES - N_GROUPS - N_EXP)))
    wr_hi = wr.astype(BF16)
    wr_lo = (wr - wr_hi.astype(F32)).astype(BF16)
    br = jnp.pad(jnp.concatenate([b_rg[l], b_re[l]]), (0, LANES - N_GROUPS - N_EXP)).reshape(1, LANES)
    return dict(
        w_in=w_in_p, wqt=wq.T.astype(BF16), wqrt=wqr.T.astype(BF16), wk=wk.astype(BF16), wvt=wv.T.astype(BF16),
        wa=wa.astype(BF16), wm=w_bo_mlstm[l].astype(BF16), wo=w_out[l].astype(BF16),
        wr=jnp.stack([wr_hi, wr_lo]), br=br)


def kernel(x, c, ctx, c_ctx, w_ada, b_ada, w_in, b_gates, w_uq, w_uk, w_uv, g_qn, g_kvn, g_mh, w_bo_mla,
           w_bo_mlstm, w_out, ln1_g, ln1_b, w_rg, b_rg, w_re, b_re, w_e_gate, w_e_up, w_e_down, ln2_g, ln2_b):
    assert x.shape[0] == 1 and c.shape[0] == 1 and ctx.shape[1] == CTX
    x_all = jnp.concatenate([ctx[0], x[0]], axis=0)
    t = x_all.shape[0]
    tm = ROW_TILE if t % ROW_TILE == 0 else M_CHUNK
    tk = ATT_TK if t % ATT_TK == 0 else M_CHUNK
    assert t % tm == 0 and t % ATT_TQ == 0 and t % tk == 0 and t % M_CHUNK == 0

    cc = jnp.pad(jnp.concatenate([c, c_ctx[None]], axis=0), ((0, 6), (0, 0)))
    mods = _ada(cc, w_ada, b_ada)
    tabs = _rope_tables(t)
    row2 = lambda a: a.reshape(1, -1)
    wg_all, wu_all, wd_all = w_e_gate.astype(BF16), w_e_up.astype(BF16), w_e_down.astype(BF16)

    for l in range(DEPTH):
        w = _layer_weights(l, w_in, w_uq, w_uk, w_uv, w_bo_mla, w_bo_mlstm, w_out, w_rg, b_rg, w_re, b_re)
        mod = mods[l]
        pqd, ckv, kr, mq, mk, mv, po, pg, pm = _inproj(x_all, mod, w["w_in"], tm)
        qt, kk, vt = _mla_prep(pqd, ckv, kr, tabs, row2(g_qn[l]), row2(g_kvn[l]),
                               w["wqt"], w["wqrt"], w["wk"], w["wvt"], tm)
        att = _attention(qt, kk, vt, ATT_TQ, tk, ATT_CB, ATT_DEPTH)
        pg_t = pg[:, :16].T
        bg_c = jnp.pad(b_gates[l], (0, LANES - 16)).reshape(1, LANES)
        bg_r = b_gates[l].reshape(16, 1)
        hf = _mlstm(mq, mk, mv, pg, pg_t, bg_c, bg_r, False, M_CHUNK)
        hb = _mlstm(mq, mk, mv, pg, pg_t, bg_c, bg_r, True, M_CHUNK)
        x1, hm, plan, cnt = _merge(att, hf, hb, po, pm, x_all, mod, row2(g_mh[l]), row2(ln1_g[l]), row2(ln1_b[l]),
                                   w["wa"], w["wm"], w["wo"], w["wr"], w["br"], tm)
        counts = cnt[:, 0, :N_GROUPS].astype(jnp.int32)
        plan_t = plan[:, :2].T
        x_all = _moe(counts, hm, plan, plan_t, x1, mod, row2(ln2_g[l]), row2(ln2_b[l]),
                     wg_all, wu_all, wd_all, l, tm, _moe_slots(tm))
    return x_all[CTX:][None]
```

```python
import functools
import math

import jax
import jax.numpy as jnp
from jax import lax
from jax.experimental import pallas as pl
from jax.experimental.pallas import tpu as pltpu

F32 = jnp.float32
BF16 = jnp.bfloat16

D = 1024
DEPTH = 4
GRID_W = 64
CTX = 256
H_MLA = 8
QK_NOPE = 64
QK_ROPE = 32
V_HEAD = 64
Q_LORA = 384
KV_LORA = 256
ROPE_BASE = 10000.0
MLA_SCALE = (QK_NOPE + QK_ROPE) ** -0.5
H_M = 4
MQK = 64
MV = 128
N_GROUPS = 4
EPG = 8
N_EXP = N_GROUPS * EPG
D_EXP = 256
ALPHA = (2 * DEPTH) ** 0.25
LN_EPS = 1e-6

LANES = 128
VMEM_LIMIT = 56 * 1024 * 1024
VMEM_LIMIT_MOE = 58 * 1024 * 1024

HP = LANES
ONES_ROW = V_HEAD
BF16_ROWS = 16
VP = (V_HEAD + 1 + BF16_ROWS - 1) // BF16_ROWS * BF16_ROWS
QSCALE = MLA_SCALE * math.log2(math.e)

SEG_PQD = Q_LORA
SEG_CKV = KV_LORA
SEG_KR = 2 * HP
SEG_Q = H_M * MQK
SEG_K = H_M * MQK
SEG_V = H_M * MV
SEG_PO = H_M * MV
SEG_PG = LANES
SEG_PM = 2 * D
IN_SEGS = (SEG_PQD, SEG_CKV, SEG_KR, SEG_Q, SEG_K, SEG_V, SEG_PO, SEG_PG, SEG_PM)
IN_DTYPES = (F32, F32, F32, BF16, BF16, BF16, BF16, F32, BF16)
NP_IN = sum(IN_SEGS)

ROW_TILE = 640
ATT_TQ = 3328
ATT_CB = 256
ATT_TK = 1280
MAX_LAG_JUMP = 64.0
ATT_DEPTH = 2
M_CHUNK = 256
MOE_PLAN_TILES = 2
PLAN_GROUP = 0
PLAN_RANK = 1


def _moe_slots(tm):
    p = 1.0 / N_GROUPS
    want = tm * p + 4.0 * math.sqrt(tm * p * (1.0 - p))
    return min(tm, int(math.ceil(want / BF16_ROWS)) * BF16_ROWS)

_NT = (((1,), (1,)), ((), ()))
_TN = (((0,), (0,)), ((), ()))


def _cparams(n_grid, vmem_limit=VMEM_LIMIT):
    return pltpu.CompilerParams(dimension_semantics=("arbitrary",) * n_grid, vmem_limit_bytes=vmem_limit)


def _ln(x):
    mu = jnp.mean(x, axis=-1, keepdims=True)
    xc = x - mu
    var = jnp.mean(xc * xc, axis=-1, keepdims=True)
    return xc * lax.rsqrt(var + LN_EPS)


def _rms(x, g):
    return x * lax.rsqrt(jnp.mean(x * x, axis=-1, keepdims=True) + LN_EPS) * g


def _sigmoid(x):
    return 1.0 / (1.0 + jnp.exp(-x))


def _is_ctx_rows(tm):
    rows = pl.program_id(0) * tm + lax.broadcasted_iota(jnp.int32, (tm, 1), 0)
    return rows < CTX


def _mod(mod_ref, k, is_ctx):
    lat = mod_ref[0:1, k * D:(k + 1) * D]
    ctx = mod_ref[1:2, k * D:(k + 1) * D]
    return jnp.where(is_ctx, ctx, lat)


def _ada_kernel(c_ref, w_ref, b_ref, o_ref):
    c = c_ref[...]
    s = c * _sigmoid(c)
    o_ref[...] = jnp.dot(s, w_ref[...], preferred_element_type=F32,
                         precision=lax.Precision.HIGHEST) + b_ref[...]


def _ada(cc, w_ada, b_ada):
    tn = 1536
    n = 6 * D
    return pl.pallas_call(
        _ada_kernel,
        grid=(DEPTH, n // tn),
        in_specs=[pl.BlockSpec((8, D), lambda l, j: (0, 0)),
                  pl.BlockSpec((None, D, tn), lambda l, j: (l, 0, j)),
                  pl.BlockSpec((None, 1, tn), lambda l, j: (l, 0, j))],
        out_specs=pl.BlockSpec((None, 8, tn), lambda l, j: (l, 0, j)),
        out_shape=jax.ShapeDtypeStruct((DEPTH, 8, n), F32),
        compiler_params=_cparams(2),
        name="ada",
    )(cc, w_ada, b_ada.reshape(DEPTH, 1, n))


def _inproj_kernel(x_ref, mod_ref, w_ref, *o_refs, tm):
    is_ctx = _is_ctx_rows(tm)
    h = _ln(x_ref[...]) * (1.0 + _mod(mod_ref, 1, is_ctx)) + _mod(mod_ref, 0, is_ctx)
    hb = h.astype(BF16)
    off = 0
    for o in o_refs:
        n = o.shape[-1]
        o[...] = jnp.dot(hb, w_ref[:, off:off + n], preferred_element_type=F32).astype(o.dtype)
        off += n


def _inproj(x_all, mod, w_in_p, tm):
    t = x_all.shape[0]
    return pl.pallas_call(
        functools.partial(_inproj_kernel, tm=tm),
        grid=(t // tm,),
        in_specs=[pl.BlockSpec((tm, D), lambda i: (i, 0)),
                  pl.BlockSpec((8, 6 * D), lambda i: (0, 0)),
                  pl.BlockSpec((D, NP_IN), lambda i: (0, 0))],
        out_specs=[pl.BlockSpec((tm, n), lambda i: (i, 0)) for n in IN_SEGS],
        out_shape=[jax.ShapeDtypeStruct((t, n), dt) for n, dt in zip(IN_SEGS, IN_DTYPES)],
        compiler_params=_cparams(1),
        name="inproj",
    )(x_all, mod, w_in_p)


def _mla_prep_kernel(pqd_ref, ckv_ref, kr_ref, cos_ref, sin_ref, cost_ref, sint_ref, gq_ref, gk_ref,
                     wqt_ref, wqrt_ref, wk_ref, wvt_ref, qt_ref, k_ref, vt_ref, *, tm):
    qn = _rms(pqd_ref[...], gq_ref[...]).astype(BF16)
    qa = lax.dot_general(wqt_ref[...], qn, _NT, preferred_element_type=F32)
    qb = lax.dot_general(wqrt_ref[...], qn, _NT, preferred_element_type=F32)
    cost = cost_ref[...]
    sint = sint_ref[...]
    ckv = _rms(ckv_ref[...], gk_ref[...]).astype(BF16)
    kn = jnp.dot(ckv, wk_ref[...], preferred_element_type=F32)
    kr = kr_ref[:, :HP] * cos_ref[...] + kr_ref[:, HP:] * sin_ref[...]
    vt = lax.dot_general(wvt_ref[...], ckv, _NT, preferred_element_type=F32)
    is_ones_row = lax.broadcasted_iota(jnp.int32, (VP, tm), 0) == ONES_ROW
    for h in range(H_MLA):
        sl = slice(h * HP, (h + 1) * HP)
        qt_ref[h] = ((qa[sl] * cost + qb[sl] * sint) * QSCALE).astype(BF16)
        k_ref[h] = (kn[:, sl] + kr).astype(BF16)
        vt_ref[h] = jnp.where(is_ones_row, 1.0, vt[h * VP:(h + 1) * VP]).astype(BF16)


def _mla_prep(pqd, ckv, kr, tabs, g_qn, g_kvn, wqt, wqrt, wk, wvt, tm):
    t = pqd.shape[0]
    cos, sin, cost, sint = tabs
    row = lambda n: pl.BlockSpec((tm, n), lambda i: (i, 0))
    col = lambda n: pl.BlockSpec((n, tm), lambda i: (0, i))
    full = lambda a: pl.BlockSpec(a.shape, lambda i: (0,) * a.ndim)
    hd = H_MLA * HP
    return pl.pallas_call(
        functools.partial(_mla_prep_kernel, tm=tm),
        grid=(t // tm,),
        in_specs=[row(Q_LORA), row(KV_LORA), row(SEG_KR), row(HP), row(HP), col(HP), col(HP),
                  full(g_qn), full(g_kvn), full(wqt), full(wqrt), full(wk), full(wvt)],
        out_specs=[pl.BlockSpec((H_MLA, HP, tm), lambda i: (0, 0, i)),
                   pl.BlockSpec((H_MLA, tm, HP), lambda i: (0, i, 0)),
                   pl.BlockSpec((H_MLA, VP, tm), lambda i: (0, 0, i))],
        out_shape=[jax.ShapeDtypeStruct((H_MLA, HP, t), BF16),
                   jax.ShapeDtypeStruct((H_MLA, t, HP), BF16),
                   jax.ShapeDtypeStruct((H_MLA, VP, t), BF16)],
        compiler_params=_cparams(1),
        name="mla_prep",
    )(pqd, ckv, kr, cos, sin, cost, sint, g_qn, g_kvn, wqt, wqrt, wk, wvt)


def _scores(k, qt, m):
    s = jnp.dot(k, qt, preferred_element_type=F32)
    return s, jnp.maximum(m, jnp.max(s, axis=0, keepdims=True))


def _accumulate(s, vt, m, m_new, acc):
    p = jnp.exp2(s - m_new).astype(BF16)
    return jnp.exp2(m - m_new) * acc + jnp.dot(vt, p, preferred_element_type=F32)


def _normalized_rows(acc):
    o = acc / acc[ONES_ROW:ONES_ROW + 1, :]
    return jnp.concatenate([o, jnp.zeros((HP - VP, o.shape[1]), F32)], axis=0).T


def _lagged_pass(qt_ref, k_ref, vt_ref, m_ref, acc_ref, p_ref, sc_ref, *, ncb, n_kv, tk, cb, depth):
    k0 = k_ref[0:BF16_ROWS, :]
    for c in range(ncb):
        s0 = jnp.dot(k0, qt_ref[:, c * cb:(c + 1) * cb], preferred_element_type=F32)
        m_ref[c] = jnp.max(s0, axis=0, keepdims=True)
    acc_ref[...] = jnp.zeros(acc_ref.shape, F32)
    p_ref[...] = jnp.zeros(p_ref.shape, BF16)
    sc_ref[...] = jnp.ones(sc_ref.shape, F32)

    def carried(vt_prev):
        return [(ncb - depth + d, p_ref[d], sc_ref[d], vt_prev) for d in range(depth)]

    def drain(entry):
        pc, pp, psc, pvt = entry
        acc_ref[pc] = (acc_ref[pc] + jnp.dot(pvt, pp, preferred_element_type=F32)) * psc

    def step(j, jump):
        off = pl.multiple_of(j * tk, tk)
        off_prev = pl.multiple_of(jnp.maximum(j - 1, 0) * tk, tk)
        k = k_ref[pl.ds(off, tk), :]
        vt = vt_ref[:, pl.ds(off, tk)]
        pending = carried(vt_ref[:, pl.ds(off_prev, tk)])
        for c in range(ncb):
            m = m_ref[c]
            s = jnp.dot(k, qt_ref[:, c * cb:(c + 1) * cb], preferred_element_type=F32)
            p = jnp.exp2(s - m).astype(BF16)
            cm = jnp.max(s, axis=0, keepdims=True)
            m_new = jnp.maximum(m, cm)
            jump = jnp.maximum(jump, cm - m)
            m_ref[c] = m_new
            drain(pending.pop(0))
            pending.append((c, p, jnp.exp2(m - m_new), vt))
        for d, (_, pp, psc, _) in enumerate(pending):
            p_ref[d] = pp
            sc_ref[d] = psc
        return jump

    jump = lax.fori_loop(0, n_kv, step, jnp.zeros((1, cb), F32))
    for entry in carried(vt_ref[:, (n_kv - 1) * tk:]):
        drain(entry)
    return jump


def _exact_pass(qt_ref, k_ref, vt_ref, m_ref, acc_ref, s_ref, mp_ref, *, ncb, n_kv, tk, cb, depth):
    m_ref[...] = jnp.full(m_ref.shape, -1e30, F32)
    acc_ref[...] = jnp.zeros(acc_ref.shape, F32)
    s_ref[...] = jnp.full(s_ref.shape, -jnp.inf, F32)
    mp_ref[...] = jnp.full(mp_ref.shape, -1e30, F32)

    def carried(vt_prev):
        return [(ncb - depth + d, s_ref[d], mp_ref[d, 0], mp_ref[d, 1], vt_prev) for d in range(depth)]

    def drain(entry):
        pc, ps, pm_old, pm_new, pvt = entry
        acc_ref[pc] = _accumulate(ps, pvt, pm_old, pm_new, acc_ref[pc])

    def step(j, carry):
        off = pl.multiple_of(j * tk, tk)
        off_prev = pl.multiple_of(jnp.maximum(j - 1, 0) * tk, tk)
        k = k_ref[pl.ds(off, tk), :]
        vt = vt_ref[:, pl.ds(off, tk)]
        pending = carried(vt_ref[:, pl.ds(off_prev, tk)])
        for c in range(ncb):
            m_old = m_ref[c]
            s, m_new = _scores(k, qt_ref[:, c * cb:(c + 1) * cb], m_old)
            m_ref[c] = m_new
            drain(pending.pop(0))
            pending.append((c, s, m_old, m_new, vt))
        for d, (_, ps, pm_old, pm_new, _) in enumerate(pending):
            s_ref[d] = ps
            mp_ref[d, 0] = pm_old
            mp_ref[d, 1] = pm_new
        return carry

    lax.fori_loop(0, n_kv, step, 0)
    for entry in carried(vt_ref[:, (n_kv - 1) * tk:]):
        drain(entry)


def _attn_kernel(qt_ref, k_ref, vt_ref, o_ref, m_ref, acc_ref, s_ref, mp_ref, p_ref, sc_ref, *,
                 tq, tk, t, cb, depth):
    geom = dict(ncb=tq // cb, n_kv=t // tk, tk=tk, cb=cb, depth=depth)

    def write_out():
        for c in range(tq // cb):
            o_ref[c * cb:(c + 1) * cb, :] = _normalized_rows(acc_ref[c]).astype(o_ref.dtype)

    jump = _lagged_pass(qt_ref, k_ref, vt_ref, m_ref, acc_ref, p_ref, sc_ref, **geom)
    write_out()

    @pl.when(jnp.max(jump) > MAX_LAG_JUMP)
    def _():
        _exact_pass(qt_ref, k_ref, vt_ref, m_ref, acc_ref, s_ref, mp_ref, **geom)
        write_out()

    @pl.when(pl.program_id(1) == 0)
    def _():
        m0 = jnp.full((1, CTX), -1e30, F32)
        s, m_new = _scores(k_ref[0:CTX, :], qt_ref[:, 0:CTX], m0)
        acc_c = _accumulate(s, vt_ref[:, 0:CTX], m0, m_new, jnp.zeros((VP, CTX), F32))
        o_ref[0:CTX, :] = _normalized_rows(acc_c).astype(o_ref.dtype)


def _attention(qt, k, vt, tq, tk, cb, depth):
    t = k.shape[1]
    assert depth <= tq // cb
    return pl.pallas_call(
        functools.partial(_attn_kernel, tq=tq, tk=tk, t=t, cb=cb, depth=depth),
        grid=(H_MLA, t // tq),
        in_specs=[pl.BlockSpec((None, HP, tq), lambda h, i: (h, 0, i)),
                  pl.BlockSpec((None, t, HP), lambda h, i: (h, 0, 0)),
                  pl.BlockSpec((None, VP, t), lambda h, i: (h, 0, 0))],
        out_specs=pl.BlockSpec((tq, HP), lambda h, i: (i, h)),
        out_shape=jax.ShapeDtypeStruct((t, H_MLA * HP), BF16),
        scratch_shapes=[pltpu.VMEM((tq // cb, 1, cb), F32), pltpu.VMEM((tq // cb, VP, cb), F32),
                        pltpu.VMEM((depth, tk, cb), F32), pltpu.VMEM((depth, 2, 1, cb), F32),
                        pltpu.VMEM((depth, tk, cb), BF16), pltpu.VMEM((depth, 1, cb), F32)],
        compiler_params=_cparams(2),
        name="attention",
    )(qt, k, vt)


def _split3(x):
    hi = x.astype(BF16)
    r = x - hi.astype(F32)
    mid = r.astype(BF16)
    lo = (r - mid.astype(F32)).astype(BF16)
    return hi, mid, lo


def _log_sigmoid(x):
    return jnp.minimum(x, 0.0) - jnp.log(1.0 + jnp.exp(-jnp.abs(x)))


def _mlstm_kernel(q_ref, k_ref, v_ref, gc_ref, gr_ref, bc_ref, br_ref, o_ref, c_scr, m_scr, *, L, rev):
    @pl.when(pl.program_id(0) == 0)
    def _():
        c_scr[...] = jnp.zeros_like(c_scr)
        m_scr[...] = jnp.zeros_like(m_scr)

    row = lax.broadcasted_iota(jnp.int32, (L, L), 0)
    col = lax.broadcasted_iota(jnp.int32, (L, L), 1)
    mask = (row <= col) if rev else (row >= col)
    maskt = (row >= col) if rev else (row <= col)
    mask_b = mask.astype(BF16)
    maskt_b = maskt.astype(BF16)

    gc = gc_ref[...] + bc_ref[...]
    gr = gr_ref[...] + br_ref[...]
    lfc = _log_sigmoid(gc)
    lfr = _log_sigmoid(gr)
    b_col = sum(jnp.dot(mask_b, p, preferred_element_type=F32) for p in _split3(lfc))
    b_row = sum(jnp.dot(p, maskt_b, preferred_element_type=F32) for p in _split3(lfr))
    ones_blk = (lax.broadcasted_iota(jnp.int32, (L, LANES), 1) == 0).astype(BF16)

    for h in range(H_M):
        ci = 8 * rev + h
        cf = ci + 4
        b_c = b_col[:, cf:cf + 1]
        i_c = gc[:, ci:ci + 1]
        b_r = b_row[cf:cf + 1, :]
        i_r = gr[ci:ci + 1, :]
        bl = jnp.sum(lfc[:, cf:cf + 1], axis=0, keepdims=True)
        m = m_scr[h][0:1, 0:1]
        c_aug = c_scr[h]

        dlog = jnp.where(mask, b_c - (b_r - i_r), -jnp.inf)
        inter = b_c + m
        mj = jnp.maximum(inter, jnp.max(dlog, axis=-1, keepdims=True))
        w = jnp.exp(dlog - mj)
        sc = jnp.exp(inter - mj)

        q = q_ref[:, h * MQK:(h + 1) * MQK] * (MQK ** -0.5)
        k = k_ref[:, h * MQK:(h + 1) * MQK]
        v_aug = jnp.concatenate([v_ref[:, h * MV:(h + 1) * MV], ones_blk], axis=1)
        qk = lax.dot_general(q, k, _NT, preferred_element_type=F32) * w
        num = (jnp.dot(qk.astype(BF16), v_aug, preferred_element_type=F32)
               + sc * jnp.dot(q, c_aug.astype(BF16), preferred_element_type=F32))
        den = jnp.maximum(jnp.abs(num[:, MV:MV + 1]), jnp.exp(-mj))
        o_ref[:, h * MV:(h + 1) * MV] = num[:, :MV] / den

        wlog = bl - b_c + i_c
        m_new = jnp.maximum(bl + m, jnp.max(wlog, axis=0, keepdims=True))
        ws = jnp.exp(wlog - m_new)
        sd = jnp.exp(bl + m - m_new)
        wv = (ws * v_aug.astype(F32)).astype(BF16)
        c_scr[h] = sd * c_aug + lax.dot_general(k, wv, _TN, preferred_element_type=F32)
        m_scr[h] = jnp.broadcast_to(m_new, m_scr.shape[1:])


def _mlstm(mq, mk, mv, pg, pg_t, bg_c, bg_r, rev, L):
    t = mq.shape[0]
    nc = t // L
    if rev:
        idx = lambda c: jnp.where(c == 0, 0, nc - c)
    else:
        idx = lambda c: c
    return pl.pallas_call(
        functools.partial(_mlstm_kernel, L=L, rev=int(rev)),
        grid=(nc,),
        in_specs=[pl.BlockSpec((L, SEG_Q), lambda c: (idx(c), 0)),
                  pl.BlockSpec((L, SEG_K), lambda c: (idx(c), 0)),
                  pl.BlockSpec((L, SEG_V), lambda c: (idx(c), 0)),
                  pl.BlockSpec((L, LANES), lambda c: (idx(c), 0)),
                  pl.BlockSpec((16, L), lambda c: (0, idx(c))),
                  pl.BlockSpec((1, LANES), lambda c: (0, 0)),
                  pl.BlockSpec((16, 1), lambda c: (0, 0))],
        out_specs=pl.BlockSpec((L, H_M * MV), lambda c: (idx(c), 0)),
        out_shape=jax.ShapeDtypeStruct((t, H_M * MV), F32),
        scratch_shapes=[pltpu.VMEM((H_M, MQK, 2 * MV), F32), pltpu.VMEM((H_M, 8, LANES), F32)],
        compiler_params=_cparams(1),
        name="mlstm_rev" if rev else "mlstm_fwd",
    )(mq, mk, mv, pg, pg_t, bg_c, bg_r)


def _router(logits):
    tm = logits.shape[0]
    lane = lax.broadcasted_iota(jnp.int32, (tm, LANES), 1)
    big = jnp.int32(LANES)
    neg = -jnp.inf
    is_g = lane < N_GROUPS
    lg = jnp.where(is_g, logits, neg)
    gmax = jnp.max(lg, axis=-1, keepdims=True)
    grp = jnp.min(jnp.where(lg == gmax, lane, big), axis=-1, keepdims=True)
    p_grp = 1.0 / jnp.sum(jnp.where(is_g, jnp.exp(lg - gmax), 0.0), axis=-1, keepdims=True)
    e_lo = N_GROUPS + grp * EPG
    in_grp = (lane >= e_lo) & (lane < e_lo + EPG)
    le = jnp.where(in_grp, logits, neg)
    l1 = jnp.max(le, axis=-1, keepdims=True)
    i1 = jnp.min(jnp.where(le == l1, lane, big), axis=-1, keepdims=True)
    le2 = jnp.where(lane == i1, neg, le)
    l2 = jnp.max(le2, axis=-1, keepdims=True)
    i2 = jnp.min(jnp.where(le2 == l2, lane, big), axis=-1, keepdims=True)
    r = jnp.exp(l2 - l1)
    w1 = p_grp / (1.0 + r)
    w2 = w1 * r
    return jnp.where(lane == i1, w1, 0.0) + jnp.where(lane == i2, w2, 0.0), grp


def _merge_kernel(att_ref, hf_ref, hb_ref, po_ref, pm_ref, x_ref, mod_ref, gmh_ref, l1g_ref, l1b_ref,
                  wa_ref, wm_ref, wo_ref, wr_ref, br_ref, x1_ref, hm_ref, plan_ref, cnt_ref, *, tm):
    is_ctx = _is_ctx_rows(tm)
    y_mla = jnp.dot(att_ref[...], wa_ref[...], preferred_element_type=F32)
    hs = hf_ref[...] + hb_ref[...]
    parts = []
    for h in range(H_M):
        sl = slice(h * MV, (h + 1) * MV)
        parts.append(_ln(hs[:, sl]))
    hn = jnp.concatenate(parts, axis=1) * gmh_ref[...] * _sigmoid(po_ref[...].astype(F32))
    y_ml = jnp.dot(hn.astype(BF16), wm_ref[...], preferred_element_type=F32)
    g_a = _sigmoid(pm_ref[:, :D].astype(F32))
    g_b = _sigmoid(pm_ref[:, D:].astype(F32))
    y = jnp.dot((g_a * y_mla + g_b * y_ml).astype(BF16), wo_ref[...], preferred_element_type=F32)
    x1 = _ln(ALPHA * x_ref[...] + _mod(mod_ref, 2, is_ctx) * y) * l1g_ref[...] + l1b_ref[...]
    x1_ref[...] = x1
    hm = _ln(x1) * (1.0 + _mod(mod_ref, 4, is_ctx)) + _mod(mod_ref, 3, is_ctx)
    hm_ref[...] = hm.astype(BF16)
    h_hi = hm.astype(BF16)
    h_lo = (hm - h_hi.astype(F32)).astype(BF16)
    logits = (jnp.dot(h_hi, wr_ref[0], preferred_element_type=F32)
              + jnp.dot(h_hi, wr_ref[1], preferred_element_type=F32)
              + jnp.dot(h_lo, wr_ref[0], preferred_element_type=F32)) + br_ref[...]
    comb, grp = _router(logits)
    lane = lax.broadcasted_iota(jnp.int32, (tm, LANES), 1)
    in_group = lane == grp
    earlier = (lax.broadcasted_iota(jnp.int32, (tm, tm), 0)
               > lax.broadcasted_iota(jnp.int32, (tm, tm), 1)).astype(BF16)
    n_before = jnp.dot(earlier, in_group.astype(BF16), preferred_element_type=F32)
    rank = jnp.sum(jnp.where(in_group, n_before, 0.0), axis=-1, keepdims=True)
    plan_ref[...] = (comb + jnp.where(lane == PLAN_GROUP, grp.astype(F32), 0.0)
                     + jnp.where(lane == PLAN_RANK, rank, 0.0))
    cnt_ref[...] = jnp.broadcast_to(jnp.sum(in_group.astype(F32), axis=0, keepdims=True), cnt_ref.shape)


def _merge(att, hf, hb, po, pm, x_all, mod, g_mh, l1g, l1b, wa, wm, wo, wr, br, tm):
    t = x_all.shape[0]
    row = lambda n: pl.BlockSpec((tm, n), lambda i: (i, 0))
    full = lambda a: pl.BlockSpec(a.shape, lambda i: (0,) * a.ndim)
    return pl.pallas_call(
        functools.partial(_merge_kernel, tm=tm),
        grid=(t // tm,),
        in_specs=[row(H_MLA * HP), row(H_M * MV), row(H_M * MV), row(SEG_PO), row(SEG_PM), row(D),
                  full(mod), full(g_mh), full(l1g), full(l1b), full(wa), full(wm), full(wo), full(wr), full(br)],
        out_specs=[row(D), row(D), row(LANES), pl.BlockSpec((None, 8, LANES), lambda i: (i, 0, 0))],
        out_shape=[jax.ShapeDtypeStruct((t, D), F32), jax.ShapeDtypeStruct((t, D), BF16),
                   jax.ShapeDtypeStruct((t, LANES), F32), jax.ShapeDtypeStruct((t // tm, 8, LANES), F32)],
        compiler_params=_cparams(1),
        name="merge",
    )(att, hf, hb, po, pm, x_all, mod, g_mh, l1g, l1b, wa, wm, wo, wr, br)


def _moe_kernel(cnt_ref, hm_ref, plan_ref, plant_ref, x1_ref, mod_ref, l2g_ref, l2b_ref, wg_ref, wu_ref, wd_ref,
                o_ref, *, tm, tp, slots):
    i = pl.program_id(0)
    grp = pl.program_id(1)
    n_sub = tm // tp

    @pl.when(grp == 0)
    def _():
        o_ref[...] = jnp.zeros_like(o_ref)

    grp_f = grp.astype(F32)
    n_tok = cnt_ref[i * n_sub, grp]
    for j in range(1, n_sub):
        n_tok = jnp.maximum(n_tok, cnt_ref[i * n_sub + j, grp])
    n_rounds = (n_tok + (slots - 1)) // slots
    slot_rows = lax.broadcasted_iota(jnp.int32, (slots, tp), 0).astype(F32)
    slot_cols = lax.broadcasted_iota(jnp.int32, (tp, slots), 1).astype(F32)
    lane = lax.broadcasted_iota(jnp.int32, (n_sub * slots, LANES), 1)

    def one_round(r, carry):
        base = (r * slots).astype(F32)
        xs, cs = [], []
        for j in range(n_sub):
            rows = slice(j * tp, (j + 1) * tp)
            in_grp = plant_ref[PLAN_GROUP:PLAN_GROUP + 1, rows] == grp_f
            rank = plant_ref[PLAN_RANK:PLAN_RANK + 1, rows]
            gather = ((slot_rows == rank - base) & in_grp).astype(BF16)
            xs.append(jnp.dot(gather, hm_ref[rows, :], preferred_element_type=F32).astype(BF16))
            cs.append(sum(jnp.dot(gather, p, preferred_element_type=F32)
                          for p in _split3(plan_ref[rows, :])))
        xs = jnp.concatenate(xs, axis=0)
        cs = jnp.concatenate(cs, axis=0)
        hid = []
        for g in range(EPG):
            c_e = jnp.sum(jnp.where(lane == N_GROUPS + grp * EPG + g, cs, 0.0), axis=-1, keepdims=True)
            a = jnp.dot(xs, wg_ref[g], preferred_element_type=F32)
            u = jnp.dot(xs, wu_ref[g], preferred_element_type=F32)
            hid.append((a * _sigmoid(a) * u * c_e).astype(BF16))
        ys = jnp.dot(jnp.concatenate(hid, axis=1), wd_ref[...].reshape(EPG * D_EXP, D),
                     preferred_element_type=F32)
        ys_hi = ys.astype(BF16)
        ys_lo = (ys - ys_hi.astype(F32)).astype(BF16)
        for j in range(n_sub):
            rows = slice(j * tp, (j + 1) * tp)
            mine = slice(j * slots, (j + 1) * slots)
            in_grp = plan_ref[rows, PLAN_GROUP:PLAN_GROUP + 1] == grp_f
            rank = plan_ref[rows, PLAN_RANK:PLAN_RANK + 1]
            scatter = ((slot_cols == rank - base) & in_grp).astype(BF16)
            o_ref[rows, :] += (jnp.dot(scatter, ys_hi[mine], preferred_element_type=F32)
                               + jnp.dot(scatter, ys_lo[mine], preferred_element_type=F32))
        return carry

    lax.fori_loop(0, n_rounds, one_round, 0)

    @pl.when(grp == N_GROUPS - 1)
    def _():
        is_ctx = _is_ctx_rows(tm)
        z = ALPHA * x1_ref[...] + _mod(mod_ref, 5, is_ctx) * o_ref[...]
        o_ref[...] = _ln(z) * l2g_ref[...] + l2b_ref[...]


def _moe(counts, hm, plan, plan_t, x1, mod, l2g, l2b, wg, wu, wd, layer, tp):
    t = hm.shape[0]
    tm = tp * (MOE_PLAN_TILES if (t // tp) % MOE_PLAN_TILES == 0 else 1)
    row = lambda n, **kw: pl.BlockSpec((tm, n), lambda i, e, cnt: (i, 0), **kw)
    full = lambda a: pl.BlockSpec(a.shape, lambda i, e, cnt: (0,) * a.ndim)
    group_of = lambda a: pl.BlockSpec((None, EPG) + a.shape[2:], lambda i, e, cnt: (layer, e, 0, 0))
    return pl.pallas_call(
        functools.partial(_moe_kernel, tm=tm, tp=tp, slots=_moe_slots(tp)),
        grid_spec=pltpu.PrefetchScalarGridSpec(
            num_scalar_prefetch=1,
            grid=(t // tm, N_GROUPS),
            in_specs=[row(D), row(LANES), pl.BlockSpec((2, tm), lambda i, e, cnt: (0, i)),
                      row(D, pipeline_mode=pl.Buffered(1)),
                      full(mod), full(l2g), full(l2b), group_of(wg), group_of(wu), group_of(wd)],
            out_specs=row(D)),
        out_shape=jax.ShapeDtypeStruct((t, D), F32),
        compiler_params=_cparams(2, VMEM_LIMIT_MOE),
        name="moe",
    )(counts, hm, plan, plan_t, x1, mod, l2g, l2b, wg, wu, wd)


def _rot_cols(w):
    a1, a2, b1, b2 = jnp.split(w, 4, axis=-1)
    return jnp.concatenate([-a2, a1, -b2, b1], axis=-1)


def _place_rope(w):
    return jnp.pad(w, ((0, 0), (QK_NOPE, HP - QK_NOPE - QK_ROPE)))


def _rope_tables(t):
    rows = (t - CTX) // GRID_W
    row, col = jnp.meshgrid(jnp.arange(rows, dtype=F32), jnp.arange(GRID_W, dtype=F32), indexing="ij")
    row, col = row.reshape(-1), col.reshape(-1)
    half = QK_ROPE // 2
    inv = ROPE_BASE ** (-jnp.arange(0, half, 2, dtype=F32) / half)
    ar, ac = row[:, None] * inv, col[:, None] * inv
    ang = jnp.concatenate([ar, ar, ac, ac], axis=-1)
    ang = jnp.concatenate([jnp.zeros((CTX, QK_ROPE), F32), ang], axis=0)
    pad = ((0, 0), (QK_NOPE, HP - QK_NOPE - QK_ROPE))
    cos = jnp.pad(jnp.cos(ang), pad) + jnp.pad(jnp.ones((t, QK_NOPE), F32), ((0, 0), (0, HP - QK_NOPE)))
    sin = jnp.pad(jnp.sin(ang), pad)
    return cos, sin, cos.T, sin.T


def _layer_weights(l, w_in, w_uq, w_uk, w_uv, w_bo_mla, w_bo_mlstm, w_out, w_rg, b_rg, w_re, b_re):
    wi = w_in[l]
    o = 0
    segs = []
    for n in (Q_LORA, KV_LORA, QK_ROPE, SEG_Q, SEG_K, SEG_V, SEG_PO, 4 * H_M, SEG_PM):
        segs.append(wi[:, o:o + n])
        o += n
    s_pqd, s_ckv, s_kr, s_q, s_k, s_v, s_po, s_pg, s_pm = segs
    w_in_p = jnp.concatenate(
        [s_pqd, s_ckv, _place_rope(s_kr), _place_rope(_rot_cols(s_kr)), s_q, s_k, s_v, s_po,
         jnp.pad(s_pg, ((0, 0), (0, SEG_PG - 4 * H_M))), s_pm], axis=1).astype(BF16)

    uq = w_uq[l].reshape(Q_LORA, H_MLA, QK_NOPE + QK_ROPE)
    padh = ((0, 0), (0, 0), (0, HP - QK_NOPE - QK_ROPE))
    wq = jnp.pad(uq, padh).reshape(Q_LORA, H_MLA * HP)
    wqr = jnp.pad(jnp.concatenate([jnp.zeros_like(uq[..., :QK_NOPE]), _rot_cols(uq[..., QK_NOPE:])], axis=-1),
                  padh).reshape(Q_LORA, H_MLA * HP)
    padv = ((0, 0), (0, 0), (0, HP - V_HEAD))
    wk = jnp.pad(w_uk[l].reshape(KV_LORA, H_MLA, QK_NOPE), padv).reshape(KV_LORA, H_MLA * HP)
    wv = jnp.pad(w_uv[l].reshape(KV_LORA, H_MLA, V_HEAD), ((0, 0), (0, 0), (0, VP - V_HEAD))).reshape(KV_LORA, H_MLA * VP)
    wa = jnp.pad(w_bo_mla[l].reshape(H_MLA, V_HEAD, D), ((0, 0), (0, HP - V_HEAD), (0, 0))).reshape(H_MLA * HP, D)

    wr = jnp.pad(jnp.concatenate([w_rg[l], w_re[l]], axis=1), ((0, 0), (0, LANES - N_GROUPS - N_EXP)))
    wr_hi = wr.astype(BF16)
    wr_lo = (wr - wr_hi.astype(F32)).astype(BF16)
    br = jnp.pad(jnp.concatenate([b_rg[l], b_re[l]]), (0, LANES - N_GROUPS - N_EXP)).reshape(1, LANES)
    return dict(
        w_in=w_in_p, wqt=wq.T.astype(BF16), wqrt=wqr.T.astype(BF16), wk=wk.astype(BF16), wvt=wv.T.astype(BF16),
        wa=wa.astype(BF16), wm=w_bo_mlstm[l].astype(BF16), wo=w_out[l].astype(BF16),
        wr=jnp.stack([wr_hi, wr_lo]), br=br)


def kernel(x, c, ctx, c_ctx, w_ada, b_ada, w_in, b_gates, w_uq, w_uk, w_uv, g_qn, g_kvn, g_mh, w_bo_mla,
           w_bo_mlstm, w_out, ln1_g, ln1_b, w_rg, b_rg, w_re, b_re, w_e_gate, w_e_up, w_e_down, ln2_g, ln2_b):
    assert x.shape[0] == 1 and c.shape[0] == 1 and ctx.shape[1] == CTX
    x_all = jnp.concatenate([ctx[0], x[0]], axis=0)
    t = x_all.shape[0]
    tm = ROW_TILE if t % ROW_TILE == 0 else M_CHUNK
    tk = ATT_TK if t % ATT_TK == 0 else M_CHUNK
    assert t % tm == 0 and t % ATT_TQ == 0 and t % tk == 0 and t % M_CHUNK == 0

    cc = jnp.pad(jnp.concatenate([c, c_ctx[None]], axis=0), ((0, 6), (0, 0)))
    mods = _ada(cc, w_ada, b_ada)
    tabs = _rope_tables(t)
    row2 = lambda a: a.reshape(1, -1)
    wg_all, wu_all, wd_all = w_e_gate.astype(BF16), w_e_up.astype(BF16), w_e_down.astype(BF16)

    for l in range(DEPTH):
        w = _layer_weights(l, w_in, w_uq, w_uk, w_uv, w_bo_mla, w_bo_mlstm, w_out, w_rg, b_rg, w_re, b_re)
        mod = mods[l]
        pqd, ckv, kr, mq, mk, mv, po, pg, pm = _inproj(x_all, mod, w["w_in"], tm)
        qt, kk, vt = _mla_prep(pqd, ckv, kr, tabs, row2(g_qn[l]), row2(g_kvn[l]),
                               w["wqt"], w["wqrt"], w["wk"], w["wvt"], tm)
        att = _attention(qt, kk, vt, ATT_TQ, tk, ATT_CB, ATT_DEPTH)
        pg_t = pg[:, :16].T
        bg_c = jnp.pad(b_gates[l], (0, LANES - 16)).reshape(1, LANES)
        bg_r = b_gates[l].reshape(16, 1)
        hf = _mlstm(mq, mk, mv, pg, pg_t, bg_c, bg_r, False, M_CHUNK)
        hb = _mlstm(mq, mk, mv, pg, pg_t, bg_c, bg_r, True, M_CHUNK)
        x1, hm, plan, cnt = _merge(att, hf, hb, po, pm, x_all, mod, row2(g_mh[l]), row2(ln1_g[l]), row2(ln1_b[l]),
                                   w["wa"], w["wm"], w["wo"], w["wr"], w["br"], tm)
        counts = cnt[:, 0, :N_GROUPS].astype(jnp.int32)
        plan_t = plan[:, :2].T
        x_all = _moe(counts, hm, plan, plan_t, x1, mod, row2(ln2_g[l]), row2(ln2_b[l]),
                     wg_all, wu_all, wd_all, l, tm)
    return x_all[CTX:][None]
```

```python
import functools
import math

import jax
import jax.numpy as jnp
from jax import lax
from jax.experimental import pallas as pl
from jax.experimental.pallas import tpu as pltpu

F32 = jnp.float32
BF16 = jnp.bfloat16

D = 1024
DEPTH = 4
GRID_W = 64
CTX = 256
H_MLA = 8
QK_NOPE = 64
QK_ROPE = 32
V_HEAD = 64
Q_LORA = 384
KV_LORA = 256
ROPE_BASE = 10000.0
MLA_SCALE = (QK_NOPE + QK_ROPE) ** -0.5
H_M = 4
MQK = 64
MV = 128
N_GROUPS = 4
EPG = 8
N_EXP = N_GROUPS * EPG
D_EXP = 256
ALPHA = (2 * DEPTH) ** 0.25
LN_EPS = 1e-6

LANES = 128
VMEM_LIMIT = 56 * 1024 * 1024
VMEM_LIMIT_MOE = 58 * 1024 * 1024

HP = LANES
ONES_ROW = V_HEAD
BF16_ROWS = 16
VP = (V_HEAD + 1 + BF16_ROWS - 1) // BF16_ROWS * BF16_ROWS
QSCALE = MLA_SCALE * math.log2(math.e)

SEG_PQD = Q_LORA
SEG_CKV = KV_LORA
SEG_KR = 2 * HP
SEG_Q = H_M * MQK
SEG_K = H_M * MQK
SEG_V = H_M * MV
SEG_PO = H_M * MV
SEG_PG = LANES
SEG_PM = 2 * D
IN_SEGS = (SEG_PQD, SEG_CKV, SEG_KR, SEG_Q, SEG_K, SEG_V, SEG_PO, SEG_PG, SEG_PM)
IN_DTYPES = (F32, F32, F32, BF16, BF16, BF16, BF16, F32, BF16)
NP_IN = sum(IN_SEGS)

ROW_TILE = 640
ATT_TQ = 3328
ATT_CB = 256
ATT_TK = 1280
MAX_LAG_JUMP = 64.0
ATT_DEPTH = 2
M_CHUNK = 256
MOE_SIZE_FACTORS = (0.7, 1.0, 1.3, 1.9)
MOE_PLAN_TILES = 2
PLAN_GROUP = 0
PLAN_RANK = 1


def _moe_slot_sizes(tm):
    mean = tm / N_GROUPS
    sizes = sorted({min(tm, int(math.ceil(f * mean / BF16_ROWS)) * BF16_ROWS) for f in MOE_SIZE_FACTORS})
    return tuple(sizes)

_NT = (((1,), (1,)), ((), ()))
_TN = (((0,), (0,)), ((), ()))


def _cparams(n_grid, vmem_limit=VMEM_LIMIT):
    return pltpu.CompilerParams(dimension_semantics=("arbitrary",) * n_grid, vmem_limit_bytes=vmem_limit)


def _ln(x):
    mu = jnp.mean(x, axis=-1, keepdims=True)
    xc = x - mu
    var = jnp.mean(xc * xc, axis=-1, keepdims=True)
    return xc * lax.rsqrt(var + LN_EPS)


def _rms(x, g):
    return x * lax.rsqrt(jnp.mean(x * x, axis=-1, keepdims=True) + LN_EPS) * g


def _sigmoid(x):
    return 1.0 / (1.0 + jnp.exp(-x))


def _is_ctx_rows(tm):
    rows = pl.program_id(0) * tm + lax.broadcasted_iota(jnp.int32, (tm, 1), 0)
    return rows < CTX


def _mod(mod_ref, k, is_ctx):
    lat = mod_ref[0:1, k * D:(k + 1) * D]
    ctx = mod_ref[1:2, k * D:(k + 1) * D]
    return jnp.where(is_ctx, ctx, lat)


def _ada_kernel(c_ref, w_ref, b_ref, o_ref):
    c = c_ref[...]
    s = c * _sigmoid(c)
    o_ref[...] = jnp.dot(s, w_ref[...], preferred_element_type=F32,
                         precision=lax.Precision.HIGHEST) + b_ref[...]


def _ada(cc, w_ada, b_ada):
    tn = 1536
    n = 6 * D
    return pl.pallas_call(
        _ada_kernel,
        grid=(DEPTH, n // tn),
        in_specs=[pl.BlockSpec((8, D), lambda l, j: (0, 0)),
                  pl.BlockSpec((None, D, tn), lambda l, j: (l, 0, j)),
                  pl.BlockSpec((None, 1, tn), lambda l, j: (l, 0, j))],
        out_specs=pl.BlockSpec((None, 8, tn), lambda l, j: (l, 0, j)),
        out_shape=jax.ShapeDtypeStruct((DEPTH, 8, n), F32),
        compiler_params=_cparams(2),
        name="ada",
    )(cc, w_ada, b_ada.reshape(DEPTH, 1, n))


def _inproj_kernel(x_ref, mod_ref, w_ref, *o_refs, tm):
    is_ctx = _is_ctx_rows(tm)
    h = _ln(x_ref[...]) * (1.0 + _mod(mod_ref, 1, is_ctx)) + _mod(mod_ref, 0, is_ctx)
    hb = h.astype(BF16)
    off = 0
    for o in o_refs:
        n = o.shape[-1]
        o[...] = jnp.dot(hb, w_ref[:, off:off + n], preferred_element_type=F32).astype(o.dtype)
        off += n


def _inproj(x_all, mod, w_in_p, tm):
    t = x_all.shape[0]
    return pl.pallas_call(
        functools.partial(_inproj_kernel, tm=tm),
        grid=(t // tm,),
        in_specs=[pl.BlockSpec((tm, D), lambda i: (i, 0)),
                  pl.BlockSpec((8, 6 * D), lambda i: (0, 0)),
                  pl.BlockSpec((D, NP_IN), lambda i: (0, 0))],
        out_specs=[pl.BlockSpec((tm, n), lambda i: (i, 0)) for n in IN_SEGS],
        out_shape=[jax.ShapeDtypeStruct((t, n), dt) for n, dt in zip(IN_SEGS, IN_DTYPES)],
        compiler_params=_cparams(1),
        name="inproj",
    )(x_all, mod, w_in_p)


def _mla_prep_kernel(pqd_ref, ckv_ref, kr_ref, cos_ref, sin_ref, cost_ref, sint_ref, gq_ref, gk_ref,
                     wqt_ref, wqrt_ref, wk_ref, wvt_ref, qt_ref, k_ref, vt_ref, *, tm):
    qn = _rms(pqd_ref[...], gq_ref[...]).astype(BF16)
    qa = lax.dot_general(wqt_ref[...], qn, _NT, preferred_element_type=F32)
    qb = lax.dot_general(wqrt_ref[...], qn, _NT, preferred_element_type=F32)
    cost = cost_ref[...]
    sint = sint_ref[...]
    ckv = _rms(ckv_ref[...], gk_ref[...]).astype(BF16)
    kn = jnp.dot(ckv, wk_ref[...], preferred_element_type=F32)
    kr = kr_ref[:, :HP] * cos_ref[...] + kr_ref[:, HP:] * sin_ref[...]
    vt = lax.dot_general(wvt_ref[...], ckv, _NT, preferred_element_type=F32)
    is_ones_row = lax.broadcasted_iota(jnp.int32, (VP, tm), 0) == ONES_ROW
    for h in range(H_MLA):
        sl = slice(h * HP, (h + 1) * HP)
        qt_ref[h] = ((qa[sl] * cost + qb[sl] * sint) * QSCALE).astype(BF16)
        k_ref[h] = (kn[:, sl] + kr).astype(BF16)
        vt_ref[h] = jnp.where(is_ones_row, 1.0, vt[h * VP:(h + 1) * VP]).astype(BF16)


def _mla_prep(pqd, ckv, kr, tabs, g_qn, g_kvn, wqt, wqrt, wk, wvt, tm):
    t = pqd.shape[0]
    cos, sin, cost, sint = tabs
    row = lambda n: pl.BlockSpec((tm, n), lambda i: (i, 0))
    col = lambda n: pl.BlockSpec((n, tm), lambda i: (0, i))
    full = lambda a: pl.BlockSpec(a.shape, lambda i: (0,) * a.ndim)
    hd = H_MLA * HP
    return pl.pallas_call(
        functools.partial(_mla_prep_kernel, tm=tm),
        grid=(t // tm,),
        in_specs=[row(Q_LORA), row(KV_LORA), row(SEG_KR), row(HP), row(HP), col(HP), col(HP),
                  full(g_qn), full(g_kvn), full(wqt), full(wqrt), full(wk), full(wvt)],
        out_specs=[pl.BlockSpec((H_MLA, HP, tm), lambda i: (0, 0, i)),
                   pl.BlockSpec((H_MLA, tm, HP), lambda i: (0, i, 0)),
                   pl.BlockSpec((H_MLA, VP, tm), lambda i: (0, 0, i))],
        out_shape=[jax.ShapeDtypeStruct((H_MLA, HP, t), BF16),
                   jax.ShapeDtypeStruct((H_MLA, t, HP), BF16),
                   jax.ShapeDtypeStruct((H_MLA, VP, t), BF16)],
        compiler_params=_cparams(1),
        name="mla_prep",
    )(pqd, ckv, kr, cos, sin, cost, sint, g_qn, g_kvn, wqt, wqrt, wk, wvt)


def _scores(k, qt, m):
    s = jnp.dot(k, qt, preferred_element_type=F32)
    return s, jnp.maximum(m, jnp.max(s, axis=0, keepdims=True))


def _accumulate(s, vt, m, m_new, acc):
    p = jnp.exp2(s - m_new).astype(BF16)
    return jnp.exp2(m - m_new) * acc + jnp.dot(vt, p, preferred_element_type=F32)


def _normalized_rows(acc):
    o = acc / acc[ONES_ROW:ONES_ROW + 1, :]
    return jnp.concatenate([o, jnp.zeros((HP - VP, o.shape[1]), F32)], axis=0).T


def _lagged_pass(qt_ref, k_ref, vt_ref, m_ref, acc_ref, p_ref, sc_ref, *, ncb, n_kv, tk, cb, depth):
    k0 = k_ref[0:BF16_ROWS, :]
    for c in range(ncb):
        s0 = jnp.dot(k0, qt_ref[:, c * cb:(c + 1) * cb], preferred_element_type=F32)
        m_ref[c] = jnp.max(s0, axis=0, keepdims=True)
    acc_ref[...] = jnp.zeros(acc_ref.shape, F32)
    p_ref[...] = jnp.zeros(p_ref.shape, BF16)
    sc_ref[...] = jnp.ones(sc_ref.shape, F32)

    def carried(vt_prev):
        return [(ncb - depth + d, p_ref[d], sc_ref[d], vt_prev) for d in range(depth)]

    def drain(entry):
        pc, pp, psc, pvt = entry
        acc_ref[pc] = (acc_ref[pc] + jnp.dot(pvt, pp, preferred_element_type=F32)) * psc

    def step(j, jump):
        off = pl.multiple_of(j * tk, tk)
        off_prev = pl.multiple_of(jnp.maximum(j - 1, 0) * tk, tk)
        k = k_ref[pl.ds(off, tk), :]
        vt = vt_ref[:, pl.ds(off, tk)]
        pending = carried(vt_ref[:, pl.ds(off_prev, tk)])
        for c in range(ncb):
            m = m_ref[c]
            s = jnp.dot(k, qt_ref[:, c * cb:(c + 1) * cb], preferred_element_type=F32)
            p = jnp.exp2(s - m).astype(BF16)
            cm = jnp.max(s, axis=0, keepdims=True)
            m_new = jnp.maximum(m, cm)
            jump = jnp.maximum(jump, cm - m)
            m_ref[c] = m_new
            drain(pending.pop(0))
            pending.append((c, p, jnp.exp2(m - m_new), vt))
        for d, (_, pp, psc, _) in enumerate(pending):
            p_ref[d] = pp
            sc_ref[d] = psc
        return jump

    jump = lax.fori_loop(0, n_kv, step, jnp.zeros((1, cb), F32))
    for entry in carried(vt_ref[:, (n_kv - 1) * tk:]):
        drain(entry)
    return jump


def _exact_pass(qt_ref, k_ref, vt_ref, m_ref, acc_ref, s_ref, mp_ref, *, ncb, n_kv, tk, cb, depth):
    m_ref[...] = jnp.full(m_ref.shape, -1e30, F32)
    acc_ref[...] = jnp.zeros(acc_ref.shape, F32)
    s_ref[...] = jnp.full(s_ref.shape, -jnp.inf, F32)
    mp_ref[...] = jnp.full(mp_ref.shape, -1e30, F32)

    def carried(vt_prev):
        return [(ncb - depth + d, s_ref[d], mp_ref[d, 0], mp_ref[d, 1], vt_prev) for d in range(depth)]

    def drain(entry):
        pc, ps, pm_old, pm_new, pvt = entry
        acc_ref[pc] = _accumulate(ps, pvt, pm_old, pm_new, acc_ref[pc])

    def step(j, carry):
        off = pl.multiple_of(j * tk, tk)
        off_prev = pl.multiple_of(jnp.maximum(j - 1, 0) * tk, tk)
        k = k_ref[pl.ds(off, tk), :]
        vt = vt_ref[:, pl.ds(off, tk)]
        pending = carried(vt_ref[:, pl.ds(off_prev, tk)])
        for c in range(ncb):
            m_old = m_ref[c]
            s, m_new = _scores(k, qt_ref[:, c * cb:(c + 1) * cb], m_old)
            m_ref[c] = m_new
            drain(pending.pop(0))
            pending.append((c, s, m_old, m_new, vt))
        for d, (_, ps, pm_old, pm_new, _) in enumerate(pending):
            s_ref[d] = ps
            mp_ref[d, 0] = pm_old
            mp_ref[d, 1] = pm_new
        return carry

    lax.fori_loop(0, n_kv, step, 0)
    for entry in carried(vt_ref[:, (n_kv - 1) * tk:]):
        drain(entry)


def _attn_kernel(qt_ref, k_ref, vt_ref, o_ref, m_ref, acc_ref, s_ref, mp_ref, p_ref, sc_ref, *,
                 tq, tk, t, cb, depth):
    geom = dict(ncb=tq // cb, n_kv=t // tk, tk=tk, cb=cb, depth=depth)

    def write_out():
        for c in range(tq // cb):
            o_ref[c * cb:(c + 1) * cb, :] = _normalized_rows(acc_ref[c]).astype(o_ref.dtype)

    jump = _lagged_pass(qt_ref, k_ref, vt_ref, m_ref, acc_ref, p_ref, sc_ref, **geom)
    write_out()

    @pl.when(jnp.max(jump) > MAX_LAG_JUMP)
    def _():
        _exact_pass(qt_ref, k_ref, vt_ref, m_ref, acc_ref, s_ref, mp_ref, **geom)
        write_out()

    @pl.when(pl.program_id(1) == 0)
    def _():
        m0 = jnp.full((1, CTX), -1e30, F32)
        s, m_new = _scores(k_ref[0:CTX, :], qt_ref[:, 0:CTX], m0)
        acc_c = _accumulate(s, vt_ref[:, 0:CTX], m0, m_new, jnp.zeros((VP, CTX), F32))
        o_ref[0:CTX, :] = _normalized_rows(acc_c).astype(o_ref.dtype)


def _attention(qt, k, vt, tq, tk, cb, depth):
    t = k.shape[1]
    assert depth <= tq // cb
    return pl.pallas_call(
        functools.partial(_attn_kernel, tq=tq, tk=tk, t=t, cb=cb, depth=depth),
        grid=(H_MLA, t // tq),
        in_specs=[pl.BlockSpec((None, HP, tq), lambda h, i: (h, 0, i)),
                  pl.BlockSpec((None, t, HP), lambda h, i: (h, 0, 0)),
                  pl.BlockSpec((None, VP, t), lambda h, i: (h, 0, 0))],
        out_specs=pl.BlockSpec((tq, HP), lambda h, i: (i, h)),
        out_shape=jax.ShapeDtypeStruct((t, H_MLA * HP), BF16),
        scratch_shapes=[pltpu.VMEM((tq // cb, 1, cb), F32), pltpu.VMEM((tq // cb, VP, cb), F32),
                        pltpu.VMEM((depth, tk, cb), F32), pltpu.VMEM((depth, 2, 1, cb), F32),
                        pltpu.VMEM((depth, tk, cb), BF16), pltpu.VMEM((depth, 1, cb), F32)],
        compiler_params=_cparams(2),
        name="attention",
    )(qt, k, vt)


def _split3(x):
    hi = x.astype(BF16)
    r = x - hi.astype(F32)
    mid = r.astype(BF16)
    lo = (r - mid.astype(F32)).astype(BF16)
    return hi, mid, lo


def _log_sigmoid(x):
    return jnp.minimum(x, 0.0) - jnp.log(1.0 + jnp.exp(-jnp.abs(x)))


def _mlstm_kernel(q_ref, k_ref, v_ref, gc_ref, gr_ref, bc_ref, br_ref, o_ref, c_scr, m_scr, *, L, rev):
    @pl.when(pl.program_id(0) == 0)
    def _():
        c_scr[...] = jnp.zeros_like(c_scr)
        m_scr[...] = jnp.zeros_like(m_scr)

    row = lax.broadcasted_iota(jnp.int32, (L, L), 0)
    col = lax.broadcasted_iota(jnp.int32, (L, L), 1)
    mask = (row <= col) if rev else (row >= col)
    maskt = (row >= col) if rev else (row <= col)
    mask_b = mask.astype(BF16)
    maskt_b = maskt.astype(BF16)

    gc = gc_ref[...] + bc_ref[...]
    gr = gr_ref[...] + br_ref[...]
    lfc = _log_sigmoid(gc)
    lfr = _log_sigmoid(gr)
    b_col = sum(jnp.dot(mask_b, p, preferred_element_type=F32) for p in _split3(lfc))
    b_row = sum(jnp.dot(p, maskt_b, preferred_element_type=F32) for p in _split3(lfr))
    ones_blk = (lax.broadcasted_iota(jnp.int32, (L, LANES), 1) == 0).astype(BF16)

    for h in range(H_M):
        ci = 8 * rev + h
        cf = ci + 4
        b_c = b_col[:, cf:cf + 1]
        i_c = gc[:, ci:ci + 1]
        b_r = b_row[cf:cf + 1, :]
        i_r = gr[ci:ci + 1, :]
        bl = jnp.sum(lfc[:, cf:cf + 1], axis=0, keepdims=True)
        m = m_scr[h][0:1, 0:1]
        c_aug = c_scr[h]

        dlog = jnp.where(mask, b_c - (b_r - i_r), -jnp.inf)
        inter = b_c + m
        mj = jnp.maximum(inter, jnp.max(dlog, axis=-1, keepdims=True))
        w = jnp.exp(dlog - mj)
        sc = jnp.exp(inter - mj)

        q = q_ref[:, h * MQK:(h + 1) * MQK] * (MQK ** -0.5)
        k = k_ref[:, h * MQK:(h + 1) * MQK]
        v_aug = jnp.concatenate([v_ref[:, h * MV:(h + 1) * MV], ones_blk], axis=1)
        qk = lax.dot_general(q, k, _NT, preferred_element_type=F32) * w
        num = (jnp.dot(qk.astype(BF16), v_aug, preferred_element_type=F32)
               + sc * jnp.dot(q, c_aug.astype(BF16), preferred_element_type=F32))
        den = jnp.maximum(jnp.abs(num[:, MV:MV + 1]), jnp.exp(-mj))
        o_ref[:, h * MV:(h + 1) * MV] = num[:, :MV] / den

        wlog = bl - b_c + i_c
        m_new = jnp.maximum(bl + m, jnp.max(wlog, axis=0, keepdims=True))
        ws = jnp.exp(wlog - m_new)
        sd = jnp.exp(bl + m - m_new)
        wv = (ws * v_aug.astype(F32)).astype(BF16)
        c_scr[h] = sd * c_aug + lax.dot_general(k, wv, _TN, preferred_element_type=F32)
        m_scr[h] = jnp.broadcast_to(m_new, m_scr.shape[1:])


def _mlstm(mq, mk, mv, pg, pg_t, bg_c, bg_r, rev, L):
    t = mq.shape[0]
    nc = t // L
    if rev:
        idx = lambda c: jnp.where(c == 0, 0, nc - c)
    else:
        idx = lambda c: c
    return pl.pallas_call(
        functools.partial(_mlstm_kernel, L=L, rev=int(rev)),
        grid=(nc,),
        in_specs=[pl.BlockSpec((L, SEG_Q), lambda c: (idx(c), 0)),
                  pl.BlockSpec((L, SEG_K), lambda c: (idx(c), 0)),
                  pl.BlockSpec((L, SEG_V), lambda c: (idx(c), 0)),
                  pl.BlockSpec((L, LANES), lambda c: (idx(c), 0)),
                  pl.BlockSpec((16, L), lambda c: (0, idx(c))),
                  pl.BlockSpec((1, LANES), lambda c: (0, 0)),
                  pl.BlockSpec((16, 1), lambda c: (0, 0))],
        out_specs=pl.BlockSpec((L, H_M * MV), lambda c: (idx(c), 0)),
        out_shape=jax.ShapeDtypeStruct((t, H_M * MV), F32),
        scratch_shapes=[pltpu.VMEM((H_M, MQK, 2 * MV), F32), pltpu.VMEM((H_M, 8, LANES), F32)],
        compiler_params=_cparams(1),
        name="mlstm_rev" if rev else "mlstm_fwd",
    )(mq, mk, mv, pg, pg_t, bg_c, bg_r)


def _router(logits):
    tm = logits.shape[0]
    lane = lax.broadcasted_iota(jnp.int32, (tm, LANES), 1)
    big = jnp.int32(LANES)
    neg = -jnp.inf
    is_g = lane < N_GROUPS
    lg = jnp.where(is_g, logits, neg)
    gmax = jnp.max(lg, axis=-1, keepdims=True)
    grp = jnp.min(jnp.where(lg == gmax, lane, big), axis=-1, keepdims=True)
    p_grp = 1.0 / jnp.sum(jnp.where(is_g, jnp.exp(lg - gmax), 0.0), axis=-1, keepdims=True)
    e_lo = N_GROUPS + grp * EPG
    in_grp = (lane >= e_lo) & (lane < e_lo + EPG)
    le = jnp.where(in_grp, logits, neg)
    l1 = jnp.max(le, axis=-1, keepdims=True)
    i1 = jnp.min(jnp.where(le == l1, lane, big), axis=-1, keepdims=True)
    le2 = jnp.where(lane == i1, neg, le)
    l2 = jnp.max(le2, axis=-1, keepdims=True)
    i2 = jnp.min(jnp.where(le2 == l2, lane, big), axis=-1, keepdims=True)
    r = jnp.exp(l2 - l1)
    w1 = p_grp / (1.0 + r)
    w2 = w1 * r
    return jnp.where(lane == i1, w1, 0.0) + jnp.where(lane == i2, w2, 0.0), grp


def _merge_kernel(att_ref, hf_ref, hb_ref, po_ref, pm_ref, x_ref, mod_ref, gmh_ref, l1g_ref, l1b_ref,
                  wa_ref, wm_ref, wo_ref, wr_ref, br_ref, x1_ref, hm_ref, plan_ref, cnt_ref, *, tm):
    is_ctx = _is_ctx_rows(tm)
    y_mla = jnp.dot(att_ref[...], wa_ref[...], preferred_element_type=F32)
    hs = hf_ref[...] + hb_ref[...]
    parts = []
    for h in range(H_M):
        sl = slice(h * MV, (h + 1) * MV)
        parts.append(_ln(hs[:, sl]))
    hn = jnp.concatenate(parts, axis=1) * gmh_ref[...] * _sigmoid(po_ref[...].astype(F32))
    y_ml = jnp.dot(hn.astype(BF16), wm_ref[...], preferred_element_type=F32)
    g_a = _sigmoid(pm_ref[:, :D].astype(F32))
    g_b = _sigmoid(pm_ref[:, D:].astype(F32))
    y = jnp.dot((g_a * y_mla + g_b * y_ml).astype(BF16), wo_ref[...], preferred_element_type=F32)
    x1 = _ln(ALPHA * x_ref[...] + _mod(mod_ref, 2, is_ctx) * y) * l1g_ref[...] + l1b_ref[...]
    x1_ref[...] = x1
    hm = _ln(x1) * (1.0 + _mod(mod_ref, 4, is_ctx)) + _mod(mod_ref, 3, is_ctx)
    hm_ref[...] = hm.astype(BF16)
    h_hi = hm.astype(BF16)
    h_lo = (hm - h_hi.astype(F32)).astype(BF16)
    logits = (jnp.dot(h_hi, wr_ref[0], preferred_element_type=F32)
              + jnp.dot(h_hi, wr_ref[1], preferred_element_type=F32)
              + jnp.dot(h_lo, wr_ref[0], preferred_element_type=F32)) + br_ref[...]
    comb, grp = _router(logits)
    lane = lax.broadcasted_iota(jnp.int32, (tm, LANES), 1)
    in_group = lane == grp
    earlier = (lax.broadcasted_iota(jnp.int32, (tm, tm), 0)
               > lax.broadcasted_iota(jnp.int32, (tm, tm), 1)).astype(BF16)
    n_before = jnp.dot(earlier, in_group.astype(BF16), preferred_element_type=F32)
    rank = jnp.sum(jnp.where(in_group, n_before, 0.0), axis=-1, keepdims=True)
    plan_ref[...] = (comb + jnp.where(lane == PLAN_GROUP, grp.astype(F32), 0.0)
                     + jnp.where(lane == PLAN_RANK, rank, 0.0))
    cnt_ref[...] = jnp.broadcast_to(jnp.sum(in_group.astype(F32), axis=0, keepdims=True), cnt_ref.shape)


def _merge(att, hf, hb, po, pm, x_all, mod, g_mh, l1g, l1b, wa, wm, wo, wr, br, tm):
    t = x_all.shape[0]
    row = lambda n: pl.BlockSpec((tm, n), lambda i: (i, 0))
    full = lambda a: pl.BlockSpec(a.shape, lambda i: (0,) * a.ndim)
    return pl.pallas_call(
        functools.partial(_merge_kernel, tm=tm),
        grid=(t // tm,),
        in_specs=[row(H_MLA * HP), row(H_M * MV), row(H_M * MV), row(SEG_PO), row(SEG_PM), row(D),
                  full(mod), full(g_mh), full(l1g), full(l1b), full(wa), full(wm), full(wo), full(wr), full(br)],
        out_specs=[row(D), row(D), row(LANES), pl.BlockSpec((None, 8, LANES), lambda i: (i, 0, 0))],
        out_shape=[jax.ShapeDtypeStruct((t, D), F32), jax.ShapeDtypeStruct((t, D), BF16),
                   jax.ShapeDtypeStruct((t, LANES), F32), jax.ShapeDtypeStruct((t // tm, 8, LANES), F32)],
        compiler_params=_cparams(1),
        name="merge",
    )(att, hf, hb, po, pm, x_all, mod, g_mh, l1g, l1b, wa, wm, wo, wr, br)


def _moe_kernel(cnt_ref, layer_ref, hm_ref, plan_ref, plant_ref, x1_ref, mod_ref, l2g_ref, l2b_ref, wg_ref, wu_ref, wd_ref,
                o_ref, *, tm, tp, slot_sizes):
    i = pl.program_id(0)
    grp = pl.program_id(1)
    n_sub = tm // tp

    @pl.when(grp == 0)
    def _():
        o_ref[...] = jnp.zeros_like(o_ref)

    n_tok = cnt_ref[i * n_sub, grp]
    for j in range(1, n_sub):
        n_tok = jnp.maximum(n_tok, cnt_ref[i * n_sub + j, grp])

    lower = 0
    for k, slots in enumerate(slot_sizes):
        is_last = k == len(slot_sizes) - 1
        fits = (n_tok > lower) if is_last else ((n_tok > lower) & (n_tok <= slots))
        one_round = functools.partial(_moe_round, hm_ref, plan_ref, plant_ref, wg_ref, wu_ref, wd_ref, o_ref,
                                      grp=grp, n_sub=n_sub, tp=tp, slots=slots)

        @pl.when(fits)
        def _(one_round=one_round, slots=slots):
            lax.fori_loop(0, (n_tok + (slots - 1)) // slots, one_round, 0)

        lower = slots

    @pl.when(grp == N_GROUPS - 1)
    def _():
        is_ctx = _is_ctx_rows(tm)
        z = ALPHA * x1_ref[...] + _mod(mod_ref, 5, is_ctx) * o_ref[...]
        o_ref[...] = _ln(z) * l2g_ref[...] + l2b_ref[...]


def _moe_round(hm_ref, plan_ref, plant_ref, wg_ref, wu_ref, wd_ref, o_ref, r, carry, *, grp, n_sub, tp, slots):
    grp_f = grp.astype(F32)
    base = (r * slots).astype(F32)
    slot_rows = lax.broadcasted_iota(jnp.int32, (slots, tp), 0).astype(F32)
    slot_cols = lax.broadcasted_iota(jnp.int32, (tp, slots), 1).astype(F32)
    lane = lax.broadcasted_iota(jnp.int32, (n_sub * slots, LANES), 1)
    xs, cs = [], []
    for j in range(n_sub):
        rows = slice(j * tp, (j + 1) * tp)
        in_grp = plant_ref[PLAN_GROUP:PLAN_GROUP + 1, rows] == grp_f
        rank = plant_ref[PLAN_RANK:PLAN_RANK + 1, rows]
        gather = ((slot_rows == rank - base) & in_grp).astype(BF16)
        xs.append(jnp.dot(gather, hm_ref[rows, :], preferred_element_type=F32).astype(BF16))
        cs.append(sum(jnp.dot(gather, p, preferred_element_type=F32)
                      for p in _split3(plan_ref[rows, :])))
    xs = jnp.concatenate(xs, axis=0)
    cs = jnp.concatenate(cs, axis=0)
    hid = []
    for g in range(EPG):
        c_e = jnp.sum(jnp.where(lane == N_GROUPS + grp * EPG + g, cs, 0.0), axis=-1, keepdims=True)
        a = jnp.dot(xs, wg_ref[g], preferred_element_type=F32)
        u = jnp.dot(xs, wu_ref[g], preferred_element_type=F32)
        hid.append((a * _sigmoid(a) * u * c_e).astype(BF16))
    ys = jnp.dot(jnp.concatenate(hid, axis=1), wd_ref[...].reshape(EPG * D_EXP, D),
                 preferred_element_type=F32)
    ys_hi = ys.astype(BF16)
    ys_lo = (ys - ys_hi.astype(F32)).astype(BF16)
    for j in range(n_sub):
        rows = slice(j * tp, (j + 1) * tp)
        mine = slice(j * slots, (j + 1) * slots)
        in_grp = plan_ref[rows, PLAN_GROUP:PLAN_GROUP + 1] == grp_f
        rank = plan_ref[rows, PLAN_RANK:PLAN_RANK + 1]
        scatter = ((slot_cols == rank - base) & in_grp).astype(BF16)
        o_ref[rows, :] += (jnp.dot(scatter, ys_hi[mine], preferred_element_type=F32)
                           + jnp.dot(scatter, ys_lo[mine], preferred_element_type=F32))
    return carry


def _moe(counts, hm, plan, plan_t, x1, mod, l2g, l2b, wg, wu, wd, layer, tp):
    t = hm.shape[0]
    tm = tp * (MOE_PLAN_TILES if (t // tp) % MOE_PLAN_TILES == 0 else 1)
    row = lambda n, **kw: pl.BlockSpec((tm, n), lambda i, e, cnt, lay: (i, 0), **kw)
    full = lambda a: pl.BlockSpec(a.shape, lambda i, e, cnt, lay: (0,) * a.ndim)
    group_of = lambda a: pl.BlockSpec((None, EPG) + a.shape[2:], lambda i, e, cnt, lay: (lay[0], e, 0, 0))
    return pl.pallas_call(
        functools.partial(_moe_kernel, tm=tm, tp=tp, slot_sizes=_moe_slot_sizes(tp)),
        grid_spec=pltpu.PrefetchScalarGridSpec(
            num_scalar_prefetch=2,
            grid=(t // tm, N_GROUPS),
            in_specs=[row(D), row(LANES), pl.BlockSpec((2, tm), lambda i, e, cnt, lay: (0, i)),
                      row(D, pipeline_mode=pl.Buffered(1)),
                      full(mod), full(l2g), full(l2b), group_of(wg), group_of(wu), group_of(wd)],
            out_specs=row(D)),
        out_shape=jax.ShapeDtypeStruct((t, D), F32),
        compiler_params=_cparams(2, VMEM_LIMIT_MOE),
        name="moe",
    )(counts, jnp.full((1,), layer, jnp.int32), hm, plan, plan_t, x1, mod, l2g, l2b, wg, wu, wd)


def _rot_cols(w):
    a1, a2, b1, b2 = jnp.split(w, 4, axis=-1)
    return jnp.concatenate([-a2, a1, -b2, b1], axis=-1)


def _place_rope(w):
    return jnp.pad(w, ((0, 0), (QK_NOPE, HP - QK_NOPE - QK_ROPE)))


def _rope_tables(t):
    rows = (t - CTX) // GRID_W
    row, col = jnp.meshgrid(jnp.arange(rows, dtype=F32), jnp.arange(GRID_W, dtype=F32), indexing="ij")
    row, col = row.reshape(-1), col.reshape(-1)
    half = QK_ROPE // 2
    inv = ROPE_BASE ** (-jnp.arange(0, half, 2, dtype=F32) / half)
    ar, ac = row[:, None] * inv, col[:, None] * inv
    ang = jnp.concatenate([ar, ar, ac, ac], axis=-1)
    ang = jnp.concatenate([jnp.zeros((CTX, QK_ROPE), F32), ang], axis=0)
    pad = ((0, 0), (QK_NOPE, HP - QK_NOPE - QK_ROPE))
    cos = jnp.pad(jnp.cos(ang), pad) + jnp.pad(jnp.ones((t, QK_NOPE), F32), ((0, 0), (0, HP - QK_NOPE)))
    sin = jnp.pad(jnp.sin(ang), pad)
    return cos, sin, cos.T, sin.T


def _layer_weights(l, w_in, w_uq, w_uk, w_uv, w_bo_mla, w_bo_mlstm, w_out, w_rg, b_rg, w_re, b_re):
    wi = w_in[l]
    o = 0
    segs = []
    for n in (Q_LORA, KV_LORA, QK_ROPE, SEG_Q, SEG_K, SEG_V, SEG_PO, 4 * H_M, SEG_PM):
        segs.append(wi[:, o:o + n])
        o += n
    s_pqd, s_ckv, s_kr, s_q, s_k, s_v, s_po, s_pg, s_pm = segs
    w_in_p = jnp.concatenate(
        [s_pqd, s_ckv, _place_rope(s_kr), _place_rope(_rot_cols(s_kr)), s_q, s_k, s_v, s_po,
         jnp.pad(s_pg, ((0, 0), (0, SEG_PG - 4 * H_M))), s_pm], axis=1).astype(BF16)

    uq = w_uq[l].reshape(Q_LORA, H_MLA, QK_NOPE + QK_ROPE)
    padh = ((0, 0), (0, 0), (0, HP - QK_NOPE - QK_ROPE))
    wq = jnp.pad(uq, padh).reshape(Q_LORA, H_MLA * HP)
    wqr = jnp.pad(jnp.concatenate([jnp.zeros_like(uq[..., :QK_NOPE]), _rot_cols(uq[..., QK_NOPE:])], axis=-1),
                  padh).reshape(Q_LORA, H_MLA * HP)
    padv = ((0, 0), (0, 0), (0, HP - V_HEAD))
    wk = jnp.pad(w_uk[l].reshape(KV_LORA, H_MLA, QK_NOPE), padv).reshape(KV_LORA, H_MLA * HP)
    wv = jnp.pad(w_uv[l].reshape(KV_LORA, H_MLA, V_HEAD), ((0, 0), (0, 0), (0, VP - V_HEAD))).reshape(KV_LORA, H_MLA * VP)
    wa = jnp.pad(w_bo_mla[l].reshape(H_MLA, V_HEAD, D), ((0, 0), (0, HP - V_HEAD), (0, 0))).reshape(H_MLA * HP, D)

    wr = jnp.pad(jnp.concatenate([w_rg[l], w_re[l]], axis=1), ((0, 0), (0, LANES - N_GROUPS - N_EXP)))
    wr_hi = wr.astype(BF16)
    wr_lo = (wr - wr_hi.astype(F32)).astype(BF16)
    br = jnp.pad(jnp.concatenate([b_rg[l], b_re[l]]), (0, LANES - N_GROUPS - N_EXP)).reshape(1, LANES)
    return dict(
        w_in=w_in_p, wqt=wq.T.astype(BF16), wqrt=wqr.T.astype(BF16), wk=wk.astype(BF16), wvt=wv.T.astype(BF16),
        wa=wa.astype(BF16), wm=w_bo_mlstm[l].astype(BF16), wo=w_out[l].astype(BF16),
        wr=jnp.stack([wr_hi, wr_lo]), br=br)


def kernel(x, c, ctx, c_ctx, w_ada, b_ada, w_in, b_gates, w_uq, w_uk, w_uv, g_qn, g_kvn, g_mh, w_bo_mla,
           w_bo_mlstm, w_out, ln1_g, ln1_b, w_rg, b_rg, w_re, b_re, w_e_gate, w_e_up, w_e_down, ln2_g, ln2_b):
    assert x.shape[0] == 1 and c.shape[0] == 1 and ctx.shape[1] == CTX
    x_all = jnp.concatenate([ctx[0], x[0]], axis=0)
    t = x_all.shape[0]
    tm = ROW_TILE if t % ROW_TILE == 0 else M_CHUNK
    tk = ATT_TK if t % ATT_TK == 0 else M_CHUNK
    assert t % tm == 0 and t % ATT_TQ == 0 and t % tk == 0 and t % M_CHUNK == 0

    cc = jnp.pad(jnp.concatenate([c, c_ctx[None]], axis=0), ((0, 6), (0, 0)))
    mods = _ada(cc, w_ada, b_ada)
    tabs = _rope_tables(t)
    row2 = lambda a: a.reshape(1, -1)
    wg_all, wu_all, wd_all = w_e_gate.astype(BF16), w_e_up.astype(BF16), w_e_down.astype(BF16)

    for l in range(DEPTH):
        w = _layer_weights(l, w_in, w_uq, w_uk, w_uv, w_bo_mla, w_bo_mlstm, w_out, w_rg, b_rg, w_re, b_re)
        mod = mods[l]
        pqd, ckv, kr, mq, mk, mv, po, pg, pm = _inproj(x_all, mod, w["w_in"], tm)
        qt, kk, vt = _mla_prep(pqd, ckv, kr, tabs, row2(g_qn[l]), row2(g_kvn[l]),
                               w["wqt"], w["wqrt"], w["wk"], w["wvt"], tm)
        att = _attention(qt, kk, vt, ATT_TQ, tk, ATT_CB, ATT_DEPTH)
        pg_t = pg[:, :16].T
        bg_c = jnp.pad(b_gates[l], (0, LANES - 16)).reshape(1, LANES)
        bg_r = b_gates[l].reshape(16, 1)
        hf = _mlstm(mq, mk, mv, pg, pg_t, bg_c, bg_r, False, M_CHUNK)
        hb = _mlstm(mq, mk, mv, pg, pg_t, bg_c, bg_r, True, M_CHUNK)
        x1, hm, plan, cnt = _merge(att, hf, hb, po, pm, x_all, mod, row2(g_mh[l]), row2(ln1_g[l]), row2(ln1_b[l]),
                                   w["wa"], w["wm"], w["wo"], w["wr"], w["br"], tm)
        counts = cnt[:, 0, :N_GROUPS].astype(jnp.int32)
        plan_t = plan[:, :2].T
        x_all = _moe(counts, hm, plan, plan_t, x1, mod, row2(ln2_g[l]), row2(ln2_b[l]),
                     wg_all, wu_all, wd_all, l, tm)
    return x_all[CTX:][None]
```

```python
import functools
import math

import jax
import jax.numpy as jnp
from jax import lax
from jax.experimental import pallas as pl
from jax.experimental.pallas import tpu as pltpu

F32 = jnp.float32
BF16 = jnp.bfloat16

D = 1024
DEPTH = 4
GRID_W = 64
CTX = 256
H_MLA = 8
QK_NOPE = 64
QK_ROPE = 32
V_HEAD = 64
Q_LORA = 384
KV_LORA = 256
ROPE_BASE = 10000.0
MLA_SCALE = (QK_NOPE + QK_ROPE) ** -0.5
H_M = 4
MQK = 64
MV = 128
N_GROUPS = 4
EPG = 8
N_EXP = N_GROUPS * EPG
D_EXP = 256
ALPHA = (2 * DEPTH) ** 0.25
LN_EPS = 1e-6

LANES = 128
VMEM_LIMIT = 56 * 1024 * 1024
VMEM_LIMIT_MOE = 58 * 1024 * 1024

HP = LANES
ONES_ROW = V_HEAD
BF16_ROWS = 16
VP = (V_HEAD + 1 + BF16_ROWS - 1) // BF16_ROWS * BF16_ROWS
QSCALE = MLA_SCALE * math.log2(math.e)

SEG_PQD = Q_LORA
SEG_CKV = KV_LORA
SEG_KR = 2 * HP
SEG_Q = H_M * MQK
SEG_K = H_M * MQK
SEG_V = H_M * MV
SEG_PO = H_M * MV
SEG_PG = LANES
SEG_PM = 2 * D
IN_SEGS = (SEG_PQD, SEG_CKV, SEG_KR, SEG_Q, SEG_K, SEG_V, SEG_PO, SEG_PG, SEG_PM)
IN_DTYPES = (F32, F32, F32, BF16, BF16, BF16, BF16, F32, BF16)
NP_IN = sum(IN_SEGS)

ROW_TILE = 640
ATT_TQ = 3328
ATT_CB = 256
ATT_TK = 1280
MAX_LAG_JUMP = 64.0
ATT_DEPTH = 2
M_CHUNK = 256
MOE_SLOT_STEP = 64
MOE_SLOT_MAX = 256
MOE_PLAN_TILES = 2
PLAN_GROUP = 0
PLAN_RANK = 1


def _moe_slot_sizes(tm):
    return tuple(s for s in range(MOE_SLOT_STEP, MOE_SLOT_MAX + 1, MOE_SLOT_STEP) if s <= tm)

_NT = (((1,), (1,)), ((), ()))
_TN = (((0,), (0,)), ((), ()))


def _cparams(n_grid, vmem_limit=VMEM_LIMIT):
    return pltpu.CompilerParams(dimension_semantics=("arbitrary",) * n_grid, vmem_limit_bytes=vmem_limit)


def _ln(x):
    mu = jnp.mean(x, axis=-1, keepdims=True)
    xc = x - mu
    var = jnp.mean(xc * xc, axis=-1, keepdims=True)
    return xc * lax.rsqrt(var + LN_EPS)


def _rms(x, g):
    return x * lax.rsqrt(jnp.mean(x * x, axis=-1, keepdims=True) + LN_EPS) * g


def _sigmoid(x):
    return 1.0 / (1.0 + jnp.exp(-x))


def _is_ctx_rows(tm):
    rows = pl.program_id(0) * tm + lax.broadcasted_iota(jnp.int32, (tm, 1), 0)
    return rows < CTX


def _mod(mod_ref, k, is_ctx):
    lat = mod_ref[0:1, k * D:(k + 1) * D]
    ctx = mod_ref[1:2, k * D:(k + 1) * D]
    return jnp.where(is_ctx, ctx, lat)


def _ada_kernel(c_ref, w_ref, b_ref, o_ref):
    c = c_ref[...]
    s = c * _sigmoid(c)
    o_ref[...] = jnp.dot(s, w_ref[...], preferred_element_type=F32,
                         precision=lax.Precision.HIGHEST) + b_ref[...]


def _ada(cc, w_ada, b_ada):
    tn = 1536
    n = 6 * D
    return pl.pallas_call(
        _ada_kernel,
        grid=(DEPTH, n // tn),
        in_specs=[pl.BlockSpec((8, D), lambda l, j: (0, 0)),
                  pl.BlockSpec((None, D, tn), lambda l, j: (l, 0, j)),
                  pl.BlockSpec((None, 1, tn), lambda l, j: (l, 0, j))],
        out_specs=pl.BlockSpec((None, 8, tn), lambda l, j: (l, 0, j)),
        out_shape=jax.ShapeDtypeStruct((DEPTH, 8, n), F32),
        compiler_params=_cparams(2),
        name="ada",
    )(cc, w_ada, b_ada.reshape(DEPTH, 1, n))


def _inproj_kernel(x_ref, mod_ref, w_ref, *o_refs, tm):
    is_ctx = _is_ctx_rows(tm)
    h = _ln(x_ref[...]) * (1.0 + _mod(mod_ref, 1, is_ctx)) + _mod(mod_ref, 0, is_ctx)
    hb = h.astype(BF16)
    off = 0
    for o in o_refs:
        n = o.shape[-1]
        o[...] = jnp.dot(hb, w_ref[:, off:off + n], preferred_element_type=F32).astype(o.dtype)
        off += n


def _inproj(x_all, mod, w_in_p, tm):
    t = x_all.shape[0]
    return pl.pallas_call(
        functools.partial(_inproj_kernel, tm=tm),
        grid=(t // tm,),
        in_specs=[pl.BlockSpec((tm, D), lambda i: (i, 0)),
                  pl.BlockSpec((8, 6 * D), lambda i: (0, 0)),
                  pl.BlockSpec((D, NP_IN), lambda i: (0, 0))],
        out_specs=[pl.BlockSpec((tm, n), lambda i: (i, 0)) for n in IN_SEGS],
        out_shape=[jax.ShapeDtypeStruct((t, n), dt) for n, dt in zip(IN_SEGS, IN_DTYPES)],
        compiler_params=_cparams(1),
        name="inproj",
    )(x_all, mod, w_in_p)


def _mla_prep_kernel(pqd_ref, ckv_ref, kr_ref, cos_ref, sin_ref, cost_ref, sint_ref, gq_ref, gk_ref,
                     wqt_ref, wqrt_ref, wk_ref, wvt_ref, qt_ref, k_ref, vt_ref, *, tm):
    qn = _rms(pqd_ref[...], gq_ref[...]).astype(BF16)
    qa = lax.dot_general(wqt_ref[...], qn, _NT, preferred_element_type=F32)
    qb = lax.dot_general(wqrt_ref[...], qn, _NT, preferred_element_type=F32)
    cost = cost_ref[...]
    sint = sint_ref[...]
    ckv = _rms(ckv_ref[...], gk_ref[...]).astype(BF16)
    kn = jnp.dot(ckv, wk_ref[...], preferred_element_type=F32)
    kr = kr_ref[:, :HP] * cos_ref[...] + kr_ref[:, HP:] * sin_ref[...]
    vt = lax.dot_general(wvt_ref[...], ckv, _NT, preferred_element_type=F32)
    is_ones_row = lax.broadcasted_iota(jnp.int32, (VP, tm), 0) == ONES_ROW
    for h in range(H_MLA):
        sl = slice(h * HP, (h + 1) * HP)
        qt_ref[h] = ((qa[sl] * cost + qb[sl] * sint) * QSCALE).astype(BF16)
        k_ref[h] = (kn[:, sl] + kr).astype(BF16)
        vt_ref[h] = jnp.where(is_ones_row, 1.0, vt[h * VP:(h + 1) * VP]).astype(BF16)


def _mla_prep(pqd, ckv, kr, tabs, g_qn, g_kvn, wqt, wqrt, wk, wvt, tm):
    t = pqd.shape[0]
    cos, sin, cost, sint = tabs
    row = lambda n: pl.BlockSpec((tm, n), lambda i: (i, 0))
    col = lambda n: pl.BlockSpec((n, tm), lambda i: (0, i))
    full = lambda a: pl.BlockSpec(a.shape, lambda i: (0,) * a.ndim)
    hd = H_MLA * HP
    return pl.pallas_call(
        functools.partial(_mla_prep_kernel, tm=tm),
        grid=(t // tm,),
        in_specs=[row(Q_LORA), row(KV_LORA), row(SEG_KR), row(HP), row(HP), col(HP), col(HP),
                  full(g_qn), full(g_kvn), full(wqt), full(wqrt), full(wk), full(wvt)],
        out_specs=[pl.BlockSpec((H_MLA, HP, tm), lambda i: (0, 0, i)),
                   pl.BlockSpec((H_MLA, tm, HP), lambda i: (0, i, 0)),
                   pl.BlockSpec((H_MLA, VP, tm), lambda i: (0, 0, i))],
        out_shape=[jax.ShapeDtypeStruct((H_MLA, HP, t), BF16),
                   jax.ShapeDtypeStruct((H_MLA, t, HP), BF16),
                   jax.ShapeDtypeStruct((H_MLA, VP, t), BF16)],
        compiler_params=_cparams(1),
        name="mla_prep",
    )(pqd, ckv, kr, cos, sin, cost, sint, g_qn, g_kvn, wqt, wqrt, wk, wvt)


def _scores(k, qt, m):
    s = jnp.dot(k, qt, preferred_element_type=F32)
    return s, jnp.maximum(m, jnp.max(s, axis=0, keepdims=True))


def _accumulate(s, vt, m, m_new, acc):
    p = jnp.exp2(s - m_new).astype(BF16)
    return jnp.exp2(m - m_new) * acc + jnp.dot(vt, p, preferred_element_type=F32)


def _normalized_rows(acc):
    o = acc / acc[ONES_ROW:ONES_ROW + 1, :]
    return jnp.concatenate([o, jnp.zeros((HP - VP, o.shape[1]), F32)], axis=0).T


def _lagged_pass(qt_ref, k_ref, vt_ref, m_ref, acc_ref, p_ref, sc_ref, *, ncb, n_kv, tk, cb, depth):
    k0 = k_ref[0:BF16_ROWS, :]
    for c in range(ncb):
        s0 = jnp.dot(k0, qt_ref[:, c * cb:(c + 1) * cb], preferred_element_type=F32)
        m_ref[c] = jnp.max(s0, axis=0, keepdims=True)
    acc_ref[...] = jnp.zeros(acc_ref.shape, F32)
    p_ref[...] = jnp.zeros(p_ref.shape, BF16)
    sc_ref[...] = jnp.ones(sc_ref.shape, F32)

    def carried(vt_prev):
        return [(ncb - depth + d, p_ref[d], sc_ref[d], vt_prev) for d in range(depth)]

    def drain(entry):
        pc, pp, psc, pvt = entry
        acc_ref[pc] = (acc_ref[pc] + jnp.dot(pvt, pp, preferred_element_type=F32)) * psc

    def step(j, jump):
        off = pl.multiple_of(j * tk, tk)
        off_prev = pl.multiple_of(jnp.maximum(j - 1, 0) * tk, tk)
        k = k_ref[pl.ds(off, tk), :]
        vt = vt_ref[:, pl.ds(off, tk)]
        pending = carried(vt_ref[:, pl.ds(off_prev, tk)])
        for c in range(ncb):
            m = m_ref[c]
            s = jnp.dot(k, qt_ref[:, c * cb:(c + 1) * cb], preferred_element_type=F32)
            p = jnp.exp2(s - m).astype(BF16)
            cm = jnp.max(s, axis=0, keepdims=True)
            m_new = jnp.maximum(m, cm)
            jump = jnp.maximum(jump, cm - m)
            m_ref[c] = m_new
            drain(pending.pop(0))
            pending.append((c, p, jnp.exp2(m - m_new), vt))
        for d, (_, pp, psc, _) in enumerate(pending):
            p_ref[d] = pp
            sc_ref[d] = psc
        return jump

    jump = lax.fori_loop(0, n_kv, step, jnp.zeros((1, cb), F32))
    for entry in carried(vt_ref[:, (n_kv - 1) * tk:]):
        drain(entry)
    return jump


def _exact_pass(qt_ref, k_ref, vt_ref, m_ref, acc_ref, s_ref, mp_ref, *, ncb, n_kv, tk, cb, depth):
    m_ref[...] = jnp.full(m_ref.shape, -1e30, F32)
    acc_ref[...] = jnp.zeros(acc_ref.shape, F32)
    s_ref[...] = jnp.full(s_ref.shape, -jnp.inf, F32)
    mp_ref[...] = jnp.full(mp_ref.shape, -1e30, F32)

    def carried(vt_prev):
        return [(ncb - depth + d, s_ref[d], mp_ref[d, 0], mp_ref[d, 1], vt_prev) for d in range(depth)]

    def drain(entry):
        pc, ps, pm_old, pm_new, pvt = entry
        acc_ref[pc] = _accumulate(ps, pvt, pm_old, pm_new, acc_ref[pc])

    def step(j, carry):
        off = pl.multiple_of(j * tk, tk)
        off_prev = pl.multiple_of(jnp.maximum(j - 1, 0) * tk, tk)
        k = k_ref[pl.ds(off, tk), :]
        vt = vt_ref[:, pl.ds(off, tk)]
        pending = carried(vt_ref[:, pl.ds(off_prev, tk)])
        for c in range(ncb):
            m_old = m_ref[c]
            s, m_new = _scores(k, qt_ref[:, c * cb:(c + 1) * cb], m_old)
            m_ref[c] = m_new
            drain(pending.pop(0))
            pending.append((c, s, m_old, m_new, vt))
        for d, (_, ps, pm_old, pm_new, _) in enumerate(pending):
            s_ref[d] = ps
            mp_ref[d, 0] = pm_old
            mp_ref[d, 1] = pm_new
        return carry

    lax.fori_loop(0, n_kv, step, 0)
    for entry in carried(vt_ref[:, (n_kv - 1) * tk:]):
        drain(entry)


def _attn_kernel(qt_ref, k_ref, vt_ref, o_ref, m_ref, acc_ref, s_ref, mp_ref, p_ref, sc_ref, *,
                 tq, tk, t, cb, depth):
    geom = dict(ncb=tq // cb, n_kv=t // tk, tk=tk, cb=cb, depth=depth)

    def write_out():
        for c in range(tq // cb):
            o_ref[c * cb:(c + 1) * cb, :] = _normalized_rows(acc_ref[c]).astype(o_ref.dtype)

    jump = _lagged_pass(qt_ref, k_ref, vt_ref, m_ref, acc_ref, p_ref, sc_ref, **geom)
    write_out()

    @pl.when(jnp.max(jump) > MAX_LAG_JUMP)
    def _():
        _exact_pass(qt_ref, k_ref, vt_ref, m_ref, acc_ref, s_ref, mp_ref, **geom)
        write_out()

    @pl.when(pl.program_id(1) == 0)
    def _():
        m0 = jnp.full((1, CTX), -1e30, F32)
        s, m_new = _scores(k_ref[0:CTX, :], qt_ref[:, 0:CTX], m0)
        acc_c = _accumulate(s, vt_ref[:, 0:CTX], m0, m_new, jnp.zeros((VP, CTX), F32))
        o_ref[0:CTX, :] = _normalized_rows(acc_c).astype(o_ref.dtype)


def _attention(qt, k, vt, tq, tk, cb, depth):
    t = k.shape[1]
    assert depth <= tq // cb
    return pl.pallas_call(
        functools.partial(_attn_kernel, tq=tq, tk=tk, t=t, cb=cb, depth=depth),
        grid=(H_MLA, t // tq),
        in_specs=[pl.BlockSpec((None, HP, tq), lambda h, i: (h, 0, i)),
                  pl.BlockSpec((None, t, HP), lambda h, i: (h, 0, 0)),
                  pl.BlockSpec((None, VP, t), lambda h, i: (h, 0, 0))],
        out_specs=pl.BlockSpec((tq, HP), lambda h, i: (i, h)),
        out_shape=jax.ShapeDtypeStruct((t, H_MLA * HP), BF16),
        scratch_shapes=[pltpu.VMEM((tq // cb, 1, cb), F32), pltpu.VMEM((tq // cb, VP, cb), F32),
                        pltpu.VMEM((depth, tk, cb), F32), pltpu.VMEM((depth, 2, 1, cb), F32),
                        pltpu.VMEM((depth, tk, cb), BF16), pltpu.VMEM((depth, 1, cb), F32)],
        compiler_params=_cparams(2),
        name="attention",
    )(qt, k, vt)


def _split3(x):
    hi = x.astype(BF16)
    r = x - hi.astype(F32)
    mid = r.astype(BF16)
    lo = (r - mid.astype(F32)).astype(BF16)
    return hi, mid, lo


def _log_sigmoid(x):
    return jnp.minimum(x, 0.0) - jnp.log(1.0 + jnp.exp(-jnp.abs(x)))


def _mlstm_kernel(q_ref, k_ref, v_ref, gc_ref, gr_ref, bc_ref, br_ref, o_ref, c_scr, m_scr, *, L, rev):
    @pl.when(pl.program_id(0) == 0)
    def _():
        c_scr[...] = jnp.zeros_like(c_scr)
        m_scr[...] = jnp.zeros_like(m_scr)

    row = lax.broadcasted_iota(jnp.int32, (L, L), 0)
    col = lax.broadcasted_iota(jnp.int32, (L, L), 1)
    mask = (row <= col) if rev else (row >= col)
    maskt = (row >= col) if rev else (row <= col)
    mask_b = mask.astype(BF16)
    maskt_b = maskt.astype(BF16)

    gc = gc_ref[...] + bc_ref[...]
    gr = gr_ref[...] + br_ref[...]
    lfc = _log_sigmoid(gc)
    lfr = _log_sigmoid(gr)
    b_col = sum(jnp.dot(mask_b, p, preferred_element_type=F32) for p in _split3(lfc))
    b_row = sum(jnp.dot(p, maskt_b, preferred_element_type=F32) for p in _split3(lfr))
    ones_blk = (lax.broadcasted_iota(jnp.int32, (L, LANES), 1) == 0).astype(BF16)

    for h in range(H_M):
        ci = 8 * rev + h
        cf = ci + 4
        b_c = b_col[:, cf:cf + 1]
        i_c = gc[:, ci:ci + 1]
        b_r = b_row[cf:cf + 1, :]
        i_r = gr[ci:ci + 1, :]
        bl = jnp.sum(lfc[:, cf:cf + 1], axis=0, keepdims=True)
        m = m_scr[h][0:1, 0:1]
        c_aug = c_scr[h]

        dlog = jnp.where(mask, b_c - (b_r - i_r), -jnp.inf)
        inter = b_c + m
        mj = jnp.maximum(inter, jnp.max(dlog, axis=-1, keepdims=True))
        w = jnp.exp(dlog - mj)
        sc = jnp.exp(inter - mj)

        q = q_ref[:, h * MQK:(h + 1) * MQK] * (MQK ** -0.5)
        k = k_ref[:, h * MQK:(h + 1) * MQK]
        v_aug = jnp.concatenate([v_ref[:, h * MV:(h + 1) * MV], ones_blk], axis=1)
        qk = lax.dot_general(q, k, _NT, preferred_element_type=F32) * w
        num = (jnp.dot(qk.astype(BF16), v_aug, preferred_element_type=F32)
               + sc * jnp.dot(q, c_aug.astype(BF16), preferred_element_type=F32))
        den = jnp.maximum(jnp.abs(num[:, MV:MV + 1]), jnp.exp(-mj))
        o_ref[:, h * MV:(h + 1) * MV] = num[:, :MV] / den

        wlog = bl - b_c + i_c
        m_new = jnp.maximum(bl + m, jnp.max(wlog, axis=0, keepdims=True))
        ws = jnp.exp(wlog - m_new)
        sd = jnp.exp(bl + m - m_new)
        wv = (ws * v_aug.astype(F32)).astype(BF16)
        c_scr[h] = sd * c_aug + lax.dot_general(k, wv, _TN, preferred_element_type=F32)
        m_scr[h] = jnp.broadcast_to(m_new, m_scr.shape[1:])


def _mlstm(mq, mk, mv, pg, pg_t, bg_c, bg_r, rev, L):
    t = mq.shape[0]
    nc = t // L
    if rev:
        idx = lambda c: jnp.where(c == 0, 0, nc - c)
    else:
        idx = lambda c: c
    return pl.pallas_call(
        functools.partial(_mlstm_kernel, L=L, rev=int(rev)),
        grid=(nc,),
        in_specs=[pl.BlockSpec((L, SEG_Q), lambda c: (idx(c), 0)),
                  pl.BlockSpec((L, SEG_K), lambda c: (idx(c), 0)),
                  pl.BlockSpec((L, SEG_V), lambda c: (idx(c), 0)),
                  pl.BlockSpec((L, LANES), lambda c: (idx(c), 0)),
                  pl.BlockSpec((16, L), lambda c: (0, idx(c))),
                  pl.BlockSpec((1, LANES), lambda c: (0, 0)),
                  pl.BlockSpec((16, 1), lambda c: (0, 0))],
        out_specs=pl.BlockSpec((L, H_M * MV), lambda c: (idx(c), 0)),
        out_shape=jax.ShapeDtypeStruct((t, H_M * MV), F32),
        scratch_shapes=[pltpu.VMEM((H_M, MQK, 2 * MV), F32), pltpu.VMEM((H_M, 8, LANES), F32)],
        compiler_params=_cparams(1),
        name="mlstm_rev" if rev else "mlstm_fwd",
    )(mq, mk, mv, pg, pg_t, bg_c, bg_r)


def _router(logits):
    tm = logits.shape[0]
    lane = lax.broadcasted_iota(jnp.int32, (tm, LANES), 1)
    big = jnp.int32(LANES)
    neg = -jnp.inf
    is_g = lane < N_GROUPS
    lg = jnp.where(is_g, logits, neg)
    gmax = jnp.max(lg, axis=-1, keepdims=True)
    grp = jnp.min(jnp.where(lg == gmax, lane, big), axis=-1, keepdims=True)
    p_grp = 1.0 / jnp.sum(jnp.where(is_g, jnp.exp(lg - gmax), 0.0), axis=-1, keepdims=True)
    e_lo = N_GROUPS + grp * EPG
    in_grp = (lane >= e_lo) & (lane < e_lo + EPG)
    le = jnp.where(in_grp, logits, neg)
    l1 = jnp.max(le, axis=-1, keepdims=True)
    i1 = jnp.min(jnp.where(le == l1, lane, big), axis=-1, keepdims=True)
    le2 = jnp.where(lane == i1, neg, le)
    l2 = jnp.max(le2, axis=-1, keepdims=True)
    i2 = jnp.min(jnp.where(le2 == l2, lane, big), axis=-1, keepdims=True)
    r = jnp.exp(l2 - l1)
    w1 = p_grp / (1.0 + r)
    w2 = w1 * r
    return jnp.where(lane == i1, w1, 0.0) + jnp.where(lane == i2, w2, 0.0), grp


def _merge_kernel(att_ref, hf_ref, hb_ref, po_ref, pm_ref, x_ref, mod_ref, gmh_ref, l1g_ref, l1b_ref,
                  wa_ref, wm_ref, wo_ref, wr_ref, br_ref, x1_ref, hm_ref, plan_ref, cnt_ref, *, tm):
    is_ctx = _is_ctx_rows(tm)
    y_mla = jnp.dot(att_ref[...], wa_ref[...], preferred_element_type=F32)
    hs = hf_ref[...] + hb_ref[...]
    parts = []
    for h in range(H_M):
        sl = slice(h * MV, (h + 1) * MV)
        parts.append(_ln(hs[:, sl]))
    hn = jnp.concatenate(parts, axis=1) * gmh_ref[...] * _sigmoid(po_ref[...].astype(F32))
    y_ml = jnp.dot(hn.astype(BF16), wm_ref[...], preferred_element_type=F32)
    g_a = _sigmoid(pm_ref[:, :D].astype(F32))
    g_b = _sigmoid(pm_ref[:, D:].astype(F32))
    y = jnp.dot((g_a * y_mla + g_b * y_ml).astype(BF16), wo_ref[...], preferred_element_type=F32)
    x1 = _ln(ALPHA * x_ref[...] + _mod(mod_ref, 2, is_ctx) * y) * l1g_ref[...] + l1b_ref[...]
    x1_ref[...] = x1
    hm = _ln(x1) * (1.0 + _mod(mod_ref, 4, is_ctx)) + _mod(mod_ref, 3, is_ctx)
    hm_ref[...] = hm.astype(BF16)
    h_hi = hm.astype(BF16)
    h_lo = (hm - h_hi.astype(F32)).astype(BF16)
    logits = (jnp.dot(h_hi, wr_ref[0], preferred_element_type=F32)
              + jnp.dot(h_hi, wr_ref[1], preferred_element_type=F32)
              + jnp.dot(h_lo, wr_ref[0], preferred_element_type=F32)) + br_ref[...]
    comb, grp = _router(logits)
    lane = lax.broadcasted_iota(jnp.int32, (tm, LANES), 1)
    in_group = lane == grp
    earlier = (lax.broadcasted_iota(jnp.int32, (tm, tm), 0)
               > lax.broadcasted_iota(jnp.int32, (tm, tm), 1)).astype(BF16)
    n_before = jnp.dot(earlier, in_group.astype(BF16), preferred_element_type=F32)
    rank = jnp.sum(jnp.where(in_group, n_before, 0.0), axis=-1, keepdims=True)
    plan_ref[...] = (comb + jnp.where(lane == PLAN_GROUP, grp.astype(F32), 0.0)
                     + jnp.where(lane == PLAN_RANK, rank, 0.0))
    cnt_ref[...] = jnp.broadcast_to(jnp.sum(in_group.astype(F32), axis=0, keepdims=True), cnt_ref.shape)


def _merge(att, hf, hb, po, pm, x_all, mod, g_mh, l1g, l1b, wa, wm, wo, wr, br, tm):
    t = x_all.shape[0]
    row = lambda n: pl.BlockSpec((tm, n), lambda i: (i, 0))
    full = lambda a: pl.BlockSpec(a.shape, lambda i: (0,) * a.ndim)
    return pl.pallas_call(
        functools.partial(_merge_kernel, tm=tm),
        grid=(t // tm,),
        in_specs=[row(H_MLA * HP), row(H_M * MV), row(H_M * MV), row(SEG_PO), row(SEG_PM), row(D),
                  full(mod), full(g_mh), full(l1g), full(l1b), full(wa), full(wm), full(wo), full(wr), full(br)],
        out_specs=[row(D), row(D), row(LANES), pl.BlockSpec((None, 8, LANES), lambda i: (i, 0, 0))],
        out_shape=[jax.ShapeDtypeStruct((t, D), F32), jax.ShapeDtypeStruct((t, D), BF16),
                   jax.ShapeDtypeStruct((t, LANES), F32), jax.ShapeDtypeStruct((t // tm, 8, LANES), F32)],
        compiler_params=_cparams(1),
        name="merge",
    )(att, hf, hb, po, pm, x_all, mod, g_mh, l1g, l1b, wa, wm, wo, wr, br)


def _moe_kernel(cnt_ref, layer_ref, hm_ref, plan_ref, plant_ref, x1_ref, mod_ref, l2g_ref, l2b_ref, wg_ref, wu_ref, wd_ref,
                o_ref, *, tm, tp, slot_sizes):
    i = pl.program_id(0)
    grp = pl.program_id(1)
    n_sub = tm // tp

    @pl.when(grp == 0)
    def _():
        o_ref[...] = jnp.zeros_like(o_ref)

    n_tok = cnt_ref[i * n_sub, grp]
    for j in range(1, n_sub):
        n_tok = jnp.maximum(n_tok, cnt_ref[i * n_sub + j, grp])

    def rounds_of(slots):
        return functools.partial(_moe_round, hm_ref, plan_ref, plant_ref, wg_ref, wu_ref, wd_ref, o_ref,
                                 grp=grp, n_sub=n_sub, tp=tp, slots=slots)

    big = slot_sizes[-1]
    tails = slot_sizes[:-1]
    n_full = lax.div(n_tok, big) + (lax.rem(n_tok, big) > (tails[-1] if tails else 0)).astype(jnp.int32)
    rest = jnp.maximum(n_tok - n_full * big, 0)
    full_round = rounds_of(big)
    lax.fori_loop(0, n_full, lambda r, carry: full_round(r * big, carry), 0)
    lower = 0
    for slots in tails:
        @pl.when((rest > lower) & (rest <= slots))
        def _(tail_round=rounds_of(slots)):
            tail_round(n_full * big, 0)

        lower = slots

    @pl.when(grp == N_GROUPS - 1)
    def _():
        is_ctx = _is_ctx_rows(tm)
        z = ALPHA * x1_ref[...] + _mod(mod_ref, 5, is_ctx) * o_ref[...]
        o_ref[...] = _ln(z) * l2g_ref[...] + l2b_ref[...]


def _moe_round(hm_ref, plan_ref, plant_ref, wg_ref, wu_ref, wd_ref, o_ref, first, carry, *, grp, n_sub, tp, slots):
    grp_f = grp.astype(F32)
    base = first.astype(F32)
    slot_rows = lax.broadcasted_iota(jnp.int32, (slots, tp), 0).astype(F32)
    slot_cols = lax.broadcasted_iota(jnp.int32, (tp, slots), 1).astype(F32)
    lane = lax.broadcasted_iota(jnp.int32, (n_sub * slots, LANES), 1)
    xs, cs = [], []
    for j in range(n_sub):
        rows = slice(j * tp, (j + 1) * tp)
        in_grp = plant_ref[PLAN_GROUP:PLAN_GROUP + 1, rows] == grp_f
        rank = plant_ref[PLAN_RANK:PLAN_RANK + 1, rows]
        gather = ((slot_rows == rank - base) & in_grp).astype(BF16)
        xs.append(jnp.dot(gather, hm_ref[rows, :], preferred_element_type=F32).astype(BF16))
        cs.append(sum(jnp.dot(gather, p, preferred_element_type=F32)
                      for p in _split3(plan_ref[rows, :])))
    xs = jnp.concatenate(xs, axis=0)
    cs = jnp.concatenate(cs, axis=0)
    hid = []
    for g in range(EPG):
        c_e = jnp.sum(jnp.where(lane == N_GROUPS + grp * EPG + g, cs, 0.0), axis=-1, keepdims=True)
        a = jnp.dot(xs, wg_ref[g], preferred_element_type=F32)
        u = jnp.dot(xs, wu_ref[g], preferred_element_type=F32)
        hid.append((a * _sigmoid(a) * u * c_e).astype(BF16))
    ys = jnp.dot(jnp.concatenate(hid, axis=1), wd_ref[...].reshape(EPG * D_EXP, D),
                 preferred_element_type=F32)
    ys_hi = ys.astype(BF16)
    ys_lo = (ys - ys_hi.astype(F32)).astype(BF16)
    for j in range(n_sub):
        rows = slice(j * tp, (j + 1) * tp)
        mine = slice(j * slots, (j + 1) * slots)
        in_grp = plan_ref[rows, PLAN_GROUP:PLAN_GROUP + 1] == grp_f
        rank = plan_ref[rows, PLAN_RANK:PLAN_RANK + 1]
        scatter = ((slot_cols == rank - base) & in_grp).astype(BF16)
        o_ref[rows, :] += (jnp.dot(scatter, ys_hi[mine], preferred_element_type=F32)
                           + jnp.dot(scatter, ys_lo[mine], preferred_element_type=F32))
    return carry


def _moe(counts, hm, plan, plan_t, x1, mod, l2g, l2b, wg, wu, wd, layer, tp):
    t = hm.shape[0]
    tm = tp * (MOE_PLAN_TILES if (t // tp) % MOE_PLAN_TILES == 0 else 1)
    row = lambda n, **kw: pl.BlockSpec((tm, n), lambda i, e, cnt, lay: (i, 0), **kw)
    full = lambda a: pl.BlockSpec(a.shape, lambda i, e, cnt, lay: (0,) * a.ndim)
    group_of = lambda a: pl.BlockSpec((None, EPG) + a.shape[2:], lambda i, e, cnt, lay: (lay[0], e, 0, 0))
    return pl.pallas_call(
        functools.partial(_moe_kernel, tm=tm, tp=tp, slot_sizes=_moe_slot_sizes(tp)),
        grid_spec=pltpu.PrefetchScalarGridSpec(
            num_scalar_prefetch=2,
            grid=(t // tm, N_GROUPS),
            in_specs=[row(D), row(LANES), pl.BlockSpec((2, tm), lambda i, e, cnt, lay: (0, i)),
                      row(D, pipeline_mode=pl.Buffered(1)),
                      full(mod), full(l2g), full(l2b), group_of(wg), group_of(wu), group_of(wd)],
            out_specs=row(D)),
        out_shape=jax.ShapeDtypeStruct((t, D), F32),
        compiler_params=_cparams(2, VMEM_LIMIT_MOE),
        name="moe",
    )(counts, jnp.full((1,), layer, jnp.int32), hm, plan, plan_t, x1, mod, l2g, l2b, wg, wu, wd)


def _rot_cols(w):
    a1, a2, b1, b2 = jnp.split(w, 4, axis=-1)
    return jnp.concatenate([-a2, a1, -b2, b1], axis=-1)


def _place_rope(w):
    return jnp.pad(w, ((0, 0), (QK_NOPE, HP - QK_NOPE - QK_ROPE)))


def _rope_tables(t):
    rows = (t - CTX) // GRID_W
    row, col = jnp.meshgrid(jnp.arange(rows, dtype=F32), jnp.arange(GRID_W, dtype=F32), indexing="ij")
    row, col = row.reshape(-1), col.reshape(-1)
    half = QK_ROPE // 2
    inv = ROPE_BASE ** (-jnp.arange(0, half, 2, dtype=F32) / half)
    ar, ac = row[:, None] * inv, col[:, None] * inv
    ang = jnp.concatenate([ar, ar, ac, ac], axis=-1)
    ang = jnp.concatenate([jnp.zeros((CTX, QK_ROPE), F32), ang], axis=0)
    pad = ((0, 0), (QK_NOPE, HP - QK_NOPE - QK_ROPE))
    cos = jnp.pad(jnp.cos(ang), pad) + jnp.pad(jnp.ones((t, QK_NOPE), F32), ((0, 0), (0, HP - QK_NOPE)))
    sin = jnp.pad(jnp.sin(ang), pad)
    return cos, sin, cos.T, sin.T


def _layer_weights(l, w_in, w_uq, w_uk, w_uv, w_bo_mla, w_bo_mlstm, w_out, w_rg, b_rg, w_re, b_re):
    wi = w_in[l]
    o = 0
    segs = []
    for n in (Q_LORA, KV_LORA, QK_ROPE, SEG_Q, SEG_K, SEG_V, SEG_PO, 4 * H_M, SEG_PM):
        segs.append(wi[:, o:o + n])
        o += n
    s_pqd, s_ckv, s_kr, s_q, s_k, s_v, s_po, s_pg, s_pm = segs
    w_in_p = jnp.concatenate(
        [s_pqd, s_ckv, _place_rope(s_kr), _place_rope(_rot_cols(s_kr)), s_q, s_k, s_v, s_po,
         jnp.pad(s_pg, ((0, 0), (0, SEG_PG - 4 * H_M))), s_pm], axis=1).astype(BF16)

    uq = w_uq[l].reshape(Q_LORA, H_MLA, QK_NOPE + QK_ROPE)
    padh = ((0, 0), (0, 0), (0, HP - QK_NOPE - QK_ROPE))
    wq = jnp.pad(uq, padh).reshape(Q_LORA, H_MLA * HP)
    wqr = jnp.pad(jnp.concatenate([jnp.zeros_like(uq[..., :QK_NOPE]), _rot_cols(uq[..., QK_NOPE:])], axis=-1),
                  padh).reshape(Q_LORA, H_MLA * HP)
    padv = ((0, 0), (0, 0), (0, HP - V_HEAD))
    wk = jnp.pad(w_uk[l].reshape(KV_LORA, H_MLA, QK_NOPE), padv).reshape(KV_LORA, H_MLA * HP)
    wv = jnp.pad(w_uv[l].reshape(KV_LORA, H_MLA, V_HEAD), ((0, 0), (0, 0), (0, VP - V_HEAD))).reshape(KV_LORA, H_MLA * VP)
    wa = jnp.pad(w_bo_mla[l].reshape(H_MLA, V_HEAD, D), ((0, 0), (0, HP - V_HEAD), (0, 0))).reshape(H_MLA * HP, D)

    wr = jnp.pad(jnp.concatenate([w_rg[l], w_re[l]], axis=1), ((0, 0), (0, LANES - N_GROUPS - N_EXP)))
    wr_hi = wr.astype(BF16)
    wr_lo = (wr - wr_hi.astype(F32)).astype(BF16)
    br = jnp.pad(jnp.concatenate([b_rg[l], b_re[l]]), (0, LANES - N_GROUPS - N_EXP)).reshape(1, LANES)
    return dict(
        w_in=w_in_p, wqt=wq.T.astype(BF16), wqrt=wqr.T.astype(BF16), wk=wk.astype(BF16), wvt=wv.T.astype(BF16),
        wa=wa.astype(BF16), wm=w_bo_mlstm[l].astype(BF16), wo=w_out[l].astype(BF16),
        wr=jnp.stack([wr_hi, wr_lo]), br=br)


def kernel(x, c, ctx, c_ctx, w_ada, b_ada, w_in, b_gates, w_uq, w_uk, w_uv, g_qn, g_kvn, g_mh, w_bo_mla,
           w_bo_mlstm, w_out, ln1_g, ln1_b, w_rg, b_rg, w_re, b_re, w_e_gate, w_e_up, w_e_down, ln2_g, ln2_b):
    assert x.shape[0] == 1 and c.shape[0] == 1 and ctx.shape[1] == CTX
    x_all = jnp.concatenate([ctx[0], x[0]], axis=0)
    t = x_all.shape[0]
    tm = ROW_TILE if t % ROW_TILE == 0 else M_CHUNK
    tk = ATT_TK if t % ATT_TK == 0 else M_CHUNK
    assert t % tm == 0 and t % ATT_TQ == 0 and t % tk == 0 and t % M_CHUNK == 0

    cc = jnp.pad(jnp.concatenate([c, c_ctx[None]], axis=0), ((0, 6), (0, 0)))
    mods = _ada(cc, w_ada, b_ada)
    tabs = _rope_tables(t)
    row2 = lambda a: a.reshape(1, -1)
    wg_all, wu_all, wd_all = w_e_gate.astype(BF16), w_e_up.astype(BF16), w_e_down.astype(BF16)

    for l in range(DEPTH):
        w = _layer_weights(l, w_in, w_uq, w_uk, w_uv, w_bo_mla, w_bo_mlstm, w_out, w_rg, b_rg, w_re, b_re)
        mod = mods[l]
        pqd, ckv, kr, mq, mk, mv, po, pg, pm = _inproj(x_all, mod, w["w_in"], tm)
        qt, kk, vt = _mla_prep(pqd, ckv, kr, tabs, row2(g_qn[l]), row2(g_kvn[l]),
                               w["wqt"], w["wqrt"], w["wk"], w["wvt"], tm)
        att = _attention(qt, kk, vt, ATT_TQ, tk, ATT_CB, ATT_DEPTH)
        pg_t = pg[:, :16].T
        bg_c = jnp.pad(b_gates[l], (0, LANES - 16)).reshape(1, LANES)
        bg_r = b_gates[l].reshape(16, 1)
        hf = _mlstm(mq, mk, mv, pg, pg_t, bg_c, bg_r, False, M_CHUNK)
        hb = _mlstm(mq, mk, mv, pg, pg_t, bg_c, bg_r, True, M_CHUNK)
        x1, hm, plan, cnt = _merge(att, hf, hb, po, pm, x_all, mod, row2(g_mh[l]), row2(ln1_g[l]), row2(ln1_b[l]),
                                   w["wa"], w["wm"], w["wo"], w["wr"], w["br"], tm)
        counts = cnt[:, 0, :N_GROUPS].astype(jnp.int32)
        plan_t = plan[:, :2].T
        x_all = _moe(counts, hm, plan, plan_t, x1, mod, row2(ln2_g[l]), row2(ln2_b[l]),
                     wg_all, wu_all, wd_all, l, tm)
    return x_all[CTX:][None]
```

```python
import functools
import math

import jax
import jax.numpy as jnp
from jax import lax
from jax.experimental import pallas as pl
from jax.experimental.pallas import tpu as pltpu

F32 = jnp.float32
BF16 = jnp.bfloat16

D = 1024
DEPTH = 4
GRID_W = 64
CTX = 256
H_MLA = 8
QK_NOPE = 64
QK_ROPE = 32
V_HEAD = 64
Q_LORA = 384
KV_LORA = 256
ROPE_BASE = 10000.0
MLA_SCALE = (QK_NOPE + QK_ROPE) ** -0.5
H_M = 4
MQK = 64
MV = 128
N_GROUPS = 4
EPG = 8
N_EXP = N_GROUPS * EPG
D_EXP = 256
ALPHA = (2 * DEPTH) ** 0.25
LN_EPS = 1e-6

LANES = 128
VMEM_LIMIT = 56 * 1024 * 1024
VMEM_LIMIT_MOE = 58 * 1024 * 1024

HP = LANES
ONES_ROW = V_HEAD
BF16_ROWS = 16
VP = (V_HEAD + 1 + BF16_ROWS - 1) // BF16_ROWS * BF16_ROWS
QSCALE = MLA_SCALE * math.log2(math.e)

SEG_PQD = Q_LORA
SEG_CKV = KV_LORA
SEG_KR = 2 * HP
SEG_Q = H_M * MQK
SEG_K = H_M * MQK
SEG_V = H_M * MV
SEG_PO = H_M * MV
SEG_PG = LANES
SEG_PM = 2 * D
IN_SEGS = (SEG_PQD, SEG_CKV, SEG_KR, SEG_Q, SEG_K, SEG_PO, SEG_PG, SEG_PM)
IN_DTYPES = (F32, F32, F32, BF16, BF16, BF16, F32, BF16)
NP_IN = sum(IN_SEGS)

ROW_TILE = 640
ATT_TQ = 3328
ATT_CB = 256
ATT_TK = 1280
MAX_LAG_JUMP = 64.0
ATT_DEPTH = 2
M_CHUNK = 256
MOE_SLOT_STEP = 64
MOE_SLOT_MAX = 256
MOE_PLAN_TILES = 2
PLAN_GROUP = 0
PLAN_RANK = 1


def _moe_slot_sizes(tm):
    return tuple(s for s in range(MOE_SLOT_STEP, MOE_SLOT_MAX + 1, MOE_SLOT_STEP) if s <= tm)

_NT = (((1,), (1,)), ((), ()))
_TN = (((0,), (0,)), ((), ()))


def _cparams(n_grid, vmem_limit=VMEM_LIMIT):
    return pltpu.CompilerParams(dimension_semantics=("arbitrary",) * n_grid, vmem_limit_bytes=vmem_limit)


def _ln(x):
    mu = jnp.mean(x, axis=-1, keepdims=True)
    xc = x - mu
    var = jnp.mean(xc * xc, axis=-1, keepdims=True)
    return xc * lax.rsqrt(var + LN_EPS)


def _rms(x, g):
    return x * lax.rsqrt(jnp.mean(x * x, axis=-1, keepdims=True) + LN_EPS) * g


def _sigmoid(x):
    return 1.0 / (1.0 + jnp.exp(-x))


def _is_ctx_rows(tm):
    rows = pl.program_id(0) * tm + lax.broadcasted_iota(jnp.int32, (tm, 1), 0)
    return rows < CTX


def _mod(mod_ref, k, is_ctx):
    lat = mod_ref[0:1, k * D:(k + 1) * D]
    ctx = mod_ref[1:2, k * D:(k + 1) * D]
    return jnp.where(is_ctx, ctx, lat)


def _ada_kernel(c_ref, w_ref, b_ref, o_ref):
    c = c_ref[...]
    s = c * _sigmoid(c)
    o_ref[...] = jnp.dot(s, w_ref[...], preferred_element_type=F32,
                         precision=lax.Precision.HIGHEST) + b_ref[...]


def _ada(cc, w_ada, b_ada):
    tn = 1536
    n = 6 * D
    return pl.pallas_call(
        _ada_kernel,
        grid=(DEPTH, n // tn),
        in_specs=[pl.BlockSpec((8, D), lambda l, j: (0, 0)),
                  pl.BlockSpec((None, D, tn), lambda l, j: (l, 0, j)),
                  pl.BlockSpec((None, 1, tn), lambda l, j: (l, 0, j))],
        out_specs=pl.BlockSpec((None, 8, tn), lambda l, j: (l, 0, j)),
        out_shape=jax.ShapeDtypeStruct((DEPTH, 8, n), F32),
        compiler_params=_cparams(2),
        name="ada",
    )(cc, w_ada, b_ada.reshape(DEPTH, 1, n))


def _inproj_kernel(x_ref, mod_ref, w_ref, wvt_ref, *o_refs, tm):
    is_ctx = _is_ctx_rows(tm)
    h = _ln(x_ref[...]) * (1.0 + _mod(mod_ref, 1, is_ctx)) + _mod(mod_ref, 0, is_ctx)
    hb = h.astype(BF16)
    off = 0
    for o in o_refs[:-1]:
        n = o.shape[-1]
        o[...] = jnp.dot(hb, w_ref[:, off:off + n], preferred_element_type=F32).astype(o.dtype)
        off += n
    o_refs[-1][...] = lax.dot_general(wvt_ref[...], hb, _NT, preferred_element_type=F32).astype(BF16)


def _inproj(x_all, mod, w_in_p, w_vt, tm):
    t = x_all.shape[0]
    return pl.pallas_call(
        functools.partial(_inproj_kernel, tm=tm),
        grid=(t // tm,),
        in_specs=[pl.BlockSpec((tm, D), lambda i: (i, 0)),
                  pl.BlockSpec((8, 6 * D), lambda i: (0, 0)),
                  pl.BlockSpec((D, NP_IN), lambda i: (0, 0)),
                  pl.BlockSpec((SEG_V, D), lambda i: (0, 0))],
        out_specs=[pl.BlockSpec((tm, n), lambda i: (i, 0)) for n in IN_SEGS]
        + [pl.BlockSpec((SEG_V, tm), lambda i: (0, i))],
        out_shape=[jax.ShapeDtypeStruct((t, n), dt) for n, dt in zip(IN_SEGS, IN_DTYPES)]
        + [jax.ShapeDtypeStruct((SEG_V, t), BF16)],
        compiler_params=_cparams(1),
        name="inproj",
    )(x_all, mod, w_in_p, w_vt)


def _mla_prep_kernel(pqd_ref, ckv_ref, kr_ref, cos_ref, sin_ref, cost_ref, sint_ref, gq_ref, gk_ref,
                     wqt_ref, wqrt_ref, wk_ref, wvt_ref, qt_ref, k_ref, vt_ref, *, tm):
    qn = _rms(pqd_ref[...], gq_ref[...]).astype(BF16)
    qa = lax.dot_general(wqt_ref[...], qn, _NT, preferred_element_type=F32)
    qb = lax.dot_general(wqrt_ref[...], qn, _NT, preferred_element_type=F32)
    cost = cost_ref[...]
    sint = sint_ref[...]
    ckv = _rms(ckv_ref[...], gk_ref[...]).astype(BF16)
    kn = jnp.dot(ckv, wk_ref[...], preferred_element_type=F32)
    kr = kr_ref[:, :HP] * cos_ref[...] + kr_ref[:, HP:] * sin_ref[...]
    vt = lax.dot_general(wvt_ref[...], ckv, _NT, preferred_element_type=F32)
    is_ones_row = lax.broadcasted_iota(jnp.int32, (VP, tm), 0) == ONES_ROW
    for h in range(H_MLA):
        sl = slice(h * HP, (h + 1) * HP)
        qt_ref[h] = ((qa[sl] * cost + qb[sl] * sint) * QSCALE).astype(BF16)
        k_ref[h] = (kn[:, sl] + kr).astype(BF16)
        vt_ref[h] = jnp.where(is_ones_row, 1.0, vt[h * VP:(h + 1) * VP]).astype(BF16)


def _mla_prep(pqd, ckv, kr, tabs, g_qn, g_kvn, wqt, wqrt, wk, wvt, tm):
    t = pqd.shape[0]
    cos, sin, cost, sint = tabs
    row = lambda n: pl.BlockSpec((tm, n), lambda i: (i, 0))
    col = lambda n: pl.BlockSpec((n, tm), lambda i: (0, i))
    full = lambda a: pl.BlockSpec(a.shape, lambda i: (0,) * a.ndim)
    hd = H_MLA * HP
    return pl.pallas_call(
        functools.partial(_mla_prep_kernel, tm=tm),
        grid=(t // tm,),
        in_specs=[row(Q_LORA), row(KV_LORA), row(SEG_KR), row(HP), row(HP), col(HP), col(HP),
                  full(g_qn), full(g_kvn), full(wqt), full(wqrt), full(wk), full(wvt)],
        out_specs=[pl.BlockSpec((H_MLA, HP, tm), lambda i: (0, 0, i)),
                   pl.BlockSpec((H_MLA, tm, HP), lambda i: (0, i, 0)),
                   pl.BlockSpec((H_MLA, VP, tm), lambda i: (0, 0, i))],
        out_shape=[jax.ShapeDtypeStruct((H_MLA, HP, t), BF16),
                   jax.ShapeDtypeStruct((H_MLA, t, HP), BF16),
                   jax.ShapeDtypeStruct((H_MLA, VP, t), BF16)],
        compiler_params=_cparams(1),
        name="mla_prep",
    )(pqd, ckv, kr, cos, sin, cost, sint, g_qn, g_kvn, wqt, wqrt, wk, wvt)


def _scores(k, qt, m):
    s = jnp.dot(k, qt, preferred_element_type=F32)
    return s, jnp.maximum(m, jnp.max(s, axis=0, keepdims=True))


def _accumulate(s, vt, m, m_new, acc):
    p = jnp.exp2(s - m_new).astype(BF16)
    return jnp.exp2(m - m_new) * acc + jnp.dot(vt, p, preferred_element_type=F32)


def _normalized_rows(acc):
    o = acc / acc[ONES_ROW:ONES_ROW + 1, :]
    return jnp.concatenate([o, jnp.zeros((HP - VP, o.shape[1]), F32)], axis=0).T


def _lagged_pass(qt_ref, k_ref, vt_ref, m_ref, acc_ref, p_ref, sc_ref, *, ncb, n_kv, tk, cb, depth):
    k0 = k_ref[0:BF16_ROWS, :]
    for c in range(ncb):
        s0 = jnp.dot(k0, qt_ref[:, c * cb:(c + 1) * cb], preferred_element_type=F32)
        m_ref[c] = jnp.max(s0, axis=0, keepdims=True)
    acc_ref[...] = jnp.zeros(acc_ref.shape, F32)
    p_ref[...] = jnp.zeros(p_ref.shape, BF16)
    sc_ref[...] = jnp.ones(sc_ref.shape, F32)

    def carried(vt_prev):
        return [(ncb - depth + d, p_ref[d], sc_ref[d], vt_prev) for d in range(depth)]

    def drain(entry):
        pc, pp, psc, pvt = entry
        acc_ref[pc] = (acc_ref[pc] + jnp.dot(pvt, pp, preferred_element_type=F32)) * psc

    def step(j, jump):
        off = pl.multiple_of(j * tk, tk)
        off_prev = pl.multiple_of(jnp.maximum(j - 1, 0) * tk, tk)
        k = k_ref[pl.ds(off, tk), :]
        vt = vt_ref[:, pl.ds(off, tk)]
        pending = carried(vt_ref[:, pl.ds(off_prev, tk)])
        for c in range(ncb):
            m = m_ref[c]
            s = jnp.dot(k, qt_ref[:, c * cb:(c + 1) * cb], preferred_element_type=F32)
            p = jnp.exp2(s - m).astype(BF16)
            cm = jnp.max(s, axis=0, keepdims=True)
            m_new = jnp.maximum(m, cm)
            jump = jnp.maximum(jump, cm - m)
            m_ref[c] = m_new
            drain(pending.pop(0))
            pending.append((c, p, jnp.exp2(m - m_new), vt))
        for d, (_, pp, psc, _) in enumerate(pending):
            p_ref[d] = pp
            sc_ref[d] = psc
        return jump

    jump = lax.fori_loop(0, n_kv, step, jnp.zeros((1, cb), F32))
    for entry in carried(vt_ref[:, (n_kv - 1) * tk:]):
        drain(entry)
    return jump


def _exact_pass(qt_ref, k_ref, vt_ref, m_ref, acc_ref, s_ref, mp_ref, *, ncb, n_kv, tk, cb, depth):
    m_ref[...] = jnp.full(m_ref.shape, -1e30, F32)
    acc_ref[...] = jnp.zeros(acc_ref.shape, F32)
    s_ref[...] = jnp.full(s_ref.shape, -jnp.inf, F32)
    mp_ref[...] = jnp.full(mp_ref.shape, -1e30, F32)

    def carried(vt_prev):
        return [(ncb - depth + d, s_ref[d], mp_ref[d, 0], mp_ref[d, 1], vt_prev) for d in range(depth)]

    def drain(entry):
        pc, ps, pm_old, pm_new, pvt = entry
        acc_ref[pc] = _accumulate(ps, pvt, pm_old, pm_new, acc_ref[pc])

    def step(j, carry):
        off = pl.multiple_of(j * tk, tk)
        off_prev = pl.multiple_of(jnp.maximum(j - 1, 0) * tk, tk)
        k = k_ref[pl.ds(off, tk), :]
        vt = vt_ref[:, pl.ds(off, tk)]
        pending = carried(vt_ref[:, pl.ds(off_prev, tk)])
        for c in range(ncb):
            m_old = m_ref[c]
            s, m_new = _scores(k, qt_ref[:, c * cb:(c + 1) * cb], m_old)
            m_ref[c] = m_new
            drain(pending.pop(0))
            pending.append((c, s, m_old, m_new, vt))
        for d, (_, ps, pm_old, pm_new, _) in enumerate(pending):
            s_ref[d] = ps
            mp_ref[d, 0] = pm_old
            mp_ref[d, 1] = pm_new
        return carry

    lax.fori_loop(0, n_kv, step, 0)
    for entry in carried(vt_ref[:, (n_kv - 1) * tk:]):
        drain(entry)


def _attn_kernel(qt_ref, k_ref, vt_ref, o_ref, m_ref, acc_ref, s_ref, mp_ref, p_ref, sc_ref, *,
                 tq, tk, t, cb, depth):
    geom = dict(ncb=tq // cb, n_kv=t // tk, tk=tk, cb=cb, depth=depth)

    def write_out():
        for c in range(tq // cb):
            o_ref[c * cb:(c + 1) * cb, :] = _normalized_rows(acc_ref[c]).astype(o_ref.dtype)

    jump = _lagged_pass(qt_ref, k_ref, vt_ref, m_ref, acc_ref, p_ref, sc_ref, **geom)
    write_out()

    @pl.when(jnp.max(jump) > MAX_LAG_JUMP)
    def _():
        _exact_pass(qt_ref, k_ref, vt_ref, m_ref, acc_ref, s_ref, mp_ref, **geom)
        write_out()

    @pl.when(pl.program_id(1) == 0)
    def _():
        m0 = jnp.full((1, CTX), -1e30, F32)
        s, m_new = _scores(k_ref[0:CTX, :], qt_ref[:, 0:CTX], m0)
        acc_c = _accumulate(s, vt_ref[:, 0:CTX], m0, m_new, jnp.zeros((VP, CTX), F32))
        o_ref[0:CTX, :] = _normalized_rows(acc_c).astype(o_ref.dtype)


def _attention(qt, k, vt, tq, tk, cb, depth):
    t = k.shape[1]
    assert depth <= tq // cb
    return pl.pallas_call(
        functools.partial(_attn_kernel, tq=tq, tk=tk, t=t, cb=cb, depth=depth),
        grid=(H_MLA, t // tq),
        in_specs=[pl.BlockSpec((None, HP, tq), lambda h, i: (h, 0, i)),
                  pl.BlockSpec((None, t, HP), lambda h, i: (h, 0, 0)),
                  pl.BlockSpec((None, VP, t), lambda h, i: (h, 0, 0))],
        out_specs=pl.BlockSpec((tq, HP), lambda h, i: (i, h)),
        out_shape=jax.ShapeDtypeStruct((t, H_MLA * HP), BF16),
        scratch_shapes=[pltpu.VMEM((tq // cb, 1, cb), F32), pltpu.VMEM((tq // cb, VP, cb), F32),
                        pltpu.VMEM((depth, tk, cb), F32), pltpu.VMEM((depth, 2, 1, cb), F32),
                        pltpu.VMEM((depth, tk, cb), BF16), pltpu.VMEM((depth, 1, cb), F32)],
        compiler_params=_cparams(2),
        name="attention",
    )(qt, k, vt)


def _split3(x):
    hi = x.astype(BF16)
    r = x - hi.astype(F32)
    mid = r.astype(BF16)
    lo = (r - mid.astype(F32)).astype(BF16)
    return hi, mid, lo


def _log_sigmoid(x):
    return jnp.minimum(x, 0.0) - jnp.log(1.0 + jnp.exp(-jnp.abs(x)))


def _mlstm_kernel(qf_ref, kf_ref, vtf_ref, gcf_ref, grf_ref, qb_ref, kb_ref, vtb_ref, gcb_ref, grb_ref,
                  bc_ref, br_ref, of_ref, ob_ref, ct_scr, m_scr, *, L):
    @pl.when(pl.program_id(0) == 0)
    def _():
        ct_scr[...] = jnp.zeros_like(ct_scr)
        m_scr[...] = jnp.zeros_like(m_scr)

    _mlstm_chunk(qf_ref, kf_ref, vtf_ref, gcf_ref, grf_ref, bc_ref, br_ref, of_ref, ct_scr.at[0], m_scr.at[0],
                 L=L, rev=0)
    _mlstm_chunk(qb_ref, kb_ref, vtb_ref, gcb_ref, grb_ref, bc_ref, br_ref, ob_ref, ct_scr.at[1], m_scr.at[1],
                 L=L, rev=1)


def _mlstm_chunk(q_ref, k_ref, vt_ref, gc_ref, gr_ref, bc_ref, br_ref, o_ref, ct_scr, m_scr, *, L, rev):

    row = lax.broadcasted_iota(jnp.int32, (L, L), 0)
    col = lax.broadcasted_iota(jnp.int32, (L, L), 1)
    before = (row >= col) if rev else (row <= col)
    before_b = before.astype(BF16)
    after_b = ((row <= col) if rev else (row >= col)).astype(BF16)

    gc = gc_ref[...] + bc_ref[...]
    gr = gr_ref[...] + br_ref[...]
    lfc = _log_sigmoid(gc)
    lfr = _log_sigmoid(gr)
    b_col = sum(jnp.dot(after_b, p, preferred_element_type=F32) for p in _split3(lfc))
    b_row = sum(jnp.dot(p, before_b, preferred_element_type=F32) for p in _split3(lfr))
    ones_rows = (lax.broadcasted_iota(jnp.int32, (MV, L), 0) == 0).astype(BF16)

    for h in range(H_M):
        ci = 8 * rev + h
        cf = ci + 4
        g_c = gc[:, ci:ci + 1] - b_col[:, cf:cf + 1]
        b_r = b_row[cf:cf + 1, :]
        i_r = gr[ci:ci + 1, :]
        bl = jnp.sum(lfr[cf:cf + 1, :], axis=1, keepdims=True)
        m = m_scr[h][0:1, 0:1]
        ct = ct_scr[h]

        pm = jnp.max(jnp.where(before, g_c, -jnp.inf), axis=0, keepdims=True)
        inter = b_r + m
        mj = jnp.maximum(inter, b_r + pm)
        w = jnp.exp(jnp.where(before, g_c + (b_r - mj), -jnp.inf))
        sc = jnp.exp(inter - mj)

        q = q_ref[:, h * MQK:(h + 1) * MQK] * (MQK ** -0.5)
        k = k_ref[:, h * MQK:(h + 1) * MQK]
        vt_aug = jnp.concatenate([vt_ref[h * MV:(h + 1) * MV, :], ones_rows], axis=0)
        kq = lax.dot_general(k, q, _NT, preferred_element_type=F32) * w
        num = (jnp.dot(vt_aug, kq.astype(BF16), preferred_element_type=F32)
               + sc * lax.dot_general(ct.astype(BF16), q, _NT, preferred_element_type=F32))
        den = jnp.maximum(jnp.abs(num[MV:MV + 1, :]), jnp.exp(-mj))
        o_ref[:, h * MV:(h + 1) * MV] = (num[:MV, :] / den).T

        wlog = bl - b_r + i_r
        m_new = jnp.maximum(bl + m, jnp.max(wlog, axis=1, keepdims=True))
        ws = jnp.exp(wlog - m_new)
        sd = jnp.exp(bl + m - m_new)
        wvt = (vt_aug.astype(F32) * ws).astype(BF16)
        ct_scr[h] = sd * ct + jnp.dot(wvt, k, preferred_element_type=F32)
        m_scr[h] = jnp.broadcast_to(m_new, m_scr.shape[1:])


def _mlstm(mq, mk, mvt, pg, pg_t, bg_c, bg_r, L):
    t = mq.shape[0]
    nc = t // L
    fwd = lambda c: c
    bwd = lambda c: jnp.where(c == 0, 0, nc - c)

    def chunk_specs(idx):
        return [pl.BlockSpec((L, SEG_Q), lambda c: (idx(c), 0)),
                pl.BlockSpec((L, SEG_K), lambda c: (idx(c), 0)),
                pl.BlockSpec((SEG_V, L), lambda c: (0, idx(c))),
                pl.BlockSpec((L, LANES), lambda c: (idx(c), 0)),
                pl.BlockSpec((16, L), lambda c: (0, idx(c)))]

    chunk_args = (mq, mk, mvt, pg, pg_t)
    return pl.pallas_call(
        functools.partial(_mlstm_kernel, L=L),
        grid=(nc,),
        in_specs=chunk_specs(fwd) + chunk_specs(bwd)
        + [pl.BlockSpec((1, LANES), lambda c: (0, 0)), pl.BlockSpec((16, 1), lambda c: (0, 0))],
        out_specs=[pl.BlockSpec((L, H_M * MV), lambda c: (fwd(c), 0)),
                   pl.BlockSpec((L, H_M * MV), lambda c: (bwd(c), 0))],
        out_shape=[jax.ShapeDtypeStruct((t, H_M * MV), F32)] * 2,
        scratch_shapes=[pltpu.VMEM((2, H_M, 2 * MV, MQK), F32), pltpu.VMEM((2, H_M, 8, LANES), F32)],
        compiler_params=_cparams(1),
        name="mlstm",
    )(*chunk_args, *chunk_args, bg_c, bg_r)


def _router(logits):
    tm = logits.shape[0]
    lane = lax.broadcasted_iota(jnp.int32, (tm, LANES), 1)
    big = jnp.int32(LANES)
    neg = -jnp.inf
    is_g = lane < N_GROUPS
    lg = jnp.where(is_g, logits, neg)
    gmax = jnp.max(lg, axis=-1, keepdims=True)
    grp = jnp.min(jnp.where(lg == gmax, lane, big), axis=-1, keepdims=True)
    p_grp = 1.0 / jnp.sum(jnp.where(is_g, jnp.exp(lg - gmax), 0.0), axis=-1, keepdims=True)
    e_lo = N_GROUPS + grp * EPG
    in_grp = (lane >= e_lo) & (lane < e_lo + EPG)
    le = jnp.where(in_grp, logits, neg)
    l1 = jnp.max(le, axis=-1, keepdims=True)
    i1 = jnp.min(jnp.where(le == l1, lane, big), axis=-1, keepdims=True)
    le2 = jnp.where(lane == i1, neg, le)
    l2 = jnp.max(le2, axis=-1, keepdims=True)
    i2 = jnp.min(jnp.where(le2 == l2, lane, big), axis=-1, keepdims=True)
    r = jnp.exp(l2 - l1)
    w1 = p_grp / (1.0 + r)
    w2 = w1 * r
    return jnp.where(lane == i1, w1, 0.0) + jnp.where(lane == i2, w2, 0.0), grp


def _merge_kernel(att_ref, hf_ref, hb_ref, po_ref, pm_ref, x_ref, mod_ref, gmh_ref, l1g_ref, l1b_ref,
                  wa_ref, wm_ref, wo_ref, wr_ref, br_ref, x1_ref, hm_ref, plan_ref, cnt_ref, *, tm):
    is_ctx = _is_ctx_rows(tm)
    y_mla = jnp.dot(att_ref[...], wa_ref[...], preferred_element_type=F32)
    hs = hf_ref[...] + hb_ref[...]
    parts = []
    for h in range(H_M):
        sl = slice(h * MV, (h + 1) * MV)
        parts.append(_ln(hs[:, sl]))
    hn = jnp.concatenate(parts, axis=1) * gmh_ref[...] * _sigmoid(po_ref[...].astype(F32))
    y_ml = jnp.dot(hn.astype(BF16), wm_ref[...], preferred_element_type=F32)
    g_a = _sigmoid(pm_ref[:, :D].astype(F32))
    g_b = _sigmoid(pm_ref[:, D:].astype(F32))
    y = jnp.dot((g_a * y_mla + g_b * y_ml).astype(BF16), wo_ref[...], preferred_element_type=F32)
    x1 = _ln(ALPHA * x_ref[...] + _mod(mod_ref, 2, is_ctx) * y) * l1g_ref[...] + l1b_ref[...]
    x1_ref[...] = x1
    hm = _ln(x1) * (1.0 + _mod(mod_ref, 4, is_ctx)) + _mod(mod_ref, 3, is_ctx)
    hm_ref[...] = hm.astype(BF16)
    h_hi = hm.astype(BF16)
    h_lo = (hm - h_hi.astype(F32)).astype(BF16)
    logits = (jnp.dot(h_hi, wr_ref[0], preferred_element_type=F32)
              + jnp.dot(h_hi, wr_ref[1], preferred_element_type=F32)
              + jnp.dot(h_lo, wr_ref[0], preferred_element_type=F32)) + br_ref[...]
    comb, grp = _router(logits)
    lane = lax.broadcasted_iota(jnp.int32, (tm, LANES), 1)
    in_group = lane == grp
    earlier = (lax.broadcasted_iota(jnp.int32, (tm, tm), 0)
               > lax.broadcasted_iota(jnp.int32, (tm, tm), 1)).astype(BF16)
    n_before = jnp.dot(earlier, in_group.astype(BF16), preferred_element_type=F32)
    rank = jnp.sum(jnp.where(in_group, n_before, 0.0), axis=-1, keepdims=True)
    plan_ref[...] = (comb + jnp.where(lane == PLAN_GROUP, grp.astype(F32), 0.0)
                     + jnp.where(lane == PLAN_RANK, rank, 0.0))
    cnt_ref[...] = jnp.broadcast_to(jnp.sum(in_group.astype(F32), axis=0, keepdims=True), cnt_ref.shape)


def _merge(att, hf, hb, po, pm, x_all, mod, g_mh, l1g, l1b, wa, wm, wo, wr, br, tm):
    t = x_all.shape[0]
    row = lambda n: pl.BlockSpec((tm, n), lambda i: (i, 0))
    full = lambda a: pl.BlockSpec(a.shape, lambda i: (0,) * a.ndim)
    return pl.pallas_call(
        functools.partial(_merge_kernel, tm=tm),
        grid=(t // tm,),
        in_specs=[row(H_MLA * HP), row(H_M * MV), row(H_M * MV), row(SEG_PO), row(SEG_PM), row(D),
                  full(mod), full(g_mh), full(l1g), full(l1b), full(wa), full(wm), full(wo), full(wr), full(br)],
        out_specs=[row(D), row(D), row(LANES), pl.BlockSpec((None, 8, LANES), lambda i: (i, 0, 0))],
        out_shape=[jax.ShapeDtypeStruct((t, D), F32), jax.ShapeDtypeStruct((t, D), BF16),
                   jax.ShapeDtypeStruct((t, LANES), F32), jax.ShapeDtypeStruct((t // tm, 8, LANES), F32)],
        compiler_params=_cparams(1),
        name="merge",
    )(att, hf, hb, po, pm, x_all, mod, g_mh, l1g, l1b, wa, wm, wo, wr, br)


def _moe_kernel(cnt_ref, layer_ref, hm_ref, plan_ref, plant_ref, x1_ref, mod_ref, l2g_ref, l2b_ref, wg_ref, wu_ref, wd_ref,
                o_ref, *, tm, tp, slot_sizes):
    i = pl.program_id(0)
    grp = pl.program_id(1)
    n_sub = tm // tp

    @pl.when(grp == 0)
    def _():
        o_ref[...] = jnp.zeros_like(o_ref)

    n_tok = cnt_ref[i * n_sub, grp]
    for j in range(1, n_sub):
        n_tok = jnp.maximum(n_tok, cnt_ref[i * n_sub + j, grp])

    def rounds_of(slots):
        return functools.partial(_moe_round, hm_ref, plan_ref, plant_ref, wg_ref, wu_ref, wd_ref, o_ref,
                                 grp=grp, n_sub=n_sub, tp=tp, slots=slots)

    big = slot_sizes[-1]
    tails = slot_sizes[:-1]
    n_full = lax.div(n_tok, big) + (lax.rem(n_tok, big) > (tails[-1] if tails else 0)).astype(jnp.int32)
    rest = jnp.maximum(n_tok - n_full * big, 0)
    full_round = rounds_of(big)
    lax.fori_loop(0, n_full, lambda r, carry: full_round(r * big, carry), 0)
    lower = 0
    for slots in tails:
        @pl.when((rest > lower) & (rest <= slots))
        def _(tail_round=rounds_of(slots)):
            tail_round(n_full * big, 0)

        lower = slots

    @pl.when(grp == N_GROUPS - 1)
    def _():
        is_ctx = _is_ctx_rows(tm)
        z = ALPHA * x1_ref[...] + _mod(mod_ref, 5, is_ctx) * o_ref[...]
        o_ref[...] = _ln(z) * l2g_ref[...] + l2b_ref[...]


def _moe_round(hm_ref, plan_ref, plant_ref, wg_ref, wu_ref, wd_ref, o_ref, first, carry, *, grp, n_sub, tp, slots):
    grp_f = grp.astype(F32)
    base = first.astype(F32)
    slot_rows = lax.broadcasted_iota(jnp.int32, (slots, tp), 0).astype(F32)
    slot_cols = lax.broadcasted_iota(jnp.int32, (tp, slots), 1).astype(F32)
    lane = lax.broadcasted_iota(jnp.int32, (n_sub * slots, LANES), 1)
    xs, cs = [], []
    for j in range(n_sub):
        rows = slice(j * tp, (j + 1) * tp)
        in_grp = plant_ref[PLAN_GROUP:PLAN_GROUP + 1, rows] == grp_f
        rank = plant_ref[PLAN_RANK:PLAN_RANK + 1, rows]
        gather = ((slot_rows == rank - base) & in_grp).astype(BF16)
        xs.append(jnp.dot(gather, hm_ref[rows, :], preferred_element_type=F32).astype(BF16))
        cs.append(sum(jnp.dot(gather, p, preferred_element_type=F32)
                      for p in _split3(plan_ref[rows, :])))
    xs = jnp.concatenate(xs, axis=0)
    cs = jnp.concatenate(cs, axis=0)
    hid = []
    for g in range(EPG):
        c_e = jnp.sum(jnp.where(lane == N_GROUPS + grp * EPG + g, cs, 0.0), axis=-1, keepdims=True)
        a = jnp.dot(xs, wg_ref[g], preferred_element_type=F32)
        u = jnp.dot(xs, wu_ref[g], preferred_element_type=F32)
        hid.append((a * _sigmoid(a) * u * c_e).astype(BF16))
    ys = jnp.dot(jnp.concatenate(hid, axis=1), wd_ref[...].reshape(EPG * D_EXP, D),
                 preferred_element_type=F32)
    ys_hi = ys.astype(BF16)
    ys_lo = (ys - ys_hi.astype(F32)).astype(BF16)
    for j in range(n_sub):
        rows = slice(j * tp, (j + 1) * tp)
        mine = slice(j * slots, (j + 1) * slots)
        in_grp = plan_ref[rows, PLAN_GROUP:PLAN_GROUP + 1] == grp_f
        rank = plan_ref[rows, PLAN_RANK:PLAN_RANK + 1]
        scatter = ((slot_cols == rank - base) & in_grp).astype(BF16)
        o_ref[rows, :] += (jnp.dot(scatter, ys_hi[mine], preferred_element_type=F32)
                           + jnp.dot(scatter, ys_lo[mine], preferred_element_type=F32))
    return carry


def _moe(counts, hm, plan, plan_t, x1, mod, l2g, l2b, wg, wu, wd, layer, tp):
    t = hm.shape[0]
    tm = tp * (MOE_PLAN_TILES if (t // tp) % MOE_PLAN_TILES == 0 else 1)
    row = lambda n, **kw: pl.BlockSpec((tm, n), lambda i, e, cnt, lay: (i, 0), **kw)
    full = lambda a: pl.BlockSpec(a.shape, lambda i, e, cnt, lay: (0,) * a.ndim)
    group_of = lambda a: pl.BlockSpec((None, EPG) + a.shape[2:], lambda i, e, cnt, lay: (lay[0], e, 0, 0))
    return pl.pallas_call(
        functools.partial(_moe_kernel, tm=tm, tp=tp, slot_sizes=_moe_slot_sizes(tp)),
        grid_spec=pltpu.PrefetchScalarGridSpec(
            num_scalar_prefetch=2,
            grid=(t // tm, N_GROUPS),
            in_specs=[row(D), row(LANES), pl.BlockSpec((2, tm), lambda i, e, cnt, lay: (0, i)),
                      row(D, pipeline_mode=pl.Buffered(1)),
                      full(mod), full(l2g), full(l2b), group_of(wg), group_of(wu), group_of(wd)],
            out_specs=row(D)),
        out_shape=jax.ShapeDtypeStruct((t, D), F32),
        compiler_params=_cparams(2, VMEM_LIMIT_MOE),
        name="moe",
    )(counts, jnp.full((1,), layer, jnp.int32), hm, plan, plan_t, x1, mod, l2g, l2b, wg, wu, wd)


def _rot_cols(w):
    a1, a2, b1, b2 = jnp.split(w, 4, axis=-1)
    return jnp.concatenate([-a2, a1, -b2, b1], axis=-1)


def _place_rope(w):
    return jnp.pad(w, ((0, 0), (QK_NOPE, HP - QK_NOPE - QK_ROPE)))


def _rope_tables(t):
    rows = (t - CTX) // GRID_W
    row, col = jnp.meshgrid(jnp.arange(rows, dtype=F32), jnp.arange(GRID_W, dtype=F32), indexing="ij")
    row, col = row.reshape(-1), col.reshape(-1)
    half = QK_ROPE // 2
    inv = ROPE_BASE ** (-jnp.arange(0, half, 2, dtype=F32) / half)
    ar, ac = row[:, None] * inv, col[:, None] * inv
    ang = jnp.concatenate([ar, ar, ac, ac], axis=-1)
    ang = jnp.concatenate([jnp.zeros((CTX, QK_ROPE), F32), ang], axis=0)
    pad = ((0, 0), (QK_NOPE, HP - QK_NOPE - QK_ROPE))
    cos = jnp.pad(jnp.cos(ang), pad) + jnp.pad(jnp.ones((t, QK_NOPE), F32), ((0, 0), (0, HP - QK_NOPE)))
    sin = jnp.pad(jnp.sin(ang), pad)
    return cos, sin, cos.T, sin.T


def _layer_weights(l, w_in, w_uq, w_uk, w_uv, w_bo_mla, w_bo_mlstm, w_out, w_rg, b_rg, w_re, b_re):
    wi = w_in[l]
    o = 0
    segs = []
    for n in (Q_LORA, KV_LORA, QK_ROPE, SEG_Q, SEG_K, SEG_V, SEG_PO, 4 * H_M, SEG_PM):
        segs.append(wi[:, o:o + n])
        o += n
    s_pqd, s_ckv, s_kr, s_q, s_k, s_v, s_po, s_pg, s_pm = segs
    w_in_p = jnp.concatenate(
        [s_pqd, s_ckv, _place_rope(s_kr), _place_rope(_rot_cols(s_kr)), s_q, s_k, s_po,
         jnp.pad(s_pg, ((0, 0), (0, SEG_PG - 4 * H_M))), s_pm], axis=1).astype(BF16)

    uq = w_uq[l].reshape(Q_LORA, H_MLA, QK_NOPE + QK_ROPE)
    padh = ((0, 0), (0, 0), (0, HP - QK_NOPE - QK_ROPE))
    wq = jnp.pad(uq, padh).reshape(Q_LORA, H_MLA * HP)
    wqr = jnp.pad(jnp.concatenate([jnp.zeros_like(uq[..., :QK_NOPE]), _rot_cols(uq[..., QK_NOPE:])], axis=-1),
                  padh).reshape(Q_LORA, H_MLA * HP)
    padv = ((0, 0), (0, 0), (0, HP - V_HEAD))
    wk = jnp.pad(w_uk[l].reshape(KV_LORA, H_MLA, QK_NOPE), padv).reshape(KV_LORA, H_MLA * HP)
    wv = jnp.pad(w_uv[l].reshape(KV_LORA, H_MLA, V_HEAD), ((0, 0), (0, 0), (0, VP - V_HEAD))).reshape(KV_LORA, H_MLA * VP)
    wa = jnp.pad(w_bo_mla[l].reshape(H_MLA, V_HEAD, D), ((0, 0), (0, HP - V_HEAD), (0, 0))).reshape(H_MLA * HP, D)

    wr = jnp.pad(jnp.concatenate([w_rg[l], w_re[l]], axis=1), ((0, 0), (0, LANES - N_GROUPS - N_EXP)))
    wr_hi = wr.astype(BF16)
    wr_lo = (wr - wr_hi.astype(F32)).astype(BF16)
    br = jnp.pad(jnp.concatenate([b_rg[l], b_re[l]]), (0, LANES - N_GROUPS - N_EXP)).reshape(1, LANES)
    return dict(
        w_in=w_in_p, w_mvt=s_v.T.astype(BF16), wqt=wq.T.astype(BF16), wqrt=wqr.T.astype(BF16), wk=wk.astype(BF16), wvt=wv.T.astype(BF16),
        wa=wa.astype(BF16), wm=w_bo_mlstm[l].astype(BF16), wo=w_out[l].astype(BF16),
        wr=jnp.stack([wr_hi, wr_lo]), br=br)


def kernel(x, c, ctx, c_ctx, w_ada, b_ada, w_in, b_gates, w_uq, w_uk, w_uv, g_qn, g_kvn, g_mh, w_bo_mla,
           w_bo_mlstm, w_out, ln1_g, ln1_b, w_rg, b_rg, w_re, b_re, w_e_gate, w_e_up, w_e_down, ln2_g, ln2_b):
    assert x.shape[0] == 1 and c.shape[0] == 1 and ctx.shape[1] == CTX
    x_all = jnp.concatenate([ctx[0], x[0]], axis=0)
    t = x_all.shape[0]
    tm = ROW_TILE if t % ROW_TILE == 0 else M_CHUNK
    tk = ATT_TK if t % ATT_TK == 0 else M_CHUNK
    assert t % tm == 0 and t % ATT_TQ == 0 and t % tk == 0 and t % M_CHUNK == 0

    cc = jnp.pad(jnp.concatenate([c, c_ctx[None]], axis=0), ((0, 6), (0, 0)))
    mods = _ada(cc, w_ada, b_ada)
    tabs = _rope_tables(t)
    row2 = lambda a: a.reshape(1, -1)
    wg_all, wu_all, wd_all = w_e_gate.astype(BF16), w_e_up.astype(BF16), w_e_down.astype(BF16)

    for l in range(DEPTH):
        w = _layer_weights(l, w_in, w_uq, w_uk, w_uv, w_bo_mla, w_bo_mlstm, w_out, w_rg, b_rg, w_re, b_re)
        mod = mods[l]
        pqd, ckv, kr, mq, mk, po, pg, pm, mvt = _inproj(x_all, mod, w["w_in"], w["w_mvt"], tm)
        qt, kk, vt = _mla_prep(pqd, ckv, kr, tabs, row2(g_qn[l]), row2(g_kvn[l]),
                               w["wqt"], w["wqrt"], w["wk"], w["wvt"], tm)
        att = _attention(qt, kk, vt, ATT_TQ, tk, ATT_CB, ATT_DEPTH)
        pg_t = pg[:, :16].T
        bg_c = jnp.pad(b_gates[l], (0, LANES - 16)).reshape(1, LANES)
        bg_r = b_gates[l].reshape(16, 1)
        hf, hb = _mlstm(mq, mk, mvt, pg, pg_t, bg_c, bg_r, M_CHUNK)
        x1, hm, plan, cnt = _merge(att, hf, hb, po, pm, x_all, mod, row2(g_mh[l]), row2(ln1_g[l]), row2(ln1_b[l]),
                                   w["wa"], w["wm"], w["wo"], w["wr"], w["br"], tm)
        counts = cnt[:, 0, :N_GROUPS].astype(jnp.int32)
        plan_t = plan[:, :2].T
        x_all = _moe(counts, hm, plan, plan_t, x1, mod, row2(ln2_g[l]), row2(ln2_b[l]),
                     wg_all, wu_all, wd_all, l, tm)
    return x_all[CTX:][None]
```

```python
import functools
import math

import jax
import jax.numpy as jnp
from jax import lax
from jax.experimental import pallas as pl
from jax.experimental.pallas import tpu as pltpu

F32 = jnp.float32
BF16 = jnp.bfloat16

D = 1024
DEPTH = 4
GRID_W = 64
CTX = 256
H_MLA = 8
QK_NOPE = 64
QK_ROPE = 32
V_HEAD = 64
Q_LORA = 384
KV_LORA = 256
ROPE_BASE = 10000.0
MLA_SCALE = (QK_NOPE + QK_ROPE) ** -0.5
H_M = 4
MQK = 64
MV = 128
N_GROUPS = 4
EPG = 8
N_EXP = N_GROUPS * EPG
D_EXP = 256
ALPHA = (2 * DEPTH) ** 0.25
LN_EPS = 1e-6

LANES = 128
VMEM_LIMIT = 56 * 1024 * 1024
VMEM_LIMIT_MOE = 58 * 1024 * 1024

HP = LANES
ONES_ROW = V_HEAD
BF16_ROWS = 16
VP = (V_HEAD + 1 + BF16_ROWS - 1) // BF16_ROWS * BF16_ROWS
QSCALE = MLA_SCALE * math.log2(math.e)

SEG_PQD = Q_LORA
SEG_CKV = KV_LORA
SEG_KR = 2 * HP
SEG_Q = H_M * MQK
SEG_K = H_M * MQK
SEG_V = H_M * MV
SEG_PO = H_M * MV
SEG_PG = LANES
SEG_PM = 2 * D
IN_SEGS = (SEG_PQD, SEG_CKV, SEG_KR, SEG_Q, SEG_K, SEG_PO, SEG_PG, SEG_PM)
IN_DTYPES = (F32, F32, F32, BF16, BF16, BF16, F32, BF16)
NP_IN = sum(IN_SEGS)

ROW_TILE = 640
ATT_TQ = 3328
ATT_CB = 256
ATT_TK = 1280
MAX_LAG_JUMP = 64.0
ATT_DEPTH = 2
M_CHUNK = 256
MOE_SLOT_STEP = 64
MOE_SLOT_MAX = 256
MOE_PLAN_TILES = 2
PLAN_GROUP = 0
PLAN_RANK = 1


def _moe_slot_sizes(tm):
    return tuple(s for s in range(MOE_SLOT_STEP, MOE_SLOT_MAX + 1, MOE_SLOT_STEP) if s <= tm)

_NT = (((1,), (1,)), ((), ()))
_TN = (((0,), (0,)), ((), ()))


def _cparams(n_grid, vmem_limit=VMEM_LIMIT):
    return pltpu.CompilerParams(dimension_semantics=("arbitrary",) * n_grid, vmem_limit_bytes=vmem_limit)


def _ln(x):
    mu = jnp.mean(x, axis=-1, keepdims=True)
    xc = x - mu
    var = jnp.mean(xc * xc, axis=-1, keepdims=True)
    return xc * lax.rsqrt(var + LN_EPS)


def _rms(x, g):
    return x * lax.rsqrt(jnp.mean(x * x, axis=-1, keepdims=True) + LN_EPS) * g


def _sigmoid(x):
    return 1.0 / (1.0 + jnp.exp(-x))


def _is_ctx_rows(tm):
    rows = pl.program_id(0) * tm + lax.broadcasted_iota(jnp.int32, (tm, 1), 0)
    return rows < CTX


def _mod(mod_ref, k, is_ctx):
    lat = mod_ref[0:1, k * D:(k + 1) * D]
    ctx = mod_ref[1:2, k * D:(k + 1) * D]
    return jnp.where(is_ctx, ctx, lat)


def _ada_kernel(c_ref, w_ref, b_ref, o_ref):
    c = c_ref[...]
    s = c * _sigmoid(c)
    o_ref[...] = jnp.dot(s, w_ref[...], preferred_element_type=F32,
                         precision=lax.Precision.HIGHEST) + b_ref[...]


def _ada(cc, w_ada, b_ada):
    tn = 1536
    n = 6 * D
    return pl.pallas_call(
        _ada_kernel,
        grid=(DEPTH, n // tn),
        in_specs=[pl.BlockSpec((8, D), lambda l, j: (0, 0)),
                  pl.BlockSpec((None, D, tn), lambda l, j: (l, 0, j)),
                  pl.BlockSpec((None, 1, tn), lambda l, j: (l, 0, j))],
        out_specs=pl.BlockSpec((None, 8, tn), lambda l, j: (l, 0, j)),
        out_shape=jax.ShapeDtypeStruct((DEPTH, 8, n), F32),
        compiler_params=_cparams(2),
        name="ada",
    )(cc, w_ada, b_ada.reshape(DEPTH, 1, n))


def _inproj_kernel(x_ref, mod_ref, w_ref, wvt_ref, *o_refs, tm):
    is_ctx = _is_ctx_rows(tm)
    h = _ln(x_ref[...]) * (1.0 + _mod(mod_ref, 1, is_ctx)) + _mod(mod_ref, 0, is_ctx)
    hb = h.astype(BF16)
    off = 0
    for o in o_refs[:-1]:
        n = o.shape[-1]
        o[...] = jnp.dot(hb, w_ref[:, off:off + n], preferred_element_type=F32).astype(o.dtype)
        off += n
    o_refs[-1][...] = lax.dot_general(wvt_ref[...], hb, _NT, preferred_element_type=F32).astype(BF16)


def _inproj(x_all, mod, w_in_p, w_vt, tm):
    t = x_all.shape[0]
    return pl.pallas_call(
        functools.partial(_inproj_kernel, tm=tm),
        grid=(t // tm,),
        in_specs=[pl.BlockSpec((tm, D), lambda i: (i, 0)),
                  pl.BlockSpec((8, 6 * D), lambda i: (0, 0)),
                  pl.BlockSpec((D, NP_IN), lambda i: (0, 0)),
                  pl.BlockSpec((SEG_V, D), lambda i: (0, 0))],
        out_specs=[pl.BlockSpec((tm, n), lambda i: (i, 0)) for n in IN_SEGS]
        + [pl.BlockSpec((SEG_V, tm), lambda i: (0, i))],
        out_shape=[jax.ShapeDtypeStruct((t, n), dt) for n, dt in zip(IN_SEGS, IN_DTYPES)]
        + [jax.ShapeDtypeStruct((SEG_V, t), BF16)],
        compiler_params=_cparams(1),
        name="inproj",
    )(x_all, mod, w_in_p, w_vt)


def _mla_prep_kernel(pqd_ref, ckv_ref, kr_ref, cos_ref, sin_ref, cost_ref, sint_ref, gq_ref, gk_ref,
                     wqt_ref, wqrt_ref, wk_ref, wvt_ref, qt_ref, k_ref, vt_ref, *, tm):
    qn = _rms(pqd_ref[...], gq_ref[...]).astype(BF16)
    qa = lax.dot_general(wqt_ref[...], qn, _NT, preferred_element_type=F32)
    qb = lax.dot_general(wqrt_ref[...], qn, _NT, preferred_element_type=F32)
    cost = cost_ref[...]
    sint = sint_ref[...]
    ckv = _rms(ckv_ref[...], gk_ref[...]).astype(BF16)
    kn = jnp.dot(ckv, wk_ref[...], preferred_element_type=F32)
    kr = kr_ref[:, :HP] * cos_ref[...] + kr_ref[:, HP:] * sin_ref[...]
    vt = lax.dot_general(wvt_ref[...], ckv, _NT, preferred_element_type=F32)
    is_ones_row = lax.broadcasted_iota(jnp.int32, (VP, tm), 0) == ONES_ROW
    for h in range(H_MLA):
        sl = slice(h * HP, (h + 1) * HP)
        qt_ref[h] = ((qa[sl] * cost + qb[sl] * sint) * QSCALE).astype(BF16)
        k_ref[h] = (kn[:, sl] + kr).astype(BF16)
        vt_ref[h] = jnp.where(is_ones_row, 1.0, vt[h * VP:(h + 1) * VP]).astype(BF16)


def _mla_prep(pqd, ckv, kr, tabs, g_qn, g_kvn, wqt, wqrt, wk, wvt, tm):
    t = pqd.shape[0]
    cos, sin, cost, sint = tabs
    row = lambda n: pl.BlockSpec((tm, n), lambda i: (i, 0))
    col = lambda n: pl.BlockSpec((n, tm), lambda i: (0, i))
    full = lambda a: pl.BlockSpec(a.shape, lambda i: (0,) * a.ndim)
    hd = H_MLA * HP
    return pl.pallas_call(
        functools.partial(_mla_prep_kernel, tm=tm),
        grid=(t // tm,),
        in_specs=[row(Q_LORA), row(KV_LORA), row(SEG_KR), row(HP), row(HP), col(HP), col(HP),
                  full(g_qn), full(g_kvn), full(wqt), full(wqrt), full(wk), full(wvt)],
        out_specs=[pl.BlockSpec((H_MLA, HP, tm), lambda i: (0, 0, i)),
                   pl.BlockSpec((H_MLA, tm, HP), lambda i: (0, i, 0)),
                   pl.BlockSpec((H_MLA, VP, tm), lambda i: (0, 0, i))],
        out_shape=[jax.ShapeDtypeStruct((H_MLA, HP, t), BF16),
                   jax.ShapeDtypeStruct((H_MLA, t, HP), BF16),
                   jax.ShapeDtypeStruct((H_MLA, VP, t), BF16)],
        compiler_params=_cparams(1),
        name="mla_prep",
    )(pqd, ckv, kr, cos, sin, cost, sint, g_qn, g_kvn, wqt, wqrt, wk, wvt)


def _scores(k, qt, m):
    s = jnp.dot(k, qt, preferred_element_type=F32)
    return s, jnp.maximum(m, jnp.max(s, axis=0, keepdims=True))


def _accumulate(s, vt, m, m_new, acc):
    p = jnp.exp2(s - m_new).astype(BF16)
    return jnp.exp2(m - m_new) * acc + jnp.dot(vt, p, preferred_element_type=F32)


def _normalized_rows(acc):
    o = acc / acc[ONES_ROW:ONES_ROW + 1, :]
    return jnp.concatenate([o, jnp.zeros((HP - VP, o.shape[1]), F32)], axis=0).T


def _lagged_pass(qt_ref, k_ref, vt_ref, m_ref, acc_ref, p_ref, sc_ref, *, ncb, n_kv, tk, cb, depth):
    k0 = k_ref[0:BF16_ROWS, :]
    for c in range(ncb):
        s0 = jnp.dot(k0, qt_ref[:, c * cb:(c + 1) * cb], preferred_element_type=F32)
        m_ref[c] = jnp.max(s0, axis=0, keepdims=True)
    acc_ref[...] = jnp.zeros(acc_ref.shape, F32)
    p_ref[...] = jnp.zeros(p_ref.shape, BF16)
    sc_ref[...] = jnp.ones(sc_ref.shape, F32)

    def carried(vt_prev):
        return [(ncb - depth + d, p_ref[d], sc_ref[d], vt_prev) for d in range(depth)]

    def drain(entry):
        pc, pp, psc, pvt = entry
        acc_ref[pc] = (acc_ref[pc] + jnp.dot(pvt, pp, preferred_element_type=F32)) * psc

    def step(j, jump):
        off = pl.multiple_of(j * tk, tk)
        off_prev = pl.multiple_of(jnp.maximum(j - 1, 0) * tk, tk)
        k = k_ref[pl.ds(off, tk), :]
        vt = vt_ref[:, pl.ds(off, tk)]
        pending = carried(vt_ref[:, pl.ds(off_prev, tk)])
        for c in range(ncb):
            m = m_ref[c]
            s = jnp.dot(k, qt_ref[:, c * cb:(c + 1) * cb], preferred_element_type=F32)
            p = jnp.exp2(s - m).astype(BF16)
            cm = jnp.max(s, axis=0, keepdims=True)
            m_new = jnp.maximum(m, cm)
            jump = jnp.maximum(jump, cm - m)
            m_ref[c] = m_new
            drain(pending.pop(0))
            pending.append((c, p, jnp.exp2(m - m_new), vt))
        for d, (_, pp, psc, _) in enumerate(pending):
            p_ref[d] = pp
            sc_ref[d] = psc
        return jump

    jump = lax.fori_loop(0, n_kv, step, jnp.zeros((1, cb), F32))
    for entry in carried(vt_ref[:, (n_kv - 1) * tk:]):
        drain(entry)
    return jump


def _exact_pass(qt_ref, k_ref, vt_ref, m_ref, acc_ref, s_ref, mp_ref, *, ncb, n_kv, tk, cb, depth):
    m_ref[...] = jnp.full(m_ref.shape, -1e30, F32)
    acc_ref[...] = jnp.zeros(acc_ref.shape, F32)
    s_ref[...] = jnp.full(s_ref.shape, -jnp.inf, F32)
    mp_ref[...] = jnp.full(mp_ref.shape, -1e30, F32)

    def carried(vt_prev):
        return [(ncb - depth + d, s_ref[d], mp_ref[d, 0], mp_ref[d, 1], vt_prev) for d in range(depth)]

    def drain(entry):
        pc, ps, pm_old, pm_new, pvt = entry
        acc_ref[pc] = _accumulate(ps, pvt, pm_old, pm_new, acc_ref[pc])

    def step(j, carry):
        off = pl.multiple_of(j * tk, tk)
        off_prev = pl.multiple_of(jnp.maximum(j - 1, 0) * tk, tk)
        k = k_ref[pl.ds(off, tk), :]
        vt = vt_ref[:, pl.ds(off, tk)]
        pending = carried(vt_ref[:, pl.ds(off_prev, tk)])
        for c in range(ncb):
            m_old = m_ref[c]
            s, m_new = _scores(k, qt_ref[:, c * cb:(c + 1) * cb], m_old)
            m_ref[c] = m_new
            drain(pending.pop(0))
            pending.append((c, s, m_old, m_new, vt))
        for d, (_, ps, pm_old, pm_new, _) in enumerate(pending):
            s_ref[d] = ps
            mp_ref[d, 0] = pm_old
            mp_ref[d, 1] = pm_new
        return carry

    lax.fori_loop(0, n_kv, step, 0)
    for entry in carried(vt_ref[:, (n_kv - 1) * tk:]):
        drain(entry)


def _attn_kernel(qt_ref, k_ref, vt_ref, o_ref, m_ref, acc_ref, s_ref, mp_ref, p_ref, sc_ref, *,
                 tq, tk, t, cb, depth):
    geom = dict(ncb=tq // cb, n_kv=t // tk, tk=tk, cb=cb, depth=depth)

    def write_out():
        for c in range(tq // cb):
            o_ref[c * cb:(c + 1) * cb, :] = _normalized_rows(acc_ref[c]).astype(o_ref.dtype)

    jump = _lagged_pass(qt_ref, k_ref, vt_ref, m_ref, acc_ref, p_ref, sc_ref, **geom)
    write_out()

    @pl.when(jnp.max(jump) > MAX_LAG_JUMP)
    def _():
        _exact_pass(qt_ref, k_ref, vt_ref, m_ref, acc_ref, s_ref, mp_ref, **geom)
        write_out()

    @pl.when(pl.program_id(1) == 0)
    def _():
        m0 = jnp.full((1, CTX), -1e30, F32)
        s, m_new = _scores(k_ref[0:CTX, :], qt_ref[:, 0:CTX], m0)
        acc_c = _accumulate(s, vt_ref[:, 0:CTX], m0, m_new, jnp.zeros((VP, CTX), F32))
        o_ref[0:CTX, :] = _normalized_rows(acc_c).astype(o_ref.dtype)


def _attention(qt, k, vt, tq, tk, cb, depth):
    t = k.shape[1]
    assert depth <= tq // cb
    return pl.pallas_call(
        functools.partial(_attn_kernel, tq=tq, tk=tk, t=t, cb=cb, depth=depth),
        grid=(H_MLA, t // tq),
        in_specs=[pl.BlockSpec((None, HP, tq), lambda h, i: (h, 0, i)),
                  pl.BlockSpec((None, t, HP), lambda h, i: (h, 0, 0)),
                  pl.BlockSpec((None, VP, t), lambda h, i: (h, 0, 0))],
        out_specs=pl.BlockSpec((tq, HP), lambda h, i: (i, h)),
        out_shape=jax.ShapeDtypeStruct((t, H_MLA * HP), BF16),
        scratch_shapes=[pltpu.VMEM((tq // cb, 1, cb), F32), pltpu.VMEM((tq // cb, VP, cb), F32),
                        pltpu.VMEM((depth, tk, cb), F32), pltpu.VMEM((depth, 2, 1, cb), F32),
                        pltpu.VMEM((depth, tk, cb), BF16), pltpu.VMEM((depth, 1, cb), F32)],
        compiler_params=_cparams(2),
        name="attention",
    )(qt, k, vt)


def _split3(x):
    hi = x.astype(BF16)
    r = x - hi.astype(F32)
    mid = r.astype(BF16)
    lo = (r - mid.astype(F32)).astype(BF16)
    return hi, mid, lo


def _log_sigmoid(x):
    return jnp.minimum(x, 0.0) - jnp.log(1.0 + jnp.exp(-jnp.abs(x)))


def _mlstm_kernel(qf_ref, kf_ref, vtf_ref, gcf_ref, grf_ref, qb_ref, kb_ref, vtb_ref, gcb_ref, grb_ref,
                  bc_ref, br_ref, of_ref, ob_ref, ct_scr, m_scr, *, L):
    @pl.when(pl.program_id(0) == 0)
    def _():
        ct_scr[...] = jnp.zeros_like(ct_scr)
        m_scr[...] = jnp.zeros_like(m_scr)

    _mlstm_chunk(qf_ref, kf_ref, vtf_ref, gcf_ref, grf_ref, bc_ref, br_ref, of_ref, ct_scr.at[0], m_scr.at[0],
                 L=L, rev=0)
    _mlstm_chunk(qb_ref, kb_ref, vtb_ref, gcb_ref, grb_ref, bc_ref, br_ref, ob_ref, ct_scr.at[1], m_scr.at[1],
                 L=L, rev=1)


def _mlstm_chunk(q_ref, k_ref, vt_ref, gc_ref, gr_ref, bc_ref, br_ref, o_ref, ct_scr, m_scr, *, L, rev):

    row = lax.broadcasted_iota(jnp.int32, (L, L), 0)
    col = lax.broadcasted_iota(jnp.int32, (L, L), 1)
    before = (row >= col) if rev else (row <= col)
    before_b = before.astype(BF16)
    after_b = ((row <= col) if rev else (row >= col)).astype(BF16)

    gc = gc_ref[...] + bc_ref[...]
    gr = gr_ref[...] + br_ref[...]
    lfc = _log_sigmoid(gc)
    lfr = _log_sigmoid(gr)
    b_col = sum(jnp.dot(after_b, p, preferred_element_type=F32) for p in _split3(lfc))
    b_row = sum(jnp.dot(p, before_b, preferred_element_type=F32) for p in _split3(lfr))
    ones_rows = (lax.broadcasted_iota(jnp.int32, (MV, L), 0) == 0).astype(BF16)

    for h in range(H_M):
        ci = 8 * rev + h
        cf = ci + 4
        g_c = gc[:, ci:ci + 1] - b_col[:, cf:cf + 1]
        b_r = b_row[cf:cf + 1, :]
        i_r = gr[ci:ci + 1, :]
        bl = jnp.sum(lfr[cf:cf + 1, :], axis=1, keepdims=True)
        m = m_scr[h][0:1, 0:1]
        ct = ct_scr[h]

        pm = jnp.max(jnp.where(before, g_c, -jnp.inf), axis=0, keepdims=True)
        inter = b_r + m
        mj = jnp.maximum(inter, b_r + pm)
        w = jnp.exp(jnp.where(before, g_c + (b_r - mj), -jnp.inf))
        sc = jnp.exp(inter - mj)

        q = q_ref[:, h * MQK:(h + 1) * MQK] * (MQK ** -0.5)
        k = k_ref[:, h * MQK:(h + 1) * MQK]
        vt_aug = jnp.concatenate([vt_ref[h * MV:(h + 1) * MV, :], ones_rows], axis=0)
        kq = lax.dot_general(k, q, _NT, preferred_element_type=F32) * w
        num = (jnp.dot(vt_aug, kq.astype(BF16), preferred_element_type=F32)
               + sc * lax.dot_general(ct.astype(BF16), q, _NT, preferred_element_type=F32))
        den = jnp.maximum(jnp.abs(num[MV:MV + 1, :]), jnp.exp(-mj))
        o_ref[:, h * MV:(h + 1) * MV] = (num[:MV, :] / den).T

        wlog = bl - b_r + i_r
        m_new = jnp.maximum(bl + m, jnp.max(wlog, axis=1, keepdims=True))
        ws = jnp.exp(wlog - m_new)
        sd = jnp.exp(bl + m - m_new)
        wvt = (vt_aug.astype(F32) * ws).astype(BF16)
        ct_scr[h] = sd * ct + jnp.dot(wvt, k, preferred_element_type=F32)
        m_scr[h] = jnp.broadcast_to(m_new, m_scr.shape[1:])


def _mlstm(mq, mk, mvt, pg, pg_t, bg_c, bg_r, L):
    t = mq.shape[0]
    nc = t // L
    fwd = lambda c: c
    bwd = lambda c: jnp.where(c == 0, 0, nc - c)

    def chunk_specs(idx):
        return [pl.BlockSpec((L, SEG_Q), lambda c: (idx(c), 0)),
                pl.BlockSpec((L, SEG_K), lambda c: (idx(c), 0)),
                pl.BlockSpec((SEG_V, L), lambda c: (0, idx(c))),
                pl.BlockSpec((L, LANES), lambda c: (idx(c), 0)),
                pl.BlockSpec((16, L), lambda c: (0, idx(c)))]

    chunk_args = (mq, mk, mvt, pg, pg_t)
    return pl.pallas_call(
        functools.partial(_mlstm_kernel, L=L),
        grid=(nc,),
        in_specs=chunk_specs(fwd) + chunk_specs(bwd)
        + [pl.BlockSpec((1, LANES), lambda c: (0, 0)), pl.BlockSpec((16, 1), lambda c: (0, 0))],
        out_specs=[pl.BlockSpec((L, H_M * MV), lambda c: (fwd(c), 0)),
                   pl.BlockSpec((L, H_M * MV), lambda c: (bwd(c), 0))],
        out_shape=[jax.ShapeDtypeStruct((t, H_M * MV), F32)] * 2,
        scratch_shapes=[pltpu.VMEM((2, H_M, 2 * MV, MQK), F32), pltpu.VMEM((2, H_M, 8, LANES), F32)],
        compiler_params=_cparams(1),
        name="mlstm",
    )(*chunk_args, *chunk_args, bg_c, bg_r)


def _router(logits):
    tm = logits.shape[0]
    lane = lax.broadcasted_iota(jnp.int32, (tm, LANES), 1)
    big = jnp.int32(LANES)
    neg = -jnp.inf
    is_g = lane < N_GROUPS
    lg = jnp.where(is_g, logits, neg)
    gmax = jnp.max(lg, axis=-1, keepdims=True)
    grp = jnp.min(jnp.where(lg == gmax, lane, big), axis=-1, keepdims=True)
    p_grp = 1.0 / jnp.sum(jnp.where(is_g, jnp.exp(lg - gmax), 0.0), axis=-1, keepdims=True)
    e_lo = N_GROUPS + grp * EPG
    in_grp = (lane >= e_lo) & (lane < e_lo + EPG)
    le = jnp.where(in_grp, logits, neg)
    l1 = jnp.max(le, axis=-1, keepdims=True)
    i1 = jnp.min(jnp.where(le == l1, lane, big), axis=-1, keepdims=True)
    le2 = jnp.where(lane == i1, neg, le)
    l2 = jnp.max(le2, axis=-1, keepdims=True)
    i2 = jnp.min(jnp.where(le2 == l2, lane, big), axis=-1, keepdims=True)
    r = jnp.exp(l2 - l1)
    w1 = p_grp / (1.0 + r)
    w2 = w1 * r
    return jnp.where(lane == i1, w1, 0.0) + jnp.where(lane == i2, w2, 0.0), grp


def _merge_kernel(att_ref, hf_ref, hb_ref, po_ref, pm_ref, x_ref, mod_ref, gmh_ref, l1g_ref, l1b_ref,
                  wa_ref, wm_ref, wo_ref, wr_ref, br_ref, x1_ref, hm_ref, plan_ref, cnt_ref, *, tm):
    is_ctx = _is_ctx_rows(tm)
    y_mla = jnp.dot(att_ref[...], wa_ref[...], preferred_element_type=F32)
    hs = hf_ref[...] + hb_ref[...]
    parts = []
    for h in range(H_M):
        sl = slice(h * MV, (h + 1) * MV)
        parts.append(_ln(hs[:, sl]))
    hn = jnp.concatenate(parts, axis=1) * gmh_ref[...] * _sigmoid(po_ref[...].astype(F32))
    y_ml = jnp.dot(hn.astype(BF16), wm_ref[...], preferred_element_type=F32)
    g_a = _sigmoid(pm_ref[:, :D].astype(F32))
    g_b = _sigmoid(pm_ref[:, D:].astype(F32))
    y = jnp.dot((g_a * y_mla + g_b * y_ml).astype(BF16), wo_ref[...], preferred_element_type=F32)
    x1 = _ln(ALPHA * x_ref[...] + _mod(mod_ref, 2, is_ctx) * y) * l1g_ref[...] + l1b_ref[...]
    x1_ref[...] = x1
    hm = _ln(x1) * (1.0 + _mod(mod_ref, 4, is_ctx)) + _mod(mod_ref, 3, is_ctx)
    hm_ref[...] = hm.astype(BF16)
    h_hi = hm.astype(BF16)
    h_lo = (hm - h_hi.astype(F32)).astype(BF16)
    logits = (jnp.dot(h_hi, wr_ref[0], preferred_element_type=F32)
              + jnp.dot(h_hi, wr_ref[1], preferred_element_type=F32)
              + jnp.dot(h_lo, wr_ref[0], preferred_element_type=F32)) + br_ref[...]
    comb, grp = _router(logits)
    lane = lax.broadcasted_iota(jnp.int32, (tm, LANES), 1)
    in_group = lane == grp
    earlier = (lax.broadcasted_iota(jnp.int32, (tm, tm), 0)
               > lax.broadcasted_iota(jnp.int32, (tm, tm), 1)).astype(BF16)
    n_before = jnp.dot(earlier, in_group.astype(BF16), preferred_element_type=F32)
    rank = jnp.sum(jnp.where(in_group, n_before, 0.0), axis=-1, keepdims=True)
    plan_ref[...] = (comb + jnp.where(lane == PLAN_GROUP, grp.astype(F32), 0.0)
                     + jnp.where(lane == PLAN_RANK, rank, 0.0))
    cnt_ref[...] = jnp.broadcast_to(jnp.sum(in_group.astype(F32), axis=0, keepdims=True), cnt_ref.shape)


def _merge(att, hf, hb, po, pm, x_all, mod, g_mh, l1g, l1b, wa, wm, wo, wr, br, tm):
    t = x_all.shape[0]
    row = lambda n: pl.BlockSpec((tm, n), lambda i: (i, 0))
    full = lambda a: pl.BlockSpec(a.shape, lambda i: (0,) * a.ndim)
    return pl.pallas_call(
        functools.partial(_merge_kernel, tm=tm),
        grid=(t // tm,),
        in_specs=[row(H_MLA * HP), row(H_M * MV), row(H_M * MV), row(SEG_PO), row(SEG_PM), row(D),
                  full(mod), full(g_mh), full(l1g), full(l1b), full(wa), full(wm), full(wo), full(wr), full(br)],
        out_specs=[row(D), row(D), row(LANES), pl.BlockSpec((None, 8, LANES), lambda i: (i, 0, 0))],
        out_shape=[jax.ShapeDtypeStruct((t, D), F32), jax.ShapeDtypeStruct((t, D), BF16),
                   jax.ShapeDtypeStruct((t, LANES), F32), jax.ShapeDtypeStruct((t // tm, 8, LANES), F32)],
        compiler_params=_cparams(1),
        name="merge",
    )(att, hf, hb, po, pm, x_all, mod, g_mh, l1g, l1b, wa, wm, wo, wr, br)


def _moe_kernel(cnt_ref, layer_ref, hm_ref, plan_ref, plant_ref, x1_ref, mod_ref, l2g_ref, l2b_ref, wg_ref, wu_ref, wd_ref,
                o_ref, *, tm, tp, slot_sizes):
    i = pl.program_id(0)
    grp = pl.program_id(1)
    n_sub = tm // tp

    @pl.when(grp == 0)
    def _():
        o_ref[...] = jnp.zeros_like(o_ref)

    n_tok = cnt_ref[i * n_sub, grp]
    for j in range(1, n_sub):
        n_tok = jnp.maximum(n_tok, cnt_ref[i * n_sub + j, grp])

    def rounds_of(slots):
        return functools.partial(_moe_round, hm_ref, plan_ref, plant_ref, wg_ref, wu_ref, wd_ref, o_ref,
                                 grp=grp, n_sub=n_sub, tp=tp, slots=slots)

    big = slot_sizes[-1]
    tails = slot_sizes[:-1]
    n_full = lax.div(n_tok, big) + (lax.rem(n_tok, big) > (tails[-1] if tails else 0)).astype(jnp.int32)
    rest = jnp.maximum(n_tok - n_full * big, 0)
    full_round = rounds_of(big)
    lax.fori_loop(0, n_full, lambda r, carry: full_round(r * big, carry), 0)
    lower = 0
    for slots in tails:
        @pl.when((rest > lower) & (rest <= slots))
        def _(tail_round=rounds_of(slots)):
            tail_round(n_full * big, 0)

        lower = slots

    @pl.when(grp == N_GROUPS - 1)
    def _():
        is_ctx = _is_ctx_rows(tm)
        z = ALPHA * x1_ref[...] + _mod(mod_ref, 5, is_ctx) * o_ref[...]
        o_ref[...] = _ln(z) * l2g_ref[...] + l2b_ref[...]


def _moe_round(hm_ref, plan_ref, plant_ref, wg_ref, wu_ref, wd_ref, o_ref, first, carry, *, grp, n_sub, tp, slots):
    grp_f = grp.astype(F32)
    base = first.astype(F32)
    slot_rows = lax.broadcasted_iota(jnp.int32, (slots, tp), 0).astype(F32)
    slot_cols = lax.broadcasted_iota(jnp.int32, (tp, slots), 1).astype(F32)
    lane = lax.broadcasted_iota(jnp.int32, (n_sub * slots, LANES), 1)
    xs, cs = [], []
    for j in range(n_sub):
        rows = slice(j * tp, (j + 1) * tp)
        in_grp = plant_ref[PLAN_GROUP:PLAN_GROUP + 1, rows] == grp_f
        rank = plant_ref[PLAN_RANK:PLAN_RANK + 1, rows]
        gather = ((slot_rows == rank - base) & in_grp).astype(BF16)
        xs.append(jnp.dot(gather, hm_ref[rows, :], preferred_element_type=F32).astype(BF16))
        cs.append(sum(jnp.dot(gather, p, preferred_element_type=F32)
                      for p in _split3(plan_ref[rows, :])))
    xs = jnp.concatenate(xs, axis=0)
    cs = jnp.concatenate(cs, axis=0)
    hid = []
    for g in range(EPG):
        c_e = jnp.sum(jnp.where(lane == N_GROUPS + grp * EPG + g, cs, 0.0), axis=-1, keepdims=True)
        a = jnp.dot(xs, wg_ref[g], preferred_element_type=F32)
        u = jnp.dot(xs, wu_ref[g], preferred_element_type=F32)
        hid.append((a * _sigmoid(a) * u * c_e).astype(BF16))
    ys = jnp.dot(jnp.concatenate(hid, axis=1), wd_ref[...].reshape(EPG * D_EXP, D),
                 preferred_element_type=F32)
    ys = ys.astype(BF16)
    for j in range(n_sub):
        rows = slice(j * tp, (j + 1) * tp)
        mine = slice(j * slots, (j + 1) * slots)
        in_grp = plan_ref[rows, PLAN_GROUP:PLAN_GROUP + 1] == grp_f
        rank = plan_ref[rows, PLAN_RANK:PLAN_RANK + 1]
        scatter = ((slot_cols == rank - base) & in_grp).astype(BF16)
        o_ref[rows, :] += jnp.dot(scatter, ys[mine], preferred_element_type=F32)
    return carry


def _moe(counts, hm, plan, plan_t, x1, mod, l2g, l2b, wg, wu, wd, layer, tp):
    t = hm.shape[0]
    tm = tp * (MOE_PLAN_TILES if (t // tp) % MOE_PLAN_TILES == 0 else 1)
    row = lambda n, **kw: pl.BlockSpec((tm, n), lambda i, e, cnt, lay: (i, 0), **kw)
    full = lambda a: pl.BlockSpec(a.shape, lambda i, e, cnt, lay: (0,) * a.ndim)
    group_of = lambda a: pl.BlockSpec((None, EPG) + a.shape[2:], lambda i, e, cnt, lay: (lay[0], e, 0, 0))
    return pl.pallas_call(
        functools.partial(_moe_kernel, tm=tm, tp=tp, slot_sizes=_moe_slot_sizes(tp)),
        grid_spec=pltpu.PrefetchScalarGridSpec(
            num_scalar_prefetch=2,
            grid=(t // tm, N_GROUPS),
            in_specs=[row(D), row(LANES), pl.BlockSpec((2, tm), lambda i, e, cnt, lay: (0, i)),
                      row(D, pipeline_mode=pl.Buffered(1)),
                      full(mod), full(l2g), full(l2b), group_of(wg), group_of(wu), group_of(wd)],
            out_specs=row(D)),
        out_shape=jax.ShapeDtypeStruct((t, D), F32),
        compiler_params=_cparams(2, VMEM_LIMIT_MOE),
        name="moe",
    )(counts, jnp.full((1,), layer, jnp.int32), hm, plan, plan_t, x1, mod, l2g, l2b, wg, wu, wd)


def _rot_cols(w):
    a1, a2, b1, b2 = jnp.split(w, 4, axis=-1)
    return jnp.concatenate([-a2, a1, -b2, b1], axis=-1)


def _place_rope(w):
    return jnp.pad(w, ((0, 0), (QK_NOPE, HP - QK_NOPE - QK_ROPE)))


def _rope_tables(t):
    rows = (t - CTX) // GRID_W
    row, col = jnp.meshgrid(jnp.arange(rows, dtype=F32), jnp.arange(GRID_W, dtype=F32), indexing="ij")
    row, col = row.reshape(-1), col.reshape(-1)
    half = QK_ROPE // 2
    inv = ROPE_BASE ** (-jnp.arange(0, half, 2, dtype=F32) / half)
    ar, ac = row[:, None] * inv, col[:, None] * inv
    ang = jnp.concatenate([ar, ar, ac, ac], axis=-1)
    ang = jnp.concatenate([jnp.zeros((CTX, QK_ROPE), F32), ang], axis=0)
    pad = ((0, 0), (QK_NOPE, HP - QK_NOPE - QK_ROPE))
    cos = jnp.pad(jnp.cos(ang), pad) + jnp.pad(jnp.ones((t, QK_NOPE), F32), ((0, 0), (0, HP - QK_NOPE)))
    sin = jnp.pad(jnp.sin(ang), pad)
    return cos, sin, cos.T, sin.T


def _layer_weights(l, w_in, w_uq, w_uk, w_uv, w_bo_mla, w_bo_mlstm, w_out, w_rg, b_rg, w_re, b_re):
    wi = w_in[l]
    o = 0
    segs = []
    for n in (Q_LORA, KV_LORA, QK_ROPE, SEG_Q, SEG_K, SEG_V, SEG_PO, 4 * H_M, SEG_PM):
        segs.append(wi[:, o:o + n])
        o += n
    s_pqd, s_ckv, s_kr, s_q, s_k, s_v, s_po, s_pg, s_pm = segs
    w_in_p = jnp.concatenate(
        [s_pqd, s_ckv, _place_rope(s_kr), _place_rope(_rot_cols(s_kr)), s_q, s_k, s_po,
         jnp.pad(s_pg, ((0, 0), (0, SEG_PG - 4 * H_M))), s_pm], axis=1).astype(BF16)

    uq = w_uq[l].reshape(Q_LORA, H_MLA, QK_NOPE + QK_ROPE)
    padh = ((0, 0), (0, 0), (0, HP - QK_NOPE - QK_ROPE))
    wq = jnp.pad(uq, padh).reshape(Q_LORA, H_MLA * HP)
    wqr = jnp.pad(jnp.concatenate([jnp.zeros_like(uq[..., :QK_NOPE]), _rot_cols(uq[..., QK_NOPE:])], axis=-1),
                  padh).reshape(Q_LORA, H_MLA * HP)
    padv = ((0, 0), (0, 0), (0, HP - V_HEAD))
    wk = jnp.pad(w_uk[l].reshape(KV_LORA, H_MLA, QK_NOPE), padv).reshape(KV_LORA, H_MLA * HP)
    wv = jnp.pad(w_uv[l].reshape(KV_LORA, H_MLA, V_HEAD), ((0, 0), (0, 0), (0, VP - V_HEAD))).reshape(KV_LORA, H_MLA * VP)
    wa = jnp.pad(w_bo_mla[l].reshape(H_MLA, V_HEAD, D), ((0, 0), (0, HP - V_HEAD), (0, 0))).reshape(H_MLA * HP, D)

    wr = jnp.pad(jnp.concatenate([w_rg[l], w_re[l]], axis=1), ((0, 0), (0, LANES - N_GROUPS - N_EXP)))
    wr_hi = wr.astype(BF16)
    wr_lo = (wr - wr_hi.astype(F32)).astype(BF16)
    br = jnp.pad(jnp.concatenate([b_rg[l], b_re[l]]), (0, LANES - N_GROUPS - N_EXP)).reshape(1, LANES)
    return dict(
        w_in=w_in_p, w_mvt=s_v.T.astype(BF16), wqt=wq.T.astype(BF16), wqrt=wqr.T.astype(BF16), wk=wk.astype(BF16), wvt=wv.T.astype(BF16),
        wa=wa.astype(BF16), wm=w_bo_mlstm[l].astype(BF16), wo=w_out[l].astype(BF16),
        wr=jnp.stack([wr_hi, wr_lo]), br=br)


def kernel(x, c, ctx, c_ctx, w_ada, b_ada, w_in, b_gates, w_uq, w_uk, w_uv, g_qn, g_kvn, g_mh, w_bo_mla,
           w_bo_mlstm, w_out, ln1_g, ln1_b, w_rg, b_rg, w_re, b_re, w_e_gate, w_e_up, w_e_down, ln2_g, ln2_b):
    assert x.shape[0] == 1 and c.shape[0] == 1 and ctx.shape[1] == CTX
    x_all = jnp.concatenate([ctx[0], x[0]], axis=0)
    t = x_all.shape[0]
    tm = ROW_TILE if t % ROW_TILE == 0 else M_CHUNK
    tk = ATT_TK if t % ATT_TK == 0 else M_CHUNK
    assert t % tm == 0 and t % ATT_TQ == 0 and t % tk == 0 and t % M_CHUNK == 0

    cc = jnp.pad(jnp.concatenate([c, c_ctx[None]], axis=0), ((0, 6), (0, 0)))
    mods = _ada(cc, w_ada, b_ada)
    tabs = _rope_tables(t)
    row2 = lambda a: a.reshape(1, -1)
    wg_all, wu_all, wd_all = w_e_gate.astype(BF16), w_e_up.astype(BF16), w_e_down.astype(BF16)

    for l in range(DEPTH):
        w = _layer_weights(l, w_in, w_uq, w_uk, w_uv, w_bo_mla, w_bo_mlstm, w_out, w_rg, b_rg, w_re, b_re)
        mod = mods[l]
        pqd, ckv, kr, mq, mk, po, pg, pm, mvt = _inproj(x_all, mod, w["w_in"], w["w_mvt"], tm)
        qt, kk, vt = _mla_prep(pqd, ckv, kr, tabs, row2(g_qn[l]), row2(g_kvn[l]),
                               w["wqt"], w["wqrt"], w["wk"], w["wvt"], tm)
        att = _attention(qt, kk, vt, ATT_TQ, tk, ATT_CB, ATT_DEPTH)
        pg_t = pg[:, :16].T
        bg_c = jnp.pad(b_gates[l], (0, LANES - 16)).reshape(1, LANES)
        bg_r = b_gates[l].reshape(16, 1)
        hf, hb = _mlstm(mq, mk, mvt, pg, pg_t, bg_c, bg_r, M_CHUNK)
        x1, hm, plan, cnt = _merge(att, hf, hb, po, pm, x_all, mod, row2(g_mh[l]), row2(ln1_g[l]), row2(ln1_b[l]),
                                   w["wa"], w["wm"], w["wo"], w["wr"], w["br"], tm)
        counts = cnt[:, 0, :N_GROUPS].astype(jnp.int32)
        plan_t = plan[:, :2].T
        x_all = _moe(counts, hm, plan, plan_t, x1, mod, row2(ln2_g[l]), row2(ln2_b[l]),
                     wg_all, wu_all, wd_all, l, tm)
    return x_all[CTX:][None]
```

```python
import functools
import math

import jax
import jax.numpy as jnp
from jax import lax
from jax.experimental import pallas as pl
from jax.experimental.pallas import tpu as pltpu

F32 = jnp.float32
BF16 = jnp.bfloat16

D = 1024
DEPTH = 4
GRID_W = 64
CTX = 256
H_MLA = 8
QK_NOPE = 64
QK_ROPE = 32
V_HEAD = 64
Q_LORA = 384
KV_LORA = 256
ROPE_BASE = 10000.0
MLA_SCALE = (QK_NOPE + QK_ROPE) ** -0.5
H_M = 4
MQK = 64
MV = 128
N_GROUPS = 4
EPG = 8
N_EXP = N_GROUPS * EPG
D_EXP = 256
ALPHA = (2 * DEPTH) ** 0.25
LN_EPS = 1e-6

LANES = 128
SUBLANES = 8
NEG_BIG = -1e30
N_GATES = 4 * H_M
ADA_TN = 1536
VMEM_LIMIT = 56 * 1024 * 1024
VMEM_LIMIT_MOE = 58 * 1024 * 1024

HP = LANES
ONES_ROW = V_HEAD
BF16_ROWS = 16
VP = (V_HEAD + 1 + BF16_ROWS - 1) // BF16_ROWS * BF16_ROWS
QSCALE = MLA_SCALE * math.log2(math.e)

SEG_PQD = Q_LORA
SEG_CKV = KV_LORA
SEG_KR = 2 * HP
SEG_Q = H_M * MQK
SEG_K = H_M * MQK
SEG_V = H_M * MV
SEG_PO = H_M * MV
SEG_PG = LANES
SEG_PM = 2 * D
IN_SEGS = (SEG_PQD, SEG_CKV, SEG_KR, SEG_Q, SEG_K, SEG_PO, SEG_PG, SEG_PM)
IN_DTYPES = (F32, F32, F32, BF16, BF16, BF16, F32, BF16)
IN_GATES = (False, False, False, False, False, True, False, True)
NP_IN = sum(IN_SEGS)

ROW_TILE = 640
ATT_TQ = 3328
ATT_CB = 256
ATT_TK = 1280
MAX_LAG_JUMP = 64.0
ATT_DEPTH = 2
M_CHUNK = 256
MOE_SLOT_STEP = 64
MOE_SLOT_MAX = 256
MOE_PLAN_TILES = 2
PLAN_GROUP = 0
PLAN_RANK = 1


def _moe_slot_sizes(tm):
    return tuple(s for s in range(MOE_SLOT_STEP, MOE_SLOT_MAX + 1, MOE_SLOT_STEP) if s <= tm)

_NT = (((1,), (1,)), ((), ()))


def _cparams(n_grid, vmem_limit=VMEM_LIMIT):
    return pltpu.CompilerParams(dimension_semantics=("arbitrary",) * n_grid, vmem_limit_bytes=vmem_limit)


def _ln(x):
    mu = jnp.mean(x, axis=-1, keepdims=True)
    xc = x - mu
    var = jnp.mean(xc * xc, axis=-1, keepdims=True)
    return xc * lax.rsqrt(var + LN_EPS)


def _rms(x, g):
    return x * lax.rsqrt(jnp.mean(x * x, axis=-1, keepdims=True) + LN_EPS) * g


def _sigmoid(x):
    return 1.0 / (1.0 + jnp.exp(-x))


def _is_ctx_rows(tm):
    rows = pl.program_id(0) * tm + lax.broadcasted_iota(jnp.int32, (tm, 1), 0)
    return rows < CTX


def _mod(mod_ref, k, is_ctx):
    lat = mod_ref[0:1, k * D:(k + 1) * D]
    ctx = mod_ref[1:2, k * D:(k + 1) * D]
    return jnp.where(is_ctx, ctx, lat)


def _ada_kernel(c_ref, w_ref, b_ref, o_ref):
    c = c_ref[...]
    s = c * _sigmoid(c)
    o_ref[...] = jnp.dot(s, w_ref[...], preferred_element_type=F32,
                         precision=lax.Precision.HIGHEST) + b_ref[...]


def _ada(cc, w_ada, b_ada):
    tn = ADA_TN
    n = 6 * D
    return pl.pallas_call(
        _ada_kernel,
        grid=(DEPTH, n // tn),
        in_specs=[pl.BlockSpec((SUBLANES, D), lambda l, j: (0, 0)),
                  pl.BlockSpec((None, D, tn), lambda l, j: (l, 0, j)),
                  pl.BlockSpec((None, 1, tn), lambda l, j: (l, 0, j))],
        out_specs=pl.BlockSpec((None, SUBLANES, tn), lambda l, j: (l, 0, j)),
        out_shape=jax.ShapeDtypeStruct((DEPTH, SUBLANES, n), F32),
        compiler_params=_cparams(2),
        name="ada",
    )(cc, w_ada, b_ada.reshape(DEPTH, 1, n))


def _inproj_kernel(x_ref, mod_ref, w_ref, wvt_ref, *o_refs, tm):
    is_ctx = _is_ctx_rows(tm)
    h = _ln(x_ref[...]) * (1.0 + _mod(mod_ref, 1, is_ctx)) + _mod(mod_ref, 0, is_ctx)
    hb = h.astype(BF16)
    off = 0
    for o, is_gate in zip(o_refs[:-1], IN_GATES):
        n = o.shape[-1]
        p = jnp.dot(hb, w_ref[:, off:off + n], preferred_element_type=F32)
        o[...] = (_sigmoid(p) if is_gate else p).astype(o.dtype)
        off += n
    o_refs[-1][...] = lax.dot_general(wvt_ref[...], hb, _NT, preferred_element_type=F32).astype(BF16)


def _inproj(x_all, mod, w_in_p, w_vt, tm):
    t = x_all.shape[0]
    return pl.pallas_call(
        functools.partial(_inproj_kernel, tm=tm),
        grid=(t // tm,),
        in_specs=[pl.BlockSpec((tm, D), lambda i: (i, 0)),
                  pl.BlockSpec((SUBLANES, 6 * D), lambda i: (0, 0)),
                  pl.BlockSpec((D, NP_IN), lambda i: (0, 0)),
                  pl.BlockSpec((SEG_V, D), lambda i: (0, 0))],
        out_specs=[pl.BlockSpec((tm, n), lambda i: (i, 0)) for n in IN_SEGS]
        + [pl.BlockSpec((SEG_V, tm), lambda i: (0, i))],
        out_shape=[jax.ShapeDtypeStruct((t, n), dt) for n, dt in zip(IN_SEGS, IN_DTYPES)]
        + [jax.ShapeDtypeStruct((SEG_V, t), BF16)],
        compiler_params=_cparams(1),
        name="inproj",
    )(x_all, mod, w_in_p, w_vt)


def _mla_prep_kernel(pqd_ref, ckv_ref, kr_ref, cos_ref, sin_ref, cost_ref, sint_ref, gq_ref, gk_ref,
                     wqt_ref, wqrt_ref, wk_ref, wvt_ref, qt_ref, k_ref, vt_ref, *, tm):
    qn = _rms(pqd_ref[...], gq_ref[...]).astype(BF16)
    qa = lax.dot_general(wqt_ref[...], qn, _NT, preferred_element_type=F32)
    qb = lax.dot_general(wqrt_ref[...], qn, _NT, preferred_element_type=F32)
    cost = cost_ref[...]
    sint = sint_ref[...]
    ckv = _rms(ckv_ref[...], gk_ref[...]).astype(BF16)
    kn = jnp.dot(ckv, wk_ref[...], preferred_element_type=F32)
    kr = kr_ref[:, :HP] * cos_ref[...] + kr_ref[:, HP:] * sin_ref[...]
    vt = lax.dot_general(wvt_ref[...], ckv, _NT, preferred_element_type=F32)
    is_ones_row = lax.broadcasted_iota(jnp.int32, (VP, tm), 0) == ONES_ROW
    for h in range(H_MLA):
        sl = slice(h * HP, (h + 1) * HP)
        qt_ref[h] = ((qa[sl] * cost + qb[sl] * sint) * QSCALE).astype(BF16)
        k_ref[h] = (kn[:, sl] + kr).astype(BF16)
        vt_ref[h] = jnp.where(is_ones_row, 1.0, vt[h * VP:(h + 1) * VP]).astype(BF16)


def _mla_prep(pqd, ckv, kr, tabs, g_qn, g_kvn, wqt, wqrt, wk, wvt, tm):
    t = pqd.shape[0]
    cos, sin, cost, sint = tabs
    row = lambda n: pl.BlockSpec((tm, n), lambda i: (i, 0))
    col = lambda n: pl.BlockSpec((n, tm), lambda i: (0, i))
    full = lambda a: pl.BlockSpec(a.shape, lambda i: (0,) * a.ndim)
    return pl.pallas_call(
        functools.partial(_mla_prep_kernel, tm=tm),
        grid=(t // tm,),
        in_specs=[row(Q_LORA), row(KV_LORA), row(SEG_KR), row(HP), row(HP), col(HP), col(HP),
                  full(g_qn), full(g_kvn), full(wqt), full(wqrt), full(wk), full(wvt)],
        out_specs=[pl.BlockSpec((H_MLA, HP, tm), lambda i: (0, 0, i)),
                   pl.BlockSpec((H_MLA, tm, HP), lambda i: (0, i, 0)),
                   pl.BlockSpec((H_MLA, VP, tm), lambda i: (0, 0, i))],
        out_shape=[jax.ShapeDtypeStruct((H_MLA, HP, t), BF16),
                   jax.ShapeDtypeStruct((H_MLA, t, HP), BF16),
                   jax.ShapeDtypeStruct((H_MLA, VP, t), BF16)],
        compiler_params=_cparams(1),
        name="mla_prep",
    )(pqd, ckv, kr, cos, sin, cost, sint, g_qn, g_kvn, wqt, wqrt, wk, wvt)


def _scores(k, qt, m):
    s = jnp.dot(k, qt, preferred_element_type=F32)
    return s, jnp.maximum(m, jnp.max(s, axis=0, keepdims=True))


def _accumulate(s, vt, m, m_new, acc):
    p = jnp.exp2(s - m_new).astype(BF16)
    return jnp.exp2(m - m_new) * acc + jnp.dot(vt, p, preferred_element_type=F32)


def _normalized_rows(acc):
    o = acc / acc[ONES_ROW:ONES_ROW + 1, :]
    return jnp.concatenate([o, jnp.zeros((HP - VP, o.shape[1]), F32)], axis=0).T


def _lagged_pass(qt_ref, k_ref, vt_ref, m_ref, acc_ref, p_ref, sc_ref, *, ncb, n_kv, tk, cb, depth):
    k0 = k_ref[0:BF16_ROWS, :]
    for c in range(ncb):
        s0 = jnp.dot(k0, qt_ref[:, c * cb:(c + 1) * cb], preferred_element_type=F32)
        m_ref[c] = jnp.max(s0, axis=0, keepdims=True)
    acc_ref[...] = jnp.zeros(acc_ref.shape, F32)
    p_ref[...] = jnp.zeros(p_ref.shape, BF16)
    sc_ref[...] = jnp.ones(sc_ref.shape, F32)

    def carried(vt_prev):
        return [(ncb - depth + d, p_ref[d], sc_ref[d], vt_prev) for d in range(depth)]

    def drain(entry):
        pc, pp, psc, pvt = entry
        acc_ref[pc] = (acc_ref[pc] + jnp.dot(pvt, pp, preferred_element_type=F32)) * psc

    def step(j, jump):
        off = pl.multiple_of(j * tk, tk)
        off_prev = pl.multiple_of(jnp.maximum(j - 1, 0) * tk, tk)
        k = k_ref[pl.ds(off, tk), :]
        vt = vt_ref[:, pl.ds(off, tk)]
        pending = carried(vt_ref[:, pl.ds(off_prev, tk)])
        for c in range(ncb):
            m = m_ref[c]
            s = jnp.dot(k, qt_ref[:, c * cb:(c + 1) * cb], preferred_element_type=F32)
            p = jnp.exp2(s - m).astype(BF16)
            cm = jnp.max(s, axis=0, keepdims=True)
            m_new = jnp.maximum(m, cm)
            jump = jnp.maximum(jump, cm - m)
            m_ref[c] = m_new
            drain(pending.pop(0))
            pending.append((c, p, jnp.exp2(m - m_new), vt))
        for d, (_, pp, psc, _) in enumerate(pending):
            p_ref[d] = pp
            sc_ref[d] = psc
        return jump

    jump = lax.fori_loop(0, n_kv, step, jnp.zeros((1, cb), F32))
    for entry in carried(vt_ref[:, (n_kv - 1) * tk:]):
        drain(entry)
    return jump


def _exact_pass(qt_ref, k_ref, vt_ref, m_ref, acc_ref, s_ref, mp_ref, *, ncb, n_kv, tk, cb, depth):
    m_ref[...] = jnp.full(m_ref.shape, NEG_BIG, F32)
    acc_ref[...] = jnp.zeros(acc_ref.shape, F32)
    s_ref[...] = jnp.full(s_ref.shape, -jnp.inf, F32)
    mp_ref[...] = jnp.full(mp_ref.shape, NEG_BIG, F32)

    def carried(vt_prev):
        return [(ncb - depth + d, s_ref[d], mp_ref[d, 0], mp_ref[d, 1], vt_prev) for d in range(depth)]

    def drain(entry):
        pc, ps, pm_old, pm_new, pvt = entry
        acc_ref[pc] = _accumulate(ps, pvt, pm_old, pm_new, acc_ref[pc])

    def step(j, carry):
        off = pl.multiple_of(j * tk, tk)
        off_prev = pl.multiple_of(jnp.maximum(j - 1, 0) * tk, tk)
        k = k_ref[pl.ds(off, tk), :]
        vt = vt_ref[:, pl.ds(off, tk)]
        pending = carried(vt_ref[:, pl.ds(off_prev, tk)])
        for c in range(ncb):
            m_old = m_ref[c]
            s, m_new = _scores(k, qt_ref[:, c * cb:(c + 1) * cb], m_old)
            m_ref[c] = m_new
            drain(pending.pop(0))
            pending.append((c, s, m_old, m_new, vt))
        for d, (_, ps, pm_old, pm_new, _) in enumerate(pending):
            s_ref[d] = ps
            mp_ref[d, 0] = pm_old
            mp_ref[d, 1] = pm_new
        return carry

    lax.fori_loop(0, n_kv, step, 0)
    for entry in carried(vt_ref[:, (n_kv - 1) * tk:]):
        drain(entry)


def _attn_kernel(qt_ref, k_ref, vt_ref, o_ref, m_ref, acc_ref, s_ref, mp_ref, p_ref, sc_ref, *,
                 tq, tk, t, cb, depth):
    geom = dict(ncb=tq // cb, n_kv=t // tk, tk=tk, cb=cb, depth=depth)

    def write_out():
        for c in range(tq // cb):
            o_ref[c * cb:(c + 1) * cb, :] = _normalized_rows(acc_ref[c]).astype(o_ref.dtype)

    jump = _lagged_pass(qt_ref, k_ref, vt_ref, m_ref, acc_ref, p_ref, sc_ref, **geom)
    write_out()

    @pl.when(jnp.max(jump) > MAX_LAG_JUMP)
    def _():
        _exact_pass(qt_ref, k_ref, vt_ref, m_ref, acc_ref, s_ref, mp_ref, **geom)
        write_out()

    @pl.when(pl.program_id(1) == 0)
    def _():
        m0 = jnp.full((1, CTX), NEG_BIG, F32)
        s, m_new = _scores(k_ref[0:CTX, :], qt_ref[:, 0:CTX], m0)
        acc_c = _accumulate(s, vt_ref[:, 0:CTX], m0, m_new, jnp.zeros((VP, CTX), F32))
        o_ref[0:CTX, :] = _normalized_rows(acc_c).astype(o_ref.dtype)


def _attention(qt, k, vt, tq, tk, cb, depth):
    t = k.shape[1]
    assert depth <= tq // cb
    return pl.pallas_call(
        functools.partial(_attn_kernel, tq=tq, tk=tk, t=t, cb=cb, depth=depth),
        grid=(H_MLA, t // tq),
        in_specs=[pl.BlockSpec((None, HP, tq), lambda h, i: (h, 0, i)),
                  pl.BlockSpec((None, t, HP), lambda h, i: (h, 0, 0)),
                  pl.BlockSpec((None, VP, t), lambda h, i: (h, 0, 0))],
        out_specs=pl.BlockSpec((tq, HP), lambda h, i: (i, h)),
        out_shape=jax.ShapeDtypeStruct((t, H_MLA * HP), BF16),
        scratch_shapes=[pltpu.VMEM((tq // cb, 1, cb), F32), pltpu.VMEM((tq // cb, VP, cb), F32),
                        pltpu.VMEM((depth, tk, cb), F32), pltpu.VMEM((depth, 2, 1, cb), F32),
                        pltpu.VMEM((depth, tk, cb), BF16), pltpu.VMEM((depth, 1, cb), F32)],
        compiler_params=_cparams(2),
        name="attention",
    )(qt, k, vt)


def _split3(x):
    hi = x.astype(BF16)
    r = x - hi.astype(F32)
    mid = r.astype(BF16)
    lo = (r - mid.astype(F32)).astype(BF16)
    return hi, mid, lo


def _log_sigmoid(x):
    return jnp.minimum(x, 0.0) - jnp.log(1.0 + jnp.exp(-jnp.abs(x)))


def _mlstm_kernel(qf_ref, kf_ref, vtf_ref, gcf_ref, grf_ref, qb_ref, kb_ref, vtb_ref, gcb_ref, grb_ref,
                  bc_ref, br_ref, of_ref, ob_ref, ct_scr, m_scr, *, L):
    @pl.when(pl.program_id(0) == 0)
    def _():
        ct_scr[...] = jnp.zeros_like(ct_scr)
        m_scr[...] = jnp.zeros_like(m_scr)

    _mlstm_chunk(qf_ref, kf_ref, vtf_ref, gcf_ref, grf_ref, bc_ref, br_ref, of_ref, ct_scr.at[0], m_scr.at[0],
                 L=L, rev=0)
    _mlstm_chunk(qb_ref, kb_ref, vtb_ref, gcb_ref, grb_ref, bc_ref, br_ref, ob_ref, ct_scr.at[1], m_scr.at[1],
                 L=L, rev=1)


def _mlstm_chunk(q_ref, k_ref, vt_ref, gc_ref, gr_ref, bc_ref, br_ref, o_ref, ct_scr, m_scr, *, L, rev):

    row = lax.broadcasted_iota(jnp.int32, (L, L), 0)
    col = lax.broadcasted_iota(jnp.int32, (L, L), 1)
    before = (row >= col) if rev else (row <= col)
    before_b = before.astype(BF16)
    after_b = ((row <= col) if rev else (row >= col)).astype(BF16)

    gc = gc_ref[...] + bc_ref[...]
    gr = gr_ref[...] + br_ref[...]
    lfc = _log_sigmoid(gc)
    lfr = _log_sigmoid(gr)
    b_col = sum(jnp.dot(after_b, p, preferred_element_type=F32) for p in _split3(lfc))
    b_row = sum(jnp.dot(p, before_b, preferred_element_type=F32) for p in _split3(lfr))
    ones_rows = (lax.broadcasted_iota(jnp.int32, (MV, L), 0) == 0).astype(BF16)

    for h in range(H_M):
        ci = 2 * H_M * rev + h
        cf = ci + H_M
        g_c = gc[:, ci:ci + 1] - b_col[:, cf:cf + 1]
        b_r = b_row[cf:cf + 1, :]
        i_r = gr[ci:ci + 1, :]
        bl = jnp.sum(lfr[cf:cf + 1, :], axis=1, keepdims=True)
        m = m_scr[h][0:1, 0:1]
        ct = ct_scr[h]

        pm = jnp.max(jnp.where(before, g_c, -jnp.inf), axis=0, keepdims=True)
        inter = b_r + m
        mj = jnp.maximum(inter, b_r + pm)
        w = jnp.exp(jnp.where(before, g_c + (b_r - mj), -jnp.inf))
        sc = jnp.exp(inter - mj)

        q = q_ref[:, h * MQK:(h + 1) * MQK] * (MQK ** -0.5)
        k = k_ref[:, h * MQK:(h + 1) * MQK]
        vt_aug = jnp.concatenate([vt_ref[h * MV:(h + 1) * MV, :], ones_rows], axis=0)
        kq = lax.dot_general(k, q, _NT, preferred_element_type=F32) * w
        num = (jnp.dot(vt_aug, kq.astype(BF16), preferred_element_type=F32)
               + sc * lax.dot_general(ct.astype(BF16), q, _NT, preferred_element_type=F32))
        den = jnp.maximum(jnp.abs(num[MV:MV + 1, :]), jnp.exp(-mj))
        o_ref[:, h * MV:(h + 1) * MV] = (num[:MV, :] / den).T

        wlog = bl - b_r + i_r
        m_new = jnp.maximum(bl + m, jnp.max(wlog, axis=1, keepdims=True))
        ws = jnp.exp(wlog - m_new)
        sd = jnp.exp(bl + m - m_new)
        wvt = (vt_aug.astype(F32) * ws).astype(BF16)
        ct_scr[h] = sd * ct + jnp.dot(wvt, k, preferred_element_type=F32)
        m_scr[h] = jnp.broadcast_to(m_new, m_scr.shape[1:])


def _mlstm(mq, mk, mvt, pg, pg_t, bg_c, bg_r, L):
    t = mq.shape[0]
    nc = t // L
    fwd = lambda c: c
    bwd = lambda c: jnp.where(c == 0, 0, nc - c)

    def chunk_specs(idx):
        return [pl.BlockSpec((L, SEG_Q), lambda c: (idx(c), 0)),
                pl.BlockSpec((L, SEG_K), lambda c: (idx(c), 0)),
                pl.BlockSpec((SEG_V, L), lambda c: (0, idx(c))),
                pl.BlockSpec((L, LANES), lambda c: (idx(c), 0)),
                pl.BlockSpec((N_GATES, L), lambda c: (0, idx(c)))]

    chunk_args = (mq, mk, mvt, pg, pg_t)
    return pl.pallas_call(
        functools.partial(_mlstm_kernel, L=L),
        grid=(nc,),
        in_specs=chunk_specs(fwd) + chunk_specs(bwd)
        + [pl.BlockSpec((1, LANES), lambda c: (0, 0)), pl.BlockSpec((N_GATES, 1), lambda c: (0, 0))],
        out_specs=[pl.BlockSpec((L, H_M * MV), lambda c: (fwd(c), 0)),
                   pl.BlockSpec((L, H_M * MV), lambda c: (bwd(c), 0))],
        out_shape=[jax.ShapeDtypeStruct((t, H_M * MV), F32)] * 2,
        scratch_shapes=[pltpu.VMEM((2, H_M, 2 * MV, MQK), F32), pltpu.VMEM((2, H_M, SUBLANES, LANES), F32)],
        compiler_params=_cparams(1),
        name="mlstm",
    )(*chunk_args, *chunk_args, bg_c, bg_r)


def _router(logits):
    tm = logits.shape[0]
    lane = lax.broadcasted_iota(jnp.int32, (tm, LANES), 1)
    big = jnp.int32(LANES)
    neg = -jnp.inf
    is_g = lane < N_GROUPS
    lg = jnp.where(is_g, logits, neg)
    gmax = jnp.max(lg, axis=-1, keepdims=True)
    grp = jnp.min(jnp.where(lg == gmax, lane, big), axis=-1, keepdims=True)
    p_grp = 1.0 / jnp.sum(jnp.where(is_g, jnp.exp(lg - gmax), 0.0), axis=-1, keepdims=True)
    e_lo = N_GROUPS + grp * EPG
    in_grp = (lane >= e_lo) & (lane < e_lo + EPG)
    le = jnp.where(in_grp, logits, neg)
    l1 = jnp.max(le, axis=-1, keepdims=True)
    i1 = jnp.min(jnp.where(le == l1, lane, big), axis=-1, keepdims=True)
    le2 = jnp.where(lane == i1, neg, le)
    l2 = jnp.max(le2, axis=-1, keepdims=True)
    i2 = jnp.min(jnp.where(le2 == l2, lane, big), axis=-1, keepdims=True)
    r = jnp.exp(l2 - l1)
    w1 = p_grp / (1.0 + r)
    w2 = w1 * r
    return jnp.where(lane == i1, w1, 0.0) + jnp.where(lane == i2, w2, 0.0), grp


def _merge_kernel(att_ref, hf_ref, hb_ref, po_ref, pm_ref, x_ref, mod_ref, gmh_ref, l1g_ref, l1b_ref,
                  wa_ref, wm_ref, wo_ref, wr_ref, br_ref, x1_ref, hm_ref, plan_ref, cnt_ref, *, tm):
    is_ctx = _is_ctx_rows(tm)
    y_mla = jnp.dot(att_ref[...], wa_ref[...], preferred_element_type=F32)
    hs = hf_ref[...] + hb_ref[...]
    parts = []
    for h in range(H_M):
        sl = slice(h * MV, (h + 1) * MV)
        parts.append(_ln(hs[:, sl]))
    hn = jnp.concatenate(parts, axis=1) * gmh_ref[...] * po_ref[...].astype(F32)
    y_ml = jnp.dot(hn.astype(BF16), wm_ref[...], preferred_element_type=F32)
    g_a = pm_ref[:, :D].astype(F32)
    g_b = pm_ref[:, D:].astype(F32)
    y = jnp.dot((g_a * y_mla + g_b * y_ml).astype(BF16), wo_ref[...], preferred_element_type=F32)
    x1 = _ln(ALPHA * x_ref[...] + _mod(mod_ref, 2, is_ctx) * y) * l1g_ref[...] + l1b_ref[...]
    x1_ref[...] = x1
    hm = _ln(x1) * (1.0 + _mod(mod_ref, 4, is_ctx)) + _mod(mod_ref, 3, is_ctx)
    hm_ref[...] = hm.astype(BF16)
    h_hi = hm.astype(BF16)
    h_lo = (hm - h_hi.astype(F32)).astype(BF16)
    logits = (jnp.dot(h_hi, wr_ref[0], preferred_element_type=F32)
              + jnp.dot(h_hi, wr_ref[1], preferred_element_type=F32)
              + jnp.dot(h_lo, wr_ref[0], preferred_element_type=F32)) + br_ref[...]
    comb, grp = _router(logits)
    lane = lax.broadcasted_iota(jnp.int32, (tm, LANES), 1)
    in_group = lane == grp
    earlier = (lax.broadcasted_iota(jnp.int32, (tm, tm), 0)
               > lax.broadcasted_iota(jnp.int32, (tm, tm), 1)).astype(BF16)
    n_before = jnp.dot(earlier, in_group.astype(BF16), preferred_element_type=F32)
    rank = jnp.sum(jnp.where(in_group, n_before, 0.0), axis=-1, keepdims=True)
    plan_ref[...] = (comb + jnp.where(lane == PLAN_GROUP, grp.astype(F32), 0.0)
                     + jnp.where(lane == PLAN_RANK, rank, 0.0))
    cnt_ref[...] = jnp.broadcast_to(jnp.sum(in_group.astype(F32), axis=0, keepdims=True), cnt_ref.shape)


def _merge(att, hf, hb, po, pm, x_all, mod, g_mh, l1g, l1b, wa, wm, wo, wr, br, tm):
    t = x_all.shape[0]
    row = lambda n: pl.BlockSpec((tm, n), lambda i: (i, 0))
    full = lambda a: pl.BlockSpec(a.shape, lambda i: (0,) * a.ndim)
    return pl.pallas_call(
        functools.partial(_merge_kernel, tm=tm),
        grid=(t // tm,),
        in_specs=[row(H_MLA * HP), row(H_M * MV), row(H_M * MV), row(SEG_PO), row(SEG_PM), row(D),
                  full(mod), full(g_mh), full(l1g), full(l1b), full(wa), full(wm), full(wo), full(wr), full(br)],
        out_specs=[row(D), row(D), row(LANES), pl.BlockSpec((None, SUBLANES, LANES), lambda i: (i, 0, 0))],
        out_shape=[jax.ShapeDtypeStruct((t, D), F32), jax.ShapeDtypeStruct((t, D), BF16),
                   jax.ShapeDtypeStruct((t, LANES), F32), jax.ShapeDtypeStruct((t // tm, SUBLANES, LANES), F32)],
        compiler_params=_cparams(1),
        name="merge",
    )(att, hf, hb, po, pm, x_all, mod, g_mh, l1g, l1b, wa, wm, wo, wr, br)


def _moe_kernel(cnt_ref, layer_ref, hm_ref, plan_ref, plant_ref, x1_ref, mod_ref, l2g_ref, l2b_ref, wg_ref, wu_ref, wd_ref,
                o_ref, *, tm, tp, slot_sizes):
    i = pl.program_id(0)
    grp = pl.program_id(1)
    n_sub = tm // tp

    @pl.when(grp == 0)
    def _():
        o_ref[...] = jnp.zeros_like(o_ref)

    n_tok = cnt_ref[i * n_sub, grp]
    for j in range(1, n_sub):
        n_tok = jnp.maximum(n_tok, cnt_ref[i * n_sub + j, grp])

    def rounds_of(slots):
        return functools.partial(_moe_round, hm_ref, plan_ref, plant_ref, wg_ref, wu_ref, wd_ref, o_ref,
                                 grp=grp, n_sub=n_sub, tp=tp, slots=slots)

    big = slot_sizes[-1]
    tails = slot_sizes[:-1]
    n_full = lax.div(n_tok, big) + (lax.rem(n_tok, big) > (tails[-1] if tails else 0)).astype(jnp.int32)
    rest = jnp.maximum(n_tok - n_full * big, 0)
    full_round = rounds_of(big)
    lax.fori_loop(0, n_full, lambda r, carry: full_round(r * big, carry), 0)
    lower = 0
    for slots in tails:
        @pl.when((rest > lower) & (rest <= slots))
        def _(tail_round=rounds_of(slots)):
            tail_round(n_full * big, 0)

        lower = slots

    @pl.when(grp == N_GROUPS - 1)
    def _():
        is_ctx = _is_ctx_rows(tm)
        z = ALPHA * x1_ref[...] + _mod(mod_ref, 5, is_ctx) * o_ref[...]
        o_ref[...] = _ln(z) * l2g_ref[...] + l2b_ref[...]


def _moe_round(hm_ref, plan_ref, plant_ref, wg_ref, wu_ref, wd_ref, o_ref, first, carry, *, grp, n_sub, tp, slots):
    grp_f = grp.astype(F32)
    base = first.astype(F32)
    slot_rows = lax.broadcasted_iota(jnp.int32, (slots, tp), 0).astype(F32)
    slot_cols = lax.broadcasted_iota(jnp.int32, (tp, slots), 1).astype(F32)
    lane = lax.broadcasted_iota(jnp.int32, (n_sub * slots, LANES), 1)
    xs, cs = [], []
    for j in range(n_sub):
        rows = slice(j * tp, (j + 1) * tp)
        in_grp = plant_ref[PLAN_GROUP:PLAN_GROUP + 1, rows] == grp_f
        rank = plant_ref[PLAN_RANK:PLAN_RANK + 1, rows]
        gather = ((slot_rows == rank - base) & in_grp).astype(BF16)
        xs.append(jnp.dot(gather, hm_ref[rows, :], preferred_element_type=F32).astype(BF16))
        cs.append(sum(jnp.dot(gather, p, preferred_element_type=F32)
                      for p in _split3(plan_ref[rows, :])))
    xs = jnp.concatenate(xs, axis=0)
    cs = jnp.concatenate(cs, axis=0)
    hid = []
    for g in range(EPG):
        c_e = jnp.sum(jnp.where(lane == N_GROUPS + grp * EPG + g, cs, 0.0), axis=-1, keepdims=True)
        a = jnp.dot(xs, wg_ref[g], preferred_element_type=F32)
        u = jnp.dot(xs, wu_ref[g], preferred_element_type=F32)
        hid.append((a * _sigmoid(a) * u * c_e).astype(BF16))
    ys = jnp.dot(jnp.concatenate(hid, axis=1), wd_ref[...].reshape(EPG * D_EXP, D),
                 preferred_element_type=F32)
    ys = ys.astype(BF16)
    for j in range(n_sub):
        rows = slice(j * tp, (j + 1) * tp)
        mine = slice(j * slots, (j + 1) * slots)
        in_grp = plan_ref[rows, PLAN_GROUP:PLAN_GROUP + 1] == grp_f
        rank = plan_ref[rows, PLAN_RANK:PLAN_RANK + 1]
        scatter = ((slot_cols == rank - base) & in_grp).astype(BF16)
        o_ref[rows, :] += jnp.dot(scatter, ys[mine], preferred_element_type=F32)
    return carry


def _moe(counts, hm, plan, plan_t, x1, mod, l2g, l2b, wg, wu, wd, layer, tp):
    t = hm.shape[0]
    tm = tp * (MOE_PLAN_TILES if (t // tp) % MOE_PLAN_TILES == 0 else 1)
    row = lambda n, **kw: pl.BlockSpec((tm, n), lambda i, e, cnt, lay: (i, 0), **kw)
    full = lambda a: pl.BlockSpec(a.shape, lambda i, e, cnt, lay: (0,) * a.ndim)
    group_of = lambda a: pl.BlockSpec((None, EPG) + a.shape[2:], lambda i, e, cnt, lay: (lay[0], e, 0, 0))
    return pl.pallas_call(
        functools.partial(_moe_kernel, tm=tm, tp=tp, slot_sizes=_moe_slot_sizes(tp)),
        grid_spec=pltpu.PrefetchScalarGridSpec(
            num_scalar_prefetch=2,
            grid=(t // tm, N_GROUPS),
            in_specs=[row(D), row(LANES), pl.BlockSpec((2, tm), lambda i, e, cnt, lay: (0, i)),
                      row(D, pipeline_mode=pl.Buffered(1)),
                      full(mod), full(l2g), full(l2b), group_of(wg), group_of(wu), group_of(wd)],
            out_specs=row(D)),
        out_shape=jax.ShapeDtypeStruct((t, D), F32),
        compiler_params=_cparams(2, VMEM_LIMIT_MOE),
        name="moe",
    )(counts, jnp.full((1,), layer, jnp.int32), hm, plan, plan_t, x1, mod, l2g, l2b, wg, wu, wd)


def _rot_cols(w):
    a1, a2, b1, b2 = jnp.split(w, 4, axis=-1)
    return jnp.concatenate([-a2, a1, -b2, b1], axis=-1)


def _place_rope(w):
    return jnp.pad(w, ((0, 0), (QK_NOPE, HP - QK_NOPE - QK_ROPE)))


def _rope_tables(t):
    rows = (t - CTX) // GRID_W
    row, col = jnp.meshgrid(jnp.arange(rows, dtype=F32), jnp.arange(GRID_W, dtype=F32), indexing="ij")
    row, col = row.reshape(-1), col.reshape(-1)
    half = QK_ROPE // 2
    inv = ROPE_BASE ** (-jnp.arange(0, half, 2, dtype=F32) / half)
    ar, ac = row[:, None] * inv, col[:, None] * inv
    ang = jnp.concatenate([ar, ar, ac, ac], axis=-1)
    ang = jnp.concatenate([jnp.zeros((CTX, QK_ROPE), F32), ang], axis=0)
    pad = ((0, 0), (QK_NOPE, HP - QK_NOPE - QK_ROPE))
    cos = jnp.pad(jnp.cos(ang), pad) + jnp.pad(jnp.ones((t, QK_NOPE), F32), ((0, 0), (0, HP - QK_NOPE)))
    sin = jnp.pad(jnp.sin(ang), pad)
    return cos, sin, cos.T, sin.T


def _layer_weights(l, w_in, w_uq, w_uk, w_uv, w_bo_mla, w_bo_mlstm, w_out, w_rg, b_rg, w_re, b_re):
    wi = w_in[l]
    o = 0
    segs = []
    for n in (Q_LORA, KV_LORA, QK_ROPE, SEG_Q, SEG_K, SEG_V, SEG_PO, N_GATES, SEG_PM):
        segs.append(wi[:, o:o + n])
        o += n
    s_pqd, s_ckv, s_kr, s_q, s_k, s_v, s_po, s_pg, s_pm = segs
    w_in_p = jnp.concatenate(
        [s_pqd, s_ckv, _place_rope(s_kr), _place_rope(_rot_cols(s_kr)), s_q, s_k, s_po,
         jnp.pad(s_pg, ((0, 0), (0, SEG_PG - N_GATES))), s_pm], axis=1).astype(BF16)

    uq = w_uq[l].reshape(Q_LORA, H_MLA, QK_NOPE + QK_ROPE)
    padh = ((0, 0), (0, 0), (0, HP - QK_NOPE - QK_ROPE))
    wq = jnp.pad(uq, padh).reshape(Q_LORA, H_MLA * HP)
    wqr = jnp.pad(jnp.concatenate([jnp.zeros_like(uq[..., :QK_NOPE]), _rot_cols(uq[..., QK_NOPE:])], axis=-1),
                  padh).reshape(Q_LORA, H_MLA * HP)
    padv = ((0, 0), (0, 0), (0, HP - V_HEAD))
    wk = jnp.pad(w_uk[l].reshape(KV_LORA, H_MLA, QK_NOPE), padv).reshape(KV_LORA, H_MLA * HP)
    wv = jnp.pad(w_uv[l].reshape(KV_LORA, H_MLA, V_HEAD), ((0, 0), (0, 0), (0, VP - V_HEAD))).reshape(KV_LORA, H_MLA * VP)
    wa = jnp.pad(w_bo_mla[l].reshape(H_MLA, V_HEAD, D), ((0, 0), (0, HP - V_HEAD), (0, 0))).reshape(H_MLA * HP, D)

    wr = jnp.pad(jnp.concatenate([w_rg[l], w_re[l]], axis=1), ((0, 0), (0, LANES - N_GROUPS - N_EXP)))
    wr_hi = wr.astype(BF16)
    wr_lo = (wr - wr_hi.astype(F32)).astype(BF16)
    br = jnp.pad(jnp.concatenate([b_rg[l], b_re[l]]), (0, LANES - N_GROUPS - N_EXP)).reshape(1, LANES)
    return dict(
        w_in=w_in_p, w_mvt=s_v.T.astype(BF16), wqt=wq.T.astype(BF16), wqrt=wqr.T.astype(BF16), wk=wk.astype(BF16), wvt=wv.T.astype(BF16),
        wa=wa.astype(BF16), wm=w_bo_mlstm[l].astype(BF16), wo=w_out[l].astype(BF16),
        wr=jnp.stack([wr_hi, wr_lo]), br=br)


def kernel(x, c, ctx, c_ctx, w_ada, b_ada, w_in, b_gates, w_uq, w_uk, w_uv, g_qn, g_kvn, g_mh, w_bo_mla,
           w_bo_mlstm, w_out, ln1_g, ln1_b, w_rg, b_rg, w_re, b_re, w_e_gate, w_e_up, w_e_down, ln2_g, ln2_b):
    assert x.shape[0] == 1 and c.shape[0] == 1 and ctx.shape[1] == CTX
    x_all = jnp.concatenate([ctx[0], x[0]], axis=0)
    t = x_all.shape[0]
    tm = ROW_TILE if t % ROW_TILE == 0 else M_CHUNK
    tk = ATT_TK if t % ATT_TK == 0 else M_CHUNK
    assert t % tm == 0 and t % ATT_TQ == 0 and t % tk == 0 and t % M_CHUNK == 0

    cc = jnp.pad(jnp.concatenate([c, c_ctx[None]], axis=0), ((0, SUBLANES - 2), (0, 0)))
    mods = _ada(cc, w_ada, b_ada)
    tabs = _rope_tables(t)
    row2 = lambda a: a.reshape(1, -1)
    wg_all, wu_all, wd_all = w_e_gate.astype(BF16), w_e_up.astype(BF16), w_e_down.astype(BF16)

    for l in range(DEPTH):
        w = _layer_weights(l, w_in, w_uq, w_uk, w_uv, w_bo_mla, w_bo_mlstm, w_out, w_rg, b_rg, w_re, b_re)
        mod = mods[l]
        pqd, ckv, kr, mq, mk, po, pg, pm, mvt = _inproj(x_all, mod, w["w_in"], w["w_mvt"], tm)
        qt, kk, vt = _mla_prep(pqd, ckv, kr, tabs, row2(g_qn[l]), row2(g_kvn[l]),
                               w["wqt"], w["wqrt"], w["wk"], w["wvt"], tm)
        att = _attention(qt, kk, vt, ATT_TQ, tk, ATT_CB, ATT_DEPTH)
        pg_t = pg[:, :N_GATES].T
        bg_c = jnp.pad(b_gates[l], (0, LANES - N_GATES)).reshape(1, LANES)
        bg_r = b_gates[l].reshape(N_GATES, 1)
        hf, hb = _mlstm(mq, mk, mvt, pg, pg_t, bg_c, bg_r, M_CHUNK)
        x1, hm, plan, cnt = _merge(att, hf, hb, po, pm, x_all, mod, row2(g_mh[l]), row2(ln1_g[l]), row2(ln1_b[l]),
                                   w["wa"], w["wm"], w["wo"], w["wr"], w["br"], tm)
        counts = cnt[:, 0, :N_GROUPS].astype(jnp.int32)
        plan_t = plan[:, :2].T
        x_all = _moe(counts, hm, plan, plan_t, x1, mod, row2(ln2_g[l]), row2(ln2_b[l]),
                     wg_all, wu_all, wd_all, l, tm)
    return x_all[CTX:][None]
```

```python
import functools
import math

import jax
import jax.numpy as jnp
from jax import lax
from jax.experimental import pallas as pl
from jax.experimental.pallas import tpu as pltpu

F32 = jnp.float32
BF16 = jnp.bfloat16

D = 1024
DEPTH = 4
GRID_W = 64
CTX = 256
H_MLA = 8
QK_NOPE = 64
QK_ROPE = 32
V_HEAD = 64
Q_LORA = 384
KV_LORA = 256
ROPE_BASE = 10000.0
MLA_SCALE = (QK_NOPE + QK_ROPE) ** -0.5
H_M = 4
MQK = 64
MV = 128
N_GROUPS = 4
EPG = 8
N_EXP = N_GROUPS * EPG
D_EXP = 256
ALPHA = (2 * DEPTH) ** 0.25
LN_EPS = 1e-6

LANES = 128
SUBLANES = 8
NEG_BIG = -1e30
N_GATES = 4 * H_M
ADA_TN = 1536
VMEM_LIMIT = 56 * 1024 * 1024
VMEM_LIMIT_MOE = 58 * 1024 * 1024

HP = LANES
ONES_ROW = V_HEAD
BF16_ROWS = 16
VP = (V_HEAD + 1 + BF16_ROWS - 1) // BF16_ROWS * BF16_ROWS
QSCALE = MLA_SCALE * math.log2(math.e)

SEG_PQD = Q_LORA
SEG_CKV = KV_LORA
SEG_KR = 2 * HP
SEG_Q = H_M * MQK
SEG_K = H_M * MQK
SEG_V = H_M * MV
SEG_PO = H_M * MV
SEG_PG = LANES
SEG_PM = 2 * D
IN_SEGS = (SEG_PQD, SEG_CKV, SEG_KR, SEG_Q, SEG_K, SEG_PO, SEG_PG, SEG_PM)
IN_DTYPES = (F32, F32, F32, BF16, BF16, BF16, F32, BF16)
IN_GATES = (False, False, False, False, False, True, False, True)
NP_IN = sum(IN_SEGS)

ROW_TILE = 640
ATT_TQ = 3328
ATT_CB = 256
ATT_TK = 1280
MAX_LAG_JUMP = 64.0
ATT_DEPTH = 2
M_CHUNK = 256
MOE_SLOT_STEP = 64
MOE_SLOT_MAX = 256
MOE_PLAN_TILES = 2
PLAN_GROUP = 0
PLAN_RANK = 1


def _moe_slot_sizes(tm):
    return tuple(s for s in range(MOE_SLOT_STEP, MOE_SLOT_MAX + 1, MOE_SLOT_STEP) if s <= tm)

_NT = (((1,), (1,)), ((), ()))


def _cparams(n_grid, vmem_limit=VMEM_LIMIT):
    return pltpu.CompilerParams(dimension_semantics=("arbitrary",) * n_grid, vmem_limit_bytes=vmem_limit)


def _ln(x):
    mu = jnp.mean(x, axis=-1, keepdims=True)
    xc = x - mu
    var = jnp.mean(xc * xc, axis=-1, keepdims=True)
    return xc * lax.rsqrt(var + LN_EPS)


def _rms(x, g):
    return x * lax.rsqrt(jnp.mean(x * x, axis=-1, keepdims=True) + LN_EPS) * g


def _sigmoid(x):
    return 1.0 / (1.0 + jnp.exp(-x))


def _is_ctx_rows(tm):
    rows = pl.program_id(0) * tm + lax.broadcasted_iota(jnp.int32, (tm, 1), 0)
    return rows < CTX


def _mod(mod_ref, k, is_ctx):
    lat = mod_ref[0:1, k * D:(k + 1) * D]
    ctx = mod_ref[1:2, k * D:(k + 1) * D]
    return jnp.where(is_ctx, ctx, lat)


def _ada_kernel(c_ref, w_ref, b_ref, o_ref):
    c = c_ref[...]
    s = c * _sigmoid(c)
    o_ref[...] = jnp.dot(s, w_ref[...], preferred_element_type=F32,
                         precision=lax.Precision.HIGHEST) + b_ref[...]


def _ada(cc, w_ada, b_ada):
    tn = ADA_TN
    n = 6 * D
    return pl.pallas_call(
        _ada_kernel,
        grid=(DEPTH, n // tn),
        in_specs=[pl.BlockSpec((SUBLANES, D), lambda l, j: (0, 0)),
                  pl.BlockSpec((None, D, tn), lambda l, j: (l, 0, j)),
                  pl.BlockSpec((None, 1, tn), lambda l, j: (l, 0, j))],
        out_specs=pl.BlockSpec((None, SUBLANES, tn), lambda l, j: (l, 0, j)),
        out_shape=jax.ShapeDtypeStruct((DEPTH, SUBLANES, n), F32),
        compiler_params=_cparams(2),
        name="ada",
    )(cc, w_ada, b_ada.reshape(DEPTH, 1, n))


def _inproj_kernel(x_ref, mod_ref, w_ref, wvt_ref, *o_refs, tm):
    is_ctx = _is_ctx_rows(tm)
    h = _ln(x_ref[...]) * (1.0 + _mod(mod_ref, 1, is_ctx)) + _mod(mod_ref, 0, is_ctx)
    hb = h.astype(BF16)
    off = 0
    for o, is_gate in zip(o_refs[:-1], IN_GATES):
        n = o.shape[-1]
        p = jnp.dot(hb, w_ref[:, off:off + n], preferred_element_type=F32)
        o[...] = (_sigmoid(p) if is_gate else p).astype(o.dtype)
        off += n
    o_refs[-1][...] = lax.dot_general(wvt_ref[...], hb, _NT, preferred_element_type=F32).astype(BF16)


def _inproj(x_all, mod, w_in_p, w_vt, tm):
    t = x_all.shape[0]
    return pl.pallas_call(
        functools.partial(_inproj_kernel, tm=tm),
        grid=(t // tm,),
        in_specs=[pl.BlockSpec((tm, D), lambda i: (i, 0)),
                  pl.BlockSpec((SUBLANES, 6 * D), lambda i: (0, 0)),
                  pl.BlockSpec((D, NP_IN), lambda i: (0, 0)),
                  pl.BlockSpec((SEG_V, D), lambda i: (0, 0))],
        out_specs=[pl.BlockSpec((tm, n), lambda i: (i, 0)) for n in IN_SEGS]
        + [pl.BlockSpec((SEG_V, tm), lambda i: (0, i))],
        out_shape=[jax.ShapeDtypeStruct((t, n), dt) for n, dt in zip(IN_SEGS, IN_DTYPES)]
        + [jax.ShapeDtypeStruct((SEG_V, t), BF16)],
        compiler_params=_cparams(1),
        name="inproj",
    )(x_all, mod, w_in_p, w_vt)


def _mla_prep_kernel(pqd_ref, ckv_ref, kr_ref, cos_ref, sin_ref, cost_ref, sint_ref, gq_ref, gk_ref,
                     wqt_ref, wqrt_ref, wk_ref, wvt_ref, qt_ref, k_ref, vt_ref, *, tm):
    qn = _rms(pqd_ref[...], gq_ref[...]).astype(BF16)
    qa = lax.dot_general(wqt_ref[...], qn, _NT, preferred_element_type=F32)
    qb = lax.dot_general(wqrt_ref[...], qn, _NT, preferred_element_type=F32)
    cost = cost_ref[...]
    sint = sint_ref[...]
    ckv = _rms(ckv_ref[...], gk_ref[...]).astype(BF16)
    kn = jnp.dot(ckv, wk_ref[...], preferred_element_type=F32)
    kr = kr_ref[:, :HP] * cos_ref[...] + kr_ref[:, HP:] * sin_ref[...]
    vt = lax.dot_general(wvt_ref[...], ckv, _NT, preferred_element_type=F32)
    is_ones_row = lax.broadcasted_iota(jnp.int32, (VP, tm), 0) == ONES_ROW
    for h in range(H_MLA):
        sl = slice(h * HP, (h + 1) * HP)
        qt_ref[h] = ((qa[sl] * cost + qb[sl] * sint) * QSCALE).astype(BF16)
        k_ref[h] = (kn[:, sl] + kr).astype(BF16)
        vt_ref[h] = jnp.where(is_ones_row, 1.0, vt[h * VP:(h + 1) * VP]).astype(BF16)


def _mla_prep(pqd, ckv, kr, tabs, g_qn, g_kvn, wqt, wqrt, wk, wvt, tm):
    t = pqd.shape[0]
    cos, sin, cost, sint = tabs
    row = lambda n: pl.BlockSpec((tm, n), lambda i: (i, 0))
    col = lambda n: pl.BlockSpec((n, tm), lambda i: (0, i))
    full = lambda a: pl.BlockSpec(a.shape, lambda i: (0,) * a.ndim)
    return pl.pallas_call(
        functools.partial(_mla_prep_kernel, tm=tm),
        grid=(t // tm,),
        in_specs=[row(Q_LORA), row(KV_LORA), row(SEG_KR), row(HP), row(HP), col(HP), col(HP),
                  full(g_qn), full(g_kvn), full(wqt), full(wqrt), full(wk), full(wvt)],
        out_specs=[pl.BlockSpec((H_MLA, HP, tm), lambda i: (0, 0, i)),
                   pl.BlockSpec((H_MLA, tm, HP), lambda i: (0, i, 0)),
                   pl.BlockSpec((H_MLA, VP, tm), lambda i: (0, 0, i))],
        out_shape=[jax.ShapeDtypeStruct((H_MLA, HP, t), BF16),
                   jax.ShapeDtypeStruct((H_MLA, t, HP), BF16),
                   jax.ShapeDtypeStruct((H_MLA, VP, t), BF16)],
        compiler_params=_cparams(1),
        name="mla_prep",
    )(pqd, ckv, kr, cos, sin, cost, sint, g_qn, g_kvn, wqt, wqrt, wk, wvt)


def _scores(k, qt, m):
    s = jnp.dot(k, qt, preferred_element_type=F32)
    return s, jnp.maximum(m, jnp.max(s, axis=0, keepdims=True))


def _accumulate(s, vt, m, m_new, acc):
    p = jnp.exp2(s - m_new).astype(BF16)
    return jnp.exp2(m - m_new) * acc + jnp.dot(vt, p, preferred_element_type=F32)


def _normalized_rows(acc):
    o = acc / acc[ONES_ROW:ONES_ROW + 1, :]
    return jnp.concatenate([o, jnp.zeros((HP - VP, o.shape[1]), F32)], axis=0).T


def _lagged_pass(qt_ref, k_ref, vt_ref, m_ref, acc_ref, p_ref, sc_ref, *, ncb, n_kv, tk, cb, depth):
    k0 = k_ref[0:BF16_ROWS, :]
    for c in range(ncb):
        s0 = jnp.dot(k0, qt_ref[:, c * cb:(c + 1) * cb], preferred_element_type=F32)
        m_ref[c] = jnp.max(s0, axis=0, keepdims=True)
    acc_ref[...] = jnp.zeros(acc_ref.shape, F32)
    p_ref[...] = jnp.zeros(p_ref.shape, BF16)
    sc_ref[...] = jnp.ones(sc_ref.shape, F32)

    def carried(vt_prev):
        return [(ncb - depth + d, p_ref[d], sc_ref[d], vt_prev) for d in range(depth)]

    def drain(entry):
        pc, pp, psc, pvt = entry
        acc_ref[pc] = (acc_ref[pc] + jnp.dot(pvt, pp, preferred_element_type=F32)) * psc

    def step(j, jump):
        off = pl.multiple_of(j * tk, tk)
        off_prev = pl.multiple_of(jnp.maximum(j - 1, 0) * tk, tk)
        k = k_ref[pl.ds(off, tk), :]
        vt = vt_ref[:, pl.ds(off, tk)]
        pending = carried(vt_ref[:, pl.ds(off_prev, tk)])
        for c in range(ncb):
            m = m_ref[c]
            s = jnp.dot(k, qt_ref[:, c * cb:(c + 1) * cb], preferred_element_type=F32)
            p = jnp.exp2(s - m).astype(BF16)
            cm = jnp.max(s, axis=0, keepdims=True)
            m_new = jnp.maximum(m, cm)
            jump = jnp.maximum(jump, cm - m)
            m_ref[c] = m_new
            drain(pending.pop(0))
            pending.append((c, p, jnp.exp2(m - m_new), vt))
        for d, (_, pp, psc, _) in enumerate(pending):
            p_ref[d] = pp
            sc_ref[d] = psc
        return jump

    jump = lax.fori_loop(0, n_kv, step, jnp.zeros((1, cb), F32))
    for entry in carried(vt_ref[:, (n_kv - 1) * tk:]):
        drain(entry)
    return jump


def _exact_pass(qt_ref, k_ref, vt_ref, m_ref, acc_ref, s_ref, mp_ref, *, ncb, n_kv, tk, cb, depth):
    m_ref[...] = jnp.full(m_ref.shape, NEG_BIG, F32)
    acc_ref[...] = jnp.zeros(acc_ref.shape, F32)
    s_ref[...] = jnp.full(s_ref.shape, -jnp.inf, F32)
    mp_ref[...] = jnp.full(mp_ref.shape, NEG_BIG, F32)

    def carried(vt_prev):
        return [(ncb - depth + d, s_ref[d], mp_ref[d, 0], mp_ref[d, 1], vt_prev) for d in range(depth)]

    def drain(entry):
        pc, ps, pm_old, pm_new, pvt = entry
        acc_ref[pc] = _accumulate(ps, pvt, pm_old, pm_new, acc_ref[pc])

    def step(j, carry):
        off = pl.multiple_of(j * tk, tk)
        off_prev = pl.multiple_of(jnp.maximum(j - 1, 0) * tk, tk)
        k = k_ref[pl.ds(off, tk), :]
        vt = vt_ref[:, pl.ds(off, tk)]
        pending = carried(vt_ref[:, pl.ds(off_prev, tk)])
        for c in range(ncb):
            m_old = m_ref[c]
            s, m_new = _scores(k, qt_ref[:, c * cb:(c + 1) * cb], m_old)
            m_ref[c] = m_new
            drain(pending.pop(0))
            pending.append((c, s, m_old, m_new, vt))
        for d, (_, ps, pm_old, pm_new, _) in enumerate(pending):
            s_ref[d] = ps
            mp_ref[d, 0] = pm_old
            mp_ref[d, 1] = pm_new
        return carry

    lax.fori_loop(0, n_kv, step, 0)
    for entry in carried(vt_ref[:, (n_kv - 1) * tk:]):
        drain(entry)


def _attn_kernel(qt_ref, k_ref, vt_ref, o_ref, m_ref, acc_ref, s_ref, mp_ref, p_ref, sc_ref, *,
                 tq, tk, t, cb, depth):
    geom = dict(ncb=tq // cb, n_kv=t // tk, tk=tk, cb=cb, depth=depth)

    def write_out():
        for c in range(tq // cb):
            o_ref[c * cb:(c + 1) * cb, :] = _normalized_rows(acc_ref[c]).astype(o_ref.dtype)

    jump = _lagged_pass(qt_ref, k_ref, vt_ref, m_ref, acc_ref, p_ref, sc_ref, **geom)
    write_out()

    @pl.when(jnp.max(jump) > MAX_LAG_JUMP)
    def _():
        _exact_pass(qt_ref, k_ref, vt_ref, m_ref, acc_ref, s_ref, mp_ref, **geom)
        write_out()

    @pl.when(pl.program_id(1) == 0)
    def _():
        m0 = jnp.full((1, CTX), NEG_BIG, F32)
        s, m_new = _scores(k_ref[0:CTX, :], qt_ref[:, 0:CTX], m0)
        acc_c = _accumulate(s, vt_ref[:, 0:CTX], m0, m_new, jnp.zeros((VP, CTX), F32))
        o_ref[0:CTX, :] = _normalized_rows(acc_c).astype(o_ref.dtype)


def _attention(qt, k, vt, tq, tk, cb, depth):
    t = k.shape[1]
    assert depth <= tq // cb
    return pl.pallas_call(
        functools.partial(_attn_kernel, tq=tq, tk=tk, t=t, cb=cb, depth=depth),
        grid=(H_MLA, t // tq),
        in_specs=[pl.BlockSpec((None, HP, tq), lambda h, i: (h, 0, i)),
                  pl.BlockSpec((None, t, HP), lambda h, i: (h, 0, 0)),
                  pl.BlockSpec((None, VP, t), lambda h, i: (h, 0, 0))],
        out_specs=pl.BlockSpec((tq, HP), lambda h, i: (i, h)),
        out_shape=jax.ShapeDtypeStruct((t, H_MLA * HP), BF16),
        scratch_shapes=[pltpu.VMEM((tq // cb, 1, cb), F32), pltpu.VMEM((tq // cb, VP, cb), F32),
                        pltpu.VMEM((depth, tk, cb), F32), pltpu.VMEM((depth, 2, 1, cb), F32),
                        pltpu.VMEM((depth, tk, cb), BF16), pltpu.VMEM((depth, 1, cb), F32)],
        compiler_params=_cparams(2),
        name="attention",
    )(qt, k, vt)


def _split3(x):
    hi = x.astype(BF16)
    r = x - hi.astype(F32)
    mid = r.astype(BF16)
    lo = (r - mid.astype(F32)).astype(BF16)
    return hi, mid, lo


def _log_sigmoid(x):
    return jnp.minimum(x, 0.0) - jnp.log(1.0 + jnp.exp(-jnp.abs(x)))


def _mlstm_kernel(qf_ref, kf_ref, vtf_ref, gcf_ref, grf_ref, qb_ref, kb_ref, vtb_ref, gcb_ref, grb_ref,
                  bc_ref, br_ref, of_ref, ob_ref, ct_scr, m_scr, *, L):
    @pl.when(pl.program_id(0) == 0)
    def _():
        ct_scr[...] = jnp.zeros_like(ct_scr)
        m_scr[...] = jnp.zeros_like(m_scr)

    row = lax.broadcasted_iota(jnp.int32, (L, L), 0)
    col = lax.broadcasted_iota(jnp.int32, (L, L), 1)
    ones_rows = (lax.broadcasted_iota(jnp.int32, (MV, L), 0) == 0).astype(BF16)
    scans = ((0, qf_ref, kf_ref, vtf_ref, gcf_ref, grf_ref, of_ref),
             (1, qb_ref, kb_ref, vtb_ref, gcb_ref, grb_ref, ob_ref))

    pairs = []
    for rev, q_ref, k_ref, vt_ref, gc_ref, gr_ref, o_ref in scans:
        before = (row >= col) if rev else (row <= col)
        before_b = before.astype(BF16)
        after_b = ((row <= col) if rev else (row >= col)).astype(BF16)
        gc = gc_ref[...] + bc_ref[...]
        gr = gr_ref[...] + br_ref[...]
        lfc = _log_sigmoid(gc)
        lfr = _log_sigmoid(gr)
        b_col = sum(jnp.dot(after_b, p, preferred_element_type=F32) for p in _split3(lfc))
        b_row = sum(jnp.dot(p, before_b, preferred_element_type=F32) for p in _split3(lfr))
        for h in range(H_M):
            ci = 2 * H_M * rev + h
            cf = ci + H_M
            q = q_ref[:, h * MQK:(h + 1) * MQK] * (MQK ** -0.5)
            k = k_ref[:, h * MQK:(h + 1) * MQK]
            ct = ct_scr[rev, h]
            pairs.append(dict(
                rev=rev, h=h, before=before, o_ref=o_ref, k=k, ct=ct,
                g_c=gc[:, ci:ci + 1] - b_col[:, cf:cf + 1],
                b_r=b_row[cf:cf + 1, :], i_r=gr[ci:ci + 1, :],
                bl=jnp.sum(lfr[cf:cf + 1, :], axis=1, keepdims=True),
                m=m_scr[rev, h][0:1, 0:1],
                vt_aug=jnp.concatenate([vt_ref[h * MV:(h + 1) * MV, :], ones_rows], axis=0),
                kq=lax.dot_general(k, q, _NT, preferred_element_type=F32),
                cq=lax.dot_general(ct.astype(BF16), q, _NT, preferred_element_type=F32)))

    for p in pairs:
        pm = jnp.max(jnp.where(p["before"], p["g_c"], -jnp.inf), axis=0, keepdims=True)
        inter = p["b_r"] + p["m"]
        p["mj"] = jnp.maximum(inter, p["b_r"] + pm)
        w = jnp.exp(jnp.where(p["before"], p["g_c"] + (p["b_r"] - p["mj"]), -jnp.inf))
        p["sc"] = jnp.exp(inter - p["mj"])
        p["kqw"] = (p["kq"] * w).astype(BF16)

    for p in pairs:
        h = p["h"]
        num = jnp.dot(p["vt_aug"], p["kqw"], preferred_element_type=F32) + p["sc"] * p["cq"]
        den = jnp.maximum(jnp.abs(num[MV:MV + 1, :]), jnp.exp(-p["mj"]))
        p["o_ref"][:, h * MV:(h + 1) * MV] = (num[:MV, :] / den).T

    for p in pairs:
        wlog = p["bl"] - p["b_r"] + p["i_r"]
        m_new = jnp.maximum(p["bl"] + p["m"], jnp.max(wlog, axis=1, keepdims=True))
        ws = jnp.exp(wlog - m_new)
        sd = jnp.exp(p["bl"] + p["m"] - m_new)
        wvt = (p["vt_aug"].astype(F32) * ws).astype(BF16)
        ct_scr[p["rev"], p["h"]] = sd * p["ct"] + jnp.dot(wvt, p["k"], preferred_element_type=F32)
        m_scr[p["rev"], p["h"]] = jnp.broadcast_to(m_new, m_scr.shape[2:])


def _mlstm(mq, mk, mvt, pg, pg_t, bg_c, bg_r, L):
    t = mq.shape[0]
    nc = t // L
    fwd = lambda c: c
    bwd = lambda c: jnp.where(c == 0, 0, nc - c)

    def chunk_specs(idx):
        return [pl.BlockSpec((L, SEG_Q), lambda c: (idx(c), 0)),
                pl.BlockSpec((L, SEG_K), lambda c: (idx(c), 0)),
                pl.BlockSpec((SEG_V, L), lambda c: (0, idx(c))),
                pl.BlockSpec((L, LANES), lambda c: (idx(c), 0)),
                pl.BlockSpec((N_GATES, L), lambda c: (0, idx(c)))]

    chunk_args = (mq, mk, mvt, pg, pg_t)
    return pl.pallas_call(
        functools.partial(_mlstm_kernel, L=L),
        grid=(nc,),
        in_specs=chunk_specs(fwd) + chunk_specs(bwd)
        + [pl.BlockSpec((1, LANES), lambda c: (0, 0)), pl.BlockSpec((N_GATES, 1), lambda c: (0, 0))],
        out_specs=[pl.BlockSpec((L, H_M * MV), lambda c: (fwd(c), 0)),
                   pl.BlockSpec((L, H_M * MV), lambda c: (bwd(c), 0))],
        out_shape=[jax.ShapeDtypeStruct((t, H_M * MV), F32)] * 2,
        scratch_shapes=[pltpu.VMEM((2, H_M, 2 * MV, MQK), F32), pltpu.VMEM((2, H_M, SUBLANES, LANES), F32)],
        compiler_params=_cparams(1),
        name="mlstm",
    )(*chunk_args, *chunk_args, bg_c, bg_r)


def _router(logits):
    tm = logits.shape[0]
    lane = lax.broadcasted_iota(jnp.int32, (tm, LANES), 1)
    big = jnp.int32(LANES)
    neg = -jnp.inf
    is_g = lane < N_GROUPS
    lg = jnp.where(is_g, logits, neg)
    gmax = jnp.max(lg, axis=-1, keepdims=True)
    grp = jnp.min(jnp.where(lg == gmax, lane, big), axis=-1, keepdims=True)
    p_grp = 1.0 / jnp.sum(jnp.where(is_g, jnp.exp(lg - gmax), 0.0), axis=-1, keepdims=True)
    e_lo = N_GROUPS + grp * EPG
    in_grp = (lane >= e_lo) & (lane < e_lo + EPG)
    le = jnp.where(in_grp, logits, neg)
    l1 = jnp.max(le, axis=-1, keepdims=True)
    i1 = jnp.min(jnp.where(le == l1, lane, big), axis=-1, keepdims=True)
    le2 = jnp.where(lane == i1, neg, le)
    l2 = jnp.max(le2, axis=-1, keepdims=True)
    i2 = jnp.min(jnp.where(le2 == l2, lane, big), axis=-1, keepdims=True)
    r = jnp.exp(l2 - l1)
    w1 = p_grp / (1.0 + r)
    w2 = w1 * r
    return jnp.where(lane == i1, w1, 0.0) + jnp.where(lane == i2, w2, 0.0), grp


def _merge_kernel(att_ref, hf_ref, hb_ref, po_ref, pm_ref, x_ref, mod_ref, gmh_ref, l1g_ref, l1b_ref,
                  wa_ref, wm_ref, wo_ref, wr_ref, br_ref, x1_ref, hm_ref, plan_ref, cnt_ref, *, tm):
    is_ctx = _is_ctx_rows(tm)
    y_mla = jnp.dot(att_ref[...], wa_ref[...], preferred_element_type=F32)
    hs = hf_ref[...] + hb_ref[...]
    parts = []
    for h in range(H_M):
        sl = slice(h * MV, (h + 1) * MV)
        parts.append(_ln(hs[:, sl]))
    hn = jnp.concatenate(parts, axis=1) * gmh_ref[...] * po_ref[...].astype(F32)
    y_ml = jnp.dot(hn.astype(BF16), wm_ref[...], preferred_element_type=F32)
    g_a = pm_ref[:, :D].astype(F32)
    g_b = pm_ref[:, D:].astype(F32)
    y = jnp.dot((g_a * y_mla + g_b * y_ml).astype(BF16), wo_ref[...], preferred_element_type=F32)
    x1 = _ln(ALPHA * x_ref[...] + _mod(mod_ref, 2, is_ctx) * y) * l1g_ref[...] + l1b_ref[...]
    x1_ref[...] = x1
    hm = _ln(x1) * (1.0 + _mod(mod_ref, 4, is_ctx)) + _mod(mod_ref, 3, is_ctx)
    hm_ref[...] = hm.astype(BF16)
    h_hi = hm.astype(BF16)
    h_lo = (hm - h_hi.astype(F32)).astype(BF16)
    logits = (jnp.dot(h_hi, wr_ref[0], preferred_element_type=F32)
              + jnp.dot(h_hi, wr_ref[1], preferred_element_type=F32)
              + jnp.dot(h_lo, wr_ref[0], preferred_element_type=F32)) + br_ref[...]
    comb, grp = _router(logits)
    lane = lax.broadcasted_iota(jnp.int32, (tm, LANES), 1)
    in_group = lane == grp
    earlier = (lax.broadcasted_iota(jnp.int32, (tm, tm), 0)
               > lax.broadcasted_iota(jnp.int32, (tm, tm), 1)).astype(BF16)
    n_before = jnp.dot(earlier, in_group.astype(BF16), preferred_element_type=F32)
    rank = jnp.sum(jnp.where(in_group, n_before, 0.0), axis=-1, keepdims=True)
    plan_ref[...] = (comb + jnp.where(lane == PLAN_GROUP, grp.astype(F32), 0.0)
                     + jnp.where(lane == PLAN_RANK, rank, 0.0))
    cnt_ref[...] = jnp.broadcast_to(jnp.sum(in_group.astype(F32), axis=0, keepdims=True), cnt_ref.shape)


def _merge(att, hf, hb, po, pm, x_all, mod, g_mh, l1g, l1b, wa, wm, wo, wr, br, tm):
    t = x_all.shape[0]
    row = lambda n: pl.BlockSpec((tm, n), lambda i: (i, 0))
    full = lambda a: pl.BlockSpec(a.shape, lambda i: (0,) * a.ndim)
    return pl.pallas_call(
        functools.partial(_merge_kernel, tm=tm),
        grid=(t // tm,),
        in_specs=[row(H_MLA * HP), row(H_M * MV), row(H_M * MV), row(SEG_PO), row(SEG_PM), row(D),
                  full(mod), full(g_mh), full(l1g), full(l1b), full(wa), full(wm), full(wo), full(wr), full(br)],
        out_specs=[row(D), row(D), row(LANES), pl.BlockSpec((None, SUBLANES, LANES), lambda i: (i, 0, 0))],
        out_shape=[jax.ShapeDtypeStruct((t, D), F32), jax.ShapeDtypeStruct((t, D), BF16),
                   jax.ShapeDtypeStruct((t, LANES), F32), jax.ShapeDtypeStruct((t // tm, SUBLANES, LANES), F32)],
        compiler_params=_cparams(1),
        name="merge",
    )(att, hf, hb, po, pm, x_all, mod, g_mh, l1g, l1b, wa, wm, wo, wr, br)


def _moe_kernel(cnt_ref, layer_ref, hm_ref, plan_ref, plant_ref, x1_ref, mod_ref, l2g_ref, l2b_ref, wg_ref, wu_ref, wd_ref,
                o_ref, *, tm, tp, slot_sizes):
    i = pl.program_id(0)
    grp = pl.program_id(1)
    n_sub = tm // tp

    @pl.when(grp == 0)
    def _():
        o_ref[...] = jnp.zeros_like(o_ref)

    n_tok = cnt_ref[i * n_sub, grp]
    for j in range(1, n_sub):
        n_tok = jnp.maximum(n_tok, cnt_ref[i * n_sub + j, grp])

    def rounds_of(slots):
        return functools.partial(_moe_round, hm_ref, plan_ref, plant_ref, wg_ref, wu_ref, wd_ref, o_ref,
                                 grp=grp, n_sub=n_sub, tp=tp, slots=slots)

    big = slot_sizes[-1]
    tails = slot_sizes[:-1]
    n_full = lax.div(n_tok, big) + (lax.rem(n_tok, big) > (tails[-1] if tails else 0)).astype(jnp.int32)
    rest = jnp.maximum(n_tok - n_full * big, 0)
    full_round = rounds_of(big)
    lax.fori_loop(0, n_full, lambda r, carry: full_round(r * big, carry), 0)
    lower = 0
    for slots in tails:
        @pl.when((rest > lower) & (rest <= slots))
        def _(tail_round=rounds_of(slots)):
            tail_round(n_full * big, 0)

        lower = slots

    @pl.when(grp == N_GROUPS - 1)
    def _():
        is_ctx = _is_ctx_rows(tm)
        z = ALPHA * x1_ref[...] + _mod(mod_ref, 5, is_ctx) * o_ref[...]
        o_ref[...] = _ln(z) * l2g_ref[...] + l2b_ref[...]


def _moe_round(hm_ref, plan_ref, plant_ref, wg_ref, wu_ref, wd_ref, o_ref, first, carry, *, grp, n_sub, tp, slots):
    grp_f = grp.astype(F32)
    base = first.astype(F32)
    slot_rows = lax.broadcasted_iota(jnp.int32, (slots, tp), 0).astype(F32)
    slot_cols = lax.broadcasted_iota(jnp.int32, (tp, slots), 1).astype(F32)
    lane = lax.broadcasted_iota(jnp.int32, (n_sub * slots, LANES), 1)
    xs, cs = [], []
    for j in range(n_sub):
        rows = slice(j * tp, (j + 1) * tp)
        in_grp = plant_ref[PLAN_GROUP:PLAN_GROUP + 1, rows] == grp_f
        rank = plant_ref[PLAN_RANK:PLAN_RANK + 1, rows]
        gather = ((slot_rows == rank - base) & in_grp).astype(BF16)
        xs.append(jnp.dot(gather, hm_ref[rows, :], preferred_element_type=F32).astype(BF16))
        cs.append(sum(jnp.dot(gather, p, preferred_element_type=F32)
                      for p in _split3(plan_ref[rows, :])))
    xs = jnp.concatenate(xs, axis=0)
    cs = jnp.concatenate(cs, axis=0)
    hid = []
    for g in range(EPG):
        c_e = jnp.sum(jnp.where(lane == N_GROUPS + grp * EPG + g, cs, 0.0), axis=-1, keepdims=True)
        a = jnp.dot(xs, wg_ref[g], preferred_element_type=F32)
        u = jnp.dot(xs, wu_ref[g], preferred_element_type=F32)
        hid.append((a * _sigmoid(a) * u * c_e).astype(BF16))
    ys = jnp.dot(jnp.concatenate(hid, axis=1), wd_ref[...].reshape(EPG * D_EXP, D),
                 preferred_element_type=F32)
    ys = ys.astype(BF16)
    for j in range(n_sub):
        rows = slice(j * tp, (j + 1) * tp)
        mine = slice(j * slots, (j + 1) * slots)
        in_grp = plan_ref[rows, PLAN_GROUP:PLAN_GROUP + 1] == grp_f
        rank = plan_ref[rows, PLAN_RANK:PLAN_RANK + 1]
        scatter = ((slot_cols == rank - base) & in_grp).astype(BF16)
        o_ref[rows, :] += jnp.dot(scatter, ys[mine], preferred_element_type=F32)
    return carry


def _moe(counts, hm, plan, plan_t, x1, mod, l2g, l2b, wg, wu, wd, layer, tp):
    t = hm.shape[0]
    tm = tp * (MOE_PLAN_TILES if (t // tp) % MOE_PLAN_TILES == 0 else 1)
    row = lambda n, **kw: pl.BlockSpec((tm, n), lambda i, e, cnt, lay: (i, 0), **kw)
    full = lambda a: pl.BlockSpec(a.shape, lambda i, e, cnt, lay: (0,) * a.ndim)
    group_of = lambda a: pl.BlockSpec((None, EPG) + a.shape[2:], lambda i, e, cnt, lay: (lay[0], e, 0, 0))
    return pl.pallas_call(
        functools.partial(_moe_kernel, tm=tm, tp=tp, slot_sizes=_moe_slot_sizes(tp)),
        grid_spec=pltpu.PrefetchScalarGridSpec(
            num_scalar_prefetch=2,
            grid=(t // tm, N_GROUPS),
            in_specs=[row(D), row(LANES), pl.BlockSpec((2, tm), lambda i, e, cnt, lay: (0, i)),
                      row(D, pipeline_mode=pl.Buffered(1)),
                      full(mod), full(l2g), full(l2b), group_of(wg), group_of(wu), group_of(wd)],
            out_specs=row(D)),
        out_shape=jax.ShapeDtypeStruct((t, D), F32),
        compiler_params=_cparams(2, VMEM_LIMIT_MOE),
        name="moe",
    )(counts, jnp.full((1,), layer, jnp.int32), hm, plan, plan_t, x1, mod, l2g, l2b, wg, wu, wd)


def _rot_cols(w):
    a1, a2, b1, b2 = jnp.split(w, 4, axis=-1)
    return jnp.concatenate([-a2, a1, -b2, b1], axis=-1)


def _place_rope(w):
    return jnp.pad(w, ((0, 0), (QK_NOPE, HP - QK_NOPE - QK_ROPE)))


def _rope_tables(t):
    rows = (t - CTX) // GRID_W
    row, col = jnp.meshgrid(jnp.arange(rows, dtype=F32), jnp.arange(GRID_W, dtype=F32), indexing="ij")
    row, col = row.reshape(-1), col.reshape(-1)
    half = QK_ROPE // 2
    inv = ROPE_BASE ** (-jnp.arange(0, half, 2, dtype=F32) / half)
    ar, ac = row[:, None] * inv, col[:, None] * inv
    ang = jnp.concatenate([ar, ar, ac, ac], axis=-1)
    ang = jnp.concatenate([jnp.zeros((CTX, QK_ROPE), F32), ang], axis=0)
    pad = ((0, 0), (QK_NOPE, HP - QK_NOPE - QK_ROPE))
    cos = jnp.pad(jnp.cos(ang), pad) + jnp.pad(jnp.ones((t, QK_NOPE), F32), ((0, 0), (0, HP - QK_NOPE)))
    sin = jnp.pad(jnp.sin(ang), pad)
    return cos, sin, cos.T, sin.T


def _layer_weights(l, w_in, w_uq, w_uk, w_uv, w_bo_mla, w_bo_mlstm, w_out, w_rg, b_rg, w_re, b_re):
    wi = w_in[l]
    o = 0
    segs = []
    for n in (Q_LORA, KV_LORA, QK_ROPE, SEG_Q, SEG_K, SEG_V, SEG_PO, N_GATES, SEG_PM):
        segs.append(wi[:, o:o + n])
        o += n
    s_pqd, s_ckv, s_kr, s_q, s_k, s_v, s_po, s_pg, s_pm = segs
    w_in_p = jnp.concatenate(
        [s_pqd, s_ckv, _place_rope(s_kr), _place_rope(_rot_cols(s_kr)), s_q, s_k, s_po,
         jnp.pad(s_pg, ((0, 0), (0, SEG_PG - N_GATES))), s_pm], axis=1).astype(BF16)

    uq = w_uq[l].reshape(Q_LORA, H_MLA, QK_NOPE + QK_ROPE)
    padh = ((0, 0), (0, 0), (0, HP - QK_NOPE - QK_ROPE))
    wq = jnp.pad(uq, padh).reshape(Q_LORA, H_MLA * HP)
    wqr = jnp.pad(jnp.concatenate([jnp.zeros_like(uq[..., :QK_NOPE]), _rot_cols(uq[..., QK_NOPE:])], axis=-1),
                  padh).reshape(Q_LORA, H_MLA * HP)
    padv = ((0, 0), (0, 0), (0, HP - V_HEAD))
    wk = jnp.pad(w_uk[l].reshape(KV_LORA, H_MLA, QK_NOPE), padv).reshape(KV_LORA, H_MLA * HP)
    wv = jnp.pad(w_uv[l].reshape(KV_LORA, H_MLA, V_HEAD), ((0, 0), (0, 0), (0, VP - V_HEAD))).reshape(KV_LORA, H_MLA * VP)
    wa = jnp.pad(w_bo_mla[l].reshape(H_MLA, V_HEAD, D), ((0, 0), (0, HP - V_HEAD), (0, 0))).reshape(H_MLA * HP, D)

    wr = jnp.pad(jnp.concatenate([w_rg[l], w_re[l]], axis=1), ((0, 0), (0, LANES - N_GROUPS - N_EXP)))
    wr_hi = wr.astype(BF16)
    wr_lo = (wr - wr_hi.astype(F32)).astype(BF16)
    br = jnp.pad(jnp.concatenate([b_rg[l], b_re[l]]), (0, LANES - N_GROUPS - N_EXP)).reshape(1, LANES)
    return dict(
        w_in=w_in_p, w_mvt=s_v.T.astype(BF16), wqt=wq.T.astype(BF16), wqrt=wqr.T.astype(BF16), wk=wk.astype(BF16), wvt=wv.T.astype(BF16),
        wa=wa.astype(BF16), wm=w_bo_mlstm[l].astype(BF16), wo=w_out[l].astype(BF16),
        wr=jnp.stack([wr_hi, wr_lo]), br=br)


def kernel(x, c, ctx, c_ctx, w_ada, b_ada, w_in, b_gates, w_uq, w_uk, w_uv, g_qn, g_kvn, g_mh, w_bo_mla,
           w_bo_mlstm, w_out, ln1_g, ln1_b, w_rg, b_rg, w_re, b_re, w_e_gate, w_e_up, w_e_down, ln2_g, ln2_b):
    assert x.shape[0] == 1 and c.shape[0] == 1 and ctx.shape[1] == CTX
    x_all = jnp.concatenate([ctx[0], x[0]], axis=0)
    t = x_all.shape[0]
    tm = ROW_TILE if t % ROW_TILE == 0 else M_CHUNK
    tk = ATT_TK if t % ATT_TK == 0 else M_CHUNK
    assert t % tm == 0 and t % ATT_TQ == 0 and t % tk == 0 and t % M_CHUNK == 0

    cc = jnp.pad(jnp.concatenate([c, c_ctx[None]], axis=0), ((0, SUBLANES - 2), (0, 0)))
    mods = _ada(cc, w_ada, b_ada)
    tabs = _rope_tables(t)
    row2 = lambda a: a.reshape(1, -1)
    wg_all, wu_all, wd_all = w_e_gate.astype(BF16), w_e_up.astype(BF16), w_e_down.astype(BF16)

    for l in range(DEPTH):
        w = _layer_weights(l, w_in, w_uq, w_uk, w_uv, w_bo_mla, w_bo_mlstm, w_out, w_rg, b_rg, w_re, b_re)
        mod = mods[l]
        pqd, ckv, kr, mq, mk, po, pg, pm, mvt = _inproj(x_all, mod, w["w_in"], w["w_mvt"], tm)
        qt, kk, vt = _mla_prep(pqd, ckv, kr, tabs, row2(g_qn[l]), row2(g_kvn[l]),
                               w["wqt"], w["wqrt"], w["wk"], w["wvt"], tm)
        att = _attention(qt, kk, vt, ATT_TQ, tk, ATT_CB, ATT_DEPTH)
        pg_t = pg[:, :N_GATES].T
        bg_c = jnp.pad(b_gates[l], (0, LANES - N_GATES)).reshape(1, LANES)
        bg_r = b_gates[l].reshape(N_GATES, 1)
        hf, hb = _mlstm(mq, mk, mvt, pg, pg_t, bg_c, bg_r, M_CHUNK)
        x1, hm, plan, cnt = _merge(att, hf, hb, po, pm, x_all, mod, row2(g_mh[l]), row2(ln1_g[l]), row2(ln1_b[l]),
                                   w["wa"], w["wm"], w["wo"], w["wr"], w["br"], tm)
        counts = cnt[:, 0, :N_GROUPS].astype(jnp.int32)
        plan_t = plan[:, :2].T
        x_all = _moe(counts, hm, plan, plan_t, x1, mod, row2(ln2_g[l]), row2(ln2_b[l]),
                     wg_all, wu_all, wd_all, l, tm)
    return x_all[CTX:][None]
```

```python
import functools
import math

import jax
import jax.numpy as jnp
from jax import lax
from jax.experimental import pallas as pl
from jax.experimental.pallas import tpu as pltpu

F32 = jnp.float32
BF16 = jnp.bfloat16

D = 1024
DEPTH = 4
GRID_W = 64
CTX = 256
H_MLA = 8
QK_NOPE = 64
QK_ROPE = 32
V_HEAD = 64
Q_LORA = 384
KV_LORA = 256
ROPE_BASE = 10000.0
MLA_SCALE = (QK_NOPE + QK_ROPE) ** -0.5
H_M = 4
MQK = 64
MV = 128
N_GROUPS = 4
EPG = 8
N_EXP = N_GROUPS * EPG
D_EXP = 256
ALPHA = (2 * DEPTH) ** 0.25
LN_EPS = 1e-6

LANES = 128
SUBLANES = 8
NEG_BIG = -1e30
N_GATES = 4 * H_M
ADA_TN = 1536
VMEM_LIMIT = 56 * 1024 * 1024
VMEM_LIMIT_MOE = 58 * 1024 * 1024

HP = LANES
ONES_ROW = V_HEAD
BF16_ROWS = 16
VP = (V_HEAD + 1 + BF16_ROWS - 1) // BF16_ROWS * BF16_ROWS
QSCALE = MLA_SCALE * math.log2(math.e)

SEG_PQD = Q_LORA
SEG_CKV = KV_LORA
SEG_KR = 2 * HP
SEG_Q = H_M * MQK
SEG_K = H_M * MQK
SEG_V = H_M * MV
SEG_PO = H_M * MV
SEG_PG = LANES
SEG_PM = 2 * D
IN_SEGS = (SEG_PQD, SEG_CKV, SEG_KR, SEG_Q, SEG_K, SEG_PO, SEG_PG, SEG_PM)
IN_DTYPES = (F32, F32, F32, BF16, BF16, BF16, F32, BF16)
IN_GATES = (False, False, False, False, False, True, False, True)
NP_IN = sum(IN_SEGS)

ROW_TILE = 640
ATT_TQ = 3328
ATT_CB = 256
ATT_TK = 1280
ATT_KT = 256
MAX_LAG_JUMP = 64.0
ATT_DEPTH = 2
M_CHUNK = 256
MOE_SLOT_STEP = 64
MOE_SLOT_MAX = 256
MOE_PLAN_TILES = 2
PLAN_GROUP = 0
PLAN_RANK = 1


def _moe_slot_sizes(tm):
    return tuple(s for s in range(MOE_SLOT_STEP, MOE_SLOT_MAX + 1, MOE_SLOT_STEP) if s <= tm)

_NT = (((1,), (1,)), ((), ()))


def _cparams(n_grid, vmem_limit=VMEM_LIMIT):
    return pltpu.CompilerParams(dimension_semantics=("arbitrary",) * n_grid, vmem_limit_bytes=vmem_limit)


def _ln(x):
    mu = jnp.mean(x, axis=-1, keepdims=True)
    xc = x - mu
    var = jnp.mean(xc * xc, axis=-1, keepdims=True)
    return xc * lax.rsqrt(var + LN_EPS)


def _rms(x, g):
    return x * lax.rsqrt(jnp.mean(x * x, axis=-1, keepdims=True) + LN_EPS) * g


def _sigmoid(x):
    return 1.0 / (1.0 + jnp.exp(-x))


def _is_ctx_rows(tm):
    rows = pl.program_id(0) * tm + lax.broadcasted_iota(jnp.int32, (tm, 1), 0)
    return rows < CTX


def _mod(mod_ref, k, is_ctx):
    lat = mod_ref[0:1, k * D:(k + 1) * D]
    ctx = mod_ref[1:2, k * D:(k + 1) * D]
    return jnp.where(is_ctx, ctx, lat)


def _ada_kernel(c_ref, w_ref, b_ref, o_ref):
    c = c_ref[...]
    s = c * _sigmoid(c)
    o_ref[...] = jnp.dot(s, w_ref[...], preferred_element_type=F32,
                         precision=lax.Precision.HIGHEST) + b_ref[...]


def _ada(cc, w_ada, b_ada):
    tn = ADA_TN
    n = 6 * D
    return pl.pallas_call(
        _ada_kernel,
        grid=(DEPTH, n // tn),
        in_specs=[pl.BlockSpec((SUBLANES, D), lambda l, j: (0, 0)),
                  pl.BlockSpec((None, D, tn), lambda l, j: (l, 0, j)),
                  pl.BlockSpec((None, 1, tn), lambda l, j: (l, 0, j))],
        out_specs=pl.BlockSpec((None, SUBLANES, tn), lambda l, j: (l, 0, j)),
        out_shape=jax.ShapeDtypeStruct((DEPTH, SUBLANES, n), F32),
        compiler_params=_cparams(2),
        name="ada",
    )(cc, w_ada, b_ada.reshape(DEPTH, 1, n))


def _inproj_kernel(x_ref, mod_ref, w_ref, wvt_ref, *o_refs, tm):
    is_ctx = _is_ctx_rows(tm)
    h = _ln(x_ref[...]) * (1.0 + _mod(mod_ref, 1, is_ctx)) + _mod(mod_ref, 0, is_ctx)
    hb = h.astype(BF16)
    off = 0
    for o, is_gate in zip(o_refs[:-1], IN_GATES):
        n = o.shape[-1]
        p = jnp.dot(hb, w_ref[:, off:off + n], preferred_element_type=F32)
        o[...] = (_sigmoid(p) if is_gate else p).astype(o.dtype)
        off += n
    o_refs[-1][...] = lax.dot_general(wvt_ref[...], hb, _NT, preferred_element_type=F32).astype(BF16)


def _inproj(x_all, mod, w_in_p, w_vt, tm):
    t = x_all.shape[0]
    return pl.pallas_call(
        functools.partial(_inproj_kernel, tm=tm),
        grid=(t // tm,),
        in_specs=[pl.BlockSpec((tm, D), lambda i: (i, 0)),
                  pl.BlockSpec((SUBLANES, 6 * D), lambda i: (0, 0)),
                  pl.BlockSpec((D, NP_IN), lambda i: (0, 0)),
                  pl.BlockSpec((SEG_V, D), lambda i: (0, 0))],
        out_specs=[pl.BlockSpec((tm, n), lambda i: (i, 0)) for n in IN_SEGS]
        + [pl.BlockSpec((SEG_V, tm), lambda i: (0, i))],
        out_shape=[jax.ShapeDtypeStruct((t, n), dt) for n, dt in zip(IN_SEGS, IN_DTYPES)]
        + [jax.ShapeDtypeStruct((SEG_V, t), BF16)],
        compiler_params=_cparams(1),
        name="inproj",
    )(x_all, mod, w_in_p, w_vt)


def _mla_prep_kernel(pqd_ref, ckv_ref, kr_ref, cos_ref, sin_ref, cost_ref, sint_ref, gq_ref, gk_ref,
                     wqt_ref, wqrt_ref, wk_ref, wvt_ref, qt_ref, k_ref, vt_ref, *, tm):
    qn = _rms(pqd_ref[...], gq_ref[...]).astype(BF16)
    qa = lax.dot_general(wqt_ref[...], qn, _NT, preferred_element_type=F32)
    qb = lax.dot_general(wqrt_ref[...], qn, _NT, preferred_element_type=F32)
    cost = cost_ref[...]
    sint = sint_ref[...]
    ckv = _rms(ckv_ref[...], gk_ref[...]).astype(BF16)
    kn = jnp.dot(ckv, wk_ref[...], preferred_element_type=F32)
    kr = kr_ref[:, :HP] * cos_ref[...] + kr_ref[:, HP:] * sin_ref[...]
    vt = lax.dot_general(wvt_ref[...], ckv, _NT, preferred_element_type=F32)
    is_ones_row = lax.broadcasted_iota(jnp.int32, (VP, tm), 0) == ONES_ROW
    for h in range(H_MLA):
        sl = slice(h * HP, (h + 1) * HP)
        qt_ref[h] = ((qa[sl] * cost + qb[sl] * sint) * QSCALE).astype(BF16)
        k_ref[h] = (kn[:, sl] + kr).astype(BF16)
        vt_ref[h] = jnp.where(is_ones_row, 1.0, vt[h * VP:(h + 1) * VP]).astype(BF16)


def _mla_prep(pqd, ckv, kr, tabs, g_qn, g_kvn, wqt, wqrt, wk, wvt, tm):
    t = pqd.shape[0]
    cos, sin, cost, sint = tabs
    row = lambda n: pl.BlockSpec((tm, n), lambda i: (i, 0))
    col = lambda n: pl.BlockSpec((n, tm), lambda i: (0, i))
    full = lambda a: pl.BlockSpec(a.shape, lambda i: (0,) * a.ndim)
    return pl.pallas_call(
        functools.partial(_mla_prep_kernel, tm=tm),
        grid=(t // tm,),
        in_specs=[row(Q_LORA), row(KV_LORA), row(SEG_KR), row(HP), row(HP), col(HP), col(HP),
                  full(g_qn), full(g_kvn), full(wqt), full(wqrt), full(wk), full(wvt)],
        out_specs=[pl.BlockSpec((H_MLA, HP, tm), lambda i: (0, 0, i)),
                   pl.BlockSpec((H_MLA, tm, HP), lambda i: (0, i, 0)),
                   pl.BlockSpec((H_MLA, VP, tm), lambda i: (0, 0, i))],
        out_shape=[jax.ShapeDtypeStruct((H_MLA, HP, t), BF16),
                   jax.ShapeDtypeStruct((H_MLA, t, HP), BF16),
                   jax.ShapeDtypeStruct((H_MLA, VP, t), BF16)],
        compiler_params=_cparams(1),
        name="mla_prep",
    )(pqd, ckv, kr, cos, sin, cost, sint, g_qn, g_kvn, wqt, wqrt, wk, wvt)


def _scores(k, qt, m):
    s = jnp.dot(k, qt, preferred_element_type=F32)
    return s, jnp.maximum(m, jnp.max(s, axis=0, keepdims=True))


def _accumulate(s, vt, m, m_new, acc):
    p = jnp.exp2(s - m_new).astype(BF16)
    return jnp.exp2(m - m_new) * acc + jnp.dot(vt, p, preferred_element_type=F32)


def _normalized_rows(acc):
    o = acc / acc[ONES_ROW:ONES_ROW + 1, :]
    return jnp.concatenate([o, jnp.zeros((HP - VP, o.shape[1]), F32)], axis=0).T


def _lagged_pass(qt_ref, k_ref, vt_ref, m_ref, acc_ref, p_ref, sc_ref, *, ncb, n_kv, tk, cb, depth):
    k0 = k_ref[0:BF16_ROWS, :]
    for c in range(ncb):
        s0 = jnp.dot(k0, qt_ref[:, c * cb:(c + 1) * cb], preferred_element_type=F32)
        m_ref[c] = jnp.max(s0, axis=0, keepdims=True)
    acc_ref[...] = jnp.zeros(acc_ref.shape, F32)
    p_ref[...] = jnp.zeros(p_ref.shape, BF16)
    sc_ref[...] = jnp.ones(sc_ref.shape, F32)

    def carried(vt_prev):
        return [(ncb - depth + d, p_ref[d], sc_ref[d], vt_prev) for d in range(depth)]

    def drain(entry):
        pc, pp, psc, pvt = entry
        acc_ref[pc] = (acc_ref[pc] + jnp.dot(pvt, pp, preferred_element_type=F32)) * psc

    def step(j, jump):
        off = pl.multiple_of(j * tk, tk)
        off_prev = pl.multiple_of(jnp.maximum(j - 1, 0) * tk, tk)
        k = k_ref[pl.ds(off, tk), :]
        vt = vt_ref[:, pl.ds(off, tk)]
        pending = carried(vt_ref[:, pl.ds(off_prev, tk)])
        for c in range(ncb):
            m = m_ref[c]
            qt = qt_ref[:, c * cb:(c + 1) * cb]
            pc, pp, psc, pvt = pending.pop(0)
            s_parts, add = [], None
            for r in range(tk // ATT_KT):
                rows = slice(r * ATT_KT, (r + 1) * ATT_KT)
                s_parts.append(jnp.dot(k[rows], qt, preferred_element_type=F32))
                part = jnp.dot(pvt[:, rows], pp[rows], preferred_element_type=F32)
                add = part if add is None else add + part
            acc_ref[pc] = (acc_ref[pc] + add) * psc
            s = jnp.concatenate(s_parts, axis=0)
            p = jnp.exp2(s - m).astype(BF16)
            cm = jnp.max(s, axis=0, keepdims=True)
            m_new = jnp.maximum(m, cm)
            jump = jnp.maximum(jump, cm - m)
            m_ref[c] = m_new
            pending.append((c, p, jnp.exp2(m - m_new), vt))
        for d, (_, pp, psc, _) in enumerate(pending):
            p_ref[d] = pp
            sc_ref[d] = psc
        return jump

    jump = lax.fori_loop(0, n_kv, step, jnp.zeros((1, cb), F32))
    for entry in carried(vt_ref[:, (n_kv - 1) * tk:]):
        drain(entry)
    return jump


def _exact_pass(qt_ref, k_ref, vt_ref, m_ref, acc_ref, s_ref, mp_ref, *, ncb, n_kv, tk, cb, depth):
    m_ref[...] = jnp.full(m_ref.shape, NEG_BIG, F32)
    acc_ref[...] = jnp.zeros(acc_ref.shape, F32)
    s_ref[...] = jnp.full(s_ref.shape, -jnp.inf, F32)
    mp_ref[...] = jnp.full(mp_ref.shape, NEG_BIG, F32)

    def carried(vt_prev):
        return [(ncb - depth + d, s_ref[d], mp_ref[d, 0], mp_ref[d, 1], vt_prev) for d in range(depth)]

    def drain(entry):
        pc, ps, pm_old, pm_new, pvt = entry
        acc_ref[pc] = _accumulate(ps, pvt, pm_old, pm_new, acc_ref[pc])

    def step(j, carry):
        off = pl.multiple_of(j * tk, tk)
        off_prev = pl.multiple_of(jnp.maximum(j - 1, 0) * tk, tk)
        k = k_ref[pl.ds(off, tk), :]
        vt = vt_ref[:, pl.ds(off, tk)]
        pending = carried(vt_ref[:, pl.ds(off_prev, tk)])
        for c in range(ncb):
            m_old = m_ref[c]
            s, m_new = _scores(k, qt_ref[:, c * cb:(c + 1) * cb], m_old)
            m_ref[c] = m_new
            drain(pending.pop(0))
            pending.append((c, s, m_old, m_new, vt))
        for d, (_, ps, pm_old, pm_new, _) in enumerate(pending):
            s_ref[d] = ps
            mp_ref[d, 0] = pm_old
            mp_ref[d, 1] = pm_new
        return carry

    lax.fori_loop(0, n_kv, step, 0)
    for entry in carried(vt_ref[:, (n_kv - 1) * tk:]):
        drain(entry)


def _attn_kernel(qt_ref, k_ref, vt_ref, o_ref, m_ref, acc_ref, s_ref, mp_ref, p_ref, sc_ref, *,
                 tq, tk, t, cb, depth):
    geom = dict(ncb=tq // cb, n_kv=t // tk, tk=tk, cb=cb, depth=depth)

    def write_out():
        for c in range(tq // cb):
            o_ref[c * cb:(c + 1) * cb, :] = _normalized_rows(acc_ref[c]).astype(o_ref.dtype)

    jump = _lagged_pass(qt_ref, k_ref, vt_ref, m_ref, acc_ref, p_ref, sc_ref, **geom)
    write_out()

    @pl.when(jnp.max(jump) > MAX_LAG_JUMP)
    def _():
        _exact_pass(qt_ref, k_ref, vt_ref, m_ref, acc_ref, s_ref, mp_ref, **geom)
        write_out()

    @pl.when(pl.program_id(1) == 0)
    def _():
        m0 = jnp.full((1, CTX), NEG_BIG, F32)
        s, m_new = _scores(k_ref[0:CTX, :], qt_ref[:, 0:CTX], m0)
        acc_c = _accumulate(s, vt_ref[:, 0:CTX], m0, m_new, jnp.zeros((VP, CTX), F32))
        o_ref[0:CTX, :] = _normalized_rows(acc_c).astype(o_ref.dtype)


def _attention(qt, k, vt, tq, tk, cb, depth):
    t = k.shape[1]
    assert depth <= tq // cb
    return pl.pallas_call(
        functools.partial(_attn_kernel, tq=tq, tk=tk, t=t, cb=cb, depth=depth),
        grid=(H_MLA, t // tq),
        in_specs=[pl.BlockSpec((None, HP, tq), lambda h, i: (h, 0, i)),
                  pl.BlockSpec((None, t, HP), lambda h, i: (h, 0, 0)),
                  pl.BlockSpec((None, VP, t), lambda h, i: (h, 0, 0))],
        out_specs=pl.BlockSpec((tq, HP), lambda h, i: (i, h)),
        out_shape=jax.ShapeDtypeStruct((t, H_MLA * HP), BF16),
        scratch_shapes=[pltpu.VMEM((tq // cb, 1, cb), F32), pltpu.VMEM((tq // cb, VP, cb), F32),
                        pltpu.VMEM((depth, tk, cb), F32), pltpu.VMEM((depth, 2, 1, cb), F32),
                        pltpu.VMEM((depth, tk, cb), BF16), pltpu.VMEM((depth, 1, cb), F32)],
        compiler_params=_cparams(2),
        name="attention",
    )(qt, k, vt)


def _split3(x):
    hi = x.astype(BF16)
    r = x - hi.astype(F32)
    mid = r.astype(BF16)
    lo = (r - mid.astype(F32)).astype(BF16)
    return hi, mid, lo


def _log_sigmoid(x):
    return jnp.minimum(x, 0.0) - jnp.log(1.0 + jnp.exp(-jnp.abs(x)))


def _mlstm_kernel(qf_ref, kf_ref, vtf_ref, gcf_ref, grf_ref, qb_ref, kb_ref, vtb_ref, gcb_ref, grb_ref,
                  bc_ref, br_ref, of_ref, ob_ref, ct_scr, m_scr, *, L):
    @pl.when(pl.program_id(0) == 0)
    def _():
        ct_scr[...] = jnp.zeros_like(ct_scr)
        m_scr[...] = jnp.zeros_like(m_scr)

    row = lax.broadcasted_iota(jnp.int32, (L, L), 0)
    col = lax.broadcasted_iota(jnp.int32, (L, L), 1)
    ones_rows = (lax.broadcasted_iota(jnp.int32, (MV, L), 0) == 0).astype(BF16)
    scans = ((0, qf_ref, kf_ref, vtf_ref, gcf_ref, grf_ref, of_ref),
             (1, qb_ref, kb_ref, vtb_ref, gcb_ref, grb_ref, ob_ref))

    pairs = []
    for rev, q_ref, k_ref, vt_ref, gc_ref, gr_ref, o_ref in scans:
        before = (row >= col) if rev else (row <= col)
        before_b = before.astype(BF16)
        after_b = ((row <= col) if rev else (row >= col)).astype(BF16)
        gc = gc_ref[...] + bc_ref[...]
        gr = gr_ref[...] + br_ref[...]
        lfc = _log_sigmoid(gc)
        lfr = _log_sigmoid(gr)
        b_col = sum(jnp.dot(after_b, p, preferred_element_type=F32) for p in _split3(lfc))
        b_row = sum(jnp.dot(p, before_b, preferred_element_type=F32) for p in _split3(lfr))
        for h in range(H_M):
            ci = 2 * H_M * rev + h
            cf = ci + H_M
            q = q_ref[:, h * MQK:(h + 1) * MQK] * (MQK ** -0.5)
            k = k_ref[:, h * MQK:(h + 1) * MQK]
            ct = ct_scr[rev, h]
            pairs.append(dict(
                rev=rev, h=h, before=before, o_ref=o_ref, k=k, ct=ct,
                g_c=gc[:, ci:ci + 1] - b_col[:, cf:cf + 1],
                b_r=b_row[cf:cf + 1, :], i_r=gr[ci:ci + 1, :],
                bl=jnp.sum(lfr[cf:cf + 1, :], axis=1, keepdims=True),
                m=m_scr[rev, h][0:1, 0:1],
                vt_aug=jnp.concatenate([vt_ref[h * MV:(h + 1) * MV, :], ones_rows], axis=0),
                kq=lax.dot_general(k, q, _NT, preferred_element_type=F32),
                cq=lax.dot_general(ct.astype(BF16), q, _NT, preferred_element_type=F32)))

    for p in pairs:
        pm = jnp.max(jnp.where(p["before"], p["g_c"], -jnp.inf), axis=0, keepdims=True)
        inter = p["b_r"] + p["m"]
        p["mj"] = jnp.maximum(inter, p["b_r"] + pm)
        w = jnp.exp(jnp.where(p["before"], p["g_c"] + (p["b_r"] - p["mj"]), -jnp.inf))
        p["sc"] = jnp.exp(inter - p["mj"])
        p["kqw"] = (p["kq"] * w).astype(BF16)

    for p in pairs:
        h = p["h"]
        num = jnp.dot(p["vt_aug"], p["kqw"], preferred_element_type=F32) + p["sc"] * p["cq"]
        den = jnp.maximum(jnp.abs(num[MV:MV + 1, :]), jnp.exp(-p["mj"]))
        p["o_ref"][:, h * MV:(h + 1) * MV] = (num[:MV, :] / den).T

    for p in pairs:
        wlog = p["bl"] - p["b_r"] + p["i_r"]
        m_new = jnp.maximum(p["bl"] + p["m"], jnp.max(wlog, axis=1, keepdims=True))
        ws = jnp.exp(wlog - m_new)
        sd = jnp.exp(p["bl"] + p["m"] - m_new)
        wvt = (p["vt_aug"].astype(F32) * ws).astype(BF16)
        ct_scr[p["rev"], p["h"]] = sd * p["ct"] + jnp.dot(wvt, p["k"], preferred_element_type=F32)
        m_scr[p["rev"], p["h"]] = jnp.broadcast_to(m_new, m_scr.shape[2:])


def _mlstm(mq, mk, mvt, pg, pg_t, bg_c, bg_r, L):
    t = mq.shape[0]
    nc = t // L
    fwd = lambda c: c
    bwd = lambda c: jnp.where(c == 0, 0, nc - c)

    def chunk_specs(idx):
        return [pl.BlockSpec((L, SEG_Q), lambda c: (idx(c), 0)),
                pl.BlockSpec((L, SEG_K), lambda c: (idx(c), 0)),
                pl.BlockSpec((SEG_V, L), lambda c: (0, idx(c))),
                pl.BlockSpec((L, LANES), lambda c: (idx(c), 0)),
                pl.BlockSpec((N_GATES, L), lambda c: (0, idx(c)))]

    chunk_args = (mq, mk, mvt, pg, pg_t)
    return pl.pallas_call(
        functools.partial(_mlstm_kernel, L=L),
        grid=(nc,),
        in_specs=chunk_specs(fwd) + chunk_specs(bwd)
        + [pl.BlockSpec((1, LANES), lambda c: (0, 0)), pl.BlockSpec((N_GATES, 1), lambda c: (0, 0))],
        out_specs=[pl.BlockSpec((L, H_M * MV), lambda c: (fwd(c), 0)),
                   pl.BlockSpec((L, H_M * MV), lambda c: (bwd(c), 0))],
        out_shape=[jax.ShapeDtypeStruct((t, H_M * MV), F32)] * 2,
        scratch_shapes=[pltpu.VMEM((2, H_M, 2 * MV, MQK), F32), pltpu.VMEM((2, H_M, SUBLANES, LANES), F32)],
        compiler_params=_cparams(1),
        name="mlstm",
    )(*chunk_args, *chunk_args, bg_c, bg_r)


def _router(logits):
    tm = logits.shape[0]
    lane = lax.broadcasted_iota(jnp.int32, (tm, LANES), 1)
    big = jnp.int32(LANES)
    neg = -jnp.inf
    is_g = lane < N_GROUPS
    lg = jnp.where(is_g, logits, neg)
    gmax = jnp.max(lg, axis=-1, keepdims=True)
    grp = jnp.min(jnp.where(lg == gmax, lane, big), axis=-1, keepdims=True)
    p_grp = 1.0 / jnp.sum(jnp.where(is_g, jnp.exp(lg - gmax), 0.0), axis=-1, keepdims=True)
    e_lo = N_GROUPS + grp * EPG
    in_grp = (lane >= e_lo) & (lane < e_lo + EPG)
    le = jnp.where(in_grp, logits, neg)
    l1 = jnp.max(le, axis=-1, keepdims=True)
    i1 = jnp.min(jnp.where(le == l1, lane, big), axis=-1, keepdims=True)
    le2 = jnp.where(lane == i1, neg, le)
    l2 = jnp.max(le2, axis=-1, keepdims=True)
    i2 = jnp.min(jnp.where(le2 == l2, lane, big), axis=-1, keepdims=True)
    r = jnp.exp(l2 - l1)
    w1 = p_grp / (1.0 + r)
    w2 = w1 * r
    return jnp.where(lane == i1, w1, 0.0) + jnp.where(lane == i2, w2, 0.0), grp


def _merge_kernel(att_ref, hf_ref, hb_ref, po_ref, pm_ref, x_ref, mod_ref, gmh_ref, l1g_ref, l1b_ref,
                  wa_ref, wm_ref, wo_ref, wr_ref, br_ref, x1_ref, hm_ref, plan_ref, cnt_ref, *, tm):
    is_ctx = _is_ctx_rows(tm)
    y_mla = jnp.dot(att_ref[...], wa_ref[...], preferred_element_type=F32)
    hs = hf_ref[...] + hb_ref[...]
    parts = []
    for h in range(H_M):
        sl = slice(h * MV, (h + 1) * MV)
        parts.append(_ln(hs[:, sl]))
    hn = jnp.concatenate(parts, axis=1) * gmh_ref[...] * po_ref[...].astype(F32)
    y_ml = jnp.dot(hn.astype(BF16), wm_ref[...], preferred_element_type=F32)
    g_a = pm_ref[:, :D].astype(F32)
    g_b = pm_ref[:, D:].astype(F32)
    y = jnp.dot((g_a * y_mla + g_b * y_ml).astype(BF16), wo_ref[...], preferred_element_type=F32)
    x1 = _ln(ALPHA * x_ref[...] + _mod(mod_ref, 2, is_ctx) * y) * l1g_ref[...] + l1b_ref[...]
    x1_ref[...] = x1
    hm = _ln(x1) * (1.0 + _mod(mod_ref, 4, is_ctx)) + _mod(mod_ref, 3, is_ctx)
    hm_ref[...] = hm.astype(BF16)
    h_hi = hm.astype(BF16)
    h_lo = (hm - h_hi.astype(F32)).astype(BF16)
    logits = (jnp.dot(h_hi, wr_ref[0], preferred_element_type=F32)
              + jnp.dot(h_hi, wr_ref[1], preferred_element_type=F32)
              + jnp.dot(h_lo, wr_ref[0], preferred_element_type=F32)) + br_ref[...]
    comb, grp = _router(logits)
    lane = lax.broadcasted_iota(jnp.int32, (tm, LANES), 1)
    in_group = lane == grp
    earlier = (lax.broadcasted_iota(jnp.int32, (tm, tm), 0)
               > lax.broadcasted_iota(jnp.int32, (tm, tm), 1)).astype(BF16)
    n_before = jnp.dot(earlier, in_group.astype(BF16), preferred_element_type=F32)
    rank = jnp.sum(jnp.where(in_group, n_before, 0.0), axis=-1, keepdims=True)
    plan_ref[...] = (comb + jnp.where(lane == PLAN_GROUP, grp.astype(F32), 0.0)
                     + jnp.where(lane == PLAN_RANK, rank, 0.0))
    cnt_ref[...] = jnp.broadcast_to(jnp.sum(in_group.astype(F32), axis=0, keepdims=True), cnt_ref.shape)


def _merge(att, hf, hb, po, pm, x_all, mod, g_mh, l1g, l1b, wa, wm, wo, wr, br, tm):
    t = x_all.shape[0]
    row = lambda n: pl.BlockSpec((tm, n), lambda i: (i, 0))
    full = lambda a: pl.BlockSpec(a.shape, lambda i: (0,) * a.ndim)
    return pl.pallas_call(
        functools.partial(_merge_kernel, tm=tm),
        grid=(t // tm,),
        in_specs=[row(H_MLA * HP), row(H_M * MV), row(H_M * MV), row(SEG_PO), row(SEG_PM), row(D),
                  full(mod), full(g_mh), full(l1g), full(l1b), full(wa), full(wm), full(wo), full(wr), full(br)],
        out_specs=[row(D), row(D), row(LANES), pl.BlockSpec((None, SUBLANES, LANES), lambda i: (i, 0, 0))],
        out_shape=[jax.ShapeDtypeStruct((t, D), F32), jax.ShapeDtypeStruct((t, D), BF16),
                   jax.ShapeDtypeStruct((t, LANES), F32), jax.ShapeDtypeStruct((t // tm, SUBLANES, LANES), F32)],
        compiler_params=_cparams(1),
        name="merge",
    )(att, hf, hb, po, pm, x_all, mod, g_mh, l1g, l1b, wa, wm, wo, wr, br)


def _moe_kernel(cnt_ref, layer_ref, hm_ref, plan_ref, plant_ref, x1_ref, mod_ref, l2g_ref, l2b_ref, wg_ref, wu_ref, wd_ref,
                o_ref, *, tm, tp, slot_sizes):
    i = pl.program_id(0)
    grp = pl.program_id(1)
    n_sub = tm // tp

    @pl.when(grp == 0)
    def _():
        o_ref[...] = jnp.zeros_like(o_ref)

    n_tok = cnt_ref[i * n_sub, grp]
    for j in range(1, n_sub):
        n_tok = jnp.maximum(n_tok, cnt_ref[i * n_sub + j, grp])

    def rounds_of(slots):
        return functools.partial(_moe_round, hm_ref, plan_ref, plant_ref, wg_ref, wu_ref, wd_ref, o_ref,
                                 grp=grp, n_sub=n_sub, tp=tp, slots=slots)

    big = slot_sizes[-1]
    tails = slot_sizes[:-1]
    n_full = lax.div(n_tok, big) + (lax.rem(n_tok, big) > (tails[-1] if tails else 0)).astype(jnp.int32)
    rest = jnp.maximum(n_tok - n_full * big, 0)
    full_round = rounds_of(big)
    lax.fori_loop(0, n_full, lambda r, carry: full_round(r * big, carry), 0)
    lower = 0
    for slots in tails:
        @pl.when((rest > lower) & (rest <= slots))
        def _(tail_round=rounds_of(slots)):
            tail_round(n_full * big, 0)

        lower = slots

    @pl.when(grp == N_GROUPS - 1)
    def _():
        is_ctx = _is_ctx_rows(tm)
        z = ALPHA * x1_ref[...] + _mod(mod_ref, 5, is_ctx) * o_ref[...]
        o_ref[...] = _ln(z) * l2g_ref[...] + l2b_ref[...]


def _moe_round(hm_ref, plan_ref, plant_ref, wg_ref, wu_ref, wd_ref, o_ref, first, carry, *, grp, n_sub, tp, slots):
    grp_f = grp.astype(F32)
    base = first.astype(F32)
    slot_rows = lax.broadcasted_iota(jnp.int32, (slots, tp), 0).astype(F32)
    slot_cols = lax.broadcasted_iota(jnp.int32, (tp, slots), 1).astype(F32)
    lane = lax.broadcasted_iota(jnp.int32, (n_sub * slots, LANES), 1)
    xs, cs = [], []
    for j in range(n_sub):
        rows = slice(j * tp, (j + 1) * tp)
        in_grp = plant_ref[PLAN_GROUP:PLAN_GROUP + 1, rows] == grp_f
        rank = plant_ref[PLAN_RANK:PLAN_RANK + 1, rows]
        gather = ((slot_rows == rank - base) & in_grp).astype(BF16)
        xs.append(jnp.dot(gather, hm_ref[rows, :], preferred_element_type=F32).astype(BF16))
        cs.append(sum(jnp.dot(gather, p, preferred_element_type=F32)
                      for p in _split3(plan_ref[rows, :])))
    xs = jnp.concatenate(xs, axis=0)
    cs = jnp.concatenate(cs, axis=0)
    hid = []
    for g in range(EPG):
        c_e = jnp.sum(jnp.where(lane == N_GROUPS + grp * EPG + g, cs, 0.0), axis=-1, keepdims=True)
        a = jnp.dot(xs, wg_ref[g], preferred_element_type=F32)
        u = jnp.dot(xs, wu_ref[g], preferred_element_type=F32)
        hid.append((a * _sigmoid(a) * u * c_e).astype(BF16))
    ys = jnp.dot(jnp.concatenate(hid, axis=1), wd_ref[...].reshape(EPG * D_EXP, D),
                 preferred_element_type=F32)
    ys = ys.astype(BF16)
    for j in range(n_sub):
        rows = slice(j * tp, (j + 1) * tp)
        mine = slice(j * slots, (j + 1) * slots)
        in_grp = plan_ref[rows, PLAN_GROUP:PLAN_GROUP + 1] == grp_f
        rank = plan_ref[rows, PLAN_RANK:PLAN_RANK + 1]
        scatter = ((slot_cols == rank - base) & in_grp).astype(BF16)
        o_ref[rows, :] += jnp.dot(scatter, ys[mine], preferred_element_type=F32)
    return carry


def _moe(counts, hm, plan, plan_t, x1, mod, l2g, l2b, wg, wu, wd, layer, tp):
    t = hm.shape[0]
    tm = tp * (MOE_PLAN_TILES if (t // tp) % MOE_PLAN_TILES == 0 else 1)
    row = lambda n, **kw: pl.BlockSpec((tm, n), lambda i, e, cnt, lay: (i, 0), **kw)
    full = lambda a: pl.BlockSpec(a.shape, lambda i, e, cnt, lay: (0,) * a.ndim)
    group_of = lambda a: pl.BlockSpec((None, EPG) + a.shape[2:], lambda i, e, cnt, lay: (lay[0], e, 0, 0))
    return pl.pallas_call(
        functools.partial(_moe_kernel, tm=tm, tp=tp, slot_sizes=_moe_slot_sizes(tp)),
        grid_spec=pltpu.PrefetchScalarGridSpec(
            num_scalar_prefetch=2,
            grid=(t // tm, N_GROUPS),
            in_specs=[row(D), row(LANES), pl.BlockSpec((2, tm), lambda i, e, cnt, lay: (0, i)),
                      row(D, pipeline_mode=pl.Buffered(1)),
                      full(mod), full(l2g), full(l2b), group_of(wg), group_of(wu), group_of(wd)],
            out_specs=row(D)),
        out_shape=jax.ShapeDtypeStruct((t, D), F32),
        compiler_params=_cparams(2, VMEM_LIMIT_MOE),
        name="moe",
    )(counts, jnp.full((1,), layer, jnp.int32), hm, plan, plan_t, x1, mod, l2g, l2b, wg, wu, wd)


def _rot_cols(w):
    a1, a2, b1, b2 = jnp.split(w, 4, axis=-1)
    return jnp.concatenate([-a2, a1, -b2, b1], axis=-1)


def _place_rope(w):
    return jnp.pad(w, ((0, 0), (QK_NOPE, HP - QK_NOPE - QK_ROPE)))


def _rope_tables(t):
    rows = (t - CTX) // GRID_W
    row, col = jnp.meshgrid(jnp.arange(rows, dtype=F32), jnp.arange(GRID_W, dtype=F32), indexing="ij")
    row, col = row.reshape(-1), col.reshape(-1)
    half = QK_ROPE // 2
    inv = ROPE_BASE ** (-jnp.arange(0, half, 2, dtype=F32) / half)
    ar, ac = row[:, None] * inv, col[:, None] * inv
    ang = jnp.concatenate([ar, ar, ac, ac], axis=-1)
    ang = jnp.concatenate([jnp.zeros((CTX, QK_ROPE), F32), ang], axis=0)
    pad = ((0, 0), (QK_NOPE, HP - QK_NOPE - QK_ROPE))
    cos = jnp.pad(jnp.cos(ang), pad) + jnp.pad(jnp.ones((t, QK_NOPE), F32), ((0, 0), (0, HP - QK_NOPE)))
    sin = jnp.pad(jnp.sin(ang), pad)
    return cos, sin, cos.T, sin.T


def _layer_weights(l, w_in, w_uq, w_uk, w_uv, w_bo_mla, w_bo_mlstm, w_out, w_rg, b_rg, w_re, b_re):
    wi = w_in[l]
    o = 0
    segs = []
    for n in (Q_LORA, KV_LORA, QK_ROPE, SEG_Q, SEG_K, SEG_V, SEG_PO, N_GATES, SEG_PM):
        segs.append(wi[:, o:o + n])
        o += n
    s_pqd, s_ckv, s_kr, s_q, s_k, s_v, s_po, s_pg, s_pm = segs
    w_in_p = jnp.concatenate(
        [s_pqd, s_ckv, _place_rope(s_kr), _place_rope(_rot_cols(s_kr)), s_q, s_k, s_po,
         jnp.pad(s_pg, ((0, 0), (0, SEG_PG - N_GATES))), s_pm], axis=1).astype(BF16)

    uq = w_uq[l].reshape(Q_LORA, H_MLA, QK_NOPE + QK_ROPE)
    padh = ((0, 0), (0, 0), (0, HP - QK_NOPE - QK_ROPE))
    wq = jnp.pad(uq, padh).reshape(Q_LORA, H_MLA * HP)
    wqr = jnp.pad(jnp.concatenate([jnp.zeros_like(uq[..., :QK_NOPE]), _rot_cols(uq[..., QK_NOPE:])], axis=-1),
                  padh).reshape(Q_LORA, H_MLA * HP)
    padv = ((0, 0), (0, 0), (0, HP - V_HEAD))
    wk = jnp.pad(w_uk[l].reshape(KV_LORA, H_MLA, QK_NOPE), padv).reshape(KV_LORA, H_MLA * HP)
    wv = jnp.pad(w_uv[l].reshape(KV_LORA, H_MLA, V_HEAD), ((0, 0), (0, 0), (0, VP - V_HEAD))).reshape(KV_LORA, H_MLA * VP)
    wa = jnp.pad(w_bo_mla[l].reshape(H_MLA, V_HEAD, D), ((0, 0), (0, HP - V_HEAD), (0, 0))).reshape(H_MLA * HP, D)

    wr = jnp.pad(jnp.concatenate([w_rg[l], w_re[l]], axis=1), ((0, 0), (0, LANES - N_GROUPS - N_EXP)))
    wr_hi = wr.astype(BF16)
    wr_lo = (wr - wr_hi.astype(F32)).astype(BF16)
    br = jnp.pad(jnp.concatenate([b_rg[l], b_re[l]]), (0, LANES - N_GROUPS - N_EXP)).reshape(1, LANES)
    return dict(
        w_in=w_in_p, w_mvt=s_v.T.astype(BF16), wqt=wq.T.astype(BF16), wqrt=wqr.T.astype(BF16), wk=wk.astype(BF16), wvt=wv.T.astype(BF16),
        wa=wa.astype(BF16), wm=w_bo_mlstm[l].astype(BF16), wo=w_out[l].astype(BF16),
        wr=jnp.stack([wr_hi, wr_lo]), br=br)


def kernel(x, c, ctx, c_ctx, w_ada, b_ada, w_in, b_gates, w_uq, w_uk, w_uv, g_qn, g_kvn, g_mh, w_bo_mla,
           w_bo_mlstm, w_out, ln1_g, ln1_b, w_rg, b_rg, w_re, b_re, w_e_gate, w_e_up, w_e_down, ln2_g, ln2_b):
    assert x.shape[0] == 1 and c.shape[0] == 1 and ctx.shape[1] == CTX
    x_all = jnp.concatenate([ctx[0], x[0]], axis=0)
    t = x_all.shape[0]
    tm = ROW_TILE if t % ROW_TILE == 0 else M_CHUNK
    tk = ATT_TK if t % ATT_TK == 0 else M_CHUNK
    assert t % tm == 0 and t % ATT_TQ == 0 and t % tk == 0 and t % M_CHUNK == 0

    cc = jnp.pad(jnp.concatenate([c, c_ctx[None]], axis=0), ((0, SUBLANES - 2), (0, 0)))
    mods = _ada(cc, w_ada, b_ada)
    tabs = _rope_tables(t)
    row2 = lambda a: a.reshape(1, -1)
    wg_all, wu_all, wd_all = w_e_gate.astype(BF16), w_e_up.astype(BF16), w_e_down.astype(BF16)

    for l in range(DEPTH):
        w = _layer_weights(l, w_in, w_uq, w_uk, w_uv, w_bo_mla, w_bo_mlstm, w_out, w_rg, b_rg, w_re, b_re)
        mod = mods[l]
        pqd, ckv, kr, mq, mk, po, pg, pm, mvt = _inproj(x_all, mod, w["w_in"], w["w_mvt"], tm)
        qt, kk, vt = _mla_prep(pqd, ckv, kr, tabs, row2(g_qn[l]), row2(g_kvn[l]),
                               w["wqt"], w["wqrt"], w["wk"], w["wvt"], tm)
        att = _attention(qt, kk, vt, ATT_TQ, tk, ATT_CB, ATT_DEPTH)
        pg_t = pg[:, :N_GATES].T
        bg_c = jnp.pad(b_gates[l], (0, LANES - N_GATES)).reshape(1, LANES)
        bg_r = b_gates[l].reshape(N_GATES, 1)
        hf, hb = _mlstm(mq, mk, mvt, pg, pg_t, bg_c, bg_r, M_CHUNK)
        x1, hm, plan, cnt = _merge(att, hf, hb, po, pm, x_all, mod, row2(g_mh[l]), row2(ln1_g[l]), row2(ln1_b[l]),
                                   w["wa"], w["wm"], w["wo"], w["wr"], w["br"], tm)
        counts = cnt[:, 0, :N_GROUPS].astype(jnp.int32)
        plan_t = plan[:, :2].T
        x_all = _moe(counts, hm, plan, plan_t, x1, mod, row2(ln2_g[l]), row2(ln2_b[l]),
                     wg_all, wu_all, wd_all, l, tm)
    return x_all[CTX:][None]
```

```python
import functools
import math

import jax
import jax.numpy as jnp
from jax import lax
from jax.experimental import pallas as pl
from jax.experimental.pallas import tpu as pltpu

F32 = jnp.float32
BF16 = jnp.bfloat16

D = 1024
DEPTH = 4
GRID_W = 64
CTX = 256
H_MLA = 8
QK_NOPE = 64
QK_ROPE = 32
V_HEAD = 64
Q_LORA = 384
KV_LORA = 256
ROPE_BASE = 10000.0
MLA_SCALE = (QK_NOPE + QK_ROPE) ** -0.5
H_M = 4
MQK = 64
MV = 128
N_GROUPS = 4
EPG = 8
N_EXP = N_GROUPS * EPG
D_EXP = 256
ALPHA = (2 * DEPTH) ** 0.25
LN_EPS = 1e-6

LANES = 128
SUBLANES = 8
NEG_BIG = -1e30
N_GATES = 4 * H_M
ADA_TN = 1536
VMEM_LIMIT = 56 * 1024 * 1024
VMEM_LIMIT_MOE = 58 * 1024 * 1024

HP = LANES
ONES_ROW = V_HEAD
BF16_ROWS = 16
VP = (V_HEAD + 1 + BF16_ROWS - 1) // BF16_ROWS * BF16_ROWS
QSCALE = MLA_SCALE * math.log2(math.e)

SEG_PQD = Q_LORA
SEG_CKV = KV_LORA
SEG_KR = 2 * HP
SEG_Q = H_M * MQK
SEG_K = H_M * MQK
SEG_V = H_M * MV
SEG_PO = H_M * MV
SEG_PG = LANES
SEG_PM = 2 * D
IN_SEGS = (SEG_PQD, SEG_CKV, SEG_KR, SEG_Q, SEG_K, SEG_PO, SEG_PG, SEG_PM)
IN_DTYPES = (F32, F32, F32, BF16, BF16, BF16, F32, BF16)
IN_GATES = (False, False, False, False, False, True, False, True)
NP_IN = sum(IN_SEGS)

ROW_TILE = 640
ATT_TQ = 3328
ATT_CB = 256
ATT_TK = 1280
ATT_KT = 256
MAX_LAG_JUMP = 64.0
ATT_DEPTH = 1
M_CHUNK = 256
MOE_SLOT_STEP = 64
MOE_SLOT_MAX = 256
MOE_PLAN_TILES = 2
PLAN_GROUP = 0
PLAN_RANK = 1


def _moe_slot_sizes(tm):
    return tuple(s for s in range(MOE_SLOT_STEP, MOE_SLOT_MAX + 1, MOE_SLOT_STEP) if s <= tm)

_NT = (((1,), (1,)), ((), ()))


def _cparams(n_grid, vmem_limit=VMEM_LIMIT):
    return pltpu.CompilerParams(dimension_semantics=("arbitrary",) * n_grid, vmem_limit_bytes=vmem_limit)


def _ln(x):
    mu = jnp.mean(x, axis=-1, keepdims=True)
    xc = x - mu
    var = jnp.mean(xc * xc, axis=-1, keepdims=True)
    return xc * lax.rsqrt(var + LN_EPS)


def _rms(x, g):
    return x * lax.rsqrt(jnp.mean(x * x, axis=-1, keepdims=True) + LN_EPS) * g


def _sigmoid(x):
    return 1.0 / (1.0 + jnp.exp(-x))


def _is_ctx_rows(tm):
    rows = pl.program_id(0) * tm + lax.broadcasted_iota(jnp.int32, (tm, 1), 0)
    return rows < CTX


def _mod(mod_ref, k, is_ctx):
    lat = mod_ref[0:1, k * D:(k + 1) * D]
    ctx = mod_ref[1:2, k * D:(k + 1) * D]
    return jnp.where(is_ctx, ctx, lat)


def _ada_kernel(c_ref, w_ref, b_ref, o_ref):
    c = c_ref[...]
    s = c * _sigmoid(c)
    o_ref[...] = jnp.dot(s, w_ref[...], preferred_element_type=F32,
                         precision=lax.Precision.HIGHEST) + b_ref[...]


def _ada(cc, w_ada, b_ada):
    tn = ADA_TN
    n = 6 * D
    return pl.pallas_call(
        _ada_kernel,
        grid=(DEPTH, n // tn),
        in_specs=[pl.BlockSpec((SUBLANES, D), lambda l, j: (0, 0)),
                  pl.BlockSpec((None, D, tn), lambda l, j: (l, 0, j)),
                  pl.BlockSpec((None, 1, tn), lambda l, j: (l, 0, j))],
        out_specs=pl.BlockSpec((None, SUBLANES, tn), lambda l, j: (l, 0, j)),
        out_shape=jax.ShapeDtypeStruct((DEPTH, SUBLANES, n), F32),
        compiler_params=_cparams(2),
        name="ada",
    )(cc, w_ada, b_ada.reshape(DEPTH, 1, n))


def _inproj_kernel(x_ref, mod_ref, w_ref, wvt_ref, *o_refs, tm):
    is_ctx = _is_ctx_rows(tm)
    h = _ln(x_ref[...]) * (1.0 + _mod(mod_ref, 1, is_ctx)) + _mod(mod_ref, 0, is_ctx)
    hb = h.astype(BF16)
    off = 0
    for o, is_gate in zip(o_refs[:-1], IN_GATES):
        n = o.shape[-1]
        p = jnp.dot(hb, w_ref[:, off:off + n], preferred_element_type=F32)
        o[...] = (_sigmoid(p) if is_gate else p).astype(o.dtype)
        off += n
    o_refs[-1][...] = lax.dot_general(wvt_ref[...], hb, _NT, preferred_element_type=F32).astype(BF16)


def _inproj(x_all, mod, w_in_p, w_vt, tm):
    t = x_all.shape[0]
    return pl.pallas_call(
        functools.partial(_inproj_kernel, tm=tm),
        grid=(t // tm,),
        in_specs=[pl.BlockSpec((tm, D), lambda i: (i, 0)),
                  pl.BlockSpec((SUBLANES, 6 * D), lambda i: (0, 0)),
                  pl.BlockSpec((D, NP_IN), lambda i: (0, 0)),
                  pl.BlockSpec((SEG_V, D), lambda i: (0, 0))],
        out_specs=[pl.BlockSpec((tm, n), lambda i: (i, 0)) for n in IN_SEGS]
        + [pl.BlockSpec((SEG_V, tm), lambda i: (0, i))],
        out_shape=[jax.ShapeDtypeStruct((t, n), dt) for n, dt in zip(IN_SEGS, IN_DTYPES)]
        + [jax.ShapeDtypeStruct((SEG_V, t), BF16)],
        compiler_params=_cparams(1),
        name="inproj",
    )(x_all, mod, w_in_p, w_vt)


def _mla_prep_kernel(pqd_ref, ckv_ref, kr_ref, cos_ref, sin_ref, cost_ref, sint_ref, gq_ref, gk_ref,
                     wqt_ref, wqrt_ref, wk_ref, wvt_ref, qt_ref, k_ref, vt_ref, *, tm):
    qn = _rms(pqd_ref[...], gq_ref[...]).astype(BF16)
    qa = lax.dot_general(wqt_ref[...], qn, _NT, preferred_element_type=F32)
    qb = lax.dot_general(wqrt_ref[...], qn, _NT, preferred_element_type=F32)
    cost = cost_ref[...]
    sint = sint_ref[...]
    ckv = _rms(ckv_ref[...], gk_ref[...]).astype(BF16)
    kn = jnp.dot(ckv, wk_ref[...], preferred_element_type=F32)
    kr = kr_ref[:, :HP] * cos_ref[...] + kr_ref[:, HP:] * sin_ref[...]
    vt = lax.dot_general(wvt_ref[...], ckv, _NT, preferred_element_type=F32)
    is_ones_row = lax.broadcasted_iota(jnp.int32, (VP, tm), 0) == ONES_ROW
    for h in range(H_MLA):
        sl = slice(h * HP, (h + 1) * HP)
        qt_ref[h] = ((qa[sl] * cost + qb[sl] * sint) * QSCALE).astype(BF16)
        k_ref[h] = (kn[:, sl] + kr).astype(BF16)
        vt_ref[h] = jnp.where(is_ones_row, 1.0, vt[h * VP:(h + 1) * VP]).astype(BF16)


def _mla_prep(pqd, ckv, kr, tabs, g_qn, g_kvn, wqt, wqrt, wk, wvt, tm):
    t = pqd.shape[0]
    cos, sin, cost, sint = tabs
    row = lambda n: pl.BlockSpec((tm, n), lambda i: (i, 0))
    col = lambda n: pl.BlockSpec((n, tm), lambda i: (0, i))
    full = lambda a: pl.BlockSpec(a.shape, lambda i: (0,) * a.ndim)
    return pl.pallas_call(
        functools.partial(_mla_prep_kernel, tm=tm),
        grid=(t // tm,),
        in_specs=[row(Q_LORA), row(KV_LORA), row(SEG_KR), row(HP), row(HP), col(HP), col(HP),
                  full(g_qn), full(g_kvn), full(wqt), full(wqrt), full(wk), full(wvt)],
        out_specs=[pl.BlockSpec((H_MLA, HP, tm), lambda i: (0, 0, i)),
                   pl.BlockSpec((H_MLA, tm, HP), lambda i: (0, i, 0)),
                   pl.BlockSpec((H_MLA, VP, tm), lambda i: (0, 0, i))],
        out_shape=[jax.ShapeDtypeStruct((H_MLA, HP, t), BF16),
                   jax.ShapeDtypeStruct((H_MLA, t, HP), BF16),
                   jax.ShapeDtypeStruct((H_MLA, VP, t), BF16)],
        compiler_params=_cparams(1),
        name="mla_prep",
    )(pqd, ckv, kr, cos, sin, cost, sint, g_qn, g_kvn, wqt, wqrt, wk, wvt)


def _scores(k, qt, m):
    s = jnp.dot(k, qt, preferred_element_type=F32)
    return s, jnp.maximum(m, jnp.max(s, axis=0, keepdims=True))


def _accumulate(s, vt, m, m_new, acc):
    p = jnp.exp2(s - m_new).astype(BF16)
    return jnp.exp2(m - m_new) * acc + jnp.dot(vt, p, preferred_element_type=F32)


def _normalized_rows(acc):
    o = acc / acc[ONES_ROW:ONES_ROW + 1, :]
    return jnp.concatenate([o, jnp.zeros((HP - VP, o.shape[1]), F32)], axis=0).T


def _lagged_pass(qt_ref, k_ref, vt_ref, m_ref, acc_ref, p_ref, sc_ref, *, ncb, n_kv, tk, cb, depth):
    k0 = k_ref[0:BF16_ROWS, :]
    for c in range(ncb):
        s0 = jnp.dot(k0, qt_ref[:, c * cb:(c + 1) * cb], preferred_element_type=F32)
        m_ref[c] = jnp.max(s0, axis=0, keepdims=True)
    acc_ref[...] = jnp.zeros(acc_ref.shape, F32)
    p_ref[...] = jnp.zeros(p_ref.shape, BF16)
    sc_ref[...] = jnp.ones(sc_ref.shape, F32)

    def carried(vt_prev):
        return [(ncb - depth + d, p_ref[d], sc_ref[d], vt_prev) for d in range(depth)]

    def drain(entry):
        pc, pp, psc, pvt = entry
        acc_ref[pc] = (acc_ref[pc] + jnp.dot(pvt, pp, preferred_element_type=F32)) * psc

    def step(j, jump):
        off = pl.multiple_of(j * tk, tk)
        off_prev = pl.multiple_of(jnp.maximum(j - 1, 0) * tk, tk)
        k = k_ref[pl.ds(off, tk), :]
        vt = vt_ref[:, pl.ds(off, tk)]
        pending = carried(vt_ref[:, pl.ds(off_prev, tk)])
        for c in range(ncb):
            m = m_ref[c]
            qt = qt_ref[:, c * cb:(c + 1) * cb]
            pc, pp, psc, pvt = pending.pop(0)
            s_parts, add = [], None
            for r in range(tk // ATT_KT):
                rows = slice(r * ATT_KT, (r + 1) * ATT_KT)
                s_parts.append(jnp.dot(k[rows], qt, preferred_element_type=F32))
                part = jnp.dot(pvt[:, rows], pp[rows], preferred_element_type=F32)
                add = part if add is None else add + part
            acc_ref[pc] = (acc_ref[pc] + add) * psc
            s = jnp.concatenate(s_parts, axis=0)
            p = jnp.exp2(s - m).astype(BF16)
            cm = jnp.max(s, axis=0, keepdims=True)
            m_new = jnp.maximum(m, cm)
            jump = jnp.maximum(jump, cm - m)
            m_ref[c] = m_new
            pending.append((c, p, jnp.exp2(m - m_new), vt))
        for d, (_, pp, psc, _) in enumerate(pending):
            p_ref[d] = pp
            sc_ref[d] = psc
        return jump

    jump = lax.fori_loop(0, n_kv, step, jnp.zeros((1, cb), F32))
    for entry in carried(vt_ref[:, (n_kv - 1) * tk:]):
        drain(entry)
    return jump


def _exact_pass(qt_ref, k_ref, vt_ref, m_ref, acc_ref, s_ref, mp_ref, *, ncb, n_kv, tk, cb, depth):
    m_ref[...] = jnp.full(m_ref.shape, NEG_BIG, F32)
    acc_ref[...] = jnp.zeros(acc_ref.shape, F32)
    s_ref[...] = jnp.full(s_ref.shape, -jnp.inf, F32)
    mp_ref[...] = jnp.full(mp_ref.shape, NEG_BIG, F32)

    def carried(vt_prev):
        return [(ncb - depth + d, s_ref[d], mp_ref[d, 0], mp_ref[d, 1], vt_prev) for d in range(depth)]

    def drain(entry):
        pc, ps, pm_old, pm_new, pvt = entry
        acc_ref[pc] = _accumulate(ps, pvt, pm_old, pm_new, acc_ref[pc])

    def step(j, carry):
        off = pl.multiple_of(j * tk, tk)
        off_prev = pl.multiple_of(jnp.maximum(j - 1, 0) * tk, tk)
        k = k_ref[pl.ds(off, tk), :]
        vt = vt_ref[:, pl.ds(off, tk)]
        pending = carried(vt_ref[:, pl.ds(off_prev, tk)])
        for c in range(ncb):
            m_old = m_ref[c]
            s, m_new = _scores(k, qt_ref[:, c * cb:(c + 1) * cb], m_old)
            m_ref[c] = m_new
            drain(pending.pop(0))
            pending.append((c, s, m_old, m_new, vt))
        for d, (_, ps, pm_old, pm_new, _) in enumerate(pending):
            s_ref[d] = ps
            mp_ref[d, 0] = pm_old
            mp_ref[d, 1] = pm_new
        return carry

    lax.fori_loop(0, n_kv, step, 0)
    for entry in carried(vt_ref[:, (n_kv - 1) * tk:]):
        drain(entry)


def _attn_kernel(qt_ref, k_ref, vt_ref, o_ref, m_ref, acc_ref, s_ref, mp_ref, p_ref, sc_ref, *,
                 tq, tk, t, cb, depth):
    geom = dict(ncb=tq // cb, n_kv=t // tk, tk=tk, cb=cb, depth=depth)

    def write_out():
        for c in range(tq // cb):
            o_ref[c * cb:(c + 1) * cb, :] = _normalized_rows(acc_ref[c]).astype(o_ref.dtype)

    jump = _lagged_pass(qt_ref, k_ref, vt_ref, m_ref, acc_ref, p_ref, sc_ref, **geom)
    write_out()

    @pl.when(jnp.max(jump) > MAX_LAG_JUMP)
    def _():
        _exact_pass(qt_ref, k_ref, vt_ref, m_ref, acc_ref, s_ref, mp_ref, **geom)
        write_out()

    @pl.when(pl.program_id(1) == 0)
    def _():
        m0 = jnp.full((1, CTX), NEG_BIG, F32)
        s, m_new = _scores(k_ref[0:CTX, :], qt_ref[:, 0:CTX], m0)
        acc_c = _accumulate(s, vt_ref[:, 0:CTX], m0, m_new, jnp.zeros((VP, CTX), F32))
        o_ref[0:CTX, :] = _normalized_rows(acc_c).astype(o_ref.dtype)


def _attention(qt, k, vt, tq, tk, cb, depth):
    t = k.shape[1]
    assert depth <= tq // cb
    return pl.pallas_call(
        functools.partial(_attn_kernel, tq=tq, tk=tk, t=t, cb=cb, depth=depth),
        grid=(H_MLA, t // tq),
        in_specs=[pl.BlockSpec((None, HP, tq), lambda h, i: (h, 0, i)),
                  pl.BlockSpec((None, t, HP), lambda h, i: (h, 0, 0)),
                  pl.BlockSpec((None, VP, t), lambda h, i: (h, 0, 0))],
        out_specs=pl.BlockSpec((tq, HP), lambda h, i: (i, h)),
        out_shape=jax.ShapeDtypeStruct((t, H_MLA * HP), BF16),
        scratch_shapes=[pltpu.VMEM((tq // cb, 1, cb), F32), pltpu.VMEM((tq // cb, VP, cb), F32),
                        pltpu.VMEM((depth, tk, cb), F32), pltpu.VMEM((depth, 2, 1, cb), F32),
                        pltpu.VMEM((depth, tk, cb), BF16), pltpu.VMEM((depth, 1, cb), F32)],
        compiler_params=_cparams(2),
        name="attention",
    )(qt, k, vt)


def _split3(x):
    hi = x.astype(BF16)
    r = x - hi.astype(F32)
    mid = r.astype(BF16)
    lo = (r - mid.astype(F32)).astype(BF16)
    return hi, mid, lo


def _log_sigmoid(x):
    return jnp.minimum(x, 0.0) - jnp.log(1.0 + jnp.exp(-jnp.abs(x)))


def _mlstm_kernel(qf_ref, kf_ref, vtf_ref, gcf_ref, grf_ref, qb_ref, kb_ref, vtb_ref, gcb_ref, grb_ref,
                  bc_ref, br_ref, of_ref, ob_ref, ct_scr, m_scr, *, L):
    @pl.when(pl.program_id(0) == 0)
    def _():
        ct_scr[...] = jnp.zeros_like(ct_scr)
        m_scr[...] = jnp.zeros_like(m_scr)

    row = lax.broadcasted_iota(jnp.int32, (L, L), 0)
    col = lax.broadcasted_iota(jnp.int32, (L, L), 1)
    ones_rows = (lax.broadcasted_iota(jnp.int32, (MV, L), 0) == 0).astype(BF16)
    scans = ((0, qf_ref, kf_ref, vtf_ref, gcf_ref, grf_ref, of_ref),
             (1, qb_ref, kb_ref, vtb_ref, gcb_ref, grb_ref, ob_ref))

    pairs = []
    for rev, q_ref, k_ref, vt_ref, gc_ref, gr_ref, o_ref in scans:
        before = (row >= col) if rev else (row <= col)
        before_b = before.astype(BF16)
        after_b = ((row <= col) if rev else (row >= col)).astype(BF16)
        gc = gc_ref[...] + bc_ref[...]
        gr = gr_ref[...] + br_ref[...]
        lfc = _log_sigmoid(gc)
        lfr = _log_sigmoid(gr)
        b_col = sum(jnp.dot(after_b, p, preferred_element_type=F32) for p in _split3(lfc))
        b_row = sum(jnp.dot(p, before_b, preferred_element_type=F32) for p in _split3(lfr))
        for h in range(H_M):
            ci = 2 * H_M * rev + h
            cf = ci + H_M
            q = q_ref[:, h * MQK:(h + 1) * MQK] * (MQK ** -0.5)
            k = k_ref[:, h * MQK:(h + 1) * MQK]
            ct = ct_scr[rev, h]
            pairs.append(dict(
                rev=rev, h=h, before=before, o_ref=o_ref, k=k, ct=ct,
                g_c=gc[:, ci:ci + 1] - b_col[:, cf:cf + 1],
                b_r=b_row[cf:cf + 1, :], i_r=gr[ci:ci + 1, :],
                bl=jnp.sum(lfr[cf:cf + 1, :], axis=1, keepdims=True),
                m=m_scr[rev, h][0:1, 0:1],
                vt_aug=jnp.concatenate([vt_ref[h * MV:(h + 1) * MV, :], ones_rows], axis=0),
                kq=lax.dot_general(k, q, _NT, preferred_element_type=F32),
                cq=lax.dot_general(ct.astype(BF16), q, _NT, preferred_element_type=F32)))

    for p in pairs:
        pm = jnp.max(jnp.where(p["before"], p["g_c"], -jnp.inf), axis=0, keepdims=True)
        inter = p["b_r"] + p["m"]
        p["mj"] = jnp.maximum(inter, p["b_r"] + pm)
        w = jnp.exp(jnp.where(p["before"], p["g_c"] + (p["b_r"] - p["mj"]), -jnp.inf))
        p["sc"] = jnp.exp(inter - p["mj"])
        p["kqw"] = (p["kq"] * w).astype(BF16)

    for p in pairs:
        h = p["h"]
        num = jnp.dot(p["vt_aug"], p["kqw"], preferred_element_type=F32) + p["sc"] * p["cq"]
        den = jnp.maximum(jnp.abs(num[MV:MV + 1, :]), jnp.exp(-p["mj"]))
        p["o_ref"][:, h * MV:(h + 1) * MV] = (num[:MV, :] / den).T

    for p in pairs:
        wlog = p["bl"] - p["b_r"] + p["i_r"]
        m_new = jnp.maximum(p["bl"] + p["m"], jnp.max(wlog, axis=1, keepdims=True))
        ws = jnp.exp(wlog - m_new)
        sd = jnp.exp(p["bl"] + p["m"] - m_new)
        wvt = (p["vt_aug"].astype(F32) * ws).astype(BF16)
        ct_scr[p["rev"], p["h"]] = sd * p["ct"] + jnp.dot(wvt, p["k"], preferred_element_type=F32)
        m_scr[p["rev"], p["h"]] = jnp.broadcast_to(m_new, m_scr.shape[2:])


def _mlstm(mq, mk, mvt, pg, pg_t, bg_c, bg_r, L):
    t = mq.shape[0]
    nc = t // L
    fwd = lambda c: c
    bwd = lambda c: jnp.where(c == 0, 0, nc - c)

    def chunk_specs(idx):
        return [pl.BlockSpec((L, SEG_Q), lambda c: (idx(c), 0)),
                pl.BlockSpec((L, SEG_K), lambda c: (idx(c), 0)),
                pl.BlockSpec((SEG_V, L), lambda c: (0, idx(c))),
                pl.BlockSpec((L, LANES), lambda c: (idx(c), 0)),
                pl.BlockSpec((N_GATES, L), lambda c: (0, idx(c)))]

    chunk_args = (mq, mk, mvt, pg, pg_t)
    return pl.pallas_call(
        functools.partial(_mlstm_kernel, L=L),
        grid=(nc,),
        in_specs=chunk_specs(fwd) + chunk_specs(bwd)
        + [pl.BlockSpec((1, LANES), lambda c: (0, 0)), pl.BlockSpec((N_GATES, 1), lambda c: (0, 0))],
        out_specs=[pl.BlockSpec((L, H_M * MV), lambda c: (fwd(c), 0)),
                   pl.BlockSpec((L, H_M * MV), lambda c: (bwd(c), 0))],
        out_shape=[jax.ShapeDtypeStruct((t, H_M * MV), F32)] * 2,
        scratch_shapes=[pltpu.VMEM((2, H_M, 2 * MV, MQK), F32), pltpu.VMEM((2, H_M, SUBLANES, LANES), F32)],
        compiler_params=_cparams(1),
        name="mlstm",
    )(*chunk_args, *chunk_args, bg_c, bg_r)


def _router(logits):
    tm = logits.shape[0]
    lane = lax.broadcasted_iota(jnp.int32, (tm, LANES), 1)
    big = jnp.int32(LANES)
    neg = -jnp.inf
    is_g = lane < N_GROUPS
    lg = jnp.where(is_g, logits, neg)
    gmax = jnp.max(lg, axis=-1, keepdims=True)
    grp = jnp.min(jnp.where(lg == gmax, lane, big), axis=-1, keepdims=True)
    p_grp = 1.0 / jnp.sum(jnp.where(is_g, jnp.exp(lg - gmax), 0.0), axis=-1, keepdims=True)
    e_lo = N_GROUPS + grp * EPG
    in_grp = (lane >= e_lo) & (lane < e_lo + EPG)
    le = jnp.where(in_grp, logits, neg)
    l1 = jnp.max(le, axis=-1, keepdims=True)
    i1 = jnp.min(jnp.where(le == l1, lane, big), axis=-1, keepdims=True)
    le2 = jnp.where(lane == i1, neg, le)
    l2 = jnp.max(le2, axis=-1, keepdims=True)
    i2 = jnp.min(jnp.where(le2 == l2, lane, big), axis=-1, keepdims=True)
    r = jnp.exp(l2 - l1)
    w1 = p_grp / (1.0 + r)
    w2 = w1 * r
    return jnp.where(lane == i1, w1, 0.0) + jnp.where(lane == i2, w2, 0.0), grp


def _merge_kernel(att_ref, hf_ref, hb_ref, po_ref, pm_ref, x_ref, mod_ref, gmh_ref, l1g_ref, l1b_ref,
                  wa_ref, wm_ref, wo_ref, wr_ref, br_ref, x1_ref, hm_ref, plan_ref, cnt_ref, *, tm):
    is_ctx = _is_ctx_rows(tm)
    y_mla = jnp.dot(att_ref[...], wa_ref[...], preferred_element_type=F32)
    hs = hf_ref[...] + hb_ref[...]
    parts = []
    for h in range(H_M):
        sl = slice(h * MV, (h + 1) * MV)
        parts.append(_ln(hs[:, sl]))
    hn = jnp.concatenate(parts, axis=1) * gmh_ref[...] * po_ref[...].astype(F32)
    y_ml = jnp.dot(hn.astype(BF16), wm_ref[...], preferred_element_type=F32)
    g_a = pm_ref[:, :D].astype(F32)
    g_b = pm_ref[:, D:].astype(F32)
    y = jnp.dot((g_a * y_mla + g_b * y_ml).astype(BF16), wo_ref[...], preferred_element_type=F32)
    x1 = _ln(ALPHA * x_ref[...] + _mod(mod_ref, 2, is_ctx) * y) * l1g_ref[...] + l1b_ref[...]
    x1_ref[...] = x1
    hm = _ln(x1) * (1.0 + _mod(mod_ref, 4, is_ctx)) + _mod(mod_ref, 3, is_ctx)
    hm_ref[...] = hm.astype(BF16)
    h_hi = hm.astype(BF16)
    h_lo = (hm - h_hi.astype(F32)).astype(BF16)
    logits = (jnp.dot(h_hi, wr_ref[0], preferred_element_type=F32)
              + jnp.dot(h_hi, wr_ref[1], preferred_element_type=F32)
              + jnp.dot(h_lo, wr_ref[0], preferred_element_type=F32)) + br_ref[...]
    comb, grp = _router(logits)
    lane = lax.broadcasted_iota(jnp.int32, (tm, LANES), 1)
    in_group = lane == grp
    earlier = (lax.broadcasted_iota(jnp.int32, (tm, tm), 0)
               > lax.broadcasted_iota(jnp.int32, (tm, tm), 1)).astype(BF16)
    n_before = jnp.dot(earlier, in_group.astype(BF16), preferred_element_type=F32)
    rank = jnp.sum(jnp.where(in_group, n_before, 0.0), axis=-1, keepdims=True)
    plan_ref[...] = (comb + jnp.where(lane == PLAN_GROUP, grp.astype(F32), 0.0)
                     + jnp.where(lane == PLAN_RANK, rank, 0.0))
    cnt_ref[...] = jnp.broadcast_to(jnp.sum(in_group.astype(F32), axis=0, keepdims=True), cnt_ref.shape)


def _merge(att, hf, hb, po, pm, x_all, mod, g_mh, l1g, l1b, wa, wm, wo, wr, br, tm):
    t = x_all.shape[0]
    row = lambda n: pl.BlockSpec((tm, n), lambda i: (i, 0))
    full = lambda a: pl.BlockSpec(a.shape, lambda i: (0,) * a.ndim)
    return pl.pallas_call(
        functools.partial(_merge_kernel, tm=tm),
        grid=(t // tm,),
        in_specs=[row(H_MLA * HP), row(H_M * MV), row(H_M * MV), row(SEG_PO), row(SEG_PM), row(D),
                  full(mod), full(g_mh), full(l1g), full(l1b), full(wa), full(wm), full(wo), full(wr), full(br)],
        out_specs=[row(D), row(D), row(LANES), pl.BlockSpec((None, SUBLANES, LANES), lambda i: (i, 0, 0))],
        out_shape=[jax.ShapeDtypeStruct((t, D), F32), jax.ShapeDtypeStruct((t, D), BF16),
                   jax.ShapeDtypeStruct((t, LANES), F32), jax.ShapeDtypeStruct((t // tm, SUBLANES, LANES), F32)],
        compiler_params=_cparams(1),
        name="merge",
    )(att, hf, hb, po, pm, x_all, mod, g_mh, l1g, l1b, wa, wm, wo, wr, br)


def _moe_kernel(cnt_ref, layer_ref, hm_ref, plan_ref, plant_ref, x1_ref, mod_ref, l2g_ref, l2b_ref, wg_ref, wu_ref, wd_ref,
                o_ref, *, tm, tp, slot_sizes):
    i = pl.program_id(0)
    grp = pl.program_id(1)
    n_sub = tm // tp

    @pl.when(grp == 0)
    def _():
        o_ref[...] = jnp.zeros_like(o_ref)

    n_tok = cnt_ref[i * n_sub, grp]
    for j in range(1, n_sub):
        n_tok = jnp.maximum(n_tok, cnt_ref[i * n_sub + j, grp])

    def rounds_of(slots):
        return functools.partial(_moe_round, hm_ref, plan_ref, plant_ref, wg_ref, wu_ref, wd_ref, o_ref,
                                 grp=grp, n_sub=n_sub, tp=tp, slots=slots)

    big = slot_sizes[-1]
    tails = slot_sizes[:-1]
    n_full = lax.div(n_tok, big) + (lax.rem(n_tok, big) > (tails[-1] if tails else 0)).astype(jnp.int32)
    rest = jnp.maximum(n_tok - n_full * big, 0)
    full_round = rounds_of(big)
    lax.fori_loop(0, n_full, lambda r, carry: full_round(r * big, carry), 0)
    lower = 0
    for slots in tails:
        @pl.when((rest > lower) & (rest <= slots))
        def _(tail_round=rounds_of(slots)):
            tail_round(n_full * big, 0)

        lower = slots

    @pl.when(grp == N_GROUPS - 1)
    def _():
        is_ctx = _is_ctx_rows(tm)
        z = ALPHA * x1_ref[...] + _mod(mod_ref, 5, is_ctx) * o_ref[...]
        o_ref[...] = _ln(z) * l2g_ref[...] + l2b_ref[...]


def _moe_round(hm_ref, plan_ref, plant_ref, wg_ref, wu_ref, wd_ref, o_ref, first, carry, *, grp, n_sub, tp, slots):
    grp_f = grp.astype(F32)
    base = first.astype(F32)
    slot_rows = lax.broadcasted_iota(jnp.int32, (slots, tp), 0).astype(F32)
    slot_cols = lax.broadcasted_iota(jnp.int32, (tp, slots), 1).astype(F32)
    lane = lax.broadcasted_iota(jnp.int32, (n_sub * slots, LANES), 1)
    xs, cs = [], []
    for j in range(n_sub):
        rows = slice(j * tp, (j + 1) * tp)
        in_grp = plant_ref[PLAN_GROUP:PLAN_GROUP + 1, rows] == grp_f
        rank = plant_ref[PLAN_RANK:PLAN_RANK + 1, rows]
        gather = ((slot_rows == rank - base) & in_grp).astype(BF16)
        xs.append(jnp.dot(gather, hm_ref[rows, :], preferred_element_type=F32).astype(BF16))
        cs.append(sum(jnp.dot(gather, p, preferred_element_type=F32)
                      for p in _split3(plan_ref[rows, :])))
    xs = jnp.concatenate(xs, axis=0)
    cs = jnp.concatenate(cs, axis=0)
    hid = []
    for g in range(EPG):
        c_e = jnp.sum(jnp.where(lane == N_GROUPS + grp * EPG + g, cs, 0.0), axis=-1, keepdims=True)
        a = jnp.dot(xs, wg_ref[g], preferred_element_type=F32)
        u = jnp.dot(xs, wu_ref[g], preferred_element_type=F32)
        hid.append((a * _sigmoid(a) * u * c_e).astype(BF16))
    ys = jnp.dot(jnp.concatenate(hid, axis=1), wd_ref[...].reshape(EPG * D_EXP, D),
                 preferred_element_type=F32)
    ys = ys.astype(BF16)
    for j in range(n_sub):
        rows = slice(j * tp, (j + 1) * tp)
        mine = slice(j * slots, (j + 1) * slots)
        in_grp = plan_ref[rows, PLAN_GROUP:PLAN_GROUP + 1] == grp_f
        rank = plan_ref[rows, PLAN_RANK:PLAN_RANK + 1]
        scatter = ((slot_cols == rank - base) & in_grp).astype(BF16)
        o_ref[rows, :] += jnp.dot(scatter, ys[mine], preferred_element_type=F32)
    return carry


def _moe(counts, hm, plan, plan_t, x1, mod, l2g, l2b, wg, wu, wd, layer, tp):
    t = hm.shape[0]
    tm = tp * (MOE_PLAN_TILES if (t // tp) % MOE_PLAN_TILES == 0 else 1)
    row = lambda n, **kw: pl.BlockSpec((tm, n), lambda i, e, cnt, lay: (i, 0), **kw)
    full = lambda a: pl.BlockSpec(a.shape, lambda i, e, cnt, lay: (0,) * a.ndim)
    group_of = lambda a: pl.BlockSpec((None, EPG) + a.shape[2:], lambda i, e, cnt, lay: (lay[0], e, 0, 0))
    return pl.pallas_call(
        functools.partial(_moe_kernel, tm=tm, tp=tp, slot_sizes=_moe_slot_sizes(tp)),
        grid_spec=pltpu.PrefetchScalarGridSpec(
            num_scalar_prefetch=2,
            grid=(t // tm, N_GROUPS),
            in_specs=[row(D), row(LANES), pl.BlockSpec((2, tm), lambda i, e, cnt, lay: (0, i)),
                      row(D, pipeline_mode=pl.Buffered(1)),
                      full(mod), full(l2g), full(l2b), group_of(wg), group_of(wu), group_of(wd)],
            out_specs=row(D)),
        out_shape=jax.ShapeDtypeStruct((t, D), F32),
        compiler_params=_cparams(2, VMEM_LIMIT_MOE),
        name="moe",
    )(counts, jnp.full((1,), layer, jnp.int32), hm, plan, plan_t, x1, mod, l2g, l2b, wg, wu, wd)


def _rot_cols(w):
    a1, a2, b1, b2 = jnp.split(w, 4, axis=-1)
    return jnp.concatenate([-a2, a1, -b2, b1], axis=-1)


def _place_rope(w):
    return jnp.pad(w, ((0, 0), (QK_NOPE, HP - QK_NOPE - QK_ROPE)))


def _rope_tables(t):
    rows = (t - CTX) // GRID_W
    row, col = jnp.meshgrid(jnp.arange(rows, dtype=F32), jnp.arange(GRID_W, dtype=F32), indexing="ij")
    row, col = row.reshape(-1), col.reshape(-1)
    half = QK_ROPE // 2
    inv = ROPE_BASE ** (-jnp.arange(0, half, 2, dtype=F32) / half)
    ar, ac = row[:, None] * inv, col[:, None] * inv
    ang = jnp.concatenate([ar, ar, ac, ac], axis=-1)
    ang = jnp.concatenate([jnp.zeros((CTX, QK_ROPE), F32), ang], axis=0)
    pad = ((0, 0), (QK_NOPE, HP - QK_NOPE - QK_ROPE))
    cos = jnp.pad(jnp.cos(ang), pad) + jnp.pad(jnp.ones((t, QK_NOPE), F32), ((0, 0), (0, HP - QK_NOPE)))
    sin = jnp.pad(jnp.sin(ang), pad)
    return cos, sin, cos.T, sin.T


def _layer_weights(l, w_in, w_uq, w_uk, w_uv, w_bo_mla, w_bo_mlstm, w_out, w_rg, b_rg, w_re, b_re):
    wi = w_in[l]
    o = 0
    segs = []
    for n in (Q_LORA, KV_LORA, QK_ROPE, SEG_Q, SEG_K, SEG_V, SEG_PO, N_GATES, SEG_PM):
        segs.append(wi[:, o:o + n])
        o += n
    s_pqd, s_ckv, s_kr, s_q, s_k, s_v, s_po, s_pg, s_pm = segs
    w_in_p = jnp.concatenate(
        [s_pqd, s_ckv, _place_rope(s_kr), _place_rope(_rot_cols(s_kr)), s_q, s_k, s_po,
         jnp.pad(s_pg, ((0, 0), (0, SEG_PG - N_GATES))), s_pm], axis=1).astype(BF16)

    uq = w_uq[l].reshape(Q_LORA, H_MLA, QK_NOPE + QK_ROPE)
    padh = ((0, 0), (0, 0), (0, HP - QK_NOPE - QK_ROPE))
    wq = jnp.pad(uq, padh).reshape(Q_LORA, H_MLA * HP)
    wqr = jnp.pad(jnp.concatenate([jnp.zeros_like(uq[..., :QK_NOPE]), _rot_cols(uq[..., QK_NOPE:])], axis=-1),
                  padh).reshape(Q_LORA, H_MLA * HP)
    padv = ((0, 0), (0, 0), (0, HP - V_HEAD))
    wk = jnp.pad(w_uk[l].reshape(KV_LORA, H_MLA, QK_NOPE), padv).reshape(KV_LORA, H_MLA * HP)
    wv = jnp.pad(w_uv[l].reshape(KV_LORA, H_MLA, V_HEAD), ((0, 0), (0, 0), (0, VP - V_HEAD))).reshape(KV_LORA, H_MLA * VP)
    wa = jnp.pad(w_bo_mla[l].reshape(H_MLA, V_HEAD, D), ((0, 0), (0, HP - V_HEAD), (0, 0))).reshape(H_MLA * HP, D)

    wr = jnp.pad(jnp.concatenate([w_rg[l], w_re[l]], axis=1), ((0, 0), (0, LANES - N_GROUPS - N_EXP)))
    wr_hi = wr.astype(BF16)
    wr_lo = (wr - wr_hi.astype(F32)).astype(BF16)
    br = jnp.pad(jnp.concatenate([b_rg[l], b_re[l]]), (0, LANES - N_GROUPS - N_EXP)).reshape(1, LANES)
    return dict(
        w_in=w_in_p, w_mvt=s_v.T.astype(BF16), wqt=wq.T.astype(BF16), wqrt=wqr.T.astype(BF16), wk=wk.astype(BF16), wvt=wv.T.astype(BF16),
        wa=wa.astype(BF16), wm=w_bo_mlstm[l].astype(BF16), wo=w_out[l].astype(BF16),
        wr=jnp.stack([wr_hi, wr_lo]), br=br)


def kernel(x, c, ctx, c_ctx, w_ada, b_ada, w_in, b_gates, w_uq, w_uk, w_uv, g_qn, g_kvn, g_mh, w_bo_mla,
           w_bo_mlstm, w_out, ln1_g, ln1_b, w_rg, b_rg, w_re, b_re, w_e_gate, w_e_up, w_e_down, ln2_g, ln2_b):
    assert x.shape[0] == 1 and c.shape[0] == 1 and ctx.shape[1] == CTX
    x_all = jnp.concatenate([ctx[0], x[0]], axis=0)
    t = x_all.shape[0]
    tm = ROW_TILE if t % ROW_TILE == 0 else M_CHUNK
    tk = ATT_TK if t % ATT_TK == 0 else M_CHUNK
    assert t % tm == 0 and t % ATT_TQ == 0 and t % tk == 0 and t % M_CHUNK == 0

    cc = jnp.pad(jnp.concatenate([c, c_ctx[None]], axis=0), ((0, SUBLANES - 2), (0, 0)))
    mods = _ada(cc, w_ada, b_ada)
    tabs = _rope_tables(t)
    row2 = lambda a: a.reshape(1, -1)
    wg_all, wu_all, wd_all = w_e_gate.astype(BF16), w_e_up.astype(BF16), w_e_down.astype(BF16)

    for l in range(DEPTH):
        w = _layer_weights(l, w_in, w_uq, w_uk, w_uv, w_bo_mla, w_bo_mlstm, w_out, w_rg, b_rg, w_re, b_re)
        mod = mods[l]
        pqd, ckv, kr, mq, mk, po, pg, pm, mvt = _inproj(x_all, mod, w["w_in"], w["w_mvt"], tm)
        qt, kk, vt = _mla_prep(pqd, ckv, kr, tabs, row2(g_qn[l]), row2(g_kvn[l]),
                               w["wqt"], w["wqrt"], w["wk"], w["wvt"], tm)
        att = _attention(qt, kk, vt, ATT_TQ, tk, ATT_CB, ATT_DEPTH)
        pg_t = pg[:, :N_GATES].T
        bg_c = jnp.pad(b_gates[l], (0, LANES - N_GATES)).reshape(1, LANES)
        bg_r = b_gates[l].reshape(N_GATES, 1)
        hf, hb = _mlstm(mq, mk, mvt, pg, pg_t, bg_c, bg_r, M_CHUNK)
        x1, hm, plan, cnt = _merge(att, hf, hb, po, pm, x_all, mod, row2(g_mh[l]), row2(ln1_g[l]), row2(ln1_b[l]),
                                   w["wa"], w["wm"], w["wo"], w["wr"], w["br"], tm)
        counts = cnt[:, 0, :N_GROUPS].astype(jnp.int32)
        plan_t = plan[:, :2].T
        x_all = _moe(counts, hm, plan, plan_t, x1, mod, row2(ln2_g[l]), row2(ln2_b[l]),
                     wg_all, wu_all, wd_all, l, tm)
    return x_all[CTX:][None]
```

```python
import functools
import math

import jax
import jax.numpy as jnp
from jax import lax
from jax.experimental import pallas as pl
from jax.experimental.pallas import tpu as pltpu

F32 = jnp.float32
BF16 = jnp.bfloat16

D = 1024
DEPTH = 4
GRID_W = 64
CTX = 256
H_MLA = 8
QK_NOPE = 64
QK_ROPE = 32
V_HEAD = 64
Q_LORA = 384
KV_LORA = 256
ROPE_BASE = 10000.0
MLA_SCALE = (QK_NOPE + QK_ROPE) ** -0.5
H_M = 4
MQK = 64
MV = 128
N_GROUPS = 4
EPG = 8
N_EXP = N_GROUPS * EPG
D_EXP = 256
ALPHA = (2 * DEPTH) ** 0.25
LN_EPS = 1e-6

LANES = 128
SUBLANES = 8
NEG_BIG = -1e30
N_GATES = 4 * H_M
ADA_TN = 1536
VMEM_LIMIT = 56 * 1024 * 1024
VMEM_LIMIT_MOE = 58 * 1024 * 1024

HP = LANES
ONES_ROW = V_HEAD
BF16_ROWS = 16
VP = (V_HEAD + 1 + BF16_ROWS - 1) // BF16_ROWS * BF16_ROWS
QSCALE = MLA_SCALE * math.log2(math.e)

SEG_PQD = Q_LORA
SEG_CKV = KV_LORA
SEG_KR = 2 * HP
SEG_Q = H_M * MQK
SEG_K = H_M * MQK
SEG_V = H_M * MV
SEG_PO = H_M * MV
SEG_PG = LANES
SEG_PM = 2 * D
IN_SEGS = (SEG_PQD, SEG_CKV, SEG_KR, SEG_Q, SEG_K, SEG_PO, SEG_PG, SEG_PM)
IN_DTYPES = (F32, F32, F32, BF16, BF16, BF16, F32, BF16)
IN_GATES = (False, False, False, False, False, True, False, True)
NP_IN = sum(IN_SEGS)

ROW_TILE = 640
ATT_TQ = 3328
ATT_CB = 256
ATT_TK = 1280
ATT_KT = 256
MAX_LAG_JUMP = 64.0
ATT_DEPTH = 1
M_CHUNK = 256
MOE_SLOT_STEP = 64
MOE_SLOT_MAX = 256
MOE_PLAN_TILES = 2
PLAN_GROUP = 0
PLAN_RANK = 1


def _moe_slot_sizes(tm):
    return tuple(s for s in range(MOE_SLOT_STEP, MOE_SLOT_MAX + 1, MOE_SLOT_STEP) if s <= tm)

_NT = (((1,), (1,)), ((), ()))


def _cparams(n_grid, vmem_limit=VMEM_LIMIT):
    return pltpu.CompilerParams(dimension_semantics=("arbitrary",) * n_grid, vmem_limit_bytes=vmem_limit)


def _ln(x):
    mu = jnp.mean(x, axis=-1, keepdims=True)
    xc = x - mu
    var = jnp.mean(xc * xc, axis=-1, keepdims=True)
    return xc * lax.rsqrt(var + LN_EPS)


def _rms(x, g):
    return x * lax.rsqrt(jnp.mean(x * x, axis=-1, keepdims=True) + LN_EPS) * g


def _sigmoid(x):
    return 1.0 / (1.0 + jnp.exp(-x))


def _is_ctx_rows(tm):
    rows = pl.program_id(0) * tm + lax.broadcasted_iota(jnp.int32, (tm, 1), 0)
    return rows < CTX


def _mod(mod_ref, k, is_ctx):
    lat = mod_ref[0:1, k * D:(k + 1) * D]
    ctx = mod_ref[1:2, k * D:(k + 1) * D]
    return jnp.where(is_ctx, ctx, lat)


def _ada_kernel(c_ref, w_ref, b_ref, o_ref):
    c = c_ref[...]
    s = c * _sigmoid(c)
    o_ref[...] = jnp.dot(s, w_ref[...], preferred_element_type=F32,
                         precision=lax.Precision.HIGHEST) + b_ref[...]


def _ada(cc, w_ada, b_ada):
    tn = ADA_TN
    n = 6 * D
    return pl.pallas_call(
        _ada_kernel,
        grid=(DEPTH, n // tn),
        in_specs=[pl.BlockSpec((SUBLANES, D), lambda l, j: (0, 0)),
                  pl.BlockSpec((None, D, tn), lambda l, j: (l, 0, j)),
                  pl.BlockSpec((None, 1, tn), lambda l, j: (l, 0, j))],
        out_specs=pl.BlockSpec((None, SUBLANES, tn), lambda l, j: (l, 0, j)),
        out_shape=jax.ShapeDtypeStruct((DEPTH, SUBLANES, n), F32),
        compiler_params=_cparams(2),
        name="ada",
    )(cc, w_ada, b_ada.reshape(DEPTH, 1, n))


def _inproj_kernel(x_ref, mod_ref, w_ref, wvt_ref, *o_refs, tm):
    is_ctx = _is_ctx_rows(tm)
    h = _ln(x_ref[...]) * (1.0 + _mod(mod_ref, 1, is_ctx)) + _mod(mod_ref, 0, is_ctx)
    hb = h.astype(BF16)
    off = 0
    for o, is_gate in zip(o_refs[:-1], IN_GATES):
        n = o.shape[-1]
        p = jnp.dot(hb, w_ref[:, off:off + n], preferred_element_type=F32)
        o[...] = (_sigmoid(p) if is_gate else p).astype(o.dtype)
        off += n
    o_refs[-1][...] = lax.dot_general(wvt_ref[...], hb, _NT, preferred_element_type=F32).astype(BF16)


def _inproj(x_all, mod, w_in_p, w_vt, tm):
    t = x_all.shape[0]
    return pl.pallas_call(
        functools.partial(_inproj_kernel, tm=tm),
        grid=(t // tm,),
        in_specs=[pl.BlockSpec((tm, D), lambda i: (i, 0)),
                  pl.BlockSpec((SUBLANES, 6 * D), lambda i: (0, 0)),
                  pl.BlockSpec((D, NP_IN), lambda i: (0, 0)),
                  pl.BlockSpec((SEG_V, D), lambda i: (0, 0))],
        out_specs=[pl.BlockSpec((tm, n), lambda i: (i, 0)) for n in IN_SEGS]
        + [pl.BlockSpec((SEG_V, tm), lambda i: (0, i))],
        out_shape=[jax.ShapeDtypeStruct((t, n), dt) for n, dt in zip(IN_SEGS, IN_DTYPES)]
        + [jax.ShapeDtypeStruct((SEG_V, t), BF16)],
        compiler_params=_cparams(1),
        name="inproj",
    )(x_all, mod, w_in_p, w_vt)


def _mla_prep_kernel(pqd_ref, ckv_ref, kr_ref, cos_ref, sin_ref, cost_ref, sint_ref, gq_ref, gk_ref,
                     wqt_ref, wqrt_ref, wk_ref, wvt_ref, qt_ref, k_ref, vt_ref, *, tm):
    qn = _rms(pqd_ref[...], gq_ref[...]).astype(BF16)
    qa = lax.dot_general(wqt_ref[...], qn, _NT, preferred_element_type=F32)
    qb = lax.dot_general(wqrt_ref[...], qn, _NT, preferred_element_type=F32)
    cost = cost_ref[...]
    sint = sint_ref[...]
    ckv = _rms(ckv_ref[...], gk_ref[...]).astype(BF16)
    kn = jnp.dot(ckv, wk_ref[...], preferred_element_type=F32)
    kr = kr_ref[:, :HP] * cos_ref[...] + kr_ref[:, HP:] * sin_ref[...]
    vt = lax.dot_general(wvt_ref[...], ckv, _NT, preferred_element_type=F32)
    is_ones_row = lax.broadcasted_iota(jnp.int32, (VP, tm), 0) == ONES_ROW
    for h in range(H_MLA):
        sl = slice(h * HP, (h + 1) * HP)
        qt_ref[h] = ((qa[sl] * cost + qb[sl] * sint) * QSCALE).astype(BF16)
        k_ref[h] = (kn[:, sl] + kr).astype(BF16)
        vt_ref[h] = jnp.where(is_ones_row, 1.0, vt[h * VP:(h + 1) * VP]).astype(BF16)


def _mla_prep(pqd, ckv, kr, tabs, g_qn, g_kvn, wqt, wqrt, wk, wvt, tm):
    t = pqd.shape[0]
    cos, sin, cost, sint = tabs
    row = lambda n: pl.BlockSpec((tm, n), lambda i: (i, 0))
    col = lambda n: pl.BlockSpec((n, tm), lambda i: (0, i))
    full = lambda a: pl.BlockSpec(a.shape, lambda i: (0,) * a.ndim)
    return pl.pallas_call(
        functools.partial(_mla_prep_kernel, tm=tm),
        grid=(t // tm,),
        in_specs=[row(Q_LORA), row(KV_LORA), row(SEG_KR), row(HP), row(HP), col(HP), col(HP),
                  full(g_qn), full(g_kvn), full(wqt), full(wqrt), full(wk), full(wvt)],
        out_specs=[pl.BlockSpec((H_MLA, HP, tm), lambda i: (0, 0, i)),
                   pl.BlockSpec((H_MLA, tm, HP), lambda i: (0, i, 0)),
                   pl.BlockSpec((H_MLA, VP, tm), lambda i: (0, 0, i))],
        out_shape=[jax.ShapeDtypeStruct((H_MLA, HP, t), BF16),
                   jax.ShapeDtypeStruct((H_MLA, t, HP), BF16),
                   jax.ShapeDtypeStruct((H_MLA, VP, t), BF16)],
        compiler_params=_cparams(1),
        name="mla_prep",
    )(pqd, ckv, kr, cos, sin, cost, sint, g_qn, g_kvn, wqt, wqrt, wk, wvt)


def _scores(k, qt, m):
    s = jnp.dot(k, qt, preferred_element_type=F32)
    return s, jnp.maximum(m, jnp.max(s, axis=0, keepdims=True))


def _accumulate(s, vt, m, m_new, acc):
    p = jnp.exp2(s - m_new).astype(BF16)
    return jnp.exp2(m - m_new) * acc + jnp.dot(vt, p, preferred_element_type=F32)


def _normalized_rows(acc):
    o = acc / acc[ONES_ROW:ONES_ROW + 1, :]
    return jnp.concatenate([o, jnp.zeros((HP - VP, o.shape[1]), F32)], axis=0).T


def _lagged_pass(qt_ref, k_ref, vt_ref, m_ref, acc_ref, p_ref, sc_ref, *, ncb, n_kv, tk, cb, depth):
    k0 = k_ref[0:BF16_ROWS, :]
    for c in range(ncb):
        s0 = jnp.dot(k0, qt_ref[:, c * cb:(c + 1) * cb], preferred_element_type=F32)
        m_ref[c] = jnp.max(s0, axis=0, keepdims=True)
    acc_ref[...] = jnp.zeros(acc_ref.shape, F32)
    p_ref[...] = jnp.zeros(p_ref.shape, BF16)
    sc_ref[...] = jnp.ones(sc_ref.shape, F32)

    def carried(vt_prev):
        return [(ncb - depth + d, p_ref[d], sc_ref[d], vt_prev) for d in range(depth)]

    def drain(entry):
        pc, pp, psc, pvt = entry
        acc_ref[pc] = (acc_ref[pc] + jnp.dot(pvt, pp, preferred_element_type=F32)) * psc

    def step(j, jump):
        off = pl.multiple_of(j * tk, tk)
        off_prev = pl.multiple_of(jnp.maximum(j - 1, 0) * tk, tk)
        k = k_ref[pl.ds(off, tk), :]
        vt = vt_ref[:, pl.ds(off, tk)]
        pending = carried(vt_ref[:, pl.ds(off_prev, tk)])
        for c in range(ncb):
            m = m_ref[c]
            qt = qt_ref[:, c * cb:(c + 1) * cb]
            pc, pp, psc, pvt = pending.pop(0)
            s_parts, add = [], None
            for r in range(tk // ATT_KT):
                rows = slice(r * ATT_KT, (r + 1) * ATT_KT)
                half = ATT_KT // 2
                s_parts.append(jnp.dot(k[r * ATT_KT:r * ATT_KT + half], qt, preferred_element_type=F32))
                part = jnp.dot(pvt[:, rows], pp[rows], preferred_element_type=F32)
                add = part if add is None else add + part
                s_parts.append(jnp.dot(k[r * ATT_KT + half:(r + 1) * ATT_KT], qt, preferred_element_type=F32))
            acc_ref[pc] = (acc_ref[pc] + add) * psc
            s = jnp.concatenate(s_parts, axis=0)
            p = jnp.exp2(s - m).astype(BF16)
            cm = jnp.max(s, axis=0, keepdims=True)
            m_new = jnp.maximum(m, cm)
            jump = jnp.maximum(jump, cm - m)
            m_ref[c] = m_new
            pending.append((c, p, jnp.exp2(m - m_new), vt))
        for d, (_, pp, psc, _) in enumerate(pending):
            p_ref[d] = pp
            sc_ref[d] = psc
        return jump

    jump = lax.fori_loop(0, n_kv, step, jnp.zeros((1, cb), F32))
    for entry in carried(vt_ref[:, (n_kv - 1) * tk:]):
        drain(entry)
    return jump


def _exact_pass(qt_ref, k_ref, vt_ref, m_ref, acc_ref, s_ref, mp_ref, *, ncb, n_kv, tk, cb, depth):
    m_ref[...] = jnp.full(m_ref.shape, NEG_BIG, F32)
    acc_ref[...] = jnp.zeros(acc_ref.shape, F32)
    s_ref[...] = jnp.full(s_ref.shape, -jnp.inf, F32)
    mp_ref[...] = jnp.full(mp_ref.shape, NEG_BIG, F32)

    def carried(vt_prev):
        return [(ncb - depth + d, s_ref[d], mp_ref[d, 0], mp_ref[d, 1], vt_prev) for d in range(depth)]

    def drain(entry):
        pc, ps, pm_old, pm_new, pvt = entry
        acc_ref[pc] = _accumulate(ps, pvt, pm_old, pm_new, acc_ref[pc])

    def step(j, carry):
        off = pl.multiple_of(j * tk, tk)
        off_prev = pl.multiple_of(jnp.maximum(j - 1, 0) * tk, tk)
        k = k_ref[pl.ds(off, tk), :]
        vt = vt_ref[:, pl.ds(off, tk)]
        pending = carried(vt_ref[:, pl.ds(off_prev, tk)])
        for c in range(ncb):
            m_old = m_ref[c]
            s, m_new = _scores(k, qt_ref[:, c * cb:(c + 1) * cb], m_old)
            m_ref[c] = m_new
            drain(pending.pop(0))
            pending.append((c, s, m_old, m_new, vt))
        for d, (_, ps, pm_old, pm_new, _) in enumerate(pending):
            s_ref[d] = ps
            mp_ref[d, 0] = pm_old
            mp_ref[d, 1] = pm_new
        return carry

    lax.fori_loop(0, n_kv, step, 0)
    for entry in carried(vt_ref[:, (n_kv - 1) * tk:]):
        drain(entry)


def _attn_kernel(qt_ref, k_ref, vt_ref, o_ref, m_ref, acc_ref, s_ref, mp_ref, p_ref, sc_ref, *,
                 tq, tk, t, cb, depth):
    geom = dict(ncb=tq // cb, n_kv=t // tk, tk=tk, cb=cb, depth=depth)

    def write_out():
        for c in range(tq // cb):
            o_ref[c * cb:(c + 1) * cb, :] = _normalized_rows(acc_ref[c]).astype(o_ref.dtype)

    jump = _lagged_pass(qt_ref, k_ref, vt_ref, m_ref, acc_ref, p_ref, sc_ref, **geom)
    write_out()

    @pl.when(jnp.max(jump) > MAX_LAG_JUMP)
    def _():
        _exact_pass(qt_ref, k_ref, vt_ref, m_ref, acc_ref, s_ref, mp_ref, **geom)
        write_out()

    @pl.when(pl.program_id(1) == 0)
    def _():
        m0 = jnp.full((1, CTX), NEG_BIG, F32)
        s, m_new = _scores(k_ref[0:CTX, :], qt_ref[:, 0:CTX], m0)
        acc_c = _accumulate(s, vt_ref[:, 0:CTX], m0, m_new, jnp.zeros((VP, CTX), F32))
        o_ref[0:CTX, :] = _normalized_rows(acc_c).astype(o_ref.dtype)


def _attention(qt, k, vt, tq, tk, cb, depth):
    t = k.shape[1]
    assert depth <= tq // cb
    return pl.pallas_call(
        functools.partial(_attn_kernel, tq=tq, tk=tk, t=t, cb=cb, depth=depth),
        grid=(H_MLA, t // tq),
        in_specs=[pl.BlockSpec((None, HP, tq), lambda h, i: (h, 0, i)),
                  pl.BlockSpec((None, t, HP), lambda h, i: (h, 0, 0)),
                  pl.BlockSpec((None, VP, t), lambda h, i: (h, 0, 0))],
        out_specs=pl.BlockSpec((tq, HP), lambda h, i: (i, h)),
        out_shape=jax.ShapeDtypeStruct((t, H_MLA * HP), BF16),
        scratch_shapes=[pltpu.VMEM((tq // cb, 1, cb), F32), pltpu.VMEM((tq // cb, VP, cb), F32),
                        pltpu.VMEM((depth, tk, cb), F32), pltpu.VMEM((depth, 2, 1, cb), F32),
                        pltpu.VMEM((depth, tk, cb), BF16), pltpu.VMEM((depth, 1, cb), F32)],
        compiler_params=_cparams(2),
        name="attention",
    )(qt, k, vt)


def _split3(x):
    hi = x.astype(BF16)
    r = x - hi.astype(F32)
    mid = r.astype(BF16)
    lo = (r - mid.astype(F32)).astype(BF16)
    return hi, mid, lo


def _log_sigmoid(x):
    return jnp.minimum(x, 0.0) - jnp.log(1.0 + jnp.exp(-jnp.abs(x)))


def _mlstm_kernel(qf_ref, kf_ref, vtf_ref, gcf_ref, grf_ref, qb_ref, kb_ref, vtb_ref, gcb_ref, grb_ref,
                  bc_ref, br_ref, of_ref, ob_ref, ct_scr, m_scr, *, L):
    @pl.when(pl.program_id(0) == 0)
    def _():
        ct_scr[...] = jnp.zeros_like(ct_scr)
        m_scr[...] = jnp.zeros_like(m_scr)

    row = lax.broadcasted_iota(jnp.int32, (L, L), 0)
    col = lax.broadcasted_iota(jnp.int32, (L, L), 1)
    ones_rows = (lax.broadcasted_iota(jnp.int32, (MV, L), 0) == 0).astype(BF16)
    scans = ((0, qf_ref, kf_ref, vtf_ref, gcf_ref, grf_ref, of_ref),
             (1, qb_ref, kb_ref, vtb_ref, gcb_ref, grb_ref, ob_ref))

    pairs = []
    for rev, q_ref, k_ref, vt_ref, gc_ref, gr_ref, o_ref in scans:
        before = (row >= col) if rev else (row <= col)
        before_b = before.astype(BF16)
        after_b = ((row <= col) if rev else (row >= col)).astype(BF16)
        gc = gc_ref[...] + bc_ref[...]
        gr = gr_ref[...] + br_ref[...]
        lfc = _log_sigmoid(gc)
        lfr = _log_sigmoid(gr)
        b_col = sum(jnp.dot(after_b, p, preferred_element_type=F32) for p in _split3(lfc))
        b_row = sum(jnp.dot(p, before_b, preferred_element_type=F32) for p in _split3(lfr))
        for h in range(H_M):
            ci = 2 * H_M * rev + h
            cf = ci + H_M
            q = q_ref[:, h * MQK:(h + 1) * MQK] * (MQK ** -0.5)
            k = k_ref[:, h * MQK:(h + 1) * MQK]
            ct = ct_scr[rev, h]
            pairs.append(dict(
                rev=rev, h=h, before=before, o_ref=o_ref, k=k, ct=ct,
                g_c=gc[:, ci:ci + 1] - b_col[:, cf:cf + 1],
                b_r=b_row[cf:cf + 1, :], i_r=gr[ci:ci + 1, :],
                bl=jnp.sum(lfr[cf:cf + 1, :], axis=1, keepdims=True),
                m=m_scr[rev, h][0:1, 0:1],
                vt_aug=jnp.concatenate([vt_ref[h * MV:(h + 1) * MV, :], ones_rows], axis=0),
                kq=lax.dot_general(k, q, _NT, preferred_element_type=F32),
                cq=lax.dot_general(ct.astype(BF16), q, _NT, preferred_element_type=F32)))

    for p in pairs:
        pm = jnp.max(jnp.where(p["before"], p["g_c"], -jnp.inf), axis=0, keepdims=True)
        inter = p["b_r"] + p["m"]
        p["mj"] = jnp.maximum(inter, p["b_r"] + pm)
        w = jnp.exp(jnp.where(p["before"], p["g_c"] + (p["b_r"] - p["mj"]), -jnp.inf))
        p["sc"] = jnp.exp(inter - p["mj"])
        p["kqw"] = (p["kq"] * w).astype(BF16)

    for p in pairs:
        h = p["h"]
        num = jnp.dot(p["vt_aug"], p["kqw"], preferred_element_type=F32) + p["sc"] * p["cq"]
        den = jnp.maximum(jnp.abs(num[MV:MV + 1, :]), jnp.exp(-p["mj"]))
        p["o_ref"][:, h * MV:(h + 1) * MV] = (num[:MV, :] / den).T

    for p in pairs:
        wlog = p["bl"] - p["b_r"] + p["i_r"]
        m_new = jnp.maximum(p["bl"] + p["m"], jnp.max(wlog, axis=1, keepdims=True))
        ws = jnp.exp(wlog - m_new)
        sd = jnp.exp(p["bl"] + p["m"] - m_new)
        wvt = (p["vt_aug"].astype(F32) * ws).astype(BF16)
        ct_scr[p["rev"], p["h"]] = sd * p["ct"] + jnp.dot(wvt, p["k"], preferred_element_type=F32)
        m_scr[p["rev"], p["h"]] = jnp.broadcast_to(m_new, m_scr.shape[2:])


def _mlstm(mq, mk, mvt, pg, pg_t, bg_c, bg_r, L):
    t = mq.shape[0]
    nc = t // L
    fwd = lambda c: c
    bwd = lambda c: jnp.where(c == 0, 0, nc - c)

    def chunk_specs(idx):
        return [pl.BlockSpec((L, SEG_Q), lambda c: (idx(c), 0)),
                pl.BlockSpec((L, SEG_K), lambda c: (idx(c), 0)),
                pl.BlockSpec((SEG_V, L), lambda c: (0, idx(c))),
                pl.BlockSpec((L, LANES), lambda c: (idx(c), 0)),
                pl.BlockSpec((N_GATES, L), lambda c: (0, idx(c)))]

    chunk_args = (mq, mk, mvt, pg, pg_t)
    return pl.pallas_call(
        functools.partial(_mlstm_kernel, L=L),
        grid=(nc,),
        in_specs=chunk_specs(fwd) + chunk_specs(bwd)
        + [pl.BlockSpec((1, LANES), lambda c: (0, 0)), pl.BlockSpec((N_GATES, 1), lambda c: (0, 0))],
        out_specs=[pl.BlockSpec((L, H_M * MV), lambda c: (fwd(c), 0)),
                   pl.BlockSpec((L, H_M * MV), lambda c: (bwd(c), 0))],
        out_shape=[jax.ShapeDtypeStruct((t, H_M * MV), F32)] * 2,
        scratch_shapes=[pltpu.VMEM((2, H_M, 2 * MV, MQK), F32), pltpu.VMEM((2, H_M, SUBLANES, LANES), F32)],
        compiler_params=_cparams(1),
        name="mlstm",
    )(*chunk_args, *chunk_args, bg_c, bg_r)


def _router(logits):
    tm = logits.shape[0]
    lane = lax.broadcasted_iota(jnp.int32, (tm, LANES), 1)
    big = jnp.int32(LANES)
    neg = -jnp.inf
    is_g = lane < N_GROUPS
    lg = jnp.where(is_g, logits, neg)
    gmax = jnp.max(lg, axis=-1, keepdims=True)
    grp = jnp.min(jnp.where(lg == gmax, lane, big), axis=-1, keepdims=True)
    p_grp = 1.0 / jnp.sum(jnp.where(is_g, jnp.exp(lg - gmax), 0.0), axis=-1, keepdims=True)
    e_lo = N_GROUPS + grp * EPG
    in_grp = (lane >= e_lo) & (lane < e_lo + EPG)
    le = jnp.where(in_grp, logits, neg)
    l1 = jnp.max(le, axis=-1, keepdims=True)
    i1 = jnp.min(jnp.where(le == l1, lane, big), axis=-1, keepdims=True)
    le2 = jnp.where(lane == i1, neg, le)
    l2 = jnp.max(le2, axis=-1, keepdims=True)
    i2 = jnp.min(jnp.where(le2 == l2, lane, big), axis=-1, keepdims=True)
    r = jnp.exp(l2 - l1)
    w1 = p_grp / (1.0 + r)
    w2 = w1 * r
    return jnp.where(lane == i1, w1, 0.0) + jnp.where(lane == i2, w2, 0.0), grp


def _merge_kernel(att_ref, hf_ref, hb_ref, po_ref, pm_ref, x_ref, mod_ref, gmh_ref, l1g_ref, l1b_ref,
                  wa_ref, wm_ref, wo_ref, wr_ref, br_ref, x1_ref, hm_ref, plan_ref, cnt_ref, *, tm):
    is_ctx = _is_ctx_rows(tm)
    y_mla = jnp.dot(att_ref[...], wa_ref[...], preferred_element_type=F32)
    hs = hf_ref[...] + hb_ref[...]
    parts = []
    for h in range(H_M):
        sl = slice(h * MV, (h + 1) * MV)
        parts.append(_ln(hs[:, sl]))
    hn = jnp.concatenate(parts, axis=1) * gmh_ref[...] * po_ref[...].astype(F32)
    y_ml = jnp.dot(hn.astype(BF16), wm_ref[...], preferred_element_type=F32)
    g_a = pm_ref[:, :D].astype(F32)
    g_b = pm_ref[:, D:].astype(F32)
    y = jnp.dot((g_a * y_mla + g_b * y_ml).astype(BF16), wo_ref[...], preferred_element_type=F32)
    x1 = _ln(ALPHA * x_ref[...] + _mod(mod_ref, 2, is_ctx) * y) * l1g_ref[...] + l1b_ref[...]
    x1_ref[...] = x1
    hm = _ln(x1) * (1.0 + _mod(mod_ref, 4, is_ctx)) + _mod(mod_ref, 3, is_ctx)
    hm_ref[...] = hm.astype(BF16)
    h_hi = hm.astype(BF16)
    h_lo = (hm - h_hi.astype(F32)).astype(BF16)
    logits = (jnp.dot(h_hi, wr_ref[0], preferred_element_type=F32)
              + jnp.dot(h_hi, wr_ref[1], preferred_element_type=F32)
              + jnp.dot(h_lo, wr_ref[0], preferred_element_type=F32)) + br_ref[...]
    comb, grp = _router(logits)
    lane = lax.broadcasted_iota(jnp.int32, (tm, LANES), 1)
    in_group = lane == grp
    earlier = (lax.broadcasted_iota(jnp.int32, (tm, tm), 0)
               > lax.broadcasted_iota(jnp.int32, (tm, tm), 1)).astype(BF16)
    n_before = jnp.dot(earlier, in_group.astype(BF16), preferred_element_type=F32)
    rank = jnp.sum(jnp.where(in_group, n_before, 0.0), axis=-1, keepdims=True)
    plan_ref[...] = (comb + jnp.where(lane == PLAN_GROUP, grp.astype(F32), 0.0)
                     + jnp.where(lane == PLAN_RANK, rank, 0.0))
    cnt_ref[...] = jnp.broadcast_to(jnp.sum(in_group.astype(F32), axis=0, keepdims=True), cnt_ref.shape)


def _merge(att, hf, hb, po, pm, x_all, mod, g_mh, l1g, l1b, wa, wm, wo, wr, br, tm):
    t = x_all.shape[0]
    row = lambda n: pl.BlockSpec((tm, n), lambda i: (i, 0))
    full = lambda a: pl.BlockSpec(a.shape, lambda i: (0,) * a.ndim)
    return pl.pallas_call(
        functools.partial(_merge_kernel, tm=tm),
        grid=(t // tm,),
        in_specs=[row(H_MLA * HP), row(H_M * MV), row(H_M * MV), row(SEG_PO), row(SEG_PM), row(D),
                  full(mod), full(g_mh), full(l1g), full(l1b), full(wa), full(wm), full(wo), full(wr), full(br)],
        out_specs=[row(D), row(D), row(LANES), pl.BlockSpec((None, SUBLANES, LANES), lambda i: (i, 0, 0))],
        out_shape=[jax.ShapeDtypeStruct((t, D), F32), jax.ShapeDtypeStruct((t, D), BF16),
                   jax.ShapeDtypeStruct((t, LANES), F32), jax.ShapeDtypeStruct((t // tm, SUBLANES, LANES), F32)],
        compiler_params=_cparams(1),
        name="merge",
    )(att, hf, hb, po, pm, x_all, mod, g_mh, l1g, l1b, wa, wm, wo, wr, br)


def _moe_kernel(cnt_ref, layer_ref, hm_ref, plan_ref, plant_ref, x1_ref, mod_ref, l2g_ref, l2b_ref, wg_ref, wu_ref, wd_ref,
                o_ref, *, tm, tp, slot_sizes):
    i = pl.program_id(0)
    grp = pl.program_id(1)
    n_sub = tm // tp

    @pl.when(grp == 0)
    def _():
        o_ref[...] = jnp.zeros_like(o_ref)

    n_tok = cnt_ref[i * n_sub, grp]
    for j in range(1, n_sub):
        n_tok = jnp.maximum(n_tok, cnt_ref[i * n_sub + j, grp])

    def rounds_of(slots):
        return functools.partial(_moe_round, hm_ref, plan_ref, plant_ref, wg_ref, wu_ref, wd_ref, o_ref,
                                 grp=grp, n_sub=n_sub, tp=tp, slots=slots)

    big = slot_sizes[-1]
    tails = slot_sizes[:-1]
    n_full = lax.div(n_tok, big) + (lax.rem(n_tok, big) > (tails[-1] if tails else 0)).astype(jnp.int32)
    rest = jnp.maximum(n_tok - n_full * big, 0)
    full_round = rounds_of(big)
    lax.fori_loop(0, n_full, lambda r, carry: full_round(r * big, carry), 0)
    lower = 0
    for slots in tails:
        @pl.when((rest > lower) & (rest <= slots))
        def _(tail_round=rounds_of(slots)):
            tail_round(n_full * big, 0)

        lower = slots

    @pl.when(grp == N_GROUPS - 1)
    def _():
        is_ctx = _is_ctx_rows(tm)
        z = ALPHA * x1_ref[...] + _mod(mod_ref, 5, is_ctx) * o_ref[...]
        o_ref[...] = _ln(z) * l2g_ref[...] + l2b_ref[...]


def _moe_round(hm_ref, plan_ref, plant_ref, wg_ref, wu_ref, wd_ref, o_ref, first, carry, *, grp, n_sub, tp, slots):
    grp_f = grp.astype(F32)
    base = first.astype(F32)
    slot_rows = lax.broadcasted_iota(jnp.int32, (slots, tp), 0).astype(F32)
    slot_cols = lax.broadcasted_iota(jnp.int32, (tp, slots), 1).astype(F32)
    lane = lax.broadcasted_iota(jnp.int32, (n_sub * slots, LANES), 1)
    xs, cs = [], []
    for j in range(n_sub):
        rows = slice(j * tp, (j + 1) * tp)
        in_grp = plant_ref[PLAN_GROUP:PLAN_GROUP + 1, rows] == grp_f
        rank = plant_ref[PLAN_RANK:PLAN_RANK + 1, rows]
        gather = ((slot_rows == rank - base) & in_grp).astype(BF16)
        xs.append(jnp.dot(gather, hm_ref[rows, :], preferred_element_type=F32).astype(BF16))
        cs.append(sum(jnp.dot(gather, p, preferred_element_type=F32)
                      for p in _split3(plan_ref[rows, :])))
    xs = jnp.concatenate(xs, axis=0)
    cs = jnp.concatenate(cs, axis=0)
    hid = []
    for g in range(EPG):
        c_e = jnp.sum(jnp.where(lane == N_GROUPS + grp * EPG + g, cs, 0.0), axis=-1, keepdims=True)
        a = jnp.dot(xs, wg_ref[g], preferred_element_type=F32)
        u = jnp.dot(xs, wu_ref[g], preferred_element_type=F32)
        hid.append((a * _sigmoid(a) * u * c_e).astype(BF16))
    ys = jnp.dot(jnp.concatenate(hid, axis=1), wd_ref[...].reshape(EPG * D_EXP, D),
                 preferred_element_type=F32)
    ys = ys.astype(BF16)
    for j in range(n_sub):
        rows = slice(j * tp, (j + 1) * tp)
        mine = slice(j * slots, (j + 1) * slots)
        in_grp = plan_ref[rows, PLAN_GROUP:PLAN_GROUP + 1] == grp_f
        rank = plan_ref[rows, PLAN_RANK:PLAN_RANK + 1]
        scatter = ((slot_cols == rank - base) & in_grp).astype(BF16)
        o_ref[rows, :] += jnp.dot(scatter, ys[mine], preferred_element_type=F32)
    return carry


def _moe(counts, hm, plan, plan_t, x1, mod, l2g, l2b, wg, wu, wd, layer, tp):
    t = hm.shape[0]
    tm = tp * (MOE_PLAN_TILES if (t // tp) % MOE_PLAN_TILES == 0 else 1)
    row = lambda n, **kw: pl.BlockSpec((tm, n), lambda i, e, cnt, lay: (i, 0), **kw)
    full = lambda a: pl.BlockSpec(a.shape, lambda i, e, cnt, lay: (0,) * a.ndim)
    group_of = lambda a: pl.BlockSpec((None, EPG) + a.shape[2:], lambda i, e, cnt, lay: (lay[0], e, 0, 0))
    return pl.pallas_call(
        functools.partial(_moe_kernel, tm=tm, tp=tp, slot_sizes=_moe_slot_sizes(tp)),
        grid_spec=pltpu.PrefetchScalarGridSpec(
            num_scalar_prefetch=2,
            grid=(t // tm, N_GROUPS),
            in_specs=[row(D), row(LANES), pl.BlockSpec((2, tm), lambda i, e, cnt, lay: (0, i)),
                      row(D, pipeline_mode=pl.Buffered(1)),
                      full(mod), full(l2g), full(l2b), group_of(wg), group_of(wu), group_of(wd)],
            out_specs=row(D)),
        out_shape=jax.ShapeDtypeStruct((t, D), F32),
        compiler_params=_cparams(2, VMEM_LIMIT_MOE),
        name="moe",
    )(counts, jnp.full((1,), layer, jnp.int32), hm, plan, plan_t, x1, mod, l2g, l2b, wg, wu, wd)


def _rot_cols(w):
    a1, a2, b1, b2 = jnp.split(w, 4, axis=-1)
    return jnp.concatenate([-a2, a1, -b2, b1], axis=-1)


def _place_rope(w):
    return jnp.pad(w, ((0, 0), (QK_NOPE, HP - QK_NOPE - QK_ROPE)))


def _rope_tables(t):
    rows = (t - CTX) // GRID_W
    row, col = jnp.meshgrid(jnp.arange(rows, dtype=F32), jnp.arange(GRID_W, dtype=F32), indexing="ij")
    row, col = row.reshape(-1), col.reshape(-1)
    half = QK_ROPE // 2
    inv = ROPE_BASE ** (-jnp.arange(0, half, 2, dtype=F32) / half)
    ar, ac = row[:, None] * inv, col[:, None] * inv
    ang = jnp.concatenate([ar, ar, ac, ac], axis=-1)
    ang = jnp.concatenate([jnp.zeros((CTX, QK_ROPE), F32), ang], axis=0)
    pad = ((0, 0), (QK_NOPE, HP - QK_NOPE - QK_ROPE))
    cos = jnp.pad(jnp.cos(ang), pad) + jnp.pad(jnp.ones((t, QK_NOPE), F32), ((0, 0), (0, HP - QK_NOPE)))
    sin = jnp.pad(jnp.sin(ang), pad)
    return cos, sin, cos.T, sin.T


def _layer_weights(l, w_in, w_uq, w_uk, w_uv, w_bo_mla, w_bo_mlstm, w_out, w_rg, b_rg, w_re, b_re):
    wi = w_in[l]
    o = 0
    segs = []
    for n in (Q_LORA, KV_LORA, QK_ROPE, SEG_Q, SEG_K, SEG_V, SEG_PO, N_GATES, SEG_PM):
        segs.append(wi[:, o:o + n])
        o += n
    s_pqd, s_ckv, s_kr, s_q, s_k, s_v, s_po, s_pg, s_pm = segs
    w_in_p = jnp.concatenate(
        [s_pqd, s_ckv, _place_rope(s_kr), _place_rope(_rot_cols(s_kr)), s_q, s_k, s_po,
         jnp.pad(s_pg, ((0, 0), (0, SEG_PG - N_GATES))), s_pm], axis=1).astype(BF16)

    uq = w_uq[l].reshape(Q_LORA, H_MLA, QK_NOPE + QK_ROPE)
    padh = ((0, 0), (0, 0), (0, HP - QK_NOPE - QK_ROPE))
    wq = jnp.pad(uq, padh).reshape(Q_LORA, H_MLA * HP)
    wqr = jnp.pad(jnp.concatenate([jnp.zeros_like(uq[..., :QK_NOPE]), _rot_cols(uq[..., QK_NOPE:])], axis=-1),
                  padh).reshape(Q_LORA, H_MLA * HP)
    padv = ((0, 0), (0, 0), (0, HP - V_HEAD))
    wk = jnp.pad(w_uk[l].reshape(KV_LORA, H_MLA, QK_NOPE), padv).reshape(KV_LORA, H_MLA * HP)
    wv = jnp.pad(w_uv[l].reshape(KV_LORA, H_MLA, V_HEAD), ((0, 0), (0, 0), (0, VP - V_HEAD))).reshape(KV_LORA, H_MLA * VP)
    wa = jnp.pad(w_bo_mla[l].reshape(H_MLA, V_HEAD, D), ((0, 0), (0, HP - V_HEAD), (0, 0))).reshape(H_MLA * HP, D)

    wr = jnp.pad(jnp.concatenate([w_rg[l], w_re[l]], axis=1), ((0, 0), (0, LANES - N_GROUPS - N_EXP)))
    wr_hi = wr.astype(BF16)
    wr_lo = (wr - wr_hi.astype(F32)).astype(BF16)
    br = jnp.pad(jnp.concatenate([b_rg[l], b_re[l]]), (0, LANES - N_GROUPS - N_EXP)).reshape(1, LANES)
    return dict(
        w_in=w_in_p, w_mvt=s_v.T.astype(BF16), wqt=wq.T.astype(BF16), wqrt=wqr.T.astype(BF16), wk=wk.astype(BF16), wvt=wv.T.astype(BF16),
        wa=wa.astype(BF16), wm=w_bo_mlstm[l].astype(BF16), wo=w_out[l].astype(BF16),
        wr=jnp.stack([wr_hi, wr_lo]), br=br)


def kernel(x, c, ctx, c_ctx, w_ada, b_ada, w_in, b_gates, w_uq, w_uk, w_uv, g_qn, g_kvn, g_mh, w_bo_mla,
           w_bo_mlstm, w_out, ln1_g, ln1_b, w_rg, b_rg, w_re, b_re, w_e_gate, w_e_up, w_e_down, ln2_g, ln2_b):
    assert x.shape[0] == 1 and c.shape[0] == 1 and ctx.shape[1] == CTX
    x_all = jnp.concatenate([ctx[0], x[0]], axis=0)
    t = x_all.shape[0]
    tm = ROW_TILE if t % ROW_TILE == 0 else M_CHUNK
    tk = ATT_TK if t % ATT_TK == 0 else M_CHUNK
    assert t % tm == 0 and t % ATT_TQ == 0 and t % tk == 0 and t % M_CHUNK == 0

    cc = jnp.pad(jnp.concatenate([c, c_ctx[None]], axis=0), ((0, SUBLANES - 2), (0, 0)))
    mods = _ada(cc, w_ada, b_ada)
    tabs = _rope_tables(t)
    row2 = lambda a: a.reshape(1, -1)
    wg_all, wu_all, wd_all = w_e_gate.astype(BF16), w_e_up.astype(BF16), w_e_down.astype(BF16)

    for l in range(DEPTH):
        w = _layer_weights(l, w_in, w_uq, w_uk, w_uv, w_bo_mla, w_bo_mlstm, w_out, w_rg, b_rg, w_re, b_re)
        mod = mods[l]
        pqd, ckv, kr, mq, mk, po, pg, pm, mvt = _inproj(x_all, mod, w["w_in"], w["w_mvt"], tm)
        qt, kk, vt = _mla_prep(pqd, ckv, kr, tabs, row2(g_qn[l]), row2(g_kvn[l]),
                               w["wqt"], w["wqrt"], w["wk"], w["wvt"], tm)
        att = _attention(qt, kk, vt, ATT_TQ, tk, ATT_CB, ATT_DEPTH)
        pg_t = pg[:, :N_GATES].T
        bg_c = jnp.pad(b_gates[l], (0, LANES - N_GATES)).reshape(1, LANES)
        bg_r = b_gates[l].reshape(N_GATES, 1)
        hf, hb = _mlstm(mq, mk, mvt, pg, pg_t, bg_c, bg_r, M_CHUNK)
        x1, hm, plan, cnt = _merge(att, hf, hb, po, pm, x_all, mod, row2(g_mh[l]), row2(ln1_g[l]), row2(ln1_b[l]),
                                   w["wa"], w["wm"], w["wo"], w["wr"], w["br"], tm)
        counts = cnt[:, 0, :N_GROUPS].astype(jnp.int32)
        plan_t = plan[:, :2].T
        x_all = _moe(counts, hm, plan, plan_t, x1, mod, row2(ln2_g[l]), row2(ln2_b[l]),
                     wg_all, wu_all, wd_all, l, tm)
    return x_all[CTX:][None]
```

```python
import functools
import math

import jax
import jax.numpy as jnp
from jax import lax
from jax.experimental import pallas as pl
from jax.experimental.pallas import tpu as pltpu

F32 = jnp.float32
BF16 = jnp.bfloat16

D = 1024
DEPTH = 4
GRID_W = 64
CTX = 256
H_MLA = 8
QK_NOPE = 64
QK_ROPE = 32
V_HEAD = 64
Q_LORA = 384
KV_LORA = 256
ROPE_BASE = 10000.0
MLA_SCALE = (QK_NOPE + QK_ROPE) ** -0.5
H_M = 4
MQK = 64
MV = 128
N_GROUPS = 4
EPG = 8
N_EXP = N_GROUPS * EPG
D_EXP = 256
ALPHA = (2 * DEPTH) ** 0.25
LN_EPS = 1e-6

LANES = 128
SUBLANES = 8
NEG_BIG = -1e30
N_GATES = 4 * H_M
ADA_TN = 1536
VMEM_LIMIT = 56 * 1024 * 1024
VMEM_LIMIT_MOE = 58 * 1024 * 1024

HP = LANES
ONES_ROW = V_HEAD
BF16_ROWS = 16
VP = (V_HEAD + 1 + BF16_ROWS - 1) // BF16_ROWS * BF16_ROWS
QSCALE = MLA_SCALE * math.log2(math.e)

SEG_PQD = Q_LORA
SEG_CKV = KV_LORA
SEG_KR = 2 * HP
SEG_Q = H_M * MQK
SEG_K = H_M * MQK
SEG_V = H_M * MV
SEG_PO = H_M * MV
SEG_PG = LANES
SEG_PM = 2 * D
IN_SEGS = (SEG_PQD, SEG_CKV, SEG_KR, SEG_Q, SEG_K, SEG_PO, SEG_PG, SEG_PM)
IN_DTYPES = (F32, F32, F32, BF16, BF16, BF16, F32, BF16)
IN_GATES = (False, False, False, False, False, True, False, True)
NP_IN = sum(IN_SEGS)

ROW_TILE = 640
ATT_TQ = 3328
ATT_CB = 256
ATT_TK = 3328
ATT_KT = 256
MAX_LAG_JUMP = 64.0
ATT_DEPTH = 1
M_CHUNK = 256
MOE_SLOT_STEP = 64
MOE_SLOT_MAX = 256
MOE_PLAN_TILES = 2
PLAN_GROUP = 0
PLAN_RANK = 1


def _moe_slot_sizes(tm):
    return tuple(s for s in range(MOE_SLOT_STEP, MOE_SLOT_MAX + 1, MOE_SLOT_STEP) if s <= tm)

_NT = (((1,), (1,)), ((), ()))


def _cparams(n_grid, vmem_limit=VMEM_LIMIT):
    return pltpu.CompilerParams(dimension_semantics=("arbitrary",) * n_grid, vmem_limit_bytes=vmem_limit)


def _ln(x):
    mu = jnp.mean(x, axis=-1, keepdims=True)
    xc = x - mu
    var = jnp.mean(xc * xc, axis=-1, keepdims=True)
    return xc * lax.rsqrt(var + LN_EPS)


def _rms(x, g):
    return x * lax.rsqrt(jnp.mean(x * x, axis=-1, keepdims=True) + LN_EPS) * g


def _sigmoid(x):
    return 1.0 / (1.0 + jnp.exp(-x))


def _is_ctx_rows(tm):
    rows = pl.program_id(0) * tm + lax.broadcasted_iota(jnp.int32, (tm, 1), 0)
    return rows < CTX


def _mod(mod_ref, k, is_ctx):
    lat = mod_ref[0:1, k * D:(k + 1) * D]
    ctx = mod_ref[1:2, k * D:(k + 1) * D]
    return jnp.where(is_ctx, ctx, lat)


def _ada_kernel(c_ref, w_ref, b_ref, o_ref):
    c = c_ref[...]
    s = c * _sigmoid(c)
    o_ref[...] = jnp.dot(s, w_ref[...], preferred_element_type=F32,
                         precision=lax.Precision.HIGHEST) + b_ref[...]


def _ada(cc, w_ada, b_ada):
    tn = ADA_TN
    n = 6 * D
    return pl.pallas_call(
        _ada_kernel,
        grid=(DEPTH, n // tn),
        in_specs=[pl.BlockSpec((SUBLANES, D), lambda l, j: (0, 0)),
                  pl.BlockSpec((None, D, tn), lambda l, j: (l, 0, j)),
                  pl.BlockSpec((None, 1, tn), lambda l, j: (l, 0, j))],
        out_specs=pl.BlockSpec((None, SUBLANES, tn), lambda l, j: (l, 0, j)),
        out_shape=jax.ShapeDtypeStruct((DEPTH, SUBLANES, n), F32),
        compiler_params=_cparams(2),
        name="ada",
    )(cc, w_ada, b_ada.reshape(DEPTH, 1, n))


def _inproj_kernel(x_ref, mod_ref, w_ref, wvt_ref, *o_refs, tm):
    is_ctx = _is_ctx_rows(tm)
    h = _ln(x_ref[...]) * (1.0 + _mod(mod_ref, 1, is_ctx)) + _mod(mod_ref, 0, is_ctx)
    hb = h.astype(BF16)
    off = 0
    for o, is_gate in zip(o_refs[:-1], IN_GATES):
        n = o.shape[-1]
        p = jnp.dot(hb, w_ref[:, off:off + n], preferred_element_type=F32)
        o[...] = (_sigmoid(p) if is_gate else p).astype(o.dtype)
        off += n
    o_refs[-1][...] = lax.dot_general(wvt_ref[...], hb, _NT, preferred_element_type=F32).astype(BF16)


def _inproj(x_all, mod, w_in_p, w_vt, tm):
    t = x_all.shape[0]
    return pl.pallas_call(
        functools.partial(_inproj_kernel, tm=tm),
        grid=(t // tm,),
        in_specs=[pl.BlockSpec((tm, D), lambda i: (i, 0)),
                  pl.BlockSpec((SUBLANES, 6 * D), lambda i: (0, 0)),
                  pl.BlockSpec((D, NP_IN), lambda i: (0, 0)),
                  pl.BlockSpec((SEG_V, D), lambda i: (0, 0))],
        out_specs=[pl.BlockSpec((tm, n), lambda i: (i, 0)) for n in IN_SEGS]
        + [pl.BlockSpec((SEG_V, tm), lambda i: (0, i))],
        out_shape=[jax.ShapeDtypeStruct((t, n), dt) for n, dt in zip(IN_SEGS, IN_DTYPES)]
        + [jax.ShapeDtypeStruct((SEG_V, t), BF16)],
        compiler_params=_cparams(1),
        name="inproj",
    )(x_all, mod, w_in_p, w_vt)


def _mla_prep_kernel(pqd_ref, ckv_ref, kr_ref, cos_ref, sin_ref, cost_ref, sint_ref, gq_ref, gk_ref,
                     wqt_ref, wqrt_ref, wk_ref, wvt_ref, qt_ref, k_ref, vt_ref, *, tm):
    qn = _rms(pqd_ref[...], gq_ref[...]).astype(BF16)
    qa = lax.dot_general(wqt_ref[...], qn, _NT, preferred_element_type=F32)
    qb = lax.dot_general(wqrt_ref[...], qn, _NT, preferred_element_type=F32)
    cost = cost_ref[...]
    sint = sint_ref[...]
    ckv = _rms(ckv_ref[...], gk_ref[...]).astype(BF16)
    kn = jnp.dot(ckv, wk_ref[...], preferred_element_type=F32)
    kr = kr_ref[:, :HP] * cos_ref[...] + kr_ref[:, HP:] * sin_ref[...]
    vt = lax.dot_general(wvt_ref[...], ckv, _NT, preferred_element_type=F32)
    is_ones_row = lax.broadcasted_iota(jnp.int32, (VP, tm), 0) == ONES_ROW
    for h in range(H_MLA):
        sl = slice(h * HP, (h + 1) * HP)
        qt_ref[h] = ((qa[sl] * cost + qb[sl] * sint) * QSCALE).astype(BF16)
        k_ref[h] = (kn[:, sl] + kr).astype(BF16)
        vt_ref[h] = jnp.where(is_ones_row, 1.0, vt[h * VP:(h + 1) * VP]).astype(BF16)


def _mla_prep(pqd, ckv, kr, tabs, g_qn, g_kvn, wqt, wqrt, wk, wvt, tm):
    t = pqd.shape[0]
    cos, sin, cost, sint = tabs
    row = lambda n: pl.BlockSpec((tm, n), lambda i: (i, 0))
    col = lambda n: pl.BlockSpec((n, tm), lambda i: (0, i))
    full = lambda a: pl.BlockSpec(a.shape, lambda i: (0,) * a.ndim)
    return pl.pallas_call(
        functools.partial(_mla_prep_kernel, tm=tm),
        grid=(t // tm,),
        in_specs=[row(Q_LORA), row(KV_LORA), row(SEG_KR), row(HP), row(HP), col(HP), col(HP),
                  full(g_qn), full(g_kvn), full(wqt), full(wqrt), full(wk), full(wvt)],
        out_specs=[pl.BlockSpec((H_MLA, HP, tm), lambda i: (0, 0, i)),
                   pl.BlockSpec((H_MLA, tm, HP), lambda i: (0, i, 0)),
                   pl.BlockSpec((H_MLA, VP, tm), lambda i: (0, 0, i))],
        out_shape=[jax.ShapeDtypeStruct((H_MLA, HP, t), BF16),
                   jax.ShapeDtypeStruct((H_MLA, t, HP), BF16),
                   jax.ShapeDtypeStruct((H_MLA, VP, t), BF16)],
        compiler_params=_cparams(1),
        name="mla_prep",
    )(pqd, ckv, kr, cos, sin, cost, sint, g_qn, g_kvn, wqt, wqrt, wk, wvt)


def _scores(k, qt, m):
    s = jnp.dot(k, qt, preferred_element_type=F32)
    return s, jnp.maximum(m, jnp.max(s, axis=0, keepdims=True))


def _accumulate(s, vt, m, m_new, acc):
    p = jnp.exp2(s - m_new).astype(BF16)
    return jnp.exp2(m - m_new) * acc + jnp.dot(vt, p, preferred_element_type=F32)


def _normalized_rows(acc):
    o = acc / acc[ONES_ROW:ONES_ROW + 1, :]
    return jnp.concatenate([o, jnp.zeros((HP - VP, o.shape[1]), F32)], axis=0).T


def _lagged_pass(qt_ref, k_ref, vt_ref, m_ref, acc_ref, p_ref, sc_ref, *, ncb, n_kv, tk, cb, depth):
    k0 = k_ref[0:BF16_ROWS, :]
    for c in range(ncb):
        s0 = jnp.dot(k0, qt_ref[:, c * cb:(c + 1) * cb], preferred_element_type=F32)
        m_ref[c] = jnp.max(s0, axis=0, keepdims=True)
    acc_ref[...] = jnp.zeros(acc_ref.shape, F32)
    p_ref[...] = jnp.zeros(p_ref.shape, BF16)
    sc_ref[...] = jnp.ones(sc_ref.shape, F32)

    def carried(vt_prev):
        return [(ncb - depth + d, p_ref[d], sc_ref[d], vt_prev) for d in range(depth)]

    def drain(entry):
        pc, pp, psc, pvt = entry
        acc_ref[pc] = (acc_ref[pc] + jnp.dot(pvt, pp, preferred_element_type=F32)) * psc

    def step(j, jump):
        off = pl.multiple_of(j * tk, tk)
        off_prev = pl.multiple_of(jnp.maximum(j - 1, 0) * tk, tk)
        k = k_ref[pl.ds(off, tk), :]
        vt = vt_ref[:, pl.ds(off, tk)]
        pending = carried(vt_ref[:, pl.ds(off_prev, tk)])
        for c in range(ncb):
            m = m_ref[c]
            qt = qt_ref[:, c * cb:(c + 1) * cb]
            pc, pp, psc, pvt = pending.pop(0)
            s_parts, add = [], None
            for r in range(tk // ATT_KT):
                rows = slice(r * ATT_KT, (r + 1) * ATT_KT)
                half = ATT_KT // 2
                s_parts.append(jnp.dot(k[r * ATT_KT:r * ATT_KT + half], qt, preferred_element_type=F32))
                part = jnp.dot(pvt[:, rows], pp[rows], preferred_element_type=F32)
                add = part if add is None else add + part
                s_parts.append(jnp.dot(k[r * ATT_KT + half:(r + 1) * ATT_KT], qt, preferred_element_type=F32))
            acc_ref[pc] = (acc_ref[pc] + add) * psc
            s = jnp.concatenate(s_parts, axis=0)
            p = jnp.exp2(s - m).astype(BF16)
            cm = jnp.max(s, axis=0, keepdims=True)
            m_new = jnp.maximum(m, cm)
            jump = jnp.maximum(jump, cm - m)
            m_ref[c] = m_new
            pending.append((c, p, jnp.exp2(m - m_new), vt))
        for d, (_, pp, psc, _) in enumerate(pending):
            p_ref[d] = pp
            sc_ref[d] = psc
        return jump

    jump = lax.fori_loop(0, n_kv, step, jnp.zeros((1, cb), F32))
    for entry in carried(vt_ref[:, (n_kv - 1) * tk:]):
        drain(entry)
    return jump


def _exact_pass(qt_ref, k_ref, vt_ref, m_ref, acc_ref, s_ref, mp_ref, *, ncb, n_kv, tk, cb, depth):
    m_ref[...] = jnp.full(m_ref.shape, NEG_BIG, F32)
    acc_ref[...] = jnp.zeros(acc_ref.shape, F32)
    s_ref[...] = jnp.full(s_ref.shape, -jnp.inf, F32)
    mp_ref[...] = jnp.full(mp_ref.shape, NEG_BIG, F32)

    def carried(vt_prev):
        return [(ncb - depth + d, s_ref[d], mp_ref[d, 0], mp_ref[d, 1], vt_prev) for d in range(depth)]

    def drain(entry):
        pc, ps, pm_old, pm_new, pvt = entry
        acc_ref[pc] = _accumulate(ps, pvt, pm_old, pm_new, acc_ref[pc])

    def step(j, carry):
        off = pl.multiple_of(j * tk, tk)
        off_prev = pl.multiple_of(jnp.maximum(j - 1, 0) * tk, tk)
        k = k_ref[pl.ds(off, tk), :]
        vt = vt_ref[:, pl.ds(off, tk)]
        pending = carried(vt_ref[:, pl.ds(off_prev, tk)])
        for c in range(ncb):
            m_old = m_ref[c]
            s, m_new = _scores(k, qt_ref[:, c * cb:(c + 1) * cb], m_old)
            m_ref[c] = m_new
            drain(pending.pop(0))
            pending.append((c, s, m_old, m_new, vt))
        for d, (_, ps, pm_old, pm_new, _) in enumerate(pending):
            s_ref[d] = ps
            mp_ref[d, 0] = pm_old
            mp_ref[d, 1] = pm_new
        return carry

    lax.fori_loop(0, n_kv, step, 0)
    for entry in carried(vt_ref[:, (n_kv - 1) * tk:]):
        drain(entry)


def _attn_kernel(qt_ref, k_ref, vt_ref, o_ref, m_ref, acc_ref, s_ref, mp_ref, p_ref, sc_ref, *,
                 tq, tk, t, cb, depth):
    geom = dict(ncb=tq // cb, n_kv=t // tk, tk=tk, cb=cb, depth=depth)

    def write_out():
        for c in range(tq // cb):
            o_ref[c * cb:(c + 1) * cb, :] = _normalized_rows(acc_ref[c]).astype(o_ref.dtype)

    jump = _lagged_pass(qt_ref, k_ref, vt_ref, m_ref, acc_ref, p_ref, sc_ref, **geom)
    write_out()

    @pl.when(jnp.max(jump) > MAX_LAG_JUMP)
    def _():
        _exact_pass(qt_ref, k_ref, vt_ref, m_ref, acc_ref, s_ref, mp_ref, **geom)
        write_out()

    @pl.when(pl.program_id(1) == 0)
    def _():
        m0 = jnp.full((1, CTX), NEG_BIG, F32)
        s, m_new = _scores(k_ref[0:CTX, :], qt_ref[:, 0:CTX], m0)
        acc_c = _accumulate(s, vt_ref[:, 0:CTX], m0, m_new, jnp.zeros((VP, CTX), F32))
        o_ref[0:CTX, :] = _normalized_rows(acc_c).astype(o_ref.dtype)


def _attention(qt, k, vt, tq, tk, cb, depth):
    t = k.shape[1]
    assert depth <= tq // cb
    return pl.pallas_call(
        functools.partial(_attn_kernel, tq=tq, tk=tk, t=t, cb=cb, depth=depth),
        grid=(H_MLA, t // tq),
        in_specs=[pl.BlockSpec((None, HP, tq), lambda h, i: (h, 0, i)),
                  pl.BlockSpec((None, t, HP), lambda h, i: (h, 0, 0)),
                  pl.BlockSpec((None, VP, t), lambda h, i: (h, 0, 0))],
        out_specs=pl.BlockSpec((tq, HP), lambda h, i: (i, h)),
        out_shape=jax.ShapeDtypeStruct((t, H_MLA * HP), BF16),
        scratch_shapes=[pltpu.VMEM((tq // cb, 1, cb), F32), pltpu.VMEM((tq // cb, VP, cb), F32),
                        pltpu.VMEM((depth, tk, cb), F32), pltpu.VMEM((depth, 2, 1, cb), F32),
                        pltpu.VMEM((depth, tk, cb), BF16), pltpu.VMEM((depth, 1, cb), F32)],
        compiler_params=_cparams(2),
        name="attention",
    )(qt, k, vt)


def _split3(x):
    hi = x.astype(BF16)
    r = x - hi.astype(F32)
    mid = r.astype(BF16)
    lo = (r - mid.astype(F32)).astype(BF16)
    return hi, mid, lo


def _log_sigmoid(x):
    return jnp.minimum(x, 0.0) - jnp.log(1.0 + jnp.exp(-jnp.abs(x)))


def _mlstm_kernel(qf_ref, kf_ref, vtf_ref, gcf_ref, grf_ref, qb_ref, kb_ref, vtb_ref, gcb_ref, grb_ref,
                  bc_ref, br_ref, of_ref, ob_ref, ct_scr, m_scr, *, L):
    @pl.when(pl.program_id(0) == 0)
    def _():
        ct_scr[...] = jnp.zeros_like(ct_scr)
        m_scr[...] = jnp.zeros_like(m_scr)

    row = lax.broadcasted_iota(jnp.int32, (L, L), 0)
    col = lax.broadcasted_iota(jnp.int32, (L, L), 1)
    ones_rows = (lax.broadcasted_iota(jnp.int32, (MV, L), 0) == 0).astype(BF16)
    scans = ((0, qf_ref, kf_ref, vtf_ref, gcf_ref, grf_ref, of_ref),
             (1, qb_ref, kb_ref, vtb_ref, gcb_ref, grb_ref, ob_ref))

    pairs = []
    for rev, q_ref, k_ref, vt_ref, gc_ref, gr_ref, o_ref in scans:
        before = (row >= col) if rev else (row <= col)
        before_b = before.astype(BF16)
        after_b = ((row <= col) if rev else (row >= col)).astype(BF16)
        gc = gc_ref[...] + bc_ref[...]
        gr = gr_ref[...] + br_ref[...]
        lfc = _log_sigmoid(gc)
        lfr = _log_sigmoid(gr)
        b_col = sum(jnp.dot(after_b, p, preferred_element_type=F32) for p in _split3(lfc))
        b_row = sum(jnp.dot(p, before_b, preferred_element_type=F32) for p in _split3(lfr))
        for h in range(H_M):
            ci = 2 * H_M * rev + h
            cf = ci + H_M
            q = q_ref[:, h * MQK:(h + 1) * MQK] * (MQK ** -0.5)
            k = k_ref[:, h * MQK:(h + 1) * MQK]
            ct = ct_scr[rev, h]
            pairs.append(dict(
                rev=rev, h=h, before=before, o_ref=o_ref, k=k, ct=ct,
                g_c=gc[:, ci:ci + 1] - b_col[:, cf:cf + 1],
                b_r=b_row[cf:cf + 1, :], i_r=gr[ci:ci + 1, :],
                bl=jnp.sum(lfr[cf:cf + 1, :], axis=1, keepdims=True),
                m=m_scr[rev, h][0:1, 0:1],
                vt_aug=jnp.concatenate([vt_ref[h * MV:(h + 1) * MV, :], ones_rows], axis=0),
                kq=lax.dot_general(k, q, _NT, preferred_element_type=F32),
                cq=lax.dot_general(ct.astype(BF16), q, _NT, preferred_element_type=F32)))

    for p in pairs:
        pm = jnp.max(jnp.where(p["before"], p["g_c"], -jnp.inf), axis=0, keepdims=True)
        inter = p["b_r"] + p["m"]
        p["mj"] = jnp.maximum(inter, p["b_r"] + pm)
        w = jnp.exp(jnp.where(p["before"], p["g_c"] + (p["b_r"] - p["mj"]), -jnp.inf))
        p["sc"] = jnp.exp(inter - p["mj"])
        p["kqw"] = (p["kq"] * w).astype(BF16)

    for p in pairs:
        h = p["h"]
        num = jnp.dot(p["vt_aug"], p["kqw"], preferred_element_type=F32) + p["sc"] * p["cq"]
        den = jnp.maximum(jnp.abs(num[MV:MV + 1, :]), jnp.exp(-p["mj"]))
        p["o_ref"][:, h * MV:(h + 1) * MV] = (num[:MV, :] / den).T

    for p in pairs:
        wlog = p["bl"] - p["b_r"] + p["i_r"]
        m_new = jnp.maximum(p["bl"] + p["m"], jnp.max(wlog, axis=1, keepdims=True))
        ws = jnp.exp(wlog - m_new)
        sd = jnp.exp(p["bl"] + p["m"] - m_new)
        wvt = (p["vt_aug"].astype(F32) * ws).astype(BF16)
        ct_scr[p["rev"], p["h"]] = sd * p["ct"] + jnp.dot(wvt, p["k"], preferred_element_type=F32)
        m_scr[p["rev"], p["h"]] = jnp.broadcast_to(m_new, m_scr.shape[2:])


def _mlstm(mq, mk, mvt, pg, pg_t, bg_c, bg_r, L):
    t = mq.shape[0]
    nc = t // L
    fwd = lambda c: c
    bwd = lambda c: jnp.where(c == 0, 0, nc - c)

    def chunk_specs(idx):
        return [pl.BlockSpec((L, SEG_Q), lambda c: (idx(c), 0)),
                pl.BlockSpec((L, SEG_K), lambda c: (idx(c), 0)),
                pl.BlockSpec((SEG_V, L), lambda c: (0, idx(c))),
                pl.BlockSpec((L, LANES), lambda c: (idx(c), 0)),
                pl.BlockSpec((N_GATES, L), lambda c: (0, idx(c)))]

    chunk_args = (mq, mk, mvt, pg, pg_t)
    return pl.pallas_call(
        functools.partial(_mlstm_kernel, L=L),
        grid=(nc,),
        in_specs=chunk_specs(fwd) + chunk_specs(bwd)
        + [pl.BlockSpec((1, LANES), lambda c: (0, 0)), pl.BlockSpec((N_GATES, 1), lambda c: (0, 0))],
        out_specs=[pl.BlockSpec((L, H_M * MV), lambda c: (fwd(c), 0)),
                   pl.BlockSpec((L, H_M * MV), lambda c: (bwd(c), 0))],
        out_shape=[jax.ShapeDtypeStruct((t, H_M * MV), F32)] * 2,
        scratch_shapes=[pltpu.VMEM((2, H_M, 2 * MV, MQK), F32), pltpu.VMEM((2, H_M, SUBLANES, LANES), F32)],
        compiler_params=_cparams(1),
        name="mlstm",
    )(*chunk_args, *chunk_args, bg_c, bg_r)


def _router(logits):
    tm = logits.shape[0]
    lane = lax.broadcasted_iota(jnp.int32, (tm, LANES), 1)
    big = jnp.int32(LANES)
    neg = -jnp.inf
    is_g = lane < N_GROUPS
    lg = jnp.where(is_g, logits, neg)
    gmax = jnp.max(lg, axis=-1, keepdims=True)
    grp = jnp.min(jnp.where(lg == gmax, lane, big), axis=-1, keepdims=True)
    p_grp = 1.0 / jnp.sum(jnp.where(is_g, jnp.exp(lg - gmax), 0.0), axis=-1, keepdims=True)
    e_lo = N_GROUPS + grp * EPG
    in_grp = (lane >= e_lo) & (lane < e_lo + EPG)
    le = jnp.where(in_grp, logits, neg)
    l1 = jnp.max(le, axis=-1, keepdims=True)
    i1 = jnp.min(jnp.where(le == l1, lane, big), axis=-1, keepdims=True)
    le2 = jnp.where(lane == i1, neg, le)
    l2 = jnp.max(le2, axis=-1, keepdims=True)
    i2 = jnp.min(jnp.where(le2 == l2, lane, big), axis=-1, keepdims=True)
    r = jnp.exp(l2 - l1)
    w1 = p_grp / (1.0 + r)
    w2 = w1 * r
    return jnp.where(lane == i1, w1, 0.0) + jnp.where(lane == i2, w2, 0.0), grp


def _merge_kernel(att_ref, hf_ref, hb_ref, po_ref, pm_ref, x_ref, mod_ref, gmh_ref, l1g_ref, l1b_ref,
                  wa_ref, wm_ref, wo_ref, wr_ref, br_ref, x1_ref, hm_ref, plan_ref, cnt_ref, *, tm):
    is_ctx = _is_ctx_rows(tm)
    y_mla = jnp.dot(att_ref[...], wa_ref[...], preferred_element_type=F32)
    hs = hf_ref[...] + hb_ref[...]
    parts = []
    for h in range(H_M):
        sl = slice(h * MV, (h + 1) * MV)
        parts.append(_ln(hs[:, sl]))
    hn = jnp.concatenate(parts, axis=1) * gmh_ref[...] * po_ref[...].astype(F32)
    y_ml = jnp.dot(hn.astype(BF16), wm_ref[...], preferred_element_type=F32)
    g_a = pm_ref[:, :D].astype(F32)
    g_b = pm_ref[:, D:].astype(F32)
    y = jnp.dot((g_a * y_mla + g_b * y_ml).astype(BF16), wo_ref[...], preferred_element_type=F32)
    x1 = _ln(ALPHA * x_ref[...] + _mod(mod_ref, 2, is_ctx) * y) * l1g_ref[...] + l1b_ref[...]
    x1_ref[...] = x1
    hm = _ln(x1) * (1.0 + _mod(mod_ref, 4, is_ctx)) + _mod(mod_ref, 3, is_ctx)
    hm_ref[...] = hm.astype(BF16)
    h_hi = hm.astype(BF16)
    h_lo = (hm - h_hi.astype(F32)).astype(BF16)
    logits = (jnp.dot(h_hi, wr_ref[0], preferred_element_type=F32)
              + jnp.dot(h_hi, wr_ref[1], preferred_element_type=F32)
              + jnp.dot(h_lo, wr_ref[0], preferred_element_type=F32)) + br_ref[...]
    comb, grp = _router(logits)
    lane = lax.broadcasted_iota(jnp.int32, (tm, LANES), 1)
    in_group = lane == grp
    earlier = (lax.broadcasted_iota(jnp.int32, (tm, tm), 0)
               > lax.broadcasted_iota(jnp.int32, (tm, tm), 1)).astype(BF16)
    n_before = jnp.dot(earlier, in_group.astype(BF16), preferred_element_type=F32)
    rank = jnp.sum(jnp.where(in_group, n_before, 0.0), axis=-1, keepdims=True)
    plan_ref[...] = (comb + jnp.where(lane == PLAN_GROUP, grp.astype(F32), 0.0)
                     + jnp.where(lane == PLAN_RANK, rank, 0.0))
    cnt_ref[...] = jnp.broadcast_to(jnp.sum(in_group.astype(F32), axis=0, keepdims=True), cnt_ref.shape)


def _merge(att, hf, hb, po, pm, x_all, mod, g_mh, l1g, l1b, wa, wm, wo, wr, br, tm):
    t = x_all.shape[0]
    row = lambda n: pl.BlockSpec((tm, n), lambda i: (i, 0))
    full = lambda a: pl.BlockSpec(a.shape, lambda i: (0,) * a.ndim)
    return pl.pallas_call(
        functools.partial(_merge_kernel, tm=tm),
        grid=(t // tm,),
        in_specs=[row(H_MLA * HP), row(H_M * MV), row(H_M * MV), row(SEG_PO), row(SEG_PM), row(D),
                  full(mod), full(g_mh), full(l1g), full(l1b), full(wa), full(wm), full(wo), full(wr), full(br)],
        out_specs=[row(D), row(D), row(LANES), pl.BlockSpec((None, SUBLANES, LANES), lambda i: (i, 0, 0))],
        out_shape=[jax.ShapeDtypeStruct((t, D), F32), jax.ShapeDtypeStruct((t, D), BF16),
                   jax.ShapeDtypeStruct((t, LANES), F32), jax.ShapeDtypeStruct((t // tm, SUBLANES, LANES), F32)],
        compiler_params=_cparams(1),
        name="merge",
    )(att, hf, hb, po, pm, x_all, mod, g_mh, l1g, l1b, wa, wm, wo, wr, br)


def _moe_kernel(cnt_ref, layer_ref, hm_ref, plan_ref, plant_ref, x1_ref, mod_ref, l2g_ref, l2b_ref, wg_ref, wu_ref, wd_ref,
                o_ref, *, tm, tp, slot_sizes):
    i = pl.program_id(0)
    grp = pl.program_id(1)
    n_sub = tm // tp

    @pl.when(grp == 0)
    def _():
        o_ref[...] = jnp.zeros_like(o_ref)

    n_tok = cnt_ref[i * n_sub, grp]
    for j in range(1, n_sub):
        n_tok = jnp.maximum(n_tok, cnt_ref[i * n_sub + j, grp])

    def rounds_of(slots):
        return functools.partial(_moe_round, hm_ref, plan_ref, plant_ref, wg_ref, wu_ref, wd_ref, o_ref,
                                 grp=grp, n_sub=n_sub, tp=tp, slots=slots)

    big = slot_sizes[-1]
    tails = slot_sizes[:-1]
    n_full = lax.div(n_tok, big) + (lax.rem(n_tok, big) > (tails[-1] if tails else 0)).astype(jnp.int32)
    rest = jnp.maximum(n_tok - n_full * big, 0)
    full_round = rounds_of(big)
    lax.fori_loop(0, n_full, lambda r, carry: full_round(r * big, carry), 0)
    lower = 0
    for slots in tails:
        @pl.when((rest > lower) & (rest <= slots))
        def _(tail_round=rounds_of(slots)):
            tail_round(n_full * big, 0)

        lower = slots

    @pl.when(grp == N_GROUPS - 1)
    def _():
        is_ctx = _is_ctx_rows(tm)
        z = ALPHA * x1_ref[...] + _mod(mod_ref, 5, is_ctx) * o_ref[...]
        o_ref[...] = _ln(z) * l2g_ref[...] + l2b_ref[...]


def _moe_round(hm_ref, plan_ref, plant_ref, wg_ref, wu_ref, wd_ref, o_ref, first, carry, *, grp, n_sub, tp, slots):
    grp_f = grp.astype(F32)
    base = first.astype(F32)
    slot_rows = lax.broadcasted_iota(jnp.int32, (slots, tp), 0).astype(F32)
    slot_cols = lax.broadcasted_iota(jnp.int32, (tp, slots), 1).astype(F32)
    lane = lax.broadcasted_iota(jnp.int32, (n_sub * slots, LANES), 1)
    xs, cs = [], []
    for j in range(n_sub):
        rows = slice(j * tp, (j + 1) * tp)
        in_grp = plant_ref[PLAN_GROUP:PLAN_GROUP + 1, rows] == grp_f
        rank = plant_ref[PLAN_RANK:PLAN_RANK + 1, rows]
        gather = ((slot_rows == rank - base) & in_grp).astype(BF16)
        xs.append(jnp.dot(gather, hm_ref[rows, :], preferred_element_type=F32).astype(BF16))
        cs.append(sum(jnp.dot(gather, p, preferred_element_type=F32)
                      for p in _split3(plan_ref[rows, :])))
    xs = jnp.concatenate(xs, axis=0)
    cs = jnp.concatenate(cs, axis=0)
    hid = []
    for g in range(EPG):
        c_e = jnp.sum(jnp.where(lane == N_GROUPS + grp * EPG + g, cs, 0.0), axis=-1, keepdims=True)
        a = jnp.dot(xs, wg_ref[g], preferred_element_type=F32)
        u = jnp.dot(xs, wu_ref[g], preferred_element_type=F32)
        hid.append((a * _sigmoid(a) * u * c_e).astype(BF16))
    ys = jnp.dot(jnp.concatenate(hid, axis=1), wd_ref[...].reshape(EPG * D_EXP, D),
                 preferred_element_type=F32)
    ys = ys.astype(BF16)
    for j in range(n_sub):
        rows = slice(j * tp, (j + 1) * tp)
        mine = slice(j * slots, (j + 1) * slots)
        in_grp = plan_ref[rows, PLAN_GROUP:PLAN_GROUP + 1] == grp_f
        rank = plan_ref[rows, PLAN_RANK:PLAN_RANK + 1]
        scatter = ((slot_cols == rank - base) & in_grp).astype(BF16)
        o_ref[rows, :] += jnp.dot(scatter, ys[mine], preferred_element_type=F32)
    return carry


def _moe(counts, hm, plan, plan_t, x1, mod, l2g, l2b, wg, wu, wd, layer, tp):
    t = hm.shape[0]
    tm = tp * (MOE_PLAN_TILES if (t // tp) % MOE_PLAN_TILES == 0 else 1)
    row = lambda n, **kw: pl.BlockSpec((tm, n), lambda i, e, cnt, lay: (i, 0), **kw)
    full = lambda a: pl.BlockSpec(a.shape, lambda i, e, cnt, lay: (0,) * a.ndim)
    group_of = lambda a: pl.BlockSpec((None, EPG) + a.shape[2:], lambda i, e, cnt, lay: (lay[0], e, 0, 0))
    return pl.pallas_call(
        functools.partial(_moe_kernel, tm=tm, tp=tp, slot_sizes=_moe_slot_sizes(tp)),
        grid_spec=pltpu.PrefetchScalarGridSpec(
            num_scalar_prefetch=2,
            grid=(t // tm, N_GROUPS),
            in_specs=[row(D), row(LANES), pl.BlockSpec((2, tm), lambda i, e, cnt, lay: (0, i)),
                      row(D, pipeline_mode=pl.Buffered(1)),
                      full(mod), full(l2g), full(l2b), group_of(wg), group_of(wu), group_of(wd)],
            out_specs=row(D)),
        out_shape=jax.ShapeDtypeStruct((t, D), F32),
        compiler_params=_cparams(2, VMEM_LIMIT_MOE),
        name="moe",
    )(counts, jnp.full((1,), layer, jnp.int32), hm, plan, plan_t, x1, mod, l2g, l2b, wg, wu, wd)


def _rot_cols(w):
    a1, a2, b1, b2 = jnp.split(w, 4, axis=-1)
    return jnp.concatenate([-a2, a1, -b2, b1], axis=-1)


def _place_rope(w):
    return jnp.pad(w, ((0, 0), (QK_NOPE, HP - QK_NOPE - QK_ROPE)))


def _rope_tables(t):
    rows = (t - CTX) // GRID_W
    row, col = jnp.meshgrid(jnp.arange(rows, dtype=F32), jnp.arange(GRID_W, dtype=F32), indexing="ij")
    row, col = row.reshape(-1), col.reshape(-1)
    half = QK_ROPE // 2
    inv = ROPE_BASE ** (-jnp.arange(0, half, 2, dtype=F32) / half)
    ar, ac = row[:, None] * inv, col[:, None] * inv
    ang = jnp.concatenate([ar, ar, ac, ac], axis=-1)
    ang = jnp.concatenate([jnp.zeros((CTX, QK_ROPE), F32), ang], axis=0)
    pad = ((0, 0), (QK_NOPE, HP - QK_NOPE - QK_ROPE))
    cos = jnp.pad(jnp.cos(ang), pad) + jnp.pad(jnp.ones((t, QK_NOPE), F32), ((0, 0), (0, HP - QK_NOPE)))
    sin = jnp.pad(jnp.sin(ang), pad)
    return cos, sin, cos.T, sin.T


def _layer_weights(l, w_in, w_uq, w_uk, w_uv, w_bo_mla, w_bo_mlstm, w_out, w_rg, b_rg, w_re, b_re):
    wi = w_in[l]
    o = 0
    segs = []
    for n in (Q_LORA, KV_LORA, QK_ROPE, SEG_Q, SEG_K, SEG_V, SEG_PO, N_GATES, SEG_PM):
        segs.append(wi[:, o:o + n])
        o += n
    s_pqd, s_ckv, s_kr, s_q, s_k, s_v, s_po, s_pg, s_pm = segs
    w_in_p = jnp.concatenate(
        [s_pqd, s_ckv, _place_rope(s_kr), _place_rope(_rot_cols(s_kr)), s_q, s_k, s_po,
         jnp.pad(s_pg, ((0, 0), (0, SEG_PG - N_GATES))), s_pm], axis=1).astype(BF16)

    uq = w_uq[l].reshape(Q_LORA, H_MLA, QK_NOPE + QK_ROPE)
    padh = ((0, 0), (0, 0), (0, HP - QK_NOPE - QK_ROPE))
    wq = jnp.pad(uq, padh).reshape(Q_LORA, H_MLA * HP)
    wqr = jnp.pad(jnp.concatenate([jnp.zeros_like(uq[..., :QK_NOPE]), _rot_cols(uq[..., QK_NOPE:])], axis=-1),
                  padh).reshape(Q_LORA, H_MLA * HP)
    padv = ((0, 0), (0, 0), (0, HP - V_HEAD))
    wk = jnp.pad(w_uk[l].reshape(KV_LORA, H_MLA, QK_NOPE), padv).reshape(KV_LORA, H_MLA * HP)
    wv = jnp.pad(w_uv[l].reshape(KV_LORA, H_MLA, V_HEAD), ((0, 0), (0, 0), (0, VP - V_HEAD))).reshape(KV_LORA, H_MLA * VP)
    wa = jnp.pad(w_bo_mla[l].reshape(H_MLA, V_HEAD, D), ((0, 0), (0, HP - V_HEAD), (0, 0))).reshape(H_MLA * HP, D)

    wr = jnp.pad(jnp.concatenate([w_rg[l], w_re[l]], axis=1), ((0, 0), (0, LANES - N_GROUPS - N_EXP)))
    wr_hi = wr.astype(BF16)
    wr_lo = (wr - wr_hi.astype(F32)).astype(BF16)
    br = jnp.pad(jnp.concatenate([b_rg[l], b_re[l]]), (0, LANES - N_GROUPS - N_EXP)).reshape(1, LANES)
    return dict(
        w_in=w_in_p, w_mvt=s_v.T.astype(BF16), wqt=wq.T.astype(BF16), wqrt=wqr.T.astype(BF16), wk=wk.astype(BF16), wvt=wv.T.astype(BF16),
        wa=wa.astype(BF16), wm=w_bo_mlstm[l].astype(BF16), wo=w_out[l].astype(BF16),
        wr=jnp.stack([wr_hi, wr_lo]), br=br)


def kernel(x, c, ctx, c_ctx, w_ada, b_ada, w_in, b_gates, w_uq, w_uk, w_uv, g_qn, g_kvn, g_mh, w_bo_mla,
           w_bo_mlstm, w_out, ln1_g, ln1_b, w_rg, b_rg, w_re, b_re, w_e_gate, w_e_up, w_e_down, ln2_g, ln2_b):
    assert x.shape[0] == 1 and c.shape[0] == 1 and ctx.shape[1] == CTX
    x_all = jnp.concatenate([ctx[0], x[0]], axis=0)
    t = x_all.shape[0]
    tm = ROW_TILE if t % ROW_TILE == 0 else M_CHUNK
    tk = ATT_TK if t % ATT_TK == 0 else M_CHUNK
    assert t % tm == 0 and t % ATT_TQ == 0 and t % tk == 0 and t % M_CHUNK == 0

    cc = jnp.pad(jnp.concatenate([c, c_ctx[None]], axis=0), ((0, SUBLANES - 2), (0, 0)))
    mods = _ada(cc, w_ada, b_ada)
    tabs = _rope_tables(t)
    row2 = lambda a: a.reshape(1, -1)
    wg_all, wu_all, wd_all = w_e_gate.astype(BF16), w_e_up.astype(BF16), w_e_down.astype(BF16)

    for l in range(DEPTH):
        w = _layer_weights(l, w_in, w_uq, w_uk, w_uv, w_bo_mla, w_bo_mlstm, w_out, w_rg, b_rg, w_re, b_re)
        mod = mods[l]
        pqd, ckv, kr, mq, mk, po, pg, pm, mvt = _inproj(x_all, mod, w["w_in"], w["w_mvt"], tm)
        qt, kk, vt = _mla_prep(pqd, ckv, kr, tabs, row2(g_qn[l]), row2(g_kvn[l]),
                               w["wqt"], w["wqrt"], w["wk"], w["wvt"], tm)
        att = _attention(qt, kk, vt, ATT_TQ, tk, ATT_CB, ATT_DEPTH)
        pg_t = pg[:, :N_GATES].T
        bg_c = jnp.pad(b_gates[l], (0, LANES - N_GATES)).reshape(1, LANES)
        bg_r = b_gates[l].reshape(N_GATES, 1)
        hf, hb = _mlstm(mq, mk, mvt, pg, pg_t, bg_c, bg_r, M_CHUNK)
        x1, hm, plan, cnt = _merge(att, hf, hb, po, pm, x_all, mod, row2(g_mh[l]), row2(ln1_g[l]), row2(ln1_b[l]),
                                   w["wa"], w["wm"], w["wo"], w["wr"], w["br"], tm)
        counts = cnt[:, 0, :N_GROUPS].astype(jnp.int32)
        plan_t = plan[:, :2].T
        x_all = _moe(counts, hm, plan, plan_t, x1, mod, row2(ln2_g[l]), row2(ln2_b[l]),
                     wg_all, wu_all, wd_all, l, tm)
    return x_all[CTX:][None]
```

```python
import functools
import math

import jax
import jax.numpy as jnp
from jax import lax
from jax.experimental import pallas as pl
from jax.experimental.pallas import tpu as pltpu

F32 = jnp.float32
BF16 = jnp.bfloat16

D = 1024
DEPTH = 4
GRID_W = 64
CTX = 256
H_MLA = 8
QK_NOPE = 64
QK_ROPE = 32
V_HEAD = 64
Q_LORA = 384
KV_LORA = 256
ROPE_BASE = 10000.0
MLA_SCALE = (QK_NOPE + QK_ROPE) ** -0.5
H_M = 4
MQK = 64
MV = 128
N_GROUPS = 4
EPG = 8
N_EXP = N_GROUPS * EPG
D_EXP = 256
ALPHA = (2 * DEPTH) ** 0.25
LN_EPS = 1e-6

LANES = 128
SUBLANES = 8
NEG_BIG = -1e30
N_GATES = 4 * H_M
ADA_TN = 1536
VMEM_LIMIT = 56 * 1024 * 1024
VMEM_LIMIT_MOE = 58 * 1024 * 1024

HP = LANES
ONES_ROW = V_HEAD
BF16_ROWS = 16
VP = (V_HEAD + 1 + BF16_ROWS - 1) // BF16_ROWS * BF16_ROWS
QSCALE = MLA_SCALE * math.log2(math.e)

SEG_PQD = Q_LORA
SEG_CKV = KV_LORA
SEG_KR = 2 * HP
SEG_Q = H_M * MQK
SEG_K = H_M * MQK
SEG_V = H_M * MV
SEG_PO = H_M * MV
SEG_PG = LANES
SEG_PM = 2 * D
IN_SEGS = (SEG_PQD, SEG_CKV, SEG_KR, SEG_Q, SEG_K, SEG_PO, SEG_PG, SEG_PM)
IN_DTYPES = (F32, F32, F32, BF16, BF16, BF16, F32, BF16)
IN_GATES = (False, False, False, False, False, True, False, True)
NP_IN = sum(IN_SEGS)

ROW_TILE = 640
ATT_TQ = 3328
ATT_CB = 256
ATT_TK = 3328
ATT_KT = 256
MAX_LAG_JUMP = 64.0
ATT_DEPTH = 1
M_CHUNK = 256
MOE_SLOT_STEP = 64
MOE_SLOT_MAX = 256
MOE_PLAN_TILES = 2
PLAN_GROUP = 0
PLAN_RANK = 1


def _moe_slot_sizes(tm):
    return tuple(s for s in range(MOE_SLOT_STEP, MOE_SLOT_MAX + 1, MOE_SLOT_STEP) if s <= tm)

_NT = (((1,), (1,)), ((), ()))


def _cparams(n_grid, vmem_limit=VMEM_LIMIT):
    return pltpu.CompilerParams(dimension_semantics=("arbitrary",) * n_grid, vmem_limit_bytes=vmem_limit)


def _ln(x):
    mu = jnp.mean(x, axis=-1, keepdims=True)
    xc = x - mu
    var = jnp.mean(xc * xc, axis=-1, keepdims=True)
    return xc * lax.rsqrt(var + LN_EPS)


def _rms(x, g):
    return x * lax.rsqrt(jnp.mean(x * x, axis=-1, keepdims=True) + LN_EPS) * g


def _sigmoid(x):
    return 1.0 / (1.0 + jnp.exp(-x))


def _is_ctx_rows(tm):
    rows = pl.program_id(0) * tm + lax.broadcasted_iota(jnp.int32, (tm, 1), 0)
    return rows < CTX


def _mod(mod_ref, k, is_ctx):
    lat = mod_ref[0:1, k * D:(k + 1) * D]
    ctx = mod_ref[1:2, k * D:(k + 1) * D]
    return jnp.where(is_ctx, ctx, lat)


def _ada_kernel(c_ref, w_ref, b_ref, o_ref):
    c = c_ref[...]
    s = c * _sigmoid(c)
    o_ref[...] = jnp.dot(s, w_ref[...], preferred_element_type=F32,
                         precision=lax.Precision.HIGHEST) + b_ref[...]


def _ada(cc, w_ada, b_ada):
    tn = ADA_TN
    n = 6 * D
    return pl.pallas_call(
        _ada_kernel,
        grid=(DEPTH, n // tn),
        in_specs=[pl.BlockSpec((SUBLANES, D), lambda l, j: (0, 0)),
                  pl.BlockSpec((None, D, tn), lambda l, j: (l, 0, j)),
                  pl.BlockSpec((None, 1, tn), lambda l, j: (l, 0, j))],
        out_specs=pl.BlockSpec((None, SUBLANES, tn), lambda l, j: (l, 0, j)),
        out_shape=jax.ShapeDtypeStruct((DEPTH, SUBLANES, n), F32),
        compiler_params=_cparams(2),
        name="ada",
    )(cc, w_ada, b_ada.reshape(DEPTH, 1, n))


def _inproj_kernel(x_ref, mod_ref, w_ref, wvt_ref, *o_refs, tm):
    is_ctx = _is_ctx_rows(tm)
    h = _ln(x_ref[...]) * (1.0 + _mod(mod_ref, 1, is_ctx)) + _mod(mod_ref, 0, is_ctx)
    hb = h.astype(BF16)
    off = 0
    for o, is_gate in zip(o_refs[:-1], IN_GATES):
        n = o.shape[-1]
        p = jnp.dot(hb, w_ref[:, off:off + n], preferred_element_type=F32)
        o[...] = (_sigmoid(p) if is_gate else p).astype(o.dtype)
        off += n
    o_refs[-1][...] = lax.dot_general(wvt_ref[...], hb, _NT, preferred_element_type=F32).astype(BF16)


def _inproj(x_all, mod, w_in_p, w_vt, tm):
    t = x_all.shape[0]
    return pl.pallas_call(
        functools.partial(_inproj_kernel, tm=tm),
        grid=(t // tm,),
        in_specs=[pl.BlockSpec((tm, D), lambda i: (i, 0)),
                  pl.BlockSpec((SUBLANES, 6 * D), lambda i: (0, 0)),
                  pl.BlockSpec((D, NP_IN), lambda i: (0, 0)),
                  pl.BlockSpec((SEG_V, D), lambda i: (0, 0))],
        out_specs=[pl.BlockSpec((tm, n), lambda i: (i, 0)) for n in IN_SEGS]
        + [pl.BlockSpec((SEG_V, tm), lambda i: (0, i))],
        out_shape=[jax.ShapeDtypeStruct((t, n), dt) for n, dt in zip(IN_SEGS, IN_DTYPES)]
        + [jax.ShapeDtypeStruct((SEG_V, t), BF16)],
        compiler_params=_cparams(1),
        name="inproj",
    )(x_all, mod, w_in_p, w_vt)


def _mla_prep_kernel(pqd_ref, ckv_ref, kr_ref, cos_ref, sin_ref, cost_ref, sint_ref, gq_ref, gk_ref,
                     wqt_ref, wqrt_ref, wk_ref, wvt_ref, qt_ref, k_ref, vt_ref, *, tm):
    qn = _rms(pqd_ref[...], gq_ref[...]).astype(BF16)
    qa = lax.dot_general(wqt_ref[...], qn, _NT, preferred_element_type=F32)
    qb = lax.dot_general(wqrt_ref[...], qn, _NT, preferred_element_type=F32)
    cost = cost_ref[...]
    sint = sint_ref[...]
    ckv = _rms(ckv_ref[...], gk_ref[...]).astype(BF16)
    kn = jnp.dot(ckv, wk_ref[...], preferred_element_type=F32)
    kr = kr_ref[:, :HP] * cos_ref[...] + kr_ref[:, HP:] * sin_ref[...]
    vt = lax.dot_general(wvt_ref[...], ckv, _NT, preferred_element_type=F32)
    is_ones_row = lax.broadcasted_iota(jnp.int32, (VP, tm), 0) == ONES_ROW
    for h in range(H_MLA):
        sl = slice(h * HP, (h + 1) * HP)
        rot = jnp.concatenate([jnp.zeros((QK_NOPE, tm), F32), qb[h * QK_ROPE:(h + 1) * QK_ROPE],
                               jnp.zeros((HP - QK_NOPE - QK_ROPE, tm), F32)], axis=0)
        qt_ref[h] = ((qa[sl] * cost + rot * sint) * QSCALE).astype(BF16)
        k_ref[h] = (kn[:, sl] + kr).astype(BF16)
        vt_ref[h] = jnp.where(is_ones_row, 1.0, vt[h * VP:(h + 1) * VP]).astype(BF16)


def _mla_prep(pqd, ckv, kr, tabs, g_qn, g_kvn, wqt, wqrt, wk, wvt, tm):
    t = pqd.shape[0]
    cos, sin, cost, sint = tabs
    row = lambda n: pl.BlockSpec((tm, n), lambda i: (i, 0))
    col = lambda n: pl.BlockSpec((n, tm), lambda i: (0, i))
    full = lambda a: pl.BlockSpec(a.shape, lambda i: (0,) * a.ndim)
    return pl.pallas_call(
        functools.partial(_mla_prep_kernel, tm=tm),
        grid=(t // tm,),
        in_specs=[row(Q_LORA), row(KV_LORA), row(SEG_KR), row(HP), row(HP), col(HP), col(HP),
                  full(g_qn), full(g_kvn), full(wqt), full(wqrt), full(wk), full(wvt)],
        out_specs=[pl.BlockSpec((H_MLA, HP, tm), lambda i: (0, 0, i)),
                   pl.BlockSpec((H_MLA, tm, HP), lambda i: (0, i, 0)),
                   pl.BlockSpec((H_MLA, VP, tm), lambda i: (0, 0, i))],
        out_shape=[jax.ShapeDtypeStruct((H_MLA, HP, t), BF16),
                   jax.ShapeDtypeStruct((H_MLA, t, HP), BF16),
                   jax.ShapeDtypeStruct((H_MLA, VP, t), BF16)],
        compiler_params=_cparams(1),
        name="mla_prep",
    )(pqd, ckv, kr, cos, sin, cost, sint, g_qn, g_kvn, wqt, wqrt, wk, wvt)


def _scores(k, qt, m):
    s = jnp.dot(k, qt, preferred_element_type=F32)
    return s, jnp.maximum(m, jnp.max(s, axis=0, keepdims=True))


def _accumulate(s, vt, m, m_new, acc):
    p = jnp.exp2(s - m_new).astype(BF16)
    return jnp.exp2(m - m_new) * acc + jnp.dot(vt, p, preferred_element_type=F32)


def _normalized_rows(acc):
    o = acc / acc[ONES_ROW:ONES_ROW + 1, :]
    return jnp.concatenate([o, jnp.zeros((HP - VP, o.shape[1]), F32)], axis=0).T


def _lagged_pass(qt_ref, k_ref, vt_ref, m_ref, acc_ref, p_ref, sc_ref, *, ncb, n_kv, tk, cb, depth):
    k0 = k_ref[0:BF16_ROWS, :]
    for c in range(ncb):
        s0 = jnp.dot(k0, qt_ref[:, c * cb:(c + 1) * cb], preferred_element_type=F32)
        m_ref[c] = jnp.max(s0, axis=0, keepdims=True)
    acc_ref[...] = jnp.zeros(acc_ref.shape, F32)
    p_ref[...] = jnp.zeros(p_ref.shape, BF16)
    sc_ref[...] = jnp.ones(sc_ref.shape, F32)

    def carried(vt_prev):
        return [(ncb - depth + d, p_ref[d], sc_ref[d], vt_prev) for d in range(depth)]

    def drain(entry):
        pc, pp, psc, pvt = entry
        acc_ref[pc] = (acc_ref[pc] + jnp.dot(pvt, pp, preferred_element_type=F32)) * psc

    def step(j, jump):
        off = pl.multiple_of(j * tk, tk)
        off_prev = pl.multiple_of(jnp.maximum(j - 1, 0) * tk, tk)
        k = k_ref[pl.ds(off, tk), :]
        vt = vt_ref[:, pl.ds(off, tk)]
        pending = carried(vt_ref[:, pl.ds(off_prev, tk)])
        for c in range(ncb):
            m = m_ref[c]
            qt = qt_ref[:, c * cb:(c + 1) * cb]
            pc, pp, psc, pvt = pending.pop(0)
            s_parts, add = [], None
            for r in range(tk // ATT_KT):
                rows = slice(r * ATT_KT, (r + 1) * ATT_KT)
                half = ATT_KT // 2
                s_parts.append(jnp.dot(k[r * ATT_KT:r * ATT_KT + half], qt, preferred_element_type=F32))
                part = jnp.dot(pvt[:, rows], pp[rows], preferred_element_type=F32)
                add = part if add is None else add + part
                s_parts.append(jnp.dot(k[r * ATT_KT + half:(r + 1) * ATT_KT], qt, preferred_element_type=F32))
            acc_ref[pc] = (acc_ref[pc] + add) * psc
            s = jnp.concatenate(s_parts, axis=0)
            p = jnp.exp2(s - m).astype(BF16)
            cm = jnp.max(s, axis=0, keepdims=True)
            m_new = jnp.maximum(m, cm)
            jump = jnp.maximum(jump, cm - m)
            m_ref[c] = m_new
            pending.append((c, p, jnp.exp2(m - m_new), vt))
        for d, (_, pp, psc, _) in enumerate(pending):
            p_ref[d] = pp
            sc_ref[d] = psc
        return jump

    jump = lax.fori_loop(0, n_kv, step, jnp.zeros((1, cb), F32))
    for entry in carried(vt_ref[:, (n_kv - 1) * tk:]):
        drain(entry)
    return jump


def _exact_pass(qt_ref, k_ref, vt_ref, m_ref, acc_ref, s_ref, mp_ref, *, ncb, n_kv, tk, cb, depth):
    m_ref[...] = jnp.full(m_ref.shape, NEG_BIG, F32)
    acc_ref[...] = jnp.zeros(acc_ref.shape, F32)
    s_ref[...] = jnp.full(s_ref.shape, -jnp.inf, F32)
    mp_ref[...] = jnp.full(mp_ref.shape, NEG_BIG, F32)

    def carried(vt_prev):
        return [(ncb - depth + d, s_ref[d], mp_ref[d, 0], mp_ref[d, 1], vt_prev) for d in range(depth)]

    def drain(entry):
        pc, ps, pm_old, pm_new, pvt = entry
        acc_ref[pc] = _accumulate(ps, pvt, pm_old, pm_new, acc_ref[pc])

    def step(j, carry):
        off = pl.multiple_of(j * tk, tk)
        off_prev = pl.multiple_of(jnp.maximum(j - 1, 0) * tk, tk)
        k = k_ref[pl.ds(off, tk), :]
        vt = vt_ref[:, pl.ds(off, tk)]
        pending = carried(vt_ref[:, pl.ds(off_prev, tk)])
        for c in range(ncb):
            m_old = m_ref[c]
            s, m_new = _scores(k, qt_ref[:, c * cb:(c + 1) * cb], m_old)
            m_ref[c] = m_new
            drain(pending.pop(0))
            pending.append((c, s, m_old, m_new, vt))
        for d, (_, ps, pm_old, pm_new, _) in enumerate(pending):
            s_ref[d] = ps
            mp_ref[d, 0] = pm_old
            mp_ref[d, 1] = pm_new
        return carry

    lax.fori_loop(0, n_kv, step, 0)
    for entry in carried(vt_ref[:, (n_kv - 1) * tk:]):
        drain(entry)


def _attn_kernel(qt_ref, k_ref, vt_ref, o_ref, m_ref, acc_ref, s_ref, mp_ref, p_ref, sc_ref, *,
                 tq, tk, t, cb, depth):
    geom = dict(ncb=tq // cb, n_kv=t // tk, tk=tk, cb=cb, depth=depth)

    def write_out():
        for c in range(tq // cb):
            o_ref[c * cb:(c + 1) * cb, :] = _normalized_rows(acc_ref[c]).astype(o_ref.dtype)

    jump = _lagged_pass(qt_ref, k_ref, vt_ref, m_ref, acc_ref, p_ref, sc_ref, **geom)
    write_out()

    @pl.when(jnp.max(jump) > MAX_LAG_JUMP)
    def _():
        _exact_pass(qt_ref, k_ref, vt_ref, m_ref, acc_ref, s_ref, mp_ref, **geom)
        write_out()

    @pl.when(pl.program_id(1) == 0)
    def _():
        m0 = jnp.full((1, CTX), NEG_BIG, F32)
        s, m_new = _scores(k_ref[0:CTX, :], qt_ref[:, 0:CTX], m0)
        acc_c = _accumulate(s, vt_ref[:, 0:CTX], m0, m_new, jnp.zeros((VP, CTX), F32))
        o_ref[0:CTX, :] = _normalized_rows(acc_c).astype(o_ref.dtype)


def _attention(qt, k, vt, tq, tk, cb, depth):
    t = k.shape[1]
    assert depth <= tq // cb
    return pl.pallas_call(
        functools.partial(_attn_kernel, tq=tq, tk=tk, t=t, cb=cb, depth=depth),
        grid=(H_MLA, t // tq),
        in_specs=[pl.BlockSpec((None, HP, tq), lambda h, i: (h, 0, i)),
                  pl.BlockSpec((None, t, HP), lambda h, i: (h, 0, 0)),
                  pl.BlockSpec((None, VP, t), lambda h, i: (h, 0, 0))],
        out_specs=pl.BlockSpec((tq, HP), lambda h, i: (i, h)),
        out_shape=jax.ShapeDtypeStruct((t, H_MLA * HP), BF16),
        scratch_shapes=[pltpu.VMEM((tq // cb, 1, cb), F32), pltpu.VMEM((tq // cb, VP, cb), F32),
                        pltpu.VMEM((depth, tk, cb), F32), pltpu.VMEM((depth, 2, 1, cb), F32),
                        pltpu.VMEM((depth, tk, cb), BF16), pltpu.VMEM((depth, 1, cb), F32)],
        compiler_params=_cparams(2),
        name="attention",
    )(qt, k, vt)


def _split3(x):
    hi = x.astype(BF16)
    r = x - hi.astype(F32)
    mid = r.astype(BF16)
    lo = (r - mid.astype(F32)).astype(BF16)
    return hi, mid, lo


def _log_sigmoid(x):
    return jnp.minimum(x, 0.0) - jnp.log(1.0 + jnp.exp(-jnp.abs(x)))


def _mlstm_kernel(qf_ref, kf_ref, vtf_ref, gcf_ref, grf_ref, qb_ref, kb_ref, vtb_ref, gcb_ref, grb_ref,
                  bc_ref, br_ref, of_ref, ob_ref, ct_scr, m_scr, *, L):
    @pl.when(pl.program_id(0) == 0)
    def _():
        ct_scr[...] = jnp.zeros_like(ct_scr)
        m_scr[...] = jnp.zeros_like(m_scr)

    row = lax.broadcasted_iota(jnp.int32, (L, L), 0)
    col = lax.broadcasted_iota(jnp.int32, (L, L), 1)
    ones_rows = (lax.broadcasted_iota(jnp.int32, (MV, L), 0) == 0).astype(BF16)
    scans = ((0, qf_ref, kf_ref, vtf_ref, gcf_ref, grf_ref, of_ref),
             (1, qb_ref, kb_ref, vtb_ref, gcb_ref, grb_ref, ob_ref))

    pairs = []
    for rev, q_ref, k_ref, vt_ref, gc_ref, gr_ref, o_ref in scans:
        before = (row >= col) if rev else (row <= col)
        before_b = before.astype(BF16)
        after_b = ((row <= col) if rev else (row >= col)).astype(BF16)
        gc = gc_ref[...] + bc_ref[...]
        gr = gr_ref[...] + br_ref[...]
        lfc = _log_sigmoid(gc)
        lfr = _log_sigmoid(gr)
        b_col = sum(jnp.dot(after_b, p, preferred_element_type=F32) for p in _split3(lfc))
        b_row = sum(jnp.dot(p, before_b, preferred_element_type=F32) for p in _split3(lfr))
        for h in range(H_M):
            ci = 2 * H_M * rev + h
            cf = ci + H_M
            q = q_ref[:, h * MQK:(h + 1) * MQK] * (MQK ** -0.5)
            k = k_ref[:, h * MQK:(h + 1) * MQK]
            ct = ct_scr[rev, h]
            pairs.append(dict(
                rev=rev, h=h, before=before, o_ref=o_ref, k=k, ct=ct,
                g_c=gc[:, ci:ci + 1] - b_col[:, cf:cf + 1],
                b_r=b_row[cf:cf + 1, :], i_r=gr[ci:ci + 1, :],
                bl=jnp.sum(lfr[cf:cf + 1, :], axis=1, keepdims=True),
                m=m_scr[rev, h][0:1, 0:1],
                vt_aug=jnp.concatenate([vt_ref[h * MV:(h + 1) * MV, :], ones_rows], axis=0),
                kq=lax.dot_general(k, q, _NT, preferred_element_type=F32),
                cq=lax.dot_general(ct.astype(BF16), q, _NT, preferred_element_type=F32)))

    for p in pairs:
        pm = jnp.max(jnp.where(p["before"], p["g_c"], -jnp.inf), axis=0, keepdims=True)
        inter = p["b_r"] + p["m"]
        p["mj"] = jnp.maximum(inter, p["b_r"] + pm)
        w = jnp.exp(jnp.where(p["before"], p["g_c"] + (p["b_r"] - p["mj"]), -jnp.inf))
        p["sc"] = jnp.exp(inter - p["mj"])
        p["kqw"] = (p["kq"] * w).astype(BF16)

    for p in pairs:
        h = p["h"]
        num = jnp.dot(p["vt_aug"], p["kqw"], preferred_element_type=F32) + p["sc"] * p["cq"]
        den = jnp.maximum(jnp.abs(num[MV:MV + 1, :]), jnp.exp(-p["mj"]))
        p["o_ref"][:, h * MV:(h + 1) * MV] = (num[:MV, :] / den).T

    for p in pairs:
        wlog = p["bl"] - p["b_r"] + p["i_r"]
        m_new = jnp.maximum(p["bl"] + p["m"], jnp.max(wlog, axis=1, keepdims=True))
        ws = jnp.exp(wlog - m_new)
        sd = jnp.exp(p["bl"] + p["m"] - m_new)
        wvt = (p["vt_aug"].astype(F32) * ws).astype(BF16)
        ct_scr[p["rev"], p["h"]] = sd * p["ct"] + jnp.dot(wvt, p["k"], preferred_element_type=F32)
        m_scr[p["rev"], p["h"]] = jnp.broadcast_to(m_new, m_scr.shape[2:])


def _mlstm(mq, mk, mvt, pg, pg_t, bg_c, bg_r, L):
    t = mq.shape[0]
    nc = t // L
    fwd = lambda c: c
    bwd = lambda c: jnp.where(c == 0, 0, nc - c)

    def chunk_specs(idx):
        return [pl.BlockSpec((L, SEG_Q), lambda c: (idx(c), 0)),
                pl.BlockSpec((L, SEG_K), lambda c: (idx(c), 0)),
                pl.BlockSpec((SEG_V, L), lambda c: (0, idx(c))),
                pl.BlockSpec((L, LANES), lambda c: (idx(c), 0)),
                pl.BlockSpec((N_GATES, L), lambda c: (0, idx(c)))]

    chunk_args = (mq, mk, mvt, pg, pg_t)
    return pl.pallas_call(
        functools.partial(_mlstm_kernel, L=L),
        grid=(nc,),
        in_specs=chunk_specs(fwd) + chunk_specs(bwd)
        + [pl.BlockSpec((1, LANES), lambda c: (0, 0)), pl.BlockSpec((N_GATES, 1), lambda c: (0, 0))],
        out_specs=[pl.BlockSpec((L, H_M * MV), lambda c: (fwd(c), 0)),
                   pl.BlockSpec((L, H_M * MV), lambda c: (bwd(c), 0))],
        out_shape=[jax.ShapeDtypeStruct((t, H_M * MV), F32)] * 2,
        scratch_shapes=[pltpu.VMEM((2, H_M, 2 * MV, MQK), F32), pltpu.VMEM((2, H_M, SUBLANES, LANES), F32)],
        compiler_params=_cparams(1),
        name="mlstm",
    )(*chunk_args, *chunk_args, bg_c, bg_r)


def _router(logits):
    tm = logits.shape[0]
    lane = lax.broadcasted_iota(jnp.int32, (tm, LANES), 1)
    big = jnp.int32(LANES)
    neg = -jnp.inf
    is_g = lane < N_GROUPS
    lg = jnp.where(is_g, logits, neg)
    gmax = jnp.max(lg, axis=-1, keepdims=True)
    grp = jnp.min(jnp.where(lg == gmax, lane, big), axis=-1, keepdims=True)
    p_grp = 1.0 / jnp.sum(jnp.where(is_g, jnp.exp(lg - gmax), 0.0), axis=-1, keepdims=True)
    e_lo = N_GROUPS + grp * EPG
    in_grp = (lane >= e_lo) & (lane < e_lo + EPG)
    le = jnp.where(in_grp, logits, neg)
    l1 = jnp.max(le, axis=-1, keepdims=True)
    i1 = jnp.min(jnp.where(le == l1, lane, big), axis=-1, keepdims=True)
    le2 = jnp.where(lane == i1, neg, le)
    l2 = jnp.max(le2, axis=-1, keepdims=True)
    i2 = jnp.min(jnp.where(le2 == l2, lane, big), axis=-1, keepdims=True)
    r = jnp.exp(l2 - l1)
    w1 = p_grp / (1.0 + r)
    w2 = w1 * r
    return jnp.where(lane == i1, w1, 0.0) + jnp.where(lane == i2, w2, 0.0), grp


def _merge_kernel(att_ref, hf_ref, hb_ref, po_ref, pm_ref, x_ref, mod_ref, gmh_ref, l1g_ref, l1b_ref,
                  wa_ref, wm_ref, wo_ref, wr_ref, br_ref, x1_ref, hm_ref, plan_ref, cnt_ref, *, tm):
    is_ctx = _is_ctx_rows(tm)
    y_mla = jnp.dot(att_ref[...], wa_ref[...], preferred_element_type=F32)
    hs = hf_ref[...] + hb_ref[...]
    parts = []
    for h in range(H_M):
        sl = slice(h * MV, (h + 1) * MV)
        parts.append(_ln(hs[:, sl]))
    hn = jnp.concatenate(parts, axis=1) * gmh_ref[...] * po_ref[...].astype(F32)
    y_ml = jnp.dot(hn.astype(BF16), wm_ref[...], preferred_element_type=F32)
    g_a = pm_ref[:, :D].astype(F32)
    g_b = pm_ref[:, D:].astype(F32)
    y = jnp.dot((g_a * y_mla + g_b * y_ml).astype(BF16), wo_ref[...], preferred_element_type=F32)
    x1 = _ln(ALPHA * x_ref[...] + _mod(mod_ref, 2, is_ctx) * y) * l1g_ref[...] + l1b_ref[...]
    x1_ref[...] = x1
    hm = _ln(x1) * (1.0 + _mod(mod_ref, 4, is_ctx)) + _mod(mod_ref, 3, is_ctx)
    hm_ref[...] = hm.astype(BF16)
    h_hi = hm.astype(BF16)
    h_lo = (hm - h_hi.astype(F32)).astype(BF16)
    logits = (jnp.dot(h_hi, wr_ref[0], preferred_element_type=F32)
              + jnp.dot(h_hi, wr_ref[1], preferred_element_type=F32)
              + jnp.dot(h_lo, wr_ref[0], preferred_element_type=F32)) + br_ref[...]
    comb, grp = _router(logits)
    lane = lax.broadcasted_iota(jnp.int32, (tm, LANES), 1)
    in_group = lane == grp
    earlier = (lax.broadcasted_iota(jnp.int32, (tm, tm), 0)
               > lax.broadcasted_iota(jnp.int32, (tm, tm), 1)).astype(BF16)
    n_before = jnp.dot(earlier, in_group.astype(BF16), preferred_element_type=F32)
    rank = jnp.sum(jnp.where(in_group, n_before, 0.0), axis=-1, keepdims=True)
    plan_ref[...] = (comb + jnp.where(lane == PLAN_GROUP, grp.astype(F32), 0.0)
                     + jnp.where(lane == PLAN_RANK, rank, 0.0))
    cnt_ref[...] = jnp.broadcast_to(jnp.sum(in_group.astype(F32), axis=0, keepdims=True), cnt_ref.shape)


def _merge(att, hf, hb, po, pm, x_all, mod, g_mh, l1g, l1b, wa, wm, wo, wr, br, tm):
    t = x_all.shape[0]
    row = lambda n: pl.BlockSpec((tm, n), lambda i: (i, 0))
    full = lambda a: pl.BlockSpec(a.shape, lambda i: (0,) * a.ndim)
    return pl.pallas_call(
        functools.partial(_merge_kernel, tm=tm),
        grid=(t // tm,),
        in_specs=[row(H_MLA * HP), row(H_M * MV), row(H_M * MV), row(SEG_PO), row(SEG_PM), row(D),
                  full(mod), full(g_mh), full(l1g), full(l1b), full(wa), full(wm), full(wo), full(wr), full(br)],
        out_specs=[row(D), row(D), row(LANES), pl.BlockSpec((None, SUBLANES, LANES), lambda i: (i, 0, 0))],
        out_shape=[jax.ShapeDtypeStruct((t, D), F32), jax.ShapeDtypeStruct((t, D), BF16),
                   jax.ShapeDtypeStruct((t, LANES), F32), jax.ShapeDtypeStruct((t // tm, SUBLANES, LANES), F32)],
        compiler_params=_cparams(1),
        name="merge",
    )(att, hf, hb, po, pm, x_all, mod, g_mh, l1g, l1b, wa, wm, wo, wr, br)


def _moe_kernel(cnt_ref, layer_ref, hm_ref, plan_ref, plant_ref, x1_ref, mod_ref, l2g_ref, l2b_ref, wg_ref, wu_ref, wd_ref,
                o_ref, *, tm, tp, slot_sizes):
    i = pl.program_id(0)
    grp = pl.program_id(1)
    n_sub = tm // tp

    @pl.when(grp == 0)
    def _():
        o_ref[...] = jnp.zeros_like(o_ref)

    n_tok = cnt_ref[i * n_sub, grp]
    for j in range(1, n_sub):
        n_tok = jnp.maximum(n_tok, cnt_ref[i * n_sub + j, grp])

    def rounds_of(slots):
        return functools.partial(_moe_round, hm_ref, plan_ref, plant_ref, wg_ref, wu_ref, wd_ref, o_ref,
                                 grp=grp, n_sub=n_sub, tp=tp, slots=slots)

    big = slot_sizes[-1]
    tails = slot_sizes[:-1]
    n_full = lax.div(n_tok, big) + (lax.rem(n_tok, big) > (tails[-1] if tails else 0)).astype(jnp.int32)
    rest = jnp.maximum(n_tok - n_full * big, 0)
    full_round = rounds_of(big)
    lax.fori_loop(0, n_full, lambda r, carry: full_round(r * big, carry), 0)
    lower = 0
    for slots in tails:
        @pl.when((rest > lower) & (rest <= slots))
        def _(tail_round=rounds_of(slots)):
            tail_round(n_full * big, 0)

        lower = slots

    @pl.when(grp == N_GROUPS - 1)
    def _():
        is_ctx = _is_ctx_rows(tm)
        z = ALPHA * x1_ref[...] + _mod(mod_ref, 5, is_ctx) * o_ref[...]
        o_ref[...] = _ln(z) * l2g_ref[...] + l2b_ref[...]


def _moe_round(hm_ref, plan_ref, plant_ref, wg_ref, wu_ref, wd_ref, o_ref, first, carry, *, grp, n_sub, tp, slots):
    grp_f = grp.astype(F32)
    base = first.astype(F32)
    slot_rows = lax.broadcasted_iota(jnp.int32, (slots, tp), 0).astype(F32)
    slot_cols = lax.broadcasted_iota(jnp.int32, (tp, slots), 1).astype(F32)
    lane = lax.broadcasted_iota(jnp.int32, (n_sub * slots, LANES), 1)
    xs, cs = [], []
    for j in range(n_sub):
        rows = slice(j * tp, (j + 1) * tp)
        in_grp = plant_ref[PLAN_GROUP:PLAN_GROUP + 1, rows] == grp_f
        rank = plant_ref[PLAN_RANK:PLAN_RANK + 1, rows]
        gather = ((slot_rows == rank - base) & in_grp).astype(BF16)
        xs.append(jnp.dot(gather, hm_ref[rows, :], preferred_element_type=F32).astype(BF16))
        cs.append(sum(jnp.dot(gather, p, preferred_element_type=F32)
                      for p in _split3(plan_ref[rows, :])))
    xs = jnp.concatenate(xs, axis=0)
    cs = jnp.concatenate(cs, axis=0)
    hid = []
    for g in range(EPG):
        c_e = jnp.sum(jnp.where(lane == N_GROUPS + grp * EPG + g, cs, 0.0), axis=-1, keepdims=True)
        a = jnp.dot(xs, wg_ref[g], preferred_element_type=F32)
        u = jnp.dot(xs, wu_ref[g], preferred_element_type=F32)
        hid.append((a * _sigmoid(a) * u * c_e).astype(BF16))
    ys = jnp.dot(jnp.concatenate(hid, axis=1), wd_ref[...].reshape(EPG * D_EXP, D),
                 preferred_element_type=F32)
    ys = ys.astype(BF16)
    for j in range(n_sub):
        rows = slice(j * tp, (j + 1) * tp)
        mine = slice(j * slots, (j + 1) * slots)
        in_grp = plan_ref[rows, PLAN_GROUP:PLAN_GROUP + 1] == grp_f
        rank = plan_ref[rows, PLAN_RANK:PLAN_RANK + 1]
        scatter = ((slot_cols == rank - base) & in_grp).astype(BF16)
        o_ref[rows, :] += jnp.dot(scatter, ys[mine], preferred_element_type=F32)
    return carry


def _moe(counts, hm, plan, plan_t, x1, mod, l2g, l2b, wg, wu, wd, layer, tp):
    t = hm.shape[0]
    tm = tp * (MOE_PLAN_TILES if (t // tp) % MOE_PLAN_TILES == 0 else 1)
    row = lambda n, **kw: pl.BlockSpec((tm, n), lambda i, e, cnt, lay: (i, 0), **kw)
    full = lambda a: pl.BlockSpec(a.shape, lambda i, e, cnt, lay: (0,) * a.ndim)
    group_of = lambda a: pl.BlockSpec((None, EPG) + a.shape[2:], lambda i, e, cnt, lay: (lay[0], e, 0, 0))
    return pl.pallas_call(
        functools.partial(_moe_kernel, tm=tm, tp=tp, slot_sizes=_moe_slot_sizes(tp)),
        grid_spec=pltpu.PrefetchScalarGridSpec(
            num_scalar_prefetch=2,
            grid=(t // tm, N_GROUPS),
            in_specs=[row(D), row(LANES), pl.BlockSpec((2, tm), lambda i, e, cnt, lay: (0, i)),
                      row(D, pipeline_mode=pl.Buffered(1)),
                      full(mod), full(l2g), full(l2b), group_of(wg), group_of(wu), group_of(wd)],
            out_specs=row(D)),
        out_shape=jax.ShapeDtypeStruct((t, D), F32),
        compiler_params=_cparams(2, VMEM_LIMIT_MOE),
        name="moe",
    )(counts, jnp.full((1,), layer, jnp.int32), hm, plan, plan_t, x1, mod, l2g, l2b, wg, wu, wd)


def _rot_cols(w):
    a1, a2, b1, b2 = jnp.split(w, 4, axis=-1)
    return jnp.concatenate([-a2, a1, -b2, b1], axis=-1)


def _place_rope(w):
    return jnp.pad(w, ((0, 0), (QK_NOPE, HP - QK_NOPE - QK_ROPE)))


def _rope_tables(t):
    rows = (t - CTX) // GRID_W
    row, col = jnp.meshgrid(jnp.arange(rows, dtype=F32), jnp.arange(GRID_W, dtype=F32), indexing="ij")
    row, col = row.reshape(-1), col.reshape(-1)
    half = QK_ROPE // 2
    inv = ROPE_BASE ** (-jnp.arange(0, half, 2, dtype=F32) / half)
    ar, ac = row[:, None] * inv, col[:, None] * inv
    ang = jnp.concatenate([ar, ar, ac, ac], axis=-1)
    ang = jnp.concatenate([jnp.zeros((CTX, QK_ROPE), F32), ang], axis=0)
    pad = ((0, 0), (QK_NOPE, HP - QK_NOPE - QK_ROPE))
    cos = jnp.pad(jnp.cos(ang), pad) + jnp.pad(jnp.ones((t, QK_NOPE), F32), ((0, 0), (0, HP - QK_NOPE)))
    sin = jnp.pad(jnp.sin(ang), pad)
    return cos, sin, cos.T, sin.T


def _layer_weights(l, w_in, w_uq, w_uk, w_uv, w_bo_mla, w_bo_mlstm, w_out, w_rg, b_rg, w_re, b_re):
    wi = w_in[l]
    o = 0
    segs = []
    for n in (Q_LORA, KV_LORA, QK_ROPE, SEG_Q, SEG_K, SEG_V, SEG_PO, N_GATES, SEG_PM):
        segs.append(wi[:, o:o + n])
        o += n
    s_pqd, s_ckv, s_kr, s_q, s_k, s_v, s_po, s_pg, s_pm = segs
    w_in_p = jnp.concatenate(
        [s_pqd, s_ckv, _place_rope(s_kr), _place_rope(_rot_cols(s_kr)), s_q, s_k, s_po,
         jnp.pad(s_pg, ((0, 0), (0, SEG_PG - N_GATES))), s_pm], axis=1).astype(BF16)

    uq = w_uq[l].reshape(Q_LORA, H_MLA, QK_NOPE + QK_ROPE)
    padh = ((0, 0), (0, 0), (0, HP - QK_NOPE - QK_ROPE))
    wq = jnp.pad(uq, padh).reshape(Q_LORA, H_MLA * HP)
    wqr = _rot_cols(uq[..., QK_NOPE:]).reshape(Q_LORA, H_MLA * QK_ROPE)
    padv = ((0, 0), (0, 0), (0, HP - V_HEAD))
    wk = jnp.pad(w_uk[l].reshape(KV_LORA, H_MLA, QK_NOPE), padv).reshape(KV_LORA, H_MLA * HP)
    wv = jnp.pad(w_uv[l].reshape(KV_LORA, H_MLA, V_HEAD), ((0, 0), (0, 0), (0, VP - V_HEAD))).reshape(KV_LORA, H_MLA * VP)
    wa = jnp.pad(w_bo_mla[l].reshape(H_MLA, V_HEAD, D), ((0, 0), (0, HP - V_HEAD), (0, 0))).reshape(H_MLA * HP, D)

    wr = jnp.pad(jnp.concatenate([w_rg[l], w_re[l]], axis=1), ((0, 0), (0, LANES - N_GROUPS - N_EXP)))
    wr_hi = wr.astype(BF16)
    wr_lo = (wr - wr_hi.astype(F32)).astype(BF16)
    br = jnp.pad(jnp.concatenate([b_rg[l], b_re[l]]), (0, LANES - N_GROUPS - N_EXP)).reshape(1, LANES)
    return dict(
        w_in=w_in_p, w_mvt=s_v.T.astype(BF16), wqt=wq.T.astype(BF16), wqrt=wqr.T.astype(BF16), wk=wk.astype(BF16), wvt=wv.T.astype(BF16),
        wa=wa.astype(BF16), wm=w_bo_mlstm[l].astype(BF16), wo=w_out[l].astype(BF16),
        wr=jnp.stack([wr_hi, wr_lo]), br=br)


def kernel(x, c, ctx, c_ctx, w_ada, b_ada, w_in, b_gates, w_uq, w_uk, w_uv, g_qn, g_kvn, g_mh, w_bo_mla,
           w_bo_mlstm, w_out, ln1_g, ln1_b, w_rg, b_rg, w_re, b_re, w_e_gate, w_e_up, w_e_down, ln2_g, ln2_b):
    assert x.shape[0] == 1 and c.shape[0] == 1 and ctx.shape[1] == CTX
    x_all = jnp.concatenate([ctx[0], x[0]], axis=0)
    t = x_all.shape[0]
    tm = ROW_TILE if t % ROW_TILE == 0 else M_CHUNK
    tk = ATT_TK if t % ATT_TK == 0 else M_CHUNK
    assert t % tm == 0 and t % ATT_TQ == 0 and t % tk == 0 and t % M_CHUNK == 0

    cc = jnp.pad(jnp.concatenate([c, c_ctx[None]], axis=0), ((0, SUBLANES - 2), (0, 0)))
    mods = _ada(cc, w_ada, b_ada)
    tabs = _rope_tables(t)
    row2 = lambda a: a.reshape(1, -1)
    wg_all, wu_all, wd_all = w_e_gate.astype(BF16), w_e_up.astype(BF16), w_e_down.astype(BF16)

    for l in range(DEPTH):
        w = _layer_weights(l, w_in, w_uq, w_uk, w_uv, w_bo_mla, w_bo_mlstm, w_out, w_rg, b_rg, w_re, b_re)
        mod = mods[l]
        pqd, ckv, kr, mq, mk, po, pg, pm, mvt = _inproj(x_all, mod, w["w_in"], w["w_mvt"], tm)
        qt, kk, vt = _mla_prep(pqd, ckv, kr, tabs, row2(g_qn[l]), row2(g_kvn[l]),
                               w["wqt"], w["wqrt"], w["wk"], w["wvt"], tm)
        att = _attention(qt, kk, vt, ATT_TQ, tk, ATT_CB, ATT_DEPTH)
        pg_t = pg[:, :N_GATES].T
        bg_c = jnp.pad(b_gates[l], (0, LANES - N_GATES)).reshape(1, LANES)
        bg_r = b_gates[l].reshape(N_GATES, 1)
        hf, hb = _mlstm(mq, mk, mvt, pg, pg_t, bg_c, bg_r, M_CHUNK)
        x1, hm, plan, cnt = _merge(att, hf, hb, po, pm, x_all, mod, row2(g_mh[l]), row2(ln1_g[l]), row2(ln1_b[l]),
                                   w["wa"], w["wm"], w["wo"], w["wr"], w["br"], tm)
        counts = cnt[:, 0, :N_GROUPS].astype(jnp.int32)
        plan_t = plan[:, :2].T
        x_all = _moe(counts, hm, plan, plan_t, x1, mod, row2(ln2_g[l]), row2(ln2_b[l]),
                     wg_all, wu_all, wd_all, l, tm)
    return x_all[CTX:][None]
```
